```python
import math
import jax, jax.numpy as jnp
from jax import lax
import numpy as np

D_MODEL = 1024
BATCH = 2
SEQ = 16384
DEPTH = 2
DEC_BATCH = 8
DEC_SEQ = 2048
PAST_LEN = 128

GRID_W = 64
Q_BLOCK = 128
NORM_EPS = 1e-6
N_BRANCH = 3
BRANCH_W = 512

DA_HEADS = 4
DA_HEAD_DIM = 64
DA_V_DIM = 2 * DA_HEAD_DIM
REL_BUCKETS = 32
REL_MAX_DIST = 128

GLA_HEADS = 4
GLA_DK = 64
GLA_DV = 128
GLA_GATE_RANK = 16
GLA_TAU = 16.0
GLA_CHUNK = 64

GQ_HEADS = 8
GQ_KV_HEADS = 2
GQ_HEAD_DIM = 64
ROPE_THETA = 10000.0

PEER_HEADS = 8
PEER_KEYS = 128
PEER_EXPERTS = PEER_KEYS * PEER_KEYS
PEER_QDIM = 256
PEER_HALF = PEER_QDIM // 2
PEER_TOPK = 16
PEER_TOK_BLOCK = 128

SPLIT_SIZES = (
    DA_HEADS * 2 * DA_HEAD_DIM,
    DA_HEADS * 2 * DA_HEAD_DIM,
    DA_HEADS * DA_V_DIM,
    GLA_HEADS * GLA_DK,
    GLA_HEADS * GLA_DK,
    GLA_HEADS * GLA_DV,
    GLA_HEADS * GLA_DV,
    2 * GLA_GATE_RANK,
    GQ_HEADS * GQ_HEAD_DIM,
    GQ_KV_HEADS * GQ_HEAD_DIM,
    GQ_KV_HEADS * GQ_HEAD_DIM,
    N_BRANCH * D_MODEL,
)
D_IN = sum(SPLIT_SIZES)

kernel_name = 'hybrid_diffattn_gla_axialgqa_peer_encoder'


def rmsnorm(x, g):
    xf = x.astype(jnp.float32)
    y = xf * lax.rsqrt(jnp.mean(xf * xf, axis=-1, keepdims=True) + NORM_EPS)
    return (y * g.astype(jnp.float32)).astype(x.dtype)


def rel_bucket(rel):
    nb = REL_BUCKETS // 2
    max_exact = nb // 2
    n = jnp.abs(rel)
    nf = jnp.maximum(n, 1).astype(jnp.float32)
    large = max_exact + (jnp.log(nf / max_exact) / math.log(REL_MAX_DIST / max_exact)
                         * (nb - max_exact)).astype(jnp.int32)
    large = jnp.minimum(large, nb - 1)
    return jnp.where(rel > 0, nb, 0) + jnp.where(n < max_exact, n, large)


def diff_attention(q, k, v, rel_table, lam, lam_init, sub_gain):
    b, n = q.shape[:2]
    nb = n // Q_BLOCK
    kpos = jnp.arange(n)
    scale = DA_HEAD_DIM ** -0.5

    def block(args):
        qb, i = args
        qpos = i * Q_BLOCK + jnp.arange(Q_BLOCK)
        bias = rel_table[rel_bucket(kpos[None, :] - qpos[:, None])]
        bias = jnp.transpose(bias, (2, 0, 1)).astype(jnp.float32)
        s = jnp.einsum('bqhcd,bkhcd->bchqk', qb, k).astype(jnp.float32) * scale + bias
        p = jax.nn.softmax(s, axis=-1)
        w = p[:, 0] - lam * p[:, 1]
        return jnp.einsum('bhqk,bkhe->bqhe', w.astype(v.dtype), v)

    qb = q.reshape(b, nb, Q_BLOCK, DA_HEADS, 2, DA_HEAD_DIM).swapaxes(0, 1)
    o = lax.map(block, (qb, jnp.arange(nb)))
    o = o.swapaxes(0, 1).reshape(b, n, DA_HEADS, DA_V_DIM)
    o = rmsnorm(o, sub_gain) * (1.0 - lam_init)
    return o.reshape(b, n, DA_HEADS * DA_V_DIM)


def gla_direction(q, k, v, logg, include_diag):
    b, h, n, dk = q.shape
    dv = v.shape[-1]
    nc = n // GLA_CHUNK
    L = GLA_CHUNK
    rs = lambda t: t.reshape(b, h, nc, L, t.shape[-1])
    q, k, v, logg = rs(q), rs(k), rs(v), rs(logg)
    cum = jnp.cumsum(logg, axis=3)
    last = cum[:, :, :, -1:, :]
    q_dec = q * jnp.exp(cum)
    k_dec = k * jnp.exp(-cum)
    mask = jnp.tril(jnp.ones((L, L), dtype=bool), 0 if include_diag else -1)
    a = jnp.where(mask, jnp.einsum('bhcid,bhcjd->bhcij', q_dec, k_dec), 0.0)
    o_intra = jnp.einsum('bhcij,bhcje->bhcie', a, v)
    kv = jnp.einsum('bhcjd,bhcje->bhcde', k * jnp.exp(last - cum), v)
    decay = jnp.exp(last[:, :, :, 0, :])

    def step(s, inp):
        kv_c, dec_c = inp
        return dec_c[..., None] * s + kv_c, s

    s0 = jnp.zeros((b, h, dk, dv), jnp.float32)
    _, s_prev = lax.scan(step, s0, (jnp.moveaxis(kv, 2, 0), jnp.moveaxis(decay, 2, 0)))
    s_prev = jnp.moveaxis(s_prev, 0, 2)
    o_inter = jnp.einsum('bhcid,bhcde->bhcie', q_dec, s_prev)
    return (o_intra + o_inter).reshape(b, h, n, dv)


def gla_mixer(q, k, v, r, z, gate_w, gate_b, gn_gain):
    b, n = q.shape[:2]
    heads = lambda t, d: t.reshape(b, n, GLA_HEADS, d).transpose(0, 2, 1, 3).astype(jnp.float32)
    qh = heads(q, GLA_DK) * GLA_DK ** -0.5
    kh = heads(k, GLA_DK)
    vh = heads(v, GLA_DV)
    zz = z.reshape(b, n, 2, GLA_GATE_RANK).astype(jnp.float32)
    glog = jax.nn.log_sigmoid(jnp.einsum('bnzr,zrk->bnzk', zz, gate_w.astype(jnp.float32))
                              + gate_b.astype(jnp.float32)) / GLA_TAU
    g_fwd = heads(glog[:, :, 0], GLA_DK)
    g_bwd = heads(glog[:, :, 1], GLA_DK)
    flip = lambda t: jnp.flip(t, axis=2)
    o_f = gla_direction(qh, kh, vh, g_fwd, True)
    o_b = flip(gla_direction(flip(qh), flip(kh), flip(vh), flip(g_bwd), False))
    o = (o_f + o_b).transpose(0, 2, 1, 3)
    o = rmsnorm(o, gn_gain).reshape(b, n, GLA_HEADS * GLA_DV)
    return (o * jax.nn.silu(r.astype(jnp.float32))).astype(q.dtype)


def axial_rope_tables(n):
    rows = n // GRID_W
    row_id = jnp.repeat(jnp.arange(rows), GRID_W).astype(jnp.float32)
    col_id = (jnp.arange(rows * GRID_W) % GRID_W).astype(jnp.float32)
    sec = GQ_HEAD_DIM // 2
    inv = ROPE_THETA ** (-jnp.arange(0, sec, 2, dtype=jnp.float32) / sec)
    ang = jnp.concatenate([row_id[:, None] * inv, col_id[:, None] * inv], axis=-1)
    return jnp.cos(ang), jnp.sin(ang)


def apply_axial_rope(x, cos, sin):
    n = x.shape[1]
    q4 = GQ_HEAD_DIM // 4
    xs = x.astype(jnp.float32).reshape(*x.shape[:-1], 2, 2, q4)
    x1, x2 = xs[..., 0, :], xs[..., 1, :]
    c = cos.reshape(n, 1, 2, q4)
    s = sin.reshape(n, 1, 2, q4)
    out = jnp.stack([x1 * c - x2 * s, x1 * s + x2 * c], axis=-2)
    return out.reshape(x.shape).astype(x.dtype)


def gqa_axial_attention(q, k, v, qk_gain):
    b, n = q.shape[:2]
    grp = GQ_HEADS // GQ_KV_HEADS
    q = rmsnorm(q.reshape(b, n, GQ_HEADS, GQ_HEAD_DIM), qk_gain[0])
    k = rmsnorm(k.reshape(b, n, GQ_KV_HEADS, GQ_HEAD_DIM), qk_gain[1])
    v = v.reshape(b, n, GQ_KV_HEADS, GQ_HEAD_DIM)
    cos, sin = axial_rope_tables(n)
    q = apply_axial_rope(q, cos, sin)
    k = apply_axial_rope(k, cos, sin)
    scale = GQ_HEAD_DIM ** -0.5
    nb = n // Q_BLOCK

    def block(qb):
        s = jnp.einsum('bqgrd,bkgd->bgrqk', qb, k).astype(jnp.float32) * scale
        p = jax.nn.softmax(s, axis=-1)
        return jnp.einsum('bgrqk,bkgd->bqgrd', p.astype(v.dtype), v)

    qb = q.reshape(b, nb, Q_BLOCK, GQ_KV_HEADS, grp, GQ_HEAD_DIM).swapaxes(0, 1)
    o = lax.map(block, qb)
    return o.swapaxes(0, 1).reshape(b, n, GQ_HEADS * GQ_HEAD_DIM)


def peer_ffn(h, w_query, sub_keys, expert_u, expert_v):
    b, n, d = h.shape
    nblk = (b * n) // PEER_TOK_BLOCK

    def block(tb):
        t = tb.shape[0]
        q = (tb @ w_query).reshape(t, PEER_HEADS, 2, PEER_HALF)
        s = jnp.einsum('thcd,hckd->thck', q, sub_keys).astype(jnp.float32)
        sv, si = lax.top_k(s, PEER_TOPK)
        cand = sv[:, :, 0, :, None] + sv[:, :, 1, None, :]
        cid = si[:, :, 0, :, None] * PEER_KEYS + si[:, :, 1, None, :]
        cv, ci = lax.top_k(cand.reshape(t, PEER_HEADS, PEER_TOPK * PEER_TOPK), PEER_TOPK)
        eid = jnp.take_along_axis(cid.reshape(t, PEER_HEADS, PEER_TOPK * PEER_TOPK), ci, axis=-1)
        g = jax.nn.softmax(cv, axis=-1)
        act = jax.nn.gelu(jnp.einsum('thkd,td->thk', expert_u[eid], tb).astype(jnp.float32),
                          approximate=False)
        coef = (g * act).astype(tb.dtype)
        return jnp.einsum('thk,thkd->td', coef, expert_v[eid])

    out = lax.map(block, h.reshape(nblk, PEER_TOK_BLOCK, d))
    return out.reshape(b, n, d)


def encoder(x, rel_bias, norm1_g, w_in, da_lambda, da_subln_g, gla_gate_w, gla_gate_b,
            gla_norm_g, gq_qk_g, w_branch, w_out, norm2_g, peer_wq, peer_subkeys, peer_u,
            peer_v, final_g):
    b, n, _ = x.shape
    offsets = np.cumsum(SPLIT_SIZES)[:-1].tolist()
    for l in range(DEPTH):
        h = rmsnorm(x, norm1_g[l])
        (aq, ak, av, bq, bk, bv, br, bz, cq, ck, cv, gz) = jnp.split(h @ w_in[l], offsets, axis=-1)
        lam_init = 0.8 - 0.6 * math.exp(-0.3 * l)
        lp = da_lambda[l].astype(jnp.float32)
        lam = jnp.exp(jnp.sum(lp[0] * lp[1])) - jnp.exp(jnp.sum(lp[2] * lp[3])) + lam_init
        y_a = diff_attention(aq.reshape(b, n, DA_HEADS, 2, DA_HEAD_DIM),
                             ak.reshape(b, n, DA_HEADS, 2, DA_HEAD_DIM),
                             av.reshape(b, n, DA_HEADS, DA_V_DIM),
                             rel_bias, lam, lam_init, da_subln_g[l])
        y_b = gla_mixer(bq, bk, bv, br, bz, gla_gate_w[l], gla_gate_b[l], gla_norm_g[l])
        y_c = gqa_axial_attention(cq, ck, cv, gq_qk_g[l])
        gates = jax.nn.sigmoid(gz.reshape(b, n, N_BRANCH, D_MODEL))
        merged = (gates[:, :, 0] * (y_a @ w_branch[l, 0])
                  + gates[:, :, 1] * (y_b @ w_branch[l, 1])
                  + gates[:, :, 2] * (y_c @ w_branch[l, 2]))
        x = x + merged @ w_out[l]
        x = x + peer_ffn(rmsnorm(x, norm2_g[l]), peer_wq[l], peer_subkeys[l], peer_u[l], peer_v[l])
    return rmsnorm(x, final_g)


def setup_inputs(seed: int = 0) -> dict:
    key = jax.random.key(seed)
    ks = jax.random.split(key, 19)
    nrm = lambda k, shape, std: jax.random.normal(k, shape, jnp.float32) * std
    gain = lambda k, shape: 1.0 + 0.02 * jax.random.normal(k, shape, jnp.float32)
    L, D = DEPTH, D_MODEL
    return {
        'x_prompt': nrm(ks[0], (BATCH, SEQ, D), 1.0),
        'x_sample': nrm(ks[1], (DEC_BATCH, DEC_SEQ, D), 1.0),
        'rel_bias': nrm(ks[2], (REL_BUCKETS, DA_HEADS), 0.5),
        'norm1_g': gain(ks[3], (L, D)),
        'w_in': nrm(ks[4], (L, D, D_IN), D ** -0.5),
        'da_lambda': nrm(ks[5], (L, 4, DA_HEAD_DIM), 0.1),
        'da_subln_g': gain(ks[6], (L, DA_V_DIM)),
        'gla_gate_w': nrm(ks[7], (L, 2, GLA_GATE_RANK, GLA_HEADS * GLA_DK), GLA_GATE_RANK ** -0.5),
        'gla_gate_b': nrm(ks[8], (L, 2, GLA_HEADS * GLA_DK), 0.01),
        'gla_norm_g': gain(ks[9], (L, GLA_DV)),
        'gq_qk_g': gain(ks[10], (L, 2, GQ_HEAD_DIM)),
        'w_branch': nrm(ks[11], (L, N_BRANCH, BRANCH_W, D), BRANCH_W ** -0.5),
        'w_out': nrm(ks[12], (L, D, D), D ** -0.5),
        'norm2_g': gain(ks[13], (L, D)),
        'peer_wq': nrm(ks[14], (L, D, PEER_HEADS * PEER_QDIM), D ** -0.5),
        'peer_subkeys': nrm(ks[15], (L, PEER_HEADS, 2, PEER_KEYS, PEER_HALF), PEER_HALF ** -0.5),
        'peer_u': nrm(ks[16], (L, PEER_EXPERTS, D), D ** -0.5),
        'peer_v': nrm(ks[17], (L, PEER_EXPERTS, D), PEER_HEADS ** -0.5),
        'final_g': gain(ks[18], (D,)),
    }


def reference(x_prompt, x_sample, rel_bias, norm1_g, w_in, da_lambda, da_subln_g, gla_gate_w,
              gla_gate_b, gla_norm_g, gq_qk_g, w_branch, w_out, norm2_g, peer_wq, peer_subkeys,
              peer_u, peer_v, final_g):
    y_prompt = encoder(x_prompt, rel_bias, norm1_g, w_in, da_lambda, da_subln_g, gla_gate_w,
                       gla_gate_b, gla_norm_g, gq_qk_g, w_branch, w_out, norm2_g, peer_wq,
                       peer_subkeys, peer_u, peer_v, final_g)
    y_sample = encoder(x_sample, rel_bias, norm1_g, w_in, da_lambda, da_subln_g, gla_gate_w,
                       gla_gate_b, gla_norm_g, gq_qk_g, w_branch, w_out, norm2_g, peer_wq,
                       peer_subkeys, peer_u, peer_v, final_g)
    return (y_prompt, y_sample)
```

```python
import functools
import math

import numpy as np
import jax
import jax.numpy as jnp
from jax import lax
from jax.experimental import pallas as pl
from jax.experimental.pallas import tpu as pltpu

D_MODEL = 1024
DEPTH = 2
GRID_W = 64
NORM_EPS = 1e-6
N_BRANCH = 3
BRANCH_W = 512

DA_HEADS = 4
DA_HEAD_DIM = 64
DA_V_DIM = 128
REL_BUCKETS = 32
REL_MAX_DIST = 128

GLA_HEADS = 4
GLA_DK = 64
GLA_DV = 128
GLA_GATE_RANK = 16
GLA_TAU = 16.0
GLA_CHUNK = 64

GQ_HEADS = 8
GQ_KV_HEADS = 2
GQ_GROUP = GQ_HEADS // GQ_KV_HEADS
GQ_HEAD_DIM = 64
ROPE_THETA = 10000.0

PEER_HEADS = 8
PEER_KEYS = 128
PEER_EXPERTS = PEER_KEYS * PEER_KEYS
PEER_QDIM = 256
PEER_HALF = 128
PEER_TOPK = 16

COL_GZ = 0
COL_AQ, COL_AK, COL_AV = 3072, 3584, 4096
COL_BQ, COL_BK, COL_BV, COL_BR = 4608, 4864, 5120, 5632
COL_CQ, COL_CK, COL_CV = 6144, 6656, 6784
COL_BZ = 6912
PROJ_W = 7168
ORIG_BZ = 3072
ORIG_CQ = ORIG_BZ + 2 * GLA_GATE_RANK
ORIG_GZ = ORIG_CQ + 768
ORIG_W = ORIG_GZ + N_BRANCH * D_MODEL

VMEM_LIMIT = 56 * 1024 * 1024

F32 = jnp.float32
BF16 = jnp.bfloat16
NEG_INF = float("-inf")


def _cparams(sem):
    return pltpu.CompilerParams(dimension_semantics=sem, vmem_limit_bytes=VMEM_LIMIT)


def _dot(a, b):
    return jnp.dot(a, b, preferred_element_type=F32)


def _dot_nt(a, b):
    return lax.dot_general(a, b, (((1,), (1,)), ((), ())), preferred_element_type=F32)


def _dot_tn(a, b):
    return lax.dot_general(a, b, (((0,), (0,)), ((), ())), preferred_element_type=F32)


def _rms(x, g):
    return x * lax.rsqrt(jnp.mean(x * x, axis=-1, keepdims=True) + NORM_EPS) * g


def _in_proj_kernel(x_ref, g_ref, w_ref, o_ref, hn_ref):
    @pl.when(pl.program_id(1) == 0)
    def _():
        hn_ref[...] = _rms(x_ref[...], g_ref[...]).astype(BF16)

    o_ref[...] = _dot(hn_ref[...], w_ref[...]).astype(o_ref.dtype)


def in_proj(x2d, g, w, *, tm=512, tn=1024):
    t = x2d.shape[0]
    nw = w.shape[1]
    return pl.pallas_call(
        _in_proj_kernel,
        grid=(t // tm, nw // tn),
        in_specs=[
            pl.BlockSpec((tm, D_MODEL), lambda i, j: (i, 0)),
            pl.BlockSpec((1, D_MODEL), lambda i, j: (0, 0)),
            pl.BlockSpec((D_MODEL, tn), lambda i, j: (0, j)),
        ],
        out_specs=pl.BlockSpec((tm, tn), lambda i, j: (i, j)),
        out_shape=jax.ShapeDtypeStruct((t, nw), BF16),
        scratch_shapes=[pltpu.VMEM((tm, D_MODEL), BF16)],
        compiler_params=_cparams(("parallel", "arbitrary")),
        name="in_proj",
    )(x2d, g, w)


def _online_softmax_step(s, v, m_ref, l_ref, acc_ref):
    m_prev = m_ref[...]
    m_cur = jnp.max(s, axis=1, keepdims=True)
    m_next = jnp.maximum(m_prev, m_cur)
    p = jnp.exp(s - m_next[:, 0:1])
    alpha = jnp.exp(m_prev - m_next)
    l_ref[...] = alpha * l_ref[...] + jnp.sum(p, axis=1, keepdims=True)
    m_ref[...] = m_next
    dv = acc_ref.shape[1]
    acc_ref[...] = alpha[:, 0:dv] * acc_ref[...] + _dot(p.astype(BF16), v)


def _dattn_kernel(lam_ref, q_ref, k_ref, v_ref, b_ref, g_ref, o_ref,
                  qs_ref, m_ref, l_ref, acc_ref, *, tq, nk, out_scale):
    kt = pl.program_id(3)

    @pl.when(kt == 0)
    def _():
        q = q_ref[0] * 0.125
        lane = lax.broadcasted_iota(jnp.int32, q.shape, 1)
        zero = jnp.zeros_like(q)
        qs_ref[0:tq, :] = jnp.where(lane < DA_HEAD_DIM, q, zero)
        qs_ref[tq:2 * tq, :] = jnp.where(lane >= DA_HEAD_DIM, q, zero)
        m_ref[...] = jnp.full(m_ref.shape, NEG_INF, F32)
        l_ref[...] = jnp.zeros(l_ref.shape, F32)
        acc_ref[...] = jnp.zeros(acc_ref.shape, F32)

    s = _dot_nt(qs_ref[...], k_ref[0])
    b = b_ref[0, 0]
    s = s + jnp.concatenate([b, b], axis=0)
    _online_softmax_step(s, v_ref[0], m_ref, l_ref, acc_ref)

    @pl.when(kt == nk - 1)
    def _():
        o = acc_ref[...] / l_ref[...]
        o = o[0:tq] - lam_ref[0] * o[tq:2 * tq]
        o_ref[0] = (_rms(o, g_ref[...]) * out_scale).astype(o_ref.dtype)


def diff_attn(proj, bias_band, lam, sub_g, *, lam_init, tq=512):
    b, n, _ = proj.shape
    nq = n // tq
    kern = functools.partial(_dattn_kernel, tq=tq, nk=nq, out_scale=1.0 - lam_init)

    def bias_map(bi, h, qi, ki):
        return (jnp.clip(ki - qi, -2, 2) + 2, h, 0, 0)

    return pl.pallas_call(
        kern,
        grid=(b, DA_HEADS, nq, nq),
        in_specs=[
            pl.BlockSpec(memory_space=pltpu.SMEM),
            pl.BlockSpec((1, tq, 128), lambda bi, h, qi, ki: (bi, qi, COL_AQ // 128 + h)),
            pl.BlockSpec((1, tq, 128), lambda bi, h, qi, ki: (bi, ki, COL_AK // 128 + h)),
            pl.BlockSpec((1, tq, 128), lambda bi, h, qi, ki: (bi, ki, COL_AV // 128 + h)),
            pl.BlockSpec((1, 1, tq, tq), bias_map),
            pl.BlockSpec((1, DA_V_DIM), lambda bi, h, qi, ki: (0, 0)),
        ],
        out_specs=pl.BlockSpec((1, tq, DA_V_DIM), lambda bi, h, qi, ki: (bi, qi, h)),
        out_shape=jax.ShapeDtypeStruct((b, n, DA_HEADS * DA_V_DIM), BF16),
        scratch_shapes=[
            pltpu.VMEM((2 * tq, 128), BF16),
            pltpu.VMEM((2 * tq, 128), F32),
            pltpu.VMEM((2 * tq, 128), F32),
            pltpu.VMEM((2 * tq, DA_V_DIM), F32),
        ],
        compiler_params=_cparams(("parallel", "parallel", "parallel", "arbitrary")),
        name="diff_attn",
    )(lam, proj, proj, proj, bias_band, sub_g)


def _swap16(x):
    w = x.shape[1]
    lane = lax.broadcasted_iota(jnp.int32, x.shape, 1)
    return jnp.where(lane % 32 < 16, pltpu.roll(x, w - 16, 1), pltpu.roll(x, 16, 1))


def _seg_rms(x, seg_ones, g):
    sq = x * x
    hi = sq.astype(BF16)
    lo = (sq - hi.astype(F32)).astype(BF16)
    ss = _dot(hi, seg_ones) + _dot(lo, seg_ones)
    return x * lax.rsqrt(ss * (1.0 / GQ_HEAD_DIM) + NORM_EPS) * g


def _gqa_prep_kernel(q_ref, k_ref, v_ref, cos_ref, sin_ref, gq_ref, gk_ref, ones_ref,
                     qo_ref, ko_ref, vo_ref):
    cos = cos_ref[...]
    sin = sin_ref[...]
    cos4 = jnp.concatenate([cos] * 4, axis=1)
    sin4 = jnp.concatenate([sin] * 4, axis=1)
    ones = ones_ref[...]
    q = _seg_rms(q_ref[0].astype(F32), ones, gq_ref[...])
    k = _seg_rms(k_ref[0].astype(F32), ones[0:128, 0:128], gk_ref[...])
    q = ((q * cos4 + _swap16(q) * sin4) * 0.125).astype(BF16)
    k = (k * cos + _swap16(k) * sin).astype(BF16)
    v = v_ref[0]
    for h in range(GQ_HEADS):
        qo_ref[0, h] = q[:, h * 64:(h + 1) * 64]
    for h in range(GQ_KV_HEADS):
        ko_ref[0, h] = k[:, h * 64:(h + 1) * 64]
        vo_ref[0, h] = v[:, h * 64:(h + 1) * 64]


def gqa_prep(proj, cos2, sin2, gq, gk, seg_ones, *, tm=512):
    b, n, _ = proj.shape
    tok = lambda bi, i: (bi, i, 0)
    return pl.pallas_call(
        _gqa_prep_kernel,
        grid=(b, n // tm),
        in_specs=[
            pl.BlockSpec((1, tm, 512), lambda bi, i: (bi, i, COL_CQ // 512)),
            pl.BlockSpec((1, tm, 128), lambda bi, i: (bi, i, COL_CK // 128)),
            pl.BlockSpec((1, tm, 128), lambda bi, i: (bi, i, COL_CV // 128)),
            pl.BlockSpec((tm, 128), lambda bi, i: (i, 0)),
            pl.BlockSpec((tm, 128), lambda bi, i: (i, 0)),
            pl.BlockSpec((1, 512), lambda bi, i: (0, 0)),
            pl.BlockSpec((1, 128), lambda bi, i: (0, 0)),
            pl.BlockSpec((512, 512), lambda bi, i: (0, 0)),
        ],
        out_specs=[
            pl.BlockSpec((1, GQ_HEADS, tm, 64), lambda bi, i: (bi, 0, i, 0)),
            pl.BlockSpec((1, GQ_KV_HEADS, tm, 64), lambda bi, i: (bi, 0, i, 0)),
            pl.BlockSpec((1, GQ_KV_HEADS, tm, 64), lambda bi, i: (bi, 0, i, 0)),
        ],
        out_shape=[
            jax.ShapeDtypeStruct((b, GQ_HEADS, n, 64), BF16),
            jax.ShapeDtypeStruct((b, GQ_KV_HEADS, n, 64), BF16),
            jax.ShapeDtypeStruct((b, GQ_KV_HEADS, n, 64), BF16),
        ],
        compiler_params=_cparams(("parallel", "parallel")),
        name="gqa_prep",
    )(proj, proj, proj, cos2, sin2, gq, gk, seg_ones)


def _gqa_kernel(q_ref, k_ref, v_ref, o_ref, m_ref, l_ref, acc_ref, *, tq, nk):
    kt = pl.program_id(3)

    @pl.when(kt == 0)
    def _():
        m_ref[...] = jnp.full(m_ref.shape, NEG_INF, F32)
        l_ref[...] = jnp.zeros(l_ref.shape, F32)
        acc_ref[...] = jnp.zeros(acc_ref.shape, F32)

    q = q_ref[0].reshape(GQ_GROUP * tq, GQ_HEAD_DIM)
    s = _dot_nt(q, k_ref[0, 0])
    _online_softmax_step(s, v_ref[0, 0], m_ref, l_ref, acc_ref)

    @pl.when(kt == nk - 1)
    def _():
        o = acc_ref[...] / l_ref[...][:, 0:GQ_HEAD_DIM]
        o_ref[0] = o.reshape(GQ_GROUP, tq, GQ_HEAD_DIM).astype(o_ref.dtype)


def gqa_attn(qh, kh, vh, *, tq=256, tk=512):
    b, _, n, _ = qh.shape
    kern = functools.partial(_gqa_kernel, tq=tq, nk=n // tk)
    return pl.pallas_call(
        kern,
        grid=(b, GQ_KV_HEADS, n // tq, n // tk),
        in_specs=[
            pl.BlockSpec((1, GQ_GROUP, tq, 64), lambda bi, g, qi, ki: (bi, g, qi, 0)),
            pl.BlockSpec((1, 1, tk, 64), lambda bi, g, qi, ki: (bi, g, ki, 0)),
            pl.BlockSpec((1, 1, tk, 64), lambda bi, g, qi, ki: (bi, g, ki, 0)),
        ],
        out_specs=pl.BlockSpec((1, GQ_GROUP, tq, 64), lambda bi, g, qi, ki: (bi, g, qi, 0)),
        out_shape=jax.ShapeDtypeStruct((b, GQ_HEADS, n, 64), BF16),
        scratch_shapes=[
            pltpu.VMEM((GQ_GROUP * tq, 128), F32),
            pltpu.VMEM((GQ_GROUP * tq, 128), F32),
            pltpu.VMEM((GQ_GROUP * tq, GQ_HEAD_DIM), F32),
        ],
        compiler_params=_cparams(("parallel", "parallel", "parallel", "arbitrary")),
        name="gqa_attn",
    )(qh, kh, vh)


def _gla_kernel(q_ref, k_ref, v_ref, z_ref, gw_ref, gb_ref, o_ref, st_ref, *, reverse, nchunk):
    L = GLA_CHUNK
    hd = GLA_HEADS * GLA_DK

    @pl.when(pl.program_id(1) == 0)
    def _():
        st_ref[...] = jnp.zeros(st_ref.shape, F32)

    row = lax.broadcasted_iota(jnp.int32, (L, L), 0)
    col = lax.broadcasted_iota(jnp.int32, (L, L), 1)
    if reverse:
        tri = (col >= row).astype(BF16)
        keep = col > row
    else:
        tri = (col <= row).astype(BF16)
        keep = col <= row
    lane = lax.broadcasted_iota(jnp.int32, (L, hd), 1)

    order = range(nchunk - 1, -1, -1) if reverse else range(nchunk)
    for c in order:
        rows = pl.ds(c * L, L)
        q = q_ref[0, rows, :].astype(F32) * (GLA_DK ** -0.5)
        k = k_ref[0, rows, :].astype(F32)
        logit = _dot(z_ref[0, rows, :], gw_ref[...]) + gb_ref[...]
        lg = jax.nn.log_sigmoid(logit) * (1.0 / GLA_TAU)
        hi = lg.astype(BF16)
        lo = (lg - hi.astype(F32)).astype(BF16)
        cum = _dot(tri, hi) + _dot(tri, lo)
        last = cum[0:1, :] if reverse else cum[L - 1:L, :]
        q_dec = q * jnp.exp(cum)
        k_dec = (k * jnp.exp(-cum)).astype(BF16)
        k_last = (k * jnp.exp(last - cum)).astype(BF16)
        decay = jnp.exp(last)
        for h in range(GLA_HEADS):
            head = (lane >= h * GLA_DK) & (lane < (h + 1) * GLA_DK)
            qh = jnp.where(head, q_dec, 0.0).astype(BF16)
            vh = v_ref[0, rows, h * GLA_DV:(h + 1) * GLA_DV]
            a = jnp.where(keep, _dot_nt(qh, k_dec), 0.0)
            st = st_ref[h]
            o = _dot(a.astype(BF16), vh) + _dot_nt(qh, st.astype(BF16))
            o_ref[0, rows, h * GLA_DV:(h + 1) * GLA_DV] = o
            st_ref[h] = st * decay + _dot_tn(vh, k_last)


def gla(proj, gw, gb, *, reverse, tm=256):
    b, n, _ = proj.shape
    nb = n // tm
    kern = functools.partial(_gla_kernel, reverse=reverse, nchunk=tm // GLA_CHUNK)
    blk = (lambda i: nb - 1 - i) if reverse else (lambda i: i)
    hd = GLA_HEADS * GLA_DK
    return pl.pallas_call(
        kern,
        grid=(b, nb),
        in_specs=[
            pl.BlockSpec((1, tm, hd), lambda bi, i: (bi, blk(i), COL_BQ // hd)),
            pl.BlockSpec((1, tm, hd), lambda bi, i: (bi, blk(i), COL_BK // hd)),
            pl.BlockSpec((1, tm, 512), lambda bi, i: (bi, blk(i), COL_BV // 512)),
            pl.BlockSpec((1, tm, 128), lambda bi, i: (bi, blk(i), COL_BZ // 128)),
            pl.BlockSpec((128, hd), lambda bi, i: (0, 0)),
            pl.BlockSpec((1, hd), lambda bi, i: (0, 0)),
        ],
        out_specs=pl.BlockSpec((1, tm, GLA_HEADS * GLA_DV), lambda bi, i: (bi, blk(i), 0)),
        out_shape=jax.ShapeDtypeStruct((b, n, GLA_HEADS * GLA_DV), F32),
        scratch_shapes=[pltpu.VMEM((GLA_HEADS, GLA_DV, hd), F32)],
        compiler_params=_cparams(("parallel", "arbitrary")),
        name="gla_bwd" if reverse else "gla_fwd",
    )(proj, proj, proj, proj, gw, gb)


def _merge_kernel(x_ref, ya_ref, of_ref, ob_ref, r_ref, yc_ref, gz_ref, gn_ref, wb_ref, wo_ref,
                  o_ref):
    gn = gn_ref[...]
    o = of_ref[0] + ob_ref[0]
    r = r_ref[0].astype(F32)
    yb = []
    for h in range(GLA_HEADS):
        sl = slice(h * GLA_DV, (h + 1) * GLA_DV)
        yb.append(_rms(o[:, sl], gn) * jax.nn.silu(r[:, sl]))
    yb = jnp.concatenate(yb, axis=1).astype(BF16)

    pa = _dot(ya_ref[0], wb_ref[0])
    pb = _dot(yb, wb_ref[1])
    pc = _dot(yc_ref[0, 0], wb_ref[2, 0:64, :])
    for h in range(1, GQ_HEADS):
        pc = pc + _dot(yc_ref[0, h], wb_ref[2, h * 64:(h + 1) * 64, :])

    gz = gz_ref[0].astype(F32)
    merged = (jax.nn.sigmoid(gz[:, 0:D_MODEL]) * pa
              + jax.nn.sigmoid(gz[:, D_MODEL:2 * D_MODEL]) * pb
              + jax.nn.sigmoid(gz[:, 2 * D_MODEL:3 * D_MODEL]) * pc)
    o_ref[0] = x_ref[0] + _dot(merged.astype(BF16), wo_ref[...])


def merge(x, ya, of, ob, proj, yc, gn, wb, wo, *, tm=256):
    b, n, _ = x.shape
    tok = lambda bi, i: (bi, i, 0)
    return pl.pallas_call(
        _merge_kernel,
        grid=(b, n // tm),
        in_specs=[
            pl.BlockSpec((1, tm, D_MODEL), tok),
            pl.BlockSpec((1, tm, BRANCH_W), tok),
            pl.BlockSpec((1, tm, BRANCH_W), tok),
            pl.BlockSpec((1, tm, BRANCH_W), tok),
            pl.BlockSpec((1, tm, 512), lambda bi, i: (bi, i, COL_BR // 512)),
            pl.BlockSpec((1, GQ_HEADS, tm, 64), lambda bi, i: (bi, 0, i, 0)),
            pl.BlockSpec((1, tm, N_BRANCH * D_MODEL), lambda bi, i: (bi, i, COL_GZ // (N_BRANCH * D_MODEL))),
            pl.BlockSpec((1, GLA_DV), lambda bi, i: (0, 0)),
            pl.BlockSpec((N_BRANCH, BRANCH_W, D_MODEL), lambda bi, i: (0, 0, 0)),
            pl.BlockSpec((D_MODEL, D_MODEL), lambda bi, i: (0, 0)),
        ],
        out_specs=pl.BlockSpec((1, tm, D_MODEL), tok),
        out_shape=jax.ShapeDtypeStruct((b, n, D_MODEL), F32),
        compiler_params=_cparams(("parallel", "parallel")),
        name="merge",
    )(x, ya, of, ob, proj, yc, proj, gn, wb, wo)


TOPX = PEER_TOPK + 1
TOPX_PAD = 24


def _top_values(vals, count):
    out = []
    for _ in range(count):
        m = jnp.max(vals, axis=0, keepdims=True)
        out.append(m)
        vals = jnp.where(vals == m, NEG_INF, vals)
    return out


def _route_kernel(x_ref, g_ref, wq_ref, sk_ref, hn_ref, s1_ref, e1_ref, th_ref, e0_ref):
    tb = x_ref.shape[0]
    hn = _rms(x_ref[...], g_ref[...])
    hn_t = hn.T.astype(BF16)
    hn_ref[...] = hn_t
    q_t = _dot(wq_ref[...], hn_t).astype(BF16)
    pad = jnp.full((TOPX_PAD - TOPX, tb), NEG_INF, F32)
    for h in range(PEER_HEADS):
        r0 = (2 * h) * PEER_HALF
        s0 = _dot(sk_ref[h, 0], q_t[r0:r0 + PEER_HALF])
        s1 = _dot(sk_ref[h, 1], q_t[r0 + PEER_HALF:r0 + 2 * PEER_HALF])
        top0 = _top_values(s0, TOPX)
        top1 = jnp.concatenate(_top_values(s1, TOPX) + [pad], axis=0)
        cand = jnp.concatenate([t0 + top1 for t0 in top0], axis=0)
        best = _top_values(cand, TOPX)
        thresh = 0.5 * (best[PEER_TOPK - 1] + best[PEER_TOPK])
        m0, m1 = top0[0], top1[0:1]
        z = jnp.zeros_like(m0)
        for c in best[:PEER_TOPK]:
            z = z + jnp.exp(c - best[0])
        s1_ref[h] = s1
        e1_ref[h] = jnp.exp(s1 - m1) / z
        th_ref[h] = thresh - s0
        e0_ref[h] = jnp.exp(s0 - m0)


def peer_route(x2d, g, wq_t, subkeys, *, tb=256):
    t = x2d.shape[0]
    hk = jax.ShapeDtypeStruct((PEER_HEADS, PEER_KEYS, t), F32)
    hk_spec = pl.BlockSpec((PEER_HEADS, PEER_KEYS, tb), lambda i: (0, 0, i))
    return pl.pallas_call(
        _route_kernel,
        grid=(t // tb,),
        in_specs=[
            pl.BlockSpec((tb, D_MODEL), lambda i: (i, 0)),
            pl.BlockSpec((1, D_MODEL), lambda i: (0, 0)),
            pl.BlockSpec((PEER_HEADS * PEER_QDIM, D_MODEL), lambda i: (0, 0)),
            pl.BlockSpec((PEER_HEADS, 2, PEER_KEYS, PEER_HALF), lambda i: (0, 0, 0, 0)),
        ],
        out_specs=[pl.BlockSpec((D_MODEL, tb), lambda i: (0, i)), hk_spec, hk_spec, hk_spec, hk_spec],
        out_shape=[jax.ShapeDtypeStruct((D_MODEL, t), BF16), hk, hk, hk, hk],
        compiler_params=_cparams(("parallel",)),
        name="peer_route",
    )(x2d, g, wq_t, subkeys)


def _dense_kernel(x_ref, hn_ref, u_ref, vt_ref, s1_ref, e1_ref, th_ref, e0_ref, fg_ref, o_ref,
                  wg_ref, acc_ref, *, ne, ib, final_norm):
    e = pl.program_id(1)
    tb = hn_ref.shape[1]

    @pl.when(e == 0)
    def _():
        acc_ref[...] = jnp.zeros(acc_ref.shape, F32)

    a = _dot(u_ref[...], hn_ref[...])
    act = 0.5 * a * (1.0 + lax.erf(a * np.float32(np.sqrt(0.5))))
    for tc in range(tb // 128):
        cols = slice(tc * 128, (tc + 1) * 128)
        for il in range(ib):
            w = jnp.zeros((PEER_KEYS, 128), F32)
            for h in range(PEER_HEADS):
                sel = s1_ref[h, :, cols] >= th_ref[h, il:il + 1, cols]
                w = w + jnp.where(sel, e1_ref[h, :, cols], 0.0) * e0_ref[h, il:il + 1, cols]
            rows = slice(il * PEER_KEYS, (il + 1) * PEER_KEYS)
            wg_ref[rows, cols] = (w * act[rows, cols]).astype(BF16)
    acc_ref[...] += _dot(vt_ref[...], wg_ref[...])

    @pl.when(e == ne - 1)
    def _():
        y = x_ref[...] + acc_ref[...].T
        if final_norm:
            y = _rms(y, fg_ref[...])
        o_ref[...] = y


def peer_dense(x2d, hn_t, u, v_t, s1, e1, th, e0, final_g, *, final_norm, tb=256, ib=8):
    t = x2d.shape[0]
    eb = ib * PEER_KEYS
    ne = PEER_EXPERTS // eb
    kern = functools.partial(_dense_kernel, ne=ne, ib=ib, final_norm=final_norm)
    full = pl.BlockSpec((PEER_HEADS, PEER_KEYS, tb), lambda i, e: (0, 0, i))
    part = pl.BlockSpec((PEER_HEADS, ib, tb), lambda i, e: (0, e, i))
    return pl.pallas_call(
        kern,
        grid=(t // tb, ne),
        in_specs=[
            pl.BlockSpec((tb, D_MODEL), lambda i, e: (i, 0)),
            pl.BlockSpec((D_MODEL, tb), lambda i, e: (0, i)),
            pl.BlockSpec((eb, D_MODEL), lambda i, e: (e, 0)),
            pl.BlockSpec((D_MODEL, eb), lambda i, e: (0, e)),
            full, full, part, part,
            pl.BlockSpec((1, D_MODEL), lambda i, e: (0, 0)),
        ],
        out_specs=pl.BlockSpec((tb, D_MODEL), lambda i, e: (i, 0)),
        out_shape=jax.ShapeDtypeStruct((t, D_MODEL), F32),
        scratch_shapes=[pltpu.VMEM((eb, tb), BF16), pltpu.VMEM((D_MODEL, tb), F32)],
        compiler_params=_cparams(("parallel", "arbitrary")),
        name="peer_dense",
    )(x2d, hn_t, u, v_t, s1, e1, th, e0, final_g)


def _rel_bucket(rel):
    nb = REL_BUCKETS // 2
    max_exact = nb // 2
    n = jnp.abs(rel)
    nf = jnp.maximum(n, 1).astype(jnp.float32)
    large = max_exact + (jnp.log(nf / max_exact) / math.log(REL_MAX_DIST / max_exact)
                         * (nb - max_exact)).astype(jnp.int32)
    large = jnp.minimum(large, nb - 1)
    return jnp.where(rel > 0, nb, 0) + jnp.where(n < max_exact, n, large)


def _bias_band(rel_table, t):
    qpos = jnp.arange(t)
    tiles = []
    for d in range(-2, 3):
        rel = (d * t + jnp.arange(t))[None, :] - qpos[:, None]
        tiles.append(jnp.transpose(rel_table[_rel_bucket(rel)], (2, 0, 1)))
    return jnp.stack(tiles).astype(F32)


def _rope_tables(n):
    rows = n // GRID_W
    row_id = jnp.repeat(jnp.arange(rows), GRID_W).astype(jnp.float32)
    col_id = (jnp.arange(rows * GRID_W) % GRID_W).astype(jnp.float32)
    sec = GQ_HEAD_DIM // 2
    inv = ROPE_THETA ** (-jnp.arange(0, sec, 2, dtype=jnp.float32) / sec)
    cr, sr = jnp.cos(row_id[:, None] * inv), jnp.sin(row_id[:, None] * inv)
    cc, sc = jnp.cos(col_id[:, None] * inv), jnp.sin(col_id[:, None] * inv)
    cos = jnp.concatenate([cr, cr, cc, cc], axis=1)
    sin = jnp.concatenate([-sr, sr, -sc, sc], axis=1)
    return jnp.concatenate([cos, cos], axis=1), jnp.concatenate([sin, sin], axis=1)


def _prep_weights(w_in, gla_gate_w, gla_gate_b, gq_qk_g, w_branch, w_out, peer_wq, peer_subkeys,
                  peer_u, peer_v):
    wi = jnp.concatenate(
        [w_in[:, :, ORIG_GZ:], w_in[:, :, :ORIG_BZ], w_in[:, :, ORIG_CQ:ORIG_GZ],
         w_in[:, :, ORIG_BZ:ORIG_CQ],
         jnp.zeros((DEPTH, D_MODEL, PROJ_W - ORIG_W), w_in.dtype)], axis=2).astype(BF16)
    hd = GLA_HEADS * GLA_DK
    gw = jnp.zeros((DEPTH, 2, 128, hd), F32)
    gw = gw.at[:, 0, 0:GLA_GATE_RANK].set(gla_gate_w[:, 0])
    gw = gw.at[:, 1, GLA_GATE_RANK:2 * GLA_GATE_RANK].set(gla_gate_w[:, 1])
    return dict(
        w_in=wi, gw=gw.astype(BF16), gb=gla_gate_b.astype(F32)[:, :, None, :],
        gq=jnp.tile(gq_qk_g[:, 0], (1, GQ_HEADS))[:, None, :],
        gk=jnp.tile(gq_qk_g[:, 1], (1, GQ_KV_HEADS))[:, None, :],
        wb=w_branch.astype(BF16), wo=w_out.astype(BF16),
        wq_t=jnp.swapaxes(peer_wq, 1, 2).astype(BF16), sk=peer_subkeys.astype(BF16),
        u=peer_u.astype(BF16), v_t=jnp.swapaxes(peer_v, 1, 2).astype(BF16))


def _encoder(x, w, rel_bias, norm1_g, da_lambda, da_subln_g, gla_norm_g, norm2_g, final_g, seg_ones):
    b, n, _ = x.shape
    t = b * n
    band = _bias_band(rel_bias, 512)
    cos2, sin2 = _rope_tables(n)
    for l in range(DEPTH):
        proj = in_proj(x.reshape(t, D_MODEL), norm1_g[l][None, :], w["w_in"][l]).reshape(b, n, PROJ_W)
        lam_init = 0.8 - 0.6 * math.exp(-0.3 * l)
        lp = da_lambda[l].astype(F32)
        lam = (jnp.exp(jnp.sum(lp[0] * lp[1])) - jnp.exp(jnp.sum(lp[2] * lp[3])) + lam_init).reshape(1)
        ya = diff_attn(proj, band, lam, da_subln_g[l][None, :], lam_init=lam_init)
        qh, kh, vh = gqa_prep(proj, cos2, sin2, w["gq"][l], w["gk"][l], seg_ones)
        yc = gqa_attn(qh, kh, vh)
        of = gla(proj, w["gw"][l, 0], w["gb"][l, 0], reverse=False)
        ob = gla(proj, w["gw"][l, 1], w["gb"][l, 1], reverse=True)
        x1 = merge(x, ya, of, ob, proj, yc, gla_norm_g[l][None, :], w["wb"][l], w["wo"][l])
        x1 = x1.reshape(t, D_MODEL)
        hn_t, s1, e1, th, e0 = peer_route(x1, norm2_g[l][None, :], w["wq_t"][l], w["sk"][l])
        x = peer_dense(x1, hn_t, w["u"][l], w["v_t"][l], s1, e1, th, e0, final_g[None, :],
                       final_norm=(l == DEPTH - 1)).reshape(b, n, D_MODEL)
    return x


def kernel(x_prompt, x_sample, rel_bias, norm1_g, w_in, da_lambda, da_subln_g, gla_gate_w, gla_gate_b,
           gla_norm_g, gq_qk_g, w_branch, w_out, norm2_g, peer_wq, peer_subkeys, peer_u, peer_v, final_g):
    w = _prep_weights(w_in, gla_gate_w, gla_gate_b, gq_qk_g, w_branch, w_out, peer_wq, peer_subkeys,
                      peer_u, peer_v)
    seg = np.arange(512) // GQ_HEAD_DIM
    seg_ones = jnp.asarray(seg[:, None] == seg[None, :], dtype=BF16)
    args = (w, rel_bias, norm1_g, da_lambda, da_subln_g, gla_norm_g, norm2_g, final_g, seg_ones)
    return (_encoder(x_prompt, *args), _encoder(x_sample, *args))
```

```python
import functools
import math

import numpy as np
import jax
import jax.numpy as jnp
from jax import lax
from jax.experimental import pallas as pl
from jax.experimental.pallas import tpu as pltpu

D_MODEL = 1024
DEPTH = 2
GRID_W = 64
NORM_EPS = 1e-6
N_BRANCH = 3
BRANCH_W = 512

DA_HEADS = 4
DA_HEAD_DIM = 64
DA_V_DIM = 128
REL_BUCKETS = 32
REL_MAX_DIST = 128

GLA_HEADS = 4
GLA_DK = 64
GLA_DV = 128
GLA_GATE_RANK = 16
GLA_TAU = 16.0
GLA_CHUNK = 64

GQ_HEADS = 8
GQ_KV_HEADS = 2
GQ_GROUP = GQ_HEADS // GQ_KV_HEADS
GQ_HEAD_DIM = 64
ROPE_THETA = 10000.0

PEER_HEADS = 8
PEER_KEYS = 128
PEER_EXPERTS = PEER_KEYS * PEER_KEYS
PEER_QDIM = 256
PEER_HALF = 128
PEER_TOPK = 16

COL_GZ = 0
COL_AQ, COL_AK, COL_AV = 3072, 3584, 4096
COL_BQ, COL_BK, COL_BV, COL_BR = 4608, 4864, 5120, 5632
COL_CQ, COL_CK, COL_CV = 6144, 6656, 6784
COL_BZ = 6912
PROJ_W = 7168
ORIG_BZ = 3072
ORIG_CQ = ORIG_BZ + 2 * GLA_GATE_RANK
ORIG_GZ = ORIG_CQ + 768
ORIG_W = ORIG_GZ + N_BRANCH * D_MODEL

VMEM_LIMIT = 56 * 1024 * 1024

F32 = jnp.float32
BF16 = jnp.bfloat16
NEG_INF = float("-inf")


def _cparams(sem):
    return pltpu.CompilerParams(dimension_semantics=sem, vmem_limit_bytes=VMEM_LIMIT)


def _dot(a, b):
    return jnp.dot(a, b, preferred_element_type=F32)


def _dot_nt(a, b):
    return lax.dot_general(a, b, (((1,), (1,)), ((), ())), preferred_element_type=F32)


def _dot_tn(a, b):
    return lax.dot_general(a, b, (((0,), (0,)), ((), ())), preferred_element_type=F32)


def _rms(x, g):
    return x * lax.rsqrt(jnp.mean(x * x, axis=-1, keepdims=True) + NORM_EPS) * g


def _in_proj_kernel(x_ref, g_ref, w_ref, o_ref, hn_ref):
    @pl.when(pl.program_id(1) == 0)
    def _():
        hn_ref[...] = _rms(x_ref[...], g_ref[...]).astype(BF16)

    o_ref[...] = _dot(hn_ref[...], w_ref[...]).astype(o_ref.dtype)


def in_proj(x2d, g, w, *, tm=512, tn=1024):
    t = x2d.shape[0]
    nw = w.shape[1]
    return pl.pallas_call(
        _in_proj_kernel,
        grid=(t // tm, nw // tn),
        in_specs=[
            pl.BlockSpec((tm, D_MODEL), lambda i, j: (i, 0)),
            pl.BlockSpec((1, D_MODEL), lambda i, j: (0, 0)),
            pl.BlockSpec((D_MODEL, tn), lambda i, j: (0, j)),
        ],
        out_specs=pl.BlockSpec((tm, tn), lambda i, j: (i, j)),
        out_shape=jax.ShapeDtypeStruct((t, nw), BF16),
        scratch_shapes=[pltpu.VMEM((tm, D_MODEL), BF16)],
        compiler_params=_cparams(("parallel", "arbitrary")),
        name="in_proj",
    )(x2d, g, w)


def _online_softmax_step(s, v, m_ref, l_ref, acc_ref):
    m_prev = m_ref[...]
    m_next = jnp.maximum(m_prev, jnp.max(s, axis=0, keepdims=True))
    p = jnp.exp(s - m_next)
    alpha = jnp.exp(m_prev - m_next)
    l_ref[...] = alpha * l_ref[...] + jnp.sum(p, axis=0, keepdims=True)
    m_ref[...] = m_next
    acc_ref[...] = alpha * acc_ref[...] + _dot_tn(v, p.astype(BF16))


def _dattn_kernel(lam_ref, q_ref, k_ref, v_ref, b_ref, g_ref, o_ref,
                  qt_ref, m_ref, l_ref, acc_ref, *, tq, nk, out_scale):
    kt = pl.program_id(3)

    @pl.when(kt == 0)
    def _():
        q = q_ref[0].astype(F32) * 0.125
        lane = lax.broadcasted_iota(jnp.int32, q.shape, 1)
        qt_ref[:, 0:tq] = jnp.where(lane < DA_HEAD_DIM, q, 0.0).T.astype(BF16)
        qt_ref[:, tq:2 * tq] = jnp.where(lane >= DA_HEAD_DIM, q, 0.0).T.astype(BF16)
        m_ref[...] = jnp.full(m_ref.shape, NEG_INF, F32)
        l_ref[...] = jnp.zeros(l_ref.shape, F32)
        acc_ref[...] = jnp.zeros(acc_ref.shape, F32)

    s = _dot(k_ref[0], qt_ref[...])
    b = b_ref[0, 0]
    s = s + jnp.concatenate([b, b], axis=1)
    _online_softmax_step(s, v_ref[0], m_ref, l_ref, acc_ref)

    @pl.when(kt == nk - 1)
    def _():
        o = acc_ref[...] / l_ref[...]
        o = (o[:, 0:tq] - lam_ref[0] * o[:, tq:2 * tq]).T
        o_ref[0] = (_rms(o, g_ref[...]) * out_scale).astype(o_ref.dtype)


def diff_attn(proj, bias_band, lam, sub_g, *, lam_init, tq=512):
    b, n, _ = proj.shape
    nq = n // tq
    kern = functools.partial(_dattn_kernel, tq=tq, nk=nq, out_scale=1.0 - lam_init)

    def bias_map(bi, h, qi, ki):
        return (jnp.clip(ki - qi, -2, 2) + 2, h, 0, 0)

    return pl.pallas_call(
        kern,
        grid=(b, DA_HEADS, nq, nq),
        in_specs=[
            pl.BlockSpec(memory_space=pltpu.SMEM),
            pl.BlockSpec((1, tq, 128), lambda bi, h, qi, ki: (bi, qi, COL_AQ // 128 + h)),
            pl.BlockSpec((1, tq, 128), lambda bi, h, qi, ki: (bi, ki, COL_AK // 128 + h)),
            pl.BlockSpec((1, tq, 128), lambda bi, h, qi, ki: (bi, ki, COL_AV // 128 + h)),
            pl.BlockSpec((1, 1, tq, tq), bias_map),
            pl.BlockSpec((1, DA_V_DIM), lambda bi, h, qi, ki: (0, 0)),
        ],
        out_specs=pl.BlockSpec((1, tq, DA_V_DIM), lambda bi, h, qi, ki: (bi, qi, h)),
        out_shape=jax.ShapeDtypeStruct((b, n, DA_HEADS * DA_V_DIM), BF16),
        scratch_shapes=[
            pltpu.VMEM((128, 2 * tq), BF16),
            pltpu.VMEM((1, 2 * tq), F32),
            pltpu.VMEM((1, 2 * tq), F32),
            pltpu.VMEM((DA_V_DIM, 2 * tq), F32),
        ],
        compiler_params=_cparams(("parallel", "parallel", "parallel", "arbitrary")),
        name="diff_attn",
    )(lam, proj, proj, proj, bias_band, sub_g)


def _swap16(x):
    w = x.shape[1]
    lane = lax.broadcasted_iota(jnp.int32, x.shape, 1)
    return jnp.where(lane % 32 < 16, pltpu.roll(x, w - 16, 1), pltpu.roll(x, 16, 1))


def _seg_rms(x, seg_ones, g):
    sq = x * x
    hi = sq.astype(BF16)
    lo = (sq - hi.astype(F32)).astype(BF16)
    ss = _dot(hi, seg_ones) + _dot(lo, seg_ones)
    return x * lax.rsqrt(ss * (1.0 / GQ_HEAD_DIM) + NORM_EPS) * g


def _gqa_prep_kernel(q_ref, k_ref, v_ref, cos_ref, sin_ref, gq_ref, gk_ref, ones_ref,
                     qo_ref, ko_ref, vo_ref):
    cos = cos_ref[...]
    sin = sin_ref[...]
    cos4 = jnp.concatenate([cos] * 4, axis=1)
    sin4 = jnp.concatenate([sin] * 4, axis=1)
    ones = ones_ref[...]
    q = _seg_rms(q_ref[0].astype(F32), ones, gq_ref[...])
    k = _seg_rms(k_ref[0].astype(F32), ones[0:128, 0:128], gk_ref[...])
    q = (q * cos4 + _swap16(q) * sin4) * 0.125
    k = (k * cos + _swap16(k) * sin).astype(BF16)
    v = v_ref[0]
    qo_ref[0] = q.T.astype(BF16)
    for h in range(GQ_KV_HEADS):
        ko_ref[0, h] = k[:, h * 64:(h + 1) * 64]
        vo_ref[0, h] = v[:, h * 64:(h + 1) * 64]


def gqa_prep(proj, cos2, sin2, gq, gk, seg_ones, *, tm=512):
    b, n, _ = proj.shape
    return pl.pallas_call(
        _gqa_prep_kernel,
        grid=(b, n // tm),
        in_specs=[
            pl.BlockSpec((1, tm, 512), lambda bi, i: (bi, i, COL_CQ // 512)),
            pl.BlockSpec((1, tm, 128), lambda bi, i: (bi, i, COL_CK // 128)),
            pl.BlockSpec((1, tm, 128), lambda bi, i: (bi, i, COL_CV // 128)),
            pl.BlockSpec((tm, 128), lambda bi, i: (i, 0)),
            pl.BlockSpec((tm, 128), lambda bi, i: (i, 0)),
            pl.BlockSpec((1, 512), lambda bi, i: (0, 0)),
            pl.BlockSpec((1, 128), lambda bi, i: (0, 0)),
            pl.BlockSpec((512, 512), lambda bi, i: (0, 0)),
        ],
        out_specs=[
            pl.BlockSpec((1, GQ_HEADS * GQ_HEAD_DIM, tm), lambda bi, i: (bi, 0, i)),
            pl.BlockSpec((1, GQ_KV_HEADS, tm, 64), lambda bi, i: (bi, 0, i, 0)),
            pl.BlockSpec((1, GQ_KV_HEADS, tm, 64), lambda bi, i: (bi, 0, i, 0)),
        ],
        out_shape=[
            jax.ShapeDtypeStruct((b, GQ_HEADS * GQ_HEAD_DIM, n), BF16),
            jax.ShapeDtypeStruct((b, GQ_KV_HEADS, n, 64), BF16),
            jax.ShapeDtypeStruct((b, GQ_KV_HEADS, n, 64), BF16),
        ],
        compiler_params=_cparams(("parallel", "parallel")),
        name="gqa_prep",
    )(proj, proj, proj, cos2, sin2, gq, gk, seg_ones)


def _gqa_kernel(q_ref, k_ref, v_ref, o_ref, qt_ref, m_ref, l_ref, acc_ref, *, tq, nk):
    kt = pl.program_id(3)
    hd = GQ_HEAD_DIM

    @pl.when(kt == 0)
    def _():
        for r in range(GQ_GROUP):
            qt_ref[:, r * tq:(r + 1) * tq] = q_ref[0, r * hd:(r + 1) * hd, :]
        m_ref[...] = jnp.full(m_ref.shape, NEG_INF, F32)
        l_ref[...] = jnp.zeros(l_ref.shape, F32)
        acc_ref[...] = jnp.zeros(acc_ref.shape, F32)

    s = _dot(k_ref[0, 0], qt_ref[...])
    _online_softmax_step(s, v_ref[0, 0], m_ref, l_ref, acc_ref)

    @pl.when(kt == nk - 1)
    def _():
        o = (acc_ref[...] / l_ref[...]).astype(o_ref.dtype)
        for r in range(GQ_GROUP):
            o_ref[0, r * hd:(r + 1) * hd, :] = o[:, r * tq:(r + 1) * tq]


def gqa_attn(qt, kh, vh, *, tq=256, tk=512):
    b, _, n = qt.shape
    gw = GQ_GROUP * GQ_HEAD_DIM
    kern = functools.partial(_gqa_kernel, tq=tq, nk=n // tk)
    return pl.pallas_call(
        kern,
        grid=(b, GQ_KV_HEADS, n // tq, n // tk),
        in_specs=[
            pl.BlockSpec((1, gw, tq), lambda bi, g, qi, ki: (bi, g, qi)),
            pl.BlockSpec((1, 1, tk, 64), lambda bi, g, qi, ki: (bi, g, ki, 0)),
            pl.BlockSpec((1, 1, tk, 64), lambda bi, g, qi, ki: (bi, g, ki, 0)),
        ],
        out_specs=pl.BlockSpec((1, gw, tq), lambda bi, g, qi, ki: (bi, g, qi)),
        out_shape=jax.ShapeDtypeStruct((b, GQ_HEADS * GQ_HEAD_DIM, n), BF16),
        scratch_shapes=[
            pltpu.VMEM((GQ_HEAD_DIM, GQ_GROUP * tq), BF16),
            pltpu.VMEM((1, GQ_GROUP * tq), F32),
            pltpu.VMEM((1, GQ_GROUP * tq), F32),
            pltpu.VMEM((GQ_HEAD_DIM, GQ_GROUP * tq), F32),
        ],
        compiler_params=_cparams(("parallel", "parallel", "parallel", "arbitrary")),
        name="gqa_attn",
    )(qt, kh, vh)


def _gla_kernel(q_ref, k_ref, v_ref, z_ref, gw_ref, gb_ref, o_ref, st_ref, *, reverse, nchunk):
    L = GLA_CHUNK
    hd = GLA_HEADS * GLA_DK

    @pl.when(pl.program_id(1) == 0)
    def _():
        st_ref[...] = jnp.zeros(st_ref.shape, F32)

    row = lax.broadcasted_iota(jnp.int32, (L, L), 0)
    col = lax.broadcasted_iota(jnp.int32, (L, L), 1)
    if reverse:
        tri = (col >= row).astype(BF16)
        keep = col > row
    else:
        tri = (col <= row).astype(BF16)
        keep = col <= row
    lane = lax.broadcasted_iota(jnp.int32, (L, hd), 1)

    order = range(nchunk - 1, -1, -1) if reverse else range(nchunk)
    for c in order:
        rows = pl.ds(c * L, L)
        q = q_ref[0, rows, :].astype(F32) * (GLA_DK ** -0.5)
        k = k_ref[0, rows, :].astype(F32)
        logit = _dot(z_ref[0, rows, :], gw_ref[...]) + gb_ref[...]
        lg = jax.nn.log_sigmoid(logit) * (1.0 / GLA_TAU)
        hi = lg.astype(BF16)
        lo = (lg - hi.astype(F32)).astype(BF16)
        cum = _dot(tri, hi) + _dot(tri, lo)
        last = cum[0:1, :] if reverse else cum[L - 1:L, :]
        q_dec = q * jnp.exp(cum)
        k_dec = (k * jnp.exp(-cum)).astype(BF16)
        k_last = (k * jnp.exp(last - cum)).astype(BF16)
        decay = jnp.exp(last)
        for h in range(GLA_HEADS):
            head = (lane >= h * GLA_DK) & (lane < (h + 1) * GLA_DK)
            qh = jnp.where(head, q_dec, 0.0).astype(BF16)
            vh = v_ref[0, rows, h * GLA_DV:(h + 1) * GLA_DV]
            a = jnp.where(keep, _dot_nt(qh, k_dec), 0.0)
            st = st_ref[h]
            o = _dot(a.astype(BF16), vh) + _dot_nt(qh, st.astype(BF16))
            o_ref[0, rows, h * GLA_DV:(h + 1) * GLA_DV] = o
            st_ref[h] = st * decay + _dot_tn(vh, k_last)


def gla(proj, gw, gb, *, reverse, tm=256):
    b, n, _ = proj.shape
    nb = n // tm
    kern = functools.partial(_gla_kernel, reverse=reverse, nchunk=tm // GLA_CHUNK)
    blk = (lambda i: nb - 1 - i) if reverse else (lambda i: i)
    hd = GLA_HEADS * GLA_DK
    return pl.pallas_call(
        kern,
        grid=(b, nb),
        in_specs=[
            pl.BlockSpec((1, tm, hd), lambda bi, i: (bi, blk(i), COL_BQ // hd)),
            pl.BlockSpec((1, tm, hd), lambda bi, i: (bi, blk(i), COL_BK // hd)),
            pl.BlockSpec((1, tm, 512), lambda bi, i: (bi, blk(i), COL_BV // 512)),
            pl.BlockSpec((1, tm, 128), lambda bi, i: (bi, blk(i), COL_BZ // 128)),
            pl.BlockSpec((128, hd), lambda bi, i: (0, 0)),
            pl.BlockSpec((1, hd), lambda bi, i: (0, 0)),
        ],
        out_specs=pl.BlockSpec((1, tm, GLA_HEADS * GLA_DV), lambda bi, i: (bi, blk(i), 0)),
        out_shape=jax.ShapeDtypeStruct((b, n, GLA_HEADS * GLA_DV), F32),
        scratch_shapes=[pltpu.VMEM((GLA_HEADS, GLA_DV, hd), F32)],
        compiler_params=_cparams(("parallel", "arbitrary")),
        name="gla_bwd" if reverse else "gla_fwd",
    )(proj, proj, proj, proj, gw, gb)


def _merge_kernel(x_ref, ya_ref, of_ref, ob_ref, r_ref, yc_ref, gz_ref, gn_ref, wb_ref, wo_ref,
                  o_ref):
    gn = gn_ref[...]
    o = of_ref[0] + ob_ref[0]
    r = r_ref[0].astype(F32)
    yb = []
    for h in range(GLA_HEADS):
        sl = slice(h * GLA_DV, (h + 1) * GLA_DV)
        yb.append(_rms(o[:, sl], gn) * jax.nn.silu(r[:, sl]))
    yb = jnp.concatenate(yb, axis=1).astype(BF16)

    pa = _dot(ya_ref[0], wb_ref[0])
    pb = _dot(yb, wb_ref[1])
    pc = _dot_tn(yc_ref[0], wb_ref[2])

    gz = gz_ref[0].astype(F32)
    merged = (jax.nn.sigmoid(gz[:, 0:D_MODEL]) * pa
              + jax.nn.sigmoid(gz[:, D_MODEL:2 * D_MODEL]) * pb
              + jax.nn.sigmoid(gz[:, 2 * D_MODEL:3 * D_MODEL]) * pc)
    o_ref[0] = x_ref[0] + _dot(merged.astype(BF16), wo_ref[...])


def merge(x, ya, of, ob, proj, yc, gn, wb, wo, *, tm=256):
    b, n, _ = x.shape
    tok = lambda bi, i: (bi, i, 0)
    return pl.pallas_call(
        _merge_kernel,
        grid=(b, n // tm),
        in_specs=[
            pl.BlockSpec((1, tm, D_MODEL), tok),
            pl.BlockSpec((1, tm, BRANCH_W), tok),
            pl.BlockSpec((1, tm, BRANCH_W), tok),
            pl.BlockSpec((1, tm, BRANCH_W), tok),
            pl.BlockSpec((1, tm, 512), lambda bi, i: (bi, i, COL_BR // 512)),
            pl.BlockSpec((1, BRANCH_W, tm), lambda bi, i: (bi, 0, i)),
            pl.BlockSpec((1, tm, N_BRANCH * D_MODEL), lambda bi, i: (bi, i, COL_GZ // (N_BRANCH * D_MODEL))),
            pl.BlockSpec((1, GLA_DV), lambda bi, i: (0, 0)),
            pl.BlockSpec((N_BRANCH, BRANCH_W, D_MODEL), lambda bi, i: (0, 0, 0)),
            pl.BlockSpec((D_MODEL, D_MODEL), lambda bi, i: (0, 0)),
        ],
        out_specs=pl.BlockSpec((1, tm, D_MODEL), tok),
        out_shape=jax.ShapeDtypeStruct((b, n, D_MODEL), F32),
        compiler_params=_cparams(("parallel", "parallel")),
        name="merge",
    )(x, ya, of, ob, proj, yc, proj, gn, wb, wo)


TOPX = PEER_TOPK + 1
TOPX_PAD = 24


def _top_values(vals, count):
    out = []
    for _ in range(count):
        m = jnp.max(vals, axis=0, keepdims=True)
        out.append(m)
        vals = jnp.where(vals == m, NEG_INF, vals)
    return out


def _route_kernel(x_ref, g_ref, wq_ref, sk_ref, hn_ref, s1_ref, e1_ref, th_ref, e0_ref):
    tb = x_ref.shape[0]
    hn = _rms(x_ref[...], g_ref[...])
    hn_t = hn.T.astype(BF16)
    hn_ref[...] = hn_t
    q_t = _dot(wq_ref[...], hn_t).astype(BF16)
    pad = jnp.full((TOPX_PAD - TOPX, tb), NEG_INF, F32)
    for h in range(PEER_HEADS):
        r0 = (2 * h) * PEER_HALF
        s0 = _dot(sk_ref[h, 0], q_t[r0:r0 + PEER_HALF])
        s1 = _dot(sk_ref[h, 1], q_t[r0 + PEER_HALF:r0 + 2 * PEER_HALF])
        top0 = _top_values(s0, TOPX)
        top1 = jnp.concatenate(_top_values(s1, TOPX) + [pad], axis=0)
        cand = jnp.concatenate([t0 + top1 for t0 in top0], axis=0)
        best = _top_values(cand, TOPX)
        thresh = 0.5 * (best[PEER_TOPK - 1] + best[PEER_TOPK])
        m0, m1 = top0[0], top1[0:1]
        z = jnp.zeros_like(m0)
        for c in best[:PEER_TOPK]:
            z = z + jnp.exp(c - best[0])
        s1_ref[h] = s1
        e1_ref[h] = jnp.exp(s1 - m1) / z
        th_ref[h] = thresh - s0
        e0_ref[h] = jnp.exp(s0 - m0)


def peer_route(x2d, g, wq_t, subkeys, *, tb=256):
    t = x2d.shape[0]
    hk = jax.ShapeDtypeStruct((PEER_HEADS, PEER_KEYS, t), F32)
    hk_spec = pl.BlockSpec((PEER_HEADS, PEER_KEYS, tb), lambda i: (0, 0, i))
    return pl.pallas_call(
        _route_kernel,
        grid=(t // tb,),
        in_specs=[
            pl.BlockSpec((tb, D_MODEL), lambda i: (i, 0)),
            pl.BlockSpec((1, D_MODEL), lambda i: (0, 0)),
            pl.BlockSpec((PEER_HEADS * PEER_QDIM, D_MODEL), lambda i: (0, 0)),
            pl.BlockSpec((PEER_HEADS, 2, PEER_KEYS, PEER_HALF), lambda i: (0, 0, 0, 0)),
        ],
        out_specs=[pl.BlockSpec((D_MODEL, tb), lambda i: (0, i)), hk_spec, hk_spec, hk_spec, hk_spec],
        out_shape=[jax.ShapeDtypeStruct((D_MODEL, t), BF16), hk, hk, hk, hk],
        compiler_params=_cparams(("parallel",)),
        name="peer_route",
    )(x2d, g, wq_t, subkeys)


def _dense_kernel(x_ref, hn_ref, u_ref, vt_ref, s1_ref, e1_ref, th_ref, e0_ref, fg_ref, o_ref,
                  wg_ref, acc_ref, *, ne, ib, final_norm):
    e = pl.program_id(1)
    tb = hn_ref.shape[1]

    @pl.when(e == 0)
    def _():
        acc_ref[...] = jnp.zeros(acc_ref.shape, F32)

    a = _dot(u_ref[...], hn_ref[...])
    act = 0.5 * a * (1.0 + lax.erf(a * np.float32(np.sqrt(0.5))))
    for tc in range(tb // 128):
        cols = slice(tc * 128, (tc + 1) * 128)
        for il in range(ib):
            w = jnp.zeros((PEER_KEYS, 128), F32)
            for h in range(PEER_HEADS):
                sel = s1_ref[h, :, cols] >= th_ref[h, il:il + 1, cols]
                w = w + jnp.where(sel, e1_ref[h, :, cols], 0.0) * e0_ref[h, il:il + 1, cols]
            rows = slice(il * PEER_KEYS, (il + 1) * PEER_KEYS)
            wg_ref[rows, cols] = (w * act[rows, cols]).astype(BF16)
    acc_ref[...] += _dot(vt_ref[...], wg_ref[...])

    @pl.when(e == ne - 1)
    def _():
        y = x_ref[...] + acc_ref[...].T
        if final_norm:
            y = _rms(y, fg_ref[...])
        o_ref[...] = y


def peer_dense(x2d, hn_t, u, v_t, s1, e1, th, e0, final_g, *, final_norm, tb=256, ib=8):
    t = x2d.shape[0]
    eb = ib * PEER_KEYS
    ne = PEER_EXPERTS // eb
    kern = functools.partial(_dense_kernel, ne=ne, ib=ib, final_norm=final_norm)
    full = pl.BlockSpec((PEER_HEADS, PEER_KEYS, tb), lambda i, e: (0, 0, i))
    part = pl.BlockSpec((PEER_HEADS, ib, tb), lambda i, e: (0, e, i))
    return pl.pallas_call(
        kern,
        grid=(t // tb, ne),
        in_specs=[
            pl.BlockSpec((tb, D_MODEL), lambda i, e: (i, 0)),
            pl.BlockSpec((D_MODEL, tb), lambda i, e: (0, i)),
            pl.BlockSpec((eb, D_MODEL), lambda i, e: (e, 0)),
            pl.BlockSpec((D_MODEL, eb), lambda i, e: (0, e)),
            full, full, part, part,
            pl.BlockSpec((1, D_MODEL), lambda i, e: (0, 0)),
        ],
        out_specs=pl.BlockSpec((tb, D_MODEL), lambda i, e: (i, 0)),
        out_shape=jax.ShapeDtypeStruct((t, D_MODEL), F32),
        scratch_shapes=[pltpu.VMEM((eb, tb), BF16), pltpu.VMEM((D_MODEL, tb), F32)],
        compiler_params=_cparams(("parallel", "arbitrary")),
        name="peer_dense",
    )(x2d, hn_t, u, v_t, s1, e1, th, e0, final_g)


def _rel_bucket(rel):
    nb = REL_BUCKETS // 2
    max_exact = nb // 2
    n = jnp.abs(rel)
    nf = jnp.maximum(n, 1).astype(jnp.float32)
    large = max_exact + (jnp.log(nf / max_exact) / math.log(REL_MAX_DIST / max_exact)
                         * (nb - max_exact)).astype(jnp.int32)
    large = jnp.minimum(large, nb - 1)
    return jnp.where(rel > 0, nb, 0) + jnp.where(n < max_exact, n, large)


def _bias_band(rel_table, t):
    qpos = jnp.arange(t)
    tiles = []
    for d in range(-2, 3):
        rel = (d * t + jnp.arange(t))[:, None] - qpos[None, :]
        onehot = jax.nn.one_hot(_rel_bucket(rel), REL_BUCKETS, dtype=F32)
        tiles.append(jnp.einsum("kqb,bh->hkq", onehot, rel_table.astype(F32),
                                precision=lax.Precision.HIGHEST))
    return jnp.stack(tiles)


def _rope_tables(n):
    rows = n // GRID_W
    row_id = jnp.repeat(jnp.arange(rows), GRID_W).astype(jnp.float32)
    col_id = (jnp.arange(rows * GRID_W) % GRID_W).astype(jnp.float32)
    sec = GQ_HEAD_DIM // 2
    inv = ROPE_THETA ** (-jnp.arange(0, sec, 2, dtype=jnp.float32) / sec)
    cr, sr = jnp.cos(row_id[:, None] * inv), jnp.sin(row_id[:, None] * inv)
    cc, sc = jnp.cos(col_id[:, None] * inv), jnp.sin(col_id[:, None] * inv)
    cos = jnp.concatenate([cr, cr, cc, cc], axis=1)
    sin = jnp.concatenate([-sr, sr, -sc, sc], axis=1)
    return jnp.concatenate([cos, cos], axis=1), jnp.concatenate([sin, sin], axis=1)


def _prep_weights(w_in, gla_gate_w, gla_gate_b, gq_qk_g, w_branch, w_out, peer_wq, peer_subkeys,
                  peer_u, peer_v):
    wi = jnp.concatenate(
        [w_in[:, :, ORIG_GZ:], w_in[:, :, :ORIG_BZ], w_in[:, :, ORIG_CQ:ORIG_GZ],
         w_in[:, :, ORIG_BZ:ORIG_CQ],
         jnp.zeros((DEPTH, D_MODEL, PROJ_W - ORIG_W), w_in.dtype)], axis=2).astype(BF16)
    hd = GLA_HEADS * GLA_DK
    gw = jnp.zeros((DEPTH, 2, 128, hd), F32)
    gw = gw.at[:, 0, 0:GLA_GATE_RANK].set(gla_gate_w[:, 0])
    gw = gw.at[:, 1, GLA_GATE_RANK:2 * GLA_GATE_RANK].set(gla_gate_w[:, 1])
    return dict(
        w_in=wi, gw=gw.astype(BF16), gb=gla_gate_b.astype(F32)[:, :, None, :],
        gq=jnp.tile(gq_qk_g[:, 0], (1, GQ_HEADS))[:, None, :],
        gk=jnp.tile(gq_qk_g[:, 1], (1, GQ_KV_HEADS))[:, None, :],
        wb=w_branch.astype(BF16), wo=w_out.astype(BF16),
        wq_t=jnp.swapaxes(peer_wq, 1, 2).astype(BF16), sk=peer_subkeys.astype(BF16),
        u=peer_u.astype(BF16), v_t=jnp.swapaxes(peer_v, 1, 2).astype(BF16))


def _encoder(x, w, rel_bias, norm1_g, da_lambda, da_subln_g, gla_norm_g, norm2_g, final_g, seg_ones):
    b, n, _ = x.shape
    t = b * n
    band = _bias_band(rel_bias, 512)
    cos2, sin2 = _rope_tables(n)
    for l in range(DEPTH):
        proj = in_proj(x.reshape(t, D_MODEL), norm1_g[l][None, :], w["w_in"][l]).reshape(b, n, PROJ_W)
        lam_init = 0.8 - 0.6 * math.exp(-0.3 * l)
        lp = da_lambda[l].astype(F32)
        lam = (jnp.exp(jnp.sum(lp[0] * lp[1])) - jnp.exp(jnp.sum(lp[2] * lp[3])) + lam_init).reshape(1)
        ya = diff_attn(proj, band, lam, da_subln_g[l][None, :], lam_init=lam_init)
        qt, kh, vh = gqa_prep(proj, cos2, sin2, w["gq"][l], w["gk"][l], seg_ones)
        yc = gqa_attn(qt, kh, vh)
        of = gla(proj, w["gw"][l, 0], w["gb"][l, 0], reverse=False)
        ob = gla(proj, w["gw"][l, 1], w["gb"][l, 1], reverse=True)
        x1 = merge(x, ya, of, ob, proj, yc, gla_norm_g[l][None, :], w["wb"][l], w["wo"][l])
        x1 = x1.reshape(t, D_MODEL)
        hn_t, s1, e1, th, e0 = peer_route(x1, norm2_g[l][None, :], w["wq_t"][l], w["sk"][l])
        x = peer_dense(x1, hn_t, w["u"][l], w["v_t"][l], s1, e1, th, e0, final_g[None, :],
                       final_norm=(l == DEPTH - 1)).reshape(b, n, D_MODEL)
    return x


def kernel(x_prompt, x_sample, rel_bias, norm1_g, w_in, da_lambda, da_subln_g, gla_gate_w, gla_gate_b,
           gla_norm_g, gq_qk_g, w_branch, w_out, norm2_g, peer_wq, peer_subkeys, peer_u, peer_v, final_g):
    w = _prep_weights(w_in, gla_gate_w, gla_gate_b, gq_qk_g, w_branch, w_out, peer_wq, peer_subkeys,
                      peer_u, peer_v)
    seg = np.arange(512) // GQ_HEAD_DIM
    seg_ones = jnp.asarray(seg[:, None] == seg[None, :], dtype=BF16)
    args = (w, rel_bias, norm1_g, da_lambda, da_subln_g, gla_norm_g, norm2_g, final_g, seg_ones)
    return (_encoder(x_prompt, *args), _encoder(x_sample, *args))
```

```python
import functools
import math

import numpy as np
import jax
import jax.numpy as jnp
from jax import lax
from jax.experimental import pallas as pl
from jax.experimental.pallas import tpu as pltpu

D_MODEL = 1024
DEPTH = 2
GRID_W = 64
NORM_EPS = 1e-6
N_BRANCH = 3
BRANCH_W = 512

DA_HEADS = 4
DA_HEAD_DIM = 64
DA_V_DIM = 128
REL_BUCKETS = 32
REL_MAX_DIST = 128

GLA_HEADS = 4
GLA_DK = 64
GLA_DV = 128
GLA_GATE_RANK = 16
GLA_TAU = 16.0
GLA_CHUNK = 64

GQ_HEADS = 8
GQ_KV_HEADS = 2
GQ_GROUP = GQ_HEADS // GQ_KV_HEADS
GQ_HEAD_DIM = 64
ROPE_THETA = 10000.0

PEER_HEADS = 8
PEER_KEYS = 128
PEER_EXPERTS = PEER_KEYS * PEER_KEYS
PEER_QDIM = 256
PEER_HALF = 128
PEER_TOPK = 16

COL_GZ = 0
COL_AQ, COL_AK, COL_AV = 3072, 3584, 4096
COL_BQ, COL_BK, COL_BV, COL_BR = 4608, 4864, 5120, 5632
COL_CQ, COL_CK, COL_CV = 6144, 6656, 6784
COL_BZ = 6912
PROJ_W = 7168
ORIG_BZ = 3072
ORIG_CQ = ORIG_BZ + 2 * GLA_GATE_RANK
ORIG_GZ = ORIG_CQ + 768
ORIG_W = ORIG_GZ + N_BRANCH * D_MODEL

DA_TQ, DA_TK, DA_CW = 512, 512, 1024
GQ_TQ, GQ_TK, GQ_CW = 256, 512, 1024

VMEM_LIMIT = 56 * 1024 * 1024

F32 = jnp.float32
BF16 = jnp.bfloat16
NEG_INF = float("-inf")
LOG2E = math.log2(math.e)


def _cparams(sem):
    return pltpu.CompilerParams(dimension_semantics=sem, vmem_limit_bytes=VMEM_LIMIT)


def _dot(a, b):
    return jnp.dot(a, b, preferred_element_type=F32)


def _dot_nt(a, b):
    return lax.dot_general(a, b, (((1,), (1,)), ((), ())), preferred_element_type=F32)


def _dot_tn(a, b):
    return lax.dot_general(a, b, (((0,), (0,)), ((), ())), preferred_element_type=F32)


def _rms(x, g):
    return x * lax.rsqrt(jnp.mean(x * x, axis=-1, keepdims=True) + NORM_EPS) * g


def _in_proj_kernel(x_ref, g_ref, w_ref, o_ref, hn_ref):
    @pl.when(pl.program_id(1) == 0)
    def _():
        hn_ref[...] = _rms(x_ref[...], g_ref[...]).astype(BF16)

    o_ref[...] = _dot(hn_ref[...], w_ref[...]).astype(o_ref.dtype)


def in_proj(x2d, g, w, *, tm=512, tn=1024):
    t = x2d.shape[0]
    nw = w.shape[1]
    return pl.pallas_call(
        _in_proj_kernel,
        grid=(t // tm, nw // tn),
        in_specs=[
            pl.BlockSpec((tm, D_MODEL), lambda i, j: (i, 0)),
            pl.BlockSpec((1, D_MODEL), lambda i, j: (0, 0)),
            pl.BlockSpec((D_MODEL, tn), lambda i, j: (0, j)),
        ],
        out_specs=pl.BlockSpec((tm, tn), lambda i, j: (i, j)),
        out_shape=jax.ShapeDtypeStruct((t, nw), BF16),
        scratch_shapes=[pltpu.VMEM((tm, D_MODEL), BF16)],
        compiler_params=_cparams(("parallel", "arbitrary")),
        name="in_proj",
    )(x2d, g, w)


def _flash_step(k, vt, qt_ref, bias, m_ref, l_ref, acc_ref, *, cw):
    cols = qt_ref.shape[1]
    for c in range(cols // cw):
        cs = slice(c * cw, (c + 1) * cw)
        s = _dot(k, qt_ref[:, cs])
        if bias is not None:
            bw = bias.shape[1]
            if cw >= bw:
                s = s + jnp.concatenate([bias] * (cw // bw), axis=1)
            else:
                b0 = (c * cw) % bw
                s = s + bias[:, b0:b0 + cw]
        m_prev = m_ref[:, cs]
        m_next = jnp.maximum(m_prev, jnp.max(s, axis=0, keepdims=True))
        p = jnp.exp2(s - m_next)
        alpha = jnp.exp2(m_prev - m_next)
        l_ref[:, cs] = alpha * l_ref[:, cs] + jnp.sum(p, axis=0, keepdims=True)
        m_ref[:, cs] = m_next
        acc_ref[:, cs] = alpha * acc_ref[:, cs] + _dot(vt, p.astype(BF16))


def _dattn_kernel(lam_ref, q_ref, k_ref, v_ref, b_ref, g_ref, o_ref,
                  qt_ref, m_ref, l_ref, acc_ref, *, tq, nk, cw, out_scale):
    kt = pl.program_id(3)

    @pl.when(kt == 0)
    def _():
        q = q_ref[0].astype(F32) * (LOG2E * DA_HEAD_DIM ** -0.5)
        lane = lax.broadcasted_iota(jnp.int32, q.shape, 1)
        qt_ref[:, 0:tq] = jnp.where(lane < DA_HEAD_DIM, q, 0.0).T.astype(BF16)
        qt_ref[:, tq:2 * tq] = jnp.where(lane >= DA_HEAD_DIM, q, 0.0).T.astype(BF16)
        m_ref[...] = jnp.full(m_ref.shape, NEG_INF, F32)
        l_ref[...] = jnp.zeros(l_ref.shape, F32)
        acc_ref[...] = jnp.zeros(acc_ref.shape, F32)

    vt = v_ref[0].astype(F32).T.astype(BF16)
    _flash_step(k_ref[0], vt, qt_ref, b_ref[0, 0], m_ref, l_ref, acc_ref, cw=cw)

    @pl.when(kt == nk - 1)
    def _():
        o = acc_ref[...] / l_ref[...]
        o = (o[:, 0:tq] - lam_ref[0] * o[:, tq:2 * tq]).T
        o_ref[0] = (_rms(o, g_ref[...]) * out_scale).astype(o_ref.dtype)


def diff_attn(proj, bias_band, lam, sub_g, *, lam_init, tq, tk, cw):
    b, n, _ = proj.shape
    r = tk // tq
    kern = functools.partial(_dattn_kernel, tq=tq, nk=n // tk, cw=cw, out_scale=1.0 - lam_init)

    def bias_map(bi, h, qi, ki):
        return (jnp.clip(r * ki - qi, -r - 1, 2) + r + 1, h, 0, 0)

    return pl.pallas_call(
        kern,
        grid=(b, DA_HEADS, n // tq, n // tk),
        in_specs=[
            pl.BlockSpec(memory_space=pltpu.SMEM),
            pl.BlockSpec((1, tq, 128), lambda bi, h, qi, ki: (bi, qi, COL_AQ // 128 + h)),
            pl.BlockSpec((1, tk, 128), lambda bi, h, qi, ki: (bi, ki, COL_AK // 128 + h)),
            pl.BlockSpec((1, tk, 128), lambda bi, h, qi, ki: (bi, ki, COL_AV // 128 + h)),
            pl.BlockSpec((1, 1, tk, tq), bias_map),
            pl.BlockSpec((1, DA_V_DIM), lambda bi, h, qi, ki: (0, 0)),
        ],
        out_specs=pl.BlockSpec((1, tq, DA_V_DIM), lambda bi, h, qi, ki: (bi, qi, h)),
        out_shape=jax.ShapeDtypeStruct((b, n, DA_HEADS * DA_V_DIM), BF16),
        scratch_shapes=[
            pltpu.VMEM((128, 2 * tq), BF16),
            pltpu.VMEM((1, 2 * tq), F32),
            pltpu.VMEM((1, 2 * tq), F32),
            pltpu.VMEM((DA_V_DIM, 2 * tq), F32),
        ],
        compiler_params=_cparams(("parallel", "parallel", "parallel", "arbitrary")),
        name="diff_attn",
    )(lam, proj, proj, proj, bias_band, sub_g)


def _swap16(x):
    w = x.shape[1]
    lane = lax.broadcasted_iota(jnp.int32, x.shape, 1)
    return jnp.where(lane % 32 < 16, pltpu.roll(x, w - 16, 1), pltpu.roll(x, 16, 1))


def _seg_rms(x, seg_ones, g):
    sq = x * x
    hi = sq.astype(BF16)
    lo = (sq - hi.astype(F32)).astype(BF16)
    ss = _dot(hi, seg_ones) + _dot(lo, seg_ones)
    return x * lax.rsqrt(ss * (1.0 / GQ_HEAD_DIM) + NORM_EPS) * g


def _gqa_prep_kernel(q_ref, k_ref, v_ref, cos_ref, sin_ref, gq_ref, gk_ref, ones_ref,
                     qo_ref, ko_ref, vo_ref):
    cos = cos_ref[...]
    sin = sin_ref[...]
    cos4 = jnp.concatenate([cos] * 4, axis=1)
    sin4 = jnp.concatenate([sin] * 4, axis=1)
    ones = ones_ref[...]
    q = _seg_rms(q_ref[0].astype(F32), ones, gq_ref[...])
    k = _seg_rms(k_ref[0].astype(F32), ones[0:128, 0:128], gk_ref[...])
    q = (q * cos4 + _swap16(q) * sin4) * (LOG2E * GQ_HEAD_DIM ** -0.5)
    qo_ref[0] = q.T.astype(BF16)
    ko_ref[0] = (k * cos + _swap16(k) * sin).astype(BF16)
    vo_ref[0] = v_ref[0].astype(F32).T.astype(BF16)


def gqa_prep(proj, cos2, sin2, gq, gk, seg_ones, *, tm=512):
    b, n, _ = proj.shape
    return pl.pallas_call(
        _gqa_prep_kernel,
        grid=(b, n // tm),
        in_specs=[
            pl.BlockSpec((1, tm, 512), lambda bi, i: (bi, i, COL_CQ // 512)),
            pl.BlockSpec((1, tm, 128), lambda bi, i: (bi, i, COL_CK // 128)),
            pl.BlockSpec((1, tm, 128), lambda bi, i: (bi, i, COL_CV // 128)),
            pl.BlockSpec((tm, 128), lambda bi, i: (i, 0)),
            pl.BlockSpec((tm, 128), lambda bi, i: (i, 0)),
            pl.BlockSpec((1, 512), lambda bi, i: (0, 0)),
            pl.BlockSpec((1, 128), lambda bi, i: (0, 0)),
            pl.BlockSpec((512, 512), lambda bi, i: (0, 0)),
        ],
        out_specs=[
            pl.BlockSpec((1, GQ_HEADS * GQ_HEAD_DIM, tm), lambda bi, i: (bi, 0, i)),
            pl.BlockSpec((1, tm, GQ_KV_HEADS * GQ_HEAD_DIM), lambda bi, i: (bi, i, 0)),
            pl.BlockSpec((1, GQ_KV_HEADS * GQ_HEAD_DIM, tm), lambda bi, i: (bi, 0, i)),
        ],
        out_shape=[
            jax.ShapeDtypeStruct((b, GQ_HEADS * GQ_HEAD_DIM, n), BF16),
            jax.ShapeDtypeStruct((b, n, GQ_KV_HEADS * GQ_HEAD_DIM), BF16),
            jax.ShapeDtypeStruct((b, GQ_KV_HEADS * GQ_HEAD_DIM, n), BF16),
        ],
        compiler_params=_cparams(("parallel", "parallel")),
        name="gqa_prep",
    )(proj, proj, proj, cos2, sin2, gq, gk, seg_ones)


def _gqa_kernel(q_ref, k_ref, vt_ref, o_ref, qt_ref, m_ref, l_ref, acc_ref, *, tq, nk, cw):
    kt = pl.program_id(3)
    hd = GQ_HEAD_DIM

    @pl.when(kt == 0)
    def _():
        qt_ref[...] = jnp.zeros(qt_ref.shape, BF16)
        base = pl.multiple_of(pl.program_id(1) * hd, hd)
        for r in range(GQ_GROUP):
            qt_ref[pl.ds(base, hd), r * tq:(r + 1) * tq] = q_ref[0, r * hd:(r + 1) * hd, :]
        m_ref[...] = jnp.full(m_ref.shape, NEG_INF, F32)
        l_ref[...] = jnp.zeros(l_ref.shape, F32)
        acc_ref[...] = jnp.zeros(acc_ref.shape, F32)

    _flash_step(k_ref[0], vt_ref[0], qt_ref, None, m_ref, l_ref, acc_ref, cw=cw)

    @pl.when(kt == nk - 1)
    def _():
        o = (acc_ref[...] / l_ref[...]).astype(o_ref.dtype)
        for r in range(GQ_GROUP):
            o_ref[0, r * hd:(r + 1) * hd, :] = o[:, r * tq:(r + 1) * tq]


def gqa_attn(qt, k, vt, *, tq, tk, cw):
    b, _, n = qt.shape
    gw = GQ_GROUP * GQ_HEAD_DIM
    kvw = GQ_KV_HEADS * GQ_HEAD_DIM
    kern = functools.partial(_gqa_kernel, tq=tq, nk=n // tk, cw=cw)
    return pl.pallas_call(
        kern,
        grid=(b, GQ_KV_HEADS, n // tq, n // tk),
        in_specs=[
            pl.BlockSpec((1, gw, tq), lambda bi, g, qi, ki: (bi, g, qi)),
            pl.BlockSpec((1, tk, kvw), lambda bi, g, qi, ki: (bi, ki, 0)),
            pl.BlockSpec((1, GQ_HEAD_DIM, tk), lambda bi, g, qi, ki: (bi, g, ki)),
        ],
        out_specs=pl.BlockSpec((1, gw, tq), lambda bi, g, qi, ki: (bi, g, qi)),
        out_shape=jax.ShapeDtypeStruct((b, GQ_HEADS * GQ_HEAD_DIM, n), BF16),
        scratch_shapes=[
            pltpu.VMEM((kvw, GQ_GROUP * tq), BF16),
            pltpu.VMEM((1, GQ_GROUP * tq), F32),
            pltpu.VMEM((1, GQ_GROUP * tq), F32),
            pltpu.VMEM((GQ_HEAD_DIM, GQ_GROUP * tq), F32),
        ],
        compiler_params=_cparams(("parallel", "parallel", "parallel", "arbitrary")),
        name="gqa_attn",
    )(qt, k, vt)


def _gla_kernel(q_ref, k_ref, v_ref, z_ref, gw_ref, gb_ref, o_ref, st_ref, *, reverse, nchunk):
    L = GLA_CHUNK
    hd = GLA_HEADS * GLA_DK

    @pl.when(pl.program_id(1) == 0)
    def _():
        st_ref[...] = jnp.zeros(st_ref.shape, F32)

    row = lax.broadcasted_iota(jnp.int32, (L, L), 0)
    col = lax.broadcasted_iota(jnp.int32, (L, L), 1)
    if reverse:
        tri = (col >= row).astype(BF16)
        keep = col > row
    else:
        tri = (col <= row).astype(BF16)
        keep = col <= row
    lane = lax.broadcasted_iota(jnp.int32, (L, hd), 1)

    order = range(nchunk - 1, -1, -1) if reverse else range(nchunk)
    for c in order:
        rows = pl.ds(c * L, L)
        q = q_ref[0, rows, :].astype(F32) * (GLA_DK ** -0.5)
        k = k_ref[0, rows, :].astype(F32)
        logit = _dot(z_ref[0, rows, :], gw_ref[...]) + gb_ref[...]
        lg = jax.nn.log_sigmoid(logit) * (1.0 / GLA_TAU)
        hi = lg.astype(BF16)
        lo = (lg - hi.astype(F32)).astype(BF16)
        cum = _dot(tri, hi) + _dot(tri, lo)
        last = cum[0:1, :] if reverse else cum[L - 1:L, :]
        q_dec = q * jnp.exp(cum)
        k_dec = (k * jnp.exp(-cum)).astype(BF16)
        k_last = (k * jnp.exp(last - cum)).astype(BF16)
        decay = jnp.exp(last)
        for h in range(GLA_HEADS):
            head = (lane >= h * GLA_DK) & (lane < (h + 1) * GLA_DK)
            qh = jnp.where(head, q_dec, 0.0).astype(BF16)
            vh = v_ref[0, rows, h * GLA_DV:(h + 1) * GLA_DV]
            a = jnp.where(keep, _dot_nt(qh, k_dec), 0.0)
            st = st_ref[h]
            o = _dot(a.astype(BF16), vh) + _dot_nt(qh, st.astype(BF16))
            o_ref[0, rows, h * GLA_DV:(h + 1) * GLA_DV] = o
            st_ref[h] = st * decay + _dot_tn(vh, k_last)


def gla(proj, gw, gb, *, reverse, tm=256):
    b, n, _ = proj.shape
    nb = n // tm
    kern = functools.partial(_gla_kernel, reverse=reverse, nchunk=tm // GLA_CHUNK)
    blk = (lambda i: nb - 1 - i) if reverse else (lambda i: i)
    hd = GLA_HEADS * GLA_DK
    return pl.pallas_call(
        kern,
        grid=(b, nb),
        in_specs=[
            pl.BlockSpec((1, tm, hd), lambda bi, i: (bi, blk(i), COL_BQ // hd)),
            pl.BlockSpec((1, tm, hd), lambda bi, i: (bi, blk(i), COL_BK // hd)),
            pl.BlockSpec((1, tm, 512), lambda bi, i: (bi, blk(i), COL_BV // 512)),
            pl.BlockSpec((1, tm, 128), lambda bi, i: (bi, blk(i), COL_BZ // 128)),
            pl.BlockSpec((128, hd), lambda bi, i: (0, 0)),
            pl.BlockSpec((1, hd), lambda bi, i: (0, 0)),
        ],
        out_specs=pl.BlockSpec((1, tm, GLA_HEADS * GLA_DV), lambda bi, i: (bi, blk(i), 0)),
        out_shape=jax.ShapeDtypeStruct((b, n, GLA_HEADS * GLA_DV), F32),
        scratch_shapes=[pltpu.VMEM((GLA_HEADS, GLA_DV, hd), F32)],
        compiler_params=_cparams(("parallel", "arbitrary")),
        name="gla_bwd" if reverse else "gla_fwd",
    )(proj, proj, proj, proj, gw, gb)


def _merge_kernel(x_ref, ya_ref, of_ref, ob_ref, r_ref, yc_ref, gz_ref, gn_ref, wb_ref, wo_ref,
                  o_ref):
    gn = gn_ref[...]
    o = of_ref[0] + ob_ref[0]
    r = r_ref[0].astype(F32)
    yb = []
    for h in range(GLA_HEADS):
        sl = slice(h * GLA_DV, (h + 1) * GLA_DV)
        yb.append(_rms(o[:, sl], gn) * jax.nn.silu(r[:, sl]))
    yb = jnp.concatenate(yb, axis=1).astype(BF16)

    pa = _dot(ya_ref[0], wb_ref[0])
    pb = _dot(yb, wb_ref[1])
    pc = _dot_tn(yc_ref[0], wb_ref[2])

    gz = gz_ref[0].astype(F32)
    merged = (jax.nn.sigmoid(gz[:, 0:D_MODEL]) * pa
              + jax.nn.sigmoid(gz[:, D_MODEL:2 * D_MODEL]) * pb
              + jax.nn.sigmoid(gz[:, 2 * D_MODEL:3 * D_MODEL]) * pc)
    o_ref[0] = x_ref[0] + _dot(merged.astype(BF16), wo_ref[...])


def merge(x, ya, of, ob, proj, yc, gn, wb, wo, *, tm=256):
    b, n, _ = x.shape
    tok = lambda bi, i: (bi, i, 0)
    return pl.pallas_call(
        _merge_kernel,
        grid=(b, n // tm),
        in_specs=[
            pl.BlockSpec((1, tm, D_MODEL), tok),
            pl.BlockSpec((1, tm, BRANCH_W), tok),
            pl.BlockSpec((1, tm, BRANCH_W), tok),
            pl.BlockSpec((1, tm, BRANCH_W), tok),
            pl.BlockSpec((1, tm, 512), lambda bi, i: (bi, i, COL_BR // 512)),
            pl.BlockSpec((1, BRANCH_W, tm), lambda bi, i: (bi, 0, i)),
            pl.BlockSpec((1, tm, N_BRANCH * D_MODEL), lambda bi, i: (bi, i, COL_GZ // (N_BRANCH * D_MODEL))),
            pl.BlockSpec((1, GLA_DV), lambda bi, i: (0, 0)),
            pl.BlockSpec((N_BRANCH, BRANCH_W, D_MODEL), lambda bi, i: (0, 0, 0)),
            pl.BlockSpec((D_MODEL, D_MODEL), lambda bi, i: (0, 0)),
        ],
        out_specs=pl.BlockSpec((1, tm, D_MODEL), tok),
        out_shape=jax.ShapeDtypeStruct((b, n, D_MODEL), F32),
        compiler_params=_cparams(("parallel", "parallel")),
        name="merge",
    )(x, ya, of, ob, proj, yc, proj, gn, wb, wo)


TOPX = PEER_TOPK + 1
TOPX_PAD = 24


def _top_values(vals, count):
    out = []
    for _ in range(count):
        m = jnp.max(vals, axis=0, keepdims=True)
        out.append(m)
        vals = jnp.where(vals == m, NEG_INF, vals)
    return out


def _route_kernel(x_ref, g_ref, wq_ref, sk_ref, hn_ref, s1_ref, e1_ref, th_ref, e0_ref):
    tb = x_ref.shape[0]
    hn = _rms(x_ref[...], g_ref[...])
    hn_t = hn.T.astype(BF16)
    hn_ref[...] = hn_t
    q_t = _dot(wq_ref[...], hn_t).astype(BF16)
    pad = jnp.full((TOPX_PAD - TOPX, tb), NEG_INF, F32)
    for h in range(PEER_HEADS):
        r0 = (2 * h) * PEER_HALF
        s0 = _dot(sk_ref[h, 0], q_t[r0:r0 + PEER_HALF])
        s1 = _dot(sk_ref[h, 1], q_t[r0 + PEER_HALF:r0 + 2 * PEER_HALF])
        top0 = _top_values(s0, TOPX)
        top1 = jnp.concatenate(_top_values(s1, TOPX) + [pad], axis=0)
        cand = jnp.concatenate([t0 + top1 for t0 in top0], axis=0)
        best = _top_values(cand, TOPX)
        thresh = 0.5 * (best[PEER_TOPK - 1] + best[PEER_TOPK])
        m0, m1 = top0[0], top1[0:1]
        z = jnp.zeros_like(m0)
        for c in best[:PEER_TOPK]:
            z = z + jnp.exp(c - best[0])
        e1 = jnp.exp(s1 - m1) / z
        th = thresh - s0
        e0 = jnp.exp(s0 - m0)
        for c in range(tb // 128):
            cs = slice(c * 128, (c + 1) * 128)
            s1_ref[h, c] = s1[:, cs]
            e1_ref[h, c] = e1[:, cs]
            th_ref[h, c] = th[:, cs]
            e0_ref[h, c] = e0[:, cs]


def peer_route(x2d, g, wq_t, subkeys, *, tb=256):
    t = x2d.shape[0]
    hk = jax.ShapeDtypeStruct((PEER_HEADS, t // 128, PEER_KEYS, 128), F32)
    hk_spec = pl.BlockSpec((PEER_HEADS, tb // 128, PEER_KEYS, 128), lambda i: (0, i, 0, 0))
    return pl.pallas_call(
        _route_kernel,
        grid=(t // tb,),
        in_specs=[
            pl.BlockSpec((tb, D_MODEL), lambda i: (i, 0)),
            pl.BlockSpec((1, D_MODEL), lambda i: (0, 0)),
            pl.BlockSpec((PEER_HEADS * PEER_QDIM, D_MODEL), lambda i: (0, 0)),
            pl.BlockSpec((PEER_HEADS, 2, PEER_KEYS, PEER_HALF), lambda i: (0, 0, 0, 0)),
        ],
        out_specs=[pl.BlockSpec((D_MODEL, tb), lambda i: (0, i)), hk_spec, hk_spec, hk_spec, hk_spec],
        out_shape=[jax.ShapeDtypeStruct((D_MODEL, t), BF16), hk, hk, hk, hk],
        compiler_params=_cparams(("parallel",)),
        name="peer_route",
    )(x2d, g, wq_t, subkeys)


DENSE_ROWS = 32


def _dense_kernel(x_ref, hn_ref, u_ref, vt_ref, s1_ref, e1_ref, th_ref, e0_ref, fg_ref, o_ref,
                  wg_ref, acc_ref, *, ne, ib, final_norm):
    e = pl.program_id(1)
    tb = hn_ref.shape[1]

    @pl.when(e == 0)
    def _():
        acc_ref[...] = jnp.zeros(acc_ref.shape, F32)

    a = _dot(u_ref[...], hn_ref[...])
    gelu_c = np.float32(np.sqrt(0.5))
    for il in range(ib):
        for tc in range(tb // 128):
            cols = slice(tc * 128, (tc + 1) * 128)
            for jc in range(PEER_KEYS // DENSE_ROWS):
                keys = slice(jc * DENSE_ROWS, (jc + 1) * DENSE_ROWS)
                rows = slice(il * PEER_KEYS + jc * DENSE_ROWS, il * PEER_KEYS + (jc + 1) * DENSE_ROWS)
                w = jnp.zeros((DENSE_ROWS, 128), F32)
                for h in range(PEER_HEADS):
                    sel = s1_ref[h, tc, keys, :] >= th_ref[h, tc, il:il + 1, :]
                    w = w + jnp.where(sel, e1_ref[h, tc, keys, :], 0.0) * e0_ref[h, tc, il:il + 1, :]
                ac = a[rows, cols]
                act = 0.5 * ac * (1.0 + lax.erf(ac * gelu_c))
                wg_ref[rows, cols] = (w * act).astype(BF16)
    acc_ref[...] += _dot(vt_ref[...], wg_ref[...])

    @pl.when(e == ne - 1)
    def _():
        y = x_ref[...] + acc_ref[...].T
        if final_norm:
            y = _rms(y, fg_ref[...])
        o_ref[...] = y


def peer_dense(x2d, hn_t, u, v_t, s1, e1, th, e0, final_g, *, final_norm, tb=512, ib=8):
    t = x2d.shape[0]
    eb = ib * PEER_KEYS
    ne = PEER_EXPERTS // eb
    kern = functools.partial(_dense_kernel, ne=ne, ib=ib, final_norm=final_norm)
    full = pl.BlockSpec((PEER_HEADS, tb // 128, PEER_KEYS, 128), lambda i, e: (0, i, 0, 0))
    part = pl.BlockSpec((PEER_HEADS, tb // 128, ib, 128), lambda i, e: (0, i, e, 0))
    return pl.pallas_call(
        kern,
        grid=(t // tb, ne),
        in_specs=[
            pl.BlockSpec((tb, D_MODEL), lambda i, e: (i, 0)),
            pl.BlockSpec((D_MODEL, tb), lambda i, e: (0, i)),
            pl.BlockSpec((eb, D_MODEL), lambda i, e: (e, 0)),
            pl.BlockSpec((D_MODEL, eb), lambda i, e: (0, e)),
            full, full, part, part,
            pl.BlockSpec((1, D_MODEL), lambda i, e: (0, 0)),
        ],
        out_specs=pl.BlockSpec((tb, D_MODEL), lambda i, e: (i, 0)),
        out_shape=jax.ShapeDtypeStruct((t, D_MODEL), F32),
        scratch_shapes=[pltpu.VMEM((eb, tb), BF16), pltpu.VMEM((D_MODEL, tb), F32)],
        compiler_params=_cparams(("parallel", "arbitrary")),
        name="peer_dense",
    )(x2d, hn_t, u, v_t, s1, e1, th, e0, final_g)


def _rel_bucket(rel):
    nb = REL_BUCKETS // 2
    max_exact = nb // 2
    n = jnp.abs(rel)
    nf = jnp.maximum(n, 1).astype(jnp.float32)
    large = max_exact + (jnp.log(nf / max_exact) / math.log(REL_MAX_DIST / max_exact)
                         * (nb - max_exact)).astype(jnp.int32)
    large = jnp.minimum(large, nb - 1)
    return jnp.where(rel > 0, nb, 0) + jnp.where(n < max_exact, n, large)


def _bias_band(rel_table, tq, tk):
    r = tk // tq
    qpos = jnp.arange(tq)
    tiles = []
    for d in range(-r - 1, 3):
        rel = (d * tq + jnp.arange(tk))[:, None] - qpos[None, :]
        onehot = jax.nn.one_hot(_rel_bucket(rel), REL_BUCKETS, dtype=F32)
        tiles.append(jnp.einsum("kqb,bh->hkq", onehot, rel_table.astype(F32),
                                precision=lax.Precision.HIGHEST))
    return jnp.stack(tiles) * LOG2E


def _rope_tables(n):
    rows = n // GRID_W
    row_id = jnp.repeat(jnp.arange(rows), GRID_W).astype(jnp.float32)
    col_id = (jnp.arange(rows * GRID_W) % GRID_W).astype(jnp.float32)
    sec = GQ_HEAD_DIM // 2
    inv = ROPE_THETA ** (-jnp.arange(0, sec, 2, dtype=jnp.float32) / sec)
    cr, sr = jnp.cos(row_id[:, None] * inv), jnp.sin(row_id[:, None] * inv)
    cc, sc = jnp.cos(col_id[:, None] * inv), jnp.sin(col_id[:, None] * inv)
    cos = jnp.concatenate([cr, cr, cc, cc], axis=1)
    sin = jnp.concatenate([-sr, sr, -sc, sc], axis=1)
    return jnp.concatenate([cos, cos], axis=1), jnp.concatenate([sin, sin], axis=1)


def _prep_weights(w_in, gla_gate_w, gla_gate_b, gq_qk_g, w_branch, w_out, peer_wq, peer_subkeys,
                  peer_u, peer_v):
    wi = jnp.concatenate(
        [w_in[:, :, ORIG_GZ:], w_in[:, :, :ORIG_BZ], w_in[:, :, ORIG_CQ:ORIG_GZ],
         w_in[:, :, ORIG_BZ:ORIG_CQ],
         jnp.zeros((DEPTH, D_MODEL, PROJ_W - ORIG_W), w_in.dtype)], axis=2).astype(BF16)
    hd = GLA_HEADS * GLA_DK
    gw = jnp.zeros((DEPTH, 2, 128, hd), F32)
    gw = gw.at[:, 0, 0:GLA_GATE_RANK].set(gla_gate_w[:, 0])
    gw = gw.at[:, 1, GLA_GATE_RANK:2 * GLA_GATE_RANK].set(gla_gate_w[:, 1])
    return dict(
        w_in=wi, gw=gw.astype(BF16), gb=gla_gate_b.astype(F32)[:, :, None, :],
        gq=jnp.tile(gq_qk_g[:, 0], (1, GQ_HEADS))[:, None, :],
        gk=jnp.tile(gq_qk_g[:, 1], (1, GQ_KV_HEADS))[:, None, :],
        wb=w_branch.astype(BF16), wo=w_out.astype(BF16),
        wq_t=jnp.swapaxes(peer_wq, 1, 2).astype(BF16), sk=peer_subkeys.astype(BF16),
        u=peer_u.astype(BF16), v_t=jnp.swapaxes(peer_v, 1, 2).astype(BF16))


def _encoder(x, w, rel_bias, norm1_g, da_lambda, da_subln_g, gla_norm_g, norm2_g, final_g, seg_ones):
    b, n, _ = x.shape
    t = b * n
    band = _bias_band(rel_bias, DA_TQ, DA_TK)
    cos2, sin2 = _rope_tables(n)
    for l in range(DEPTH):
        proj = in_proj(x.reshape(t, D_MODEL), norm1_g[l][None, :], w["w_in"][l]).reshape(b, n, PROJ_W)
        lam_init = 0.8 - 0.6 * math.exp(-0.3 * l)
        lp = da_lambda[l].astype(F32)
        lam = (jnp.exp(jnp.sum(lp[0] * lp[1])) - jnp.exp(jnp.sum(lp[2] * lp[3])) + lam_init).reshape(1)
        ya = diff_attn(proj, band, lam, da_subln_g[l][None, :], lam_init=lam_init,
                       tq=DA_TQ, tk=DA_TK, cw=DA_CW)
        qt, kr, vt = gqa_prep(proj, cos2, sin2, w["gq"][l], w["gk"][l], seg_ones)
        yc = gqa_attn(qt, kr, vt, tq=GQ_TQ, tk=GQ_TK, cw=GQ_CW)
        of = gla(proj, w["gw"][l, 0], w["gb"][l, 0], reverse=False)
        ob = gla(proj, w["gw"][l, 1], w["gb"][l, 1], reverse=True)
        x1 = merge(x, ya, of, ob, proj, yc, gla_norm_g[l][None, :], w["wb"][l], w["wo"][l])
        x1 = x1.reshape(t, D_MODEL)
        hn_t, s1, e1, th, e0 = peer_route(x1, norm2_g[l][None, :], w["wq_t"][l], w["sk"][l])
        x = peer_dense(x1, hn_t, w["u"][l], w["v_t"][l], s1, e1, th, e0, final_g[None, :],
                       final_norm=(l == DEPTH - 1)).reshape(b, n, D_MODEL)
    return x


def kernel(x_prompt, x_sample, rel_bias, norm1_g, w_in, da_lambda, da_subln_g, gla_gate_w, gla_gate_b,
           gla_norm_g, gq_qk_g, w_branch, w_out, norm2_g, peer_wq, peer_subkeys, peer_u, peer_v, final_g):
    w = _prep_weights(w_in, gla_gate_w, gla_gate_b, gq_qk_g, w_branch, w_out, peer_wq, peer_subkeys,
                      peer_u, peer_v)
    seg = np.arange(512) // GQ_HEAD_DIM
    seg_ones = jnp.asarray(seg[:, None] == seg[None, :], dtype=BF16)
    args = (w, rel_bias, norm1_g, da_lambda, da_subln_g, gla_norm_g, norm2_g, final_g, seg_ones)
    return (_encoder(x_prompt, *args), _encoder(x_sample, *args))
```

```python
import functools
import math

import numpy as np
import jax
import jax.numpy as jnp
from jax import lax
from jax.experimental import pallas as pl
from jax.experimental.pallas import tpu as pltpu

D_MODEL = 1024
DEPTH = 2
GRID_W = 64
NORM_EPS = 1e-6
N_BRANCH = 3
BRANCH_W = 512

DA_HEADS = 4
DA_HEAD_DIM = 64
DA_V_DIM = 128
REL_BUCKETS = 32
REL_MAX_DIST = 128

GLA_HEADS = 4
GLA_DK = 64
GLA_DV = 128
GLA_GATE_RANK = 16
GLA_TAU = 16.0
GLA_CHUNK = 64

GQ_HEADS = 8
GQ_KV_HEADS = 2
GQ_GROUP = GQ_HEADS // GQ_KV_HEADS
GQ_HEAD_DIM = 64
ROPE_THETA = 10000.0

PEER_HEADS = 8
PEER_KEYS = 128
PEER_EXPERTS = PEER_KEYS * PEER_KEYS
PEER_QDIM = 256
PEER_HALF = 128
PEER_TOPK = 16

COL_GZ = 0
COL_AQ, COL_AK, COL_AV = 3072, 3584, 4096
COL_BQ, COL_BK, COL_BV, COL_BR = 4608, 4864, 5120, 5632
COL_CQ, COL_CK, COL_CV = 6144, 6656, 6784
COL_BZ = 6912
PROJ_W = 7168
ORIG_BZ = 3072
ORIG_CQ = ORIG_BZ + 2 * GLA_GATE_RANK
ORIG_GZ = ORIG_CQ + 768
ORIG_W = ORIG_GZ + N_BRANCH * D_MODEL

DA_TQ, DA_TK = 512, 512
GQ_TQ, GQ_TK = 256, 512

VMEM_LIMIT = 56 * 1024 * 1024

F32 = jnp.float32
BF16 = jnp.bfloat16
NEG_INF = float("-inf")
LOG2E = math.log2(math.e)


def _cparams(sem):
    return pltpu.CompilerParams(dimension_semantics=sem, vmem_limit_bytes=VMEM_LIMIT)


def _dot(a, b):
    return jnp.dot(a, b, preferred_element_type=F32)


def _dot_nt(a, b):
    return lax.dot_general(a, b, (((1,), (1,)), ((), ())), preferred_element_type=F32)


def _dot_tn(a, b):
    return lax.dot_general(a, b, (((0,), (0,)), ((), ())), preferred_element_type=F32)


def _rms(x, g):
    return x * lax.rsqrt(jnp.mean(x * x, axis=-1, keepdims=True) + NORM_EPS) * g


def _in_proj_kernel(x_ref, g_ref, w_ref, o_ref, hn_ref):
    @pl.when(pl.program_id(1) == 0)
    def _():
        hn_ref[...] = _rms(x_ref[...], g_ref[...]).astype(BF16)

    o_ref[...] = _dot(hn_ref[...], w_ref[...]).astype(o_ref.dtype)


def in_proj(x2d, g, w, *, tm=512, tn=1024):
    t = x2d.shape[0]
    nw = w.shape[1]
    return pl.pallas_call(
        _in_proj_kernel,
        grid=(t // tm, nw // tn),
        in_specs=[
            pl.BlockSpec((tm, D_MODEL), lambda i, j: (i, 0)),
            pl.BlockSpec((1, D_MODEL), lambda i, j: (0, 0)),
            pl.BlockSpec((D_MODEL, tn), lambda i, j: (0, j)),
        ],
        out_specs=pl.BlockSpec((tm, tn), lambda i, j: (i, j)),
        out_shape=jax.ShapeDtypeStruct((t, nw), BF16),
        scratch_shapes=[pltpu.VMEM((tm, D_MODEL), BF16)],
        compiler_params=_cparams(("parallel", "arbitrary")),
        name="in_proj",
    )(x2d, g, w)


def _flash_pipe_step(k, vt, bias, qt_ref, s_cur, s_prev, p_cur, p_prev, al_cur, al_prev,
                     m_ref, l_ref, acc_ref):
    s_cur[...] = _dot(k, qt_ref[...])
    acc_ref[...] = al_cur[...] * acc_ref[...] + _dot(vt, p_cur[...])
    s = s_prev[...]
    if bias is not None:
        s = s + jnp.concatenate([bias] * (s.shape[1] // bias.shape[1]), axis=1)
    m_prev = m_ref[...]
    m_next = jnp.maximum(m_prev, jnp.max(s, axis=0, keepdims=True))
    p = jnp.exp2(s - m_next)
    alpha = jnp.exp2(m_prev - m_next)
    l_ref[...] = alpha * l_ref[...] + jnp.sum(p, axis=0, keepdims=True)
    m_ref[...] = m_next
    al_prev[...] = alpha
    p_prev[...] = p.astype(BF16)


def _flash_pipe(kt, nk, build_qt, k_ref, get_vt, bias_ref, qt_ref, s_refs, p_refs, al_refs,
                m_ref, l_ref, lfin_ref, acc_ref):
    @pl.when(kt == 0)
    def _():
        build_qt()
        for r in (s_refs[1], p_refs[0], al_refs[0], acc_ref):
            r[...] = jnp.zeros(r.shape, r.dtype)

    @pl.when(kt <= 1)
    def _():
        m_ref[...] = jnp.full(m_ref.shape, NEG_INF, F32)
        l_ref[...] = jnp.zeros(l_ref.shape, F32)

    @pl.when(kt == 2)
    def _():
        acc_ref[...] = jnp.zeros(acc_ref.shape, F32)

    @pl.when(kt == nk + 1)
    def _():
        lfin_ref[...] = l_ref[...]

    for cur in (0, 1):
        @pl.when(kt % 2 == cur)
        def _(cur=cur):
            prev = 1 - cur
            bias = None if bias_ref is None else bias_ref[0, 0]
            _flash_pipe_step(k_ref[0], get_vt(), bias, qt_ref, s_refs[cur], s_refs[prev],
                             p_refs[cur], p_refs[prev], al_refs[cur], al_refs[prev], m_ref, l_ref, acc_ref)


def _flash_scratch(tk, cols, dv):
    return [pltpu.VMEM((tk, cols), F32), pltpu.VMEM((tk, cols), F32),
            pltpu.VMEM((tk, cols), BF16), pltpu.VMEM((tk, cols), BF16),
            pltpu.VMEM((1, cols), F32), pltpu.VMEM((1, cols), F32),
            pltpu.VMEM((1, cols), F32), pltpu.VMEM((1, cols), F32), pltpu.VMEM((1, cols), F32),
            pltpu.VMEM((dv, cols), F32)]


def _dattn_kernel(lam_ref, q_ref, k_ref, v_ref, b_ref, g_ref, o_ref, qt_ref,
                  s0_ref, s1_ref, p0_ref, p1_ref, al0_ref, al1_ref, m_ref, l_ref, lfin_ref, acc_ref,
                  *, tq, nk, out_scale):
    kt = pl.program_id(3)

    def build_qt():
        q = q_ref[0].astype(F32) * (LOG2E * DA_HEAD_DIM ** -0.5)
        lane = lax.broadcasted_iota(jnp.int32, q.shape, 1)
        qt_ref[:, 0:tq] = jnp.where(lane < DA_HEAD_DIM, q, 0.0).T.astype(BF16)
        qt_ref[:, tq:2 * tq] = jnp.where(lane >= DA_HEAD_DIM, q, 0.0).T.astype(BF16)

    _flash_pipe(kt, nk, build_qt, k_ref, lambda: v_ref[0].astype(F32).T.astype(BF16), b_ref, qt_ref,
                (s0_ref, s1_ref), (p0_ref, p1_ref), (al0_ref, al1_ref), m_ref, l_ref, lfin_ref, acc_ref)

    @pl.when(kt == nk + 1)
    def _():
        o = acc_ref[...] / lfin_ref[...]
        o = (o[:, 0:tq] - lam_ref[0] * o[:, tq:2 * tq]).T
        o_ref[0] = (_rms(o, g_ref[...]) * out_scale).astype(o_ref.dtype)


def diff_attn(proj, bias_band, lam, sub_g, *, lam_init, tq, tk):
    b, n, _ = proj.shape
    r = tk // tq
    nk = n // tk
    kern = functools.partial(_dattn_kernel, tq=tq, nk=nk, out_scale=1.0 - lam_init)
    tile = lambda ki, lag: jnp.clip(ki - lag, 0, nk - 1)

    def bias_map(bi, h, qi, ki):
        return (jnp.clip(r * tile(ki, 1) - qi, -r - 1, 2) + r + 1, h, 0, 0)

    return pl.pallas_call(
        kern,
        grid=(b, DA_HEADS, n // tq, nk + 2),
        in_specs=[
            pl.BlockSpec(memory_space=pltpu.SMEM),
            pl.BlockSpec((1, tq, 128), lambda bi, h, qi, ki: (bi, qi, COL_AQ // 128 + h)),
            pl.BlockSpec((1, tk, 128), lambda bi, h, qi, ki: (bi, tile(ki, 0), COL_AK // 128 + h)),
            pl.BlockSpec((1, tk, 128), lambda bi, h, qi, ki: (bi, tile(ki, 2), COL_AV // 128 + h)),
            pl.BlockSpec((1, 1, tk, tq), bias_map),
            pl.BlockSpec((1, DA_V_DIM), lambda bi, h, qi, ki: (0, 0)),
        ],
        out_specs=pl.BlockSpec((1, tq, DA_V_DIM), lambda bi, h, qi, ki: (bi, qi, h)),
        out_shape=jax.ShapeDtypeStruct((b, n, DA_HEADS * DA_V_DIM), BF16),
        scratch_shapes=[pltpu.VMEM((128, 2 * tq), BF16)] + _flash_scratch(tk, 2 * tq, DA_V_DIM),
        compiler_params=_cparams(("parallel", "parallel", "parallel", "arbitrary")),
        name="diff_attn",
    )(lam, proj, proj, proj, bias_band, sub_g)


def _swap16(x):
    w = x.shape[1]
    lane = lax.broadcasted_iota(jnp.int32, x.shape, 1)
    return jnp.where(lane % 32 < 16, pltpu.roll(x, w - 16, 1), pltpu.roll(x, 16, 1))


def _seg_rms(x, seg_ones, g):
    sq = x * x
    hi = sq.astype(BF16)
    lo = (sq - hi.astype(F32)).astype(BF16)
    ss = _dot(hi, seg_ones) + _dot(lo, seg_ones)
    return x * lax.rsqrt(ss * (1.0 / GQ_HEAD_DIM) + NORM_EPS) * g


def _gqa_prep_kernel(q_ref, k_ref, v_ref, cos_ref, sin_ref, gq_ref, gk_ref, ones_ref,
                     qo_ref, ko_ref, vo_ref):
    cos = cos_ref[...]
    sin = sin_ref[...]
    cos4 = jnp.concatenate([cos] * 4, axis=1)
    sin4 = jnp.concatenate([sin] * 4, axis=1)
    ones = ones_ref[...]
    q = _seg_rms(q_ref[0].astype(F32), ones, gq_ref[...])
    k = _seg_rms(k_ref[0].astype(F32), ones[0:128, 0:128], gk_ref[...])
    q = (q * cos4 + _swap16(q) * sin4) * (LOG2E * GQ_HEAD_DIM ** -0.5)
    qo_ref[0] = q.T.astype(BF16)
    ko_ref[0] = (k * cos + _swap16(k) * sin).astype(BF16)
    vo_ref[0] = v_ref[0].astype(F32).T.astype(BF16)


def gqa_prep(proj, cos2, sin2, gq, gk, seg_ones, *, tm=512):
    b, n, _ = proj.shape
    return pl.pallas_call(
        _gqa_prep_kernel,
        grid=(b, n // tm),
        in_specs=[
            pl.BlockSpec((1, tm, 512), lambda bi, i: (bi, i, COL_CQ // 512)),
            pl.BlockSpec((1, tm, 128), lambda bi, i: (bi, i, COL_CK // 128)),
            pl.BlockSpec((1, tm, 128), lambda bi, i: (bi, i, COL_CV // 128)),
            pl.BlockSpec((tm, 128), lambda bi, i: (i, 0)),
            pl.BlockSpec((tm, 128), lambda bi, i: (i, 0)),
            pl.BlockSpec((1, 512), lambda bi, i: (0, 0)),
            pl.BlockSpec((1, 128), lambda bi, i: (0, 0)),
            pl.BlockSpec((512, 512), lambda bi, i: (0, 0)),
        ],
        out_specs=[
            pl.BlockSpec((1, GQ_HEADS * GQ_HEAD_DIM, tm), lambda bi, i: (bi, 0, i)),
            pl.BlockSpec((1, tm, GQ_KV_HEADS * GQ_HEAD_DIM), lambda bi, i: (bi, i, 0)),
            pl.BlockSpec((1, GQ_KV_HEADS * GQ_HEAD_DIM, tm), lambda bi, i: (bi, 0, i)),
        ],
        out_shape=[
            jax.ShapeDtypeStruct((b, GQ_HEADS * GQ_HEAD_DIM, n), BF16),
            jax.ShapeDtypeStruct((b, n, GQ_KV_HEADS * GQ_HEAD_DIM), BF16),
            jax.ShapeDtypeStruct((b, GQ_KV_HEADS * GQ_HEAD_DIM, n), BF16),
        ],
        compiler_params=_cparams(("parallel", "parallel")),
        name="gqa_prep",
    )(proj, proj, proj, cos2, sin2, gq, gk, seg_ones)


def _gqa_kernel(q_ref, k_ref, vt_ref, o_ref, qt_ref, s0_ref, s1_ref, p0_ref, p1_ref, al0_ref, al1_ref,
                m_ref, l_ref, lfin_ref, acc_ref, *, tq, nk):
    kt = pl.program_id(3)
    hd = GQ_HEAD_DIM

    def build_qt():
        qt_ref[...] = jnp.zeros(qt_ref.shape, BF16)
        base = pl.multiple_of(pl.program_id(1) * hd, hd)
        for r in range(GQ_GROUP):
            qt_ref[pl.ds(base, hd), r * tq:(r + 1) * tq] = q_ref[0, r * hd:(r + 1) * hd, :]

    _flash_pipe(kt, nk, build_qt, k_ref, lambda: vt_ref[0], None, qt_ref,
                (s0_ref, s1_ref), (p0_ref, p1_ref), (al0_ref, al1_ref), m_ref, l_ref, lfin_ref, acc_ref)

    @pl.when(kt == nk + 1)
    def _():
        o = (acc_ref[...] / lfin_ref[...]).astype(o_ref.dtype)
        for r in range(GQ_GROUP):
            o_ref[0, r * hd:(r + 1) * hd, :] = o[:, r * tq:(r + 1) * tq]


def gqa_attn(qt, k, vt, *, tq, tk):
    b, _, n = qt.shape
    gw = GQ_GROUP * GQ_HEAD_DIM
    kvw = GQ_KV_HEADS * GQ_HEAD_DIM
    nk = n // tk
    cols = GQ_GROUP * tq
    kern = functools.partial(_gqa_kernel, tq=tq, nk=nk)
    tile = lambda ki, lag: jnp.clip(ki - lag, 0, nk - 1)
    return pl.pallas_call(
        kern,
        grid=(b, GQ_KV_HEADS, n // tq, nk + 2),
        in_specs=[
            pl.BlockSpec((1, gw, tq), lambda bi, g, qi, ki: (bi, g, qi)),
            pl.BlockSpec((1, tk, kvw), lambda bi, g, qi, ki: (bi, tile(ki, 0), 0)),
            pl.BlockSpec((1, GQ_HEAD_DIM, tk), lambda bi, g, qi, ki: (bi, g, tile(ki, 2))),
        ],
        out_specs=pl.BlockSpec((1, gw, tq), lambda bi, g, qi, ki: (bi, g, qi)),
        out_shape=jax.ShapeDtypeStruct((b, GQ_HEADS * GQ_HEAD_DIM, n), BF16),
        scratch_shapes=[pltpu.VMEM((kvw, cols), BF16)] + _flash_scratch(tk, cols, GQ_HEAD_DIM),
        compiler_params=_cparams(("parallel", "parallel", "parallel", "arbitrary")),
        name="gqa_attn",
    )(qt, k, vt)


def _gla_kernel(q_ref, k_ref, v_ref, z_ref, gw_ref, gb_ref, o_ref, st_ref, *, reverse, nchunk):
    L = GLA_CHUNK
    hd = GLA_HEADS * GLA_DK

    @pl.when(pl.program_id(1) == 0)
    def _():
        st_ref[...] = jnp.zeros(st_ref.shape, F32)

    row = lax.broadcasted_iota(jnp.int32, (L, L), 0)
    col = lax.broadcasted_iota(jnp.int32, (L, L), 1)
    if reverse:
        tri = (col >= row).astype(BF16)
        keep = col > row
    else:
        tri = (col <= row).astype(BF16)
        keep = col <= row
    lane = lax.broadcasted_iota(jnp.int32, (L, hd), 1)

    order = range(nchunk - 1, -1, -1) if reverse else range(nchunk)
    for c in order:
        rows = pl.ds(c * L, L)
        q = q_ref[0, rows, :].astype(F32) * (GLA_DK ** -0.5)
        k = k_ref[0, rows, :].astype(F32)
        logit = _dot(z_ref[0, rows, :], gw_ref[...]) + gb_ref[...]
        lg = jax.nn.log_sigmoid(logit) * (1.0 / GLA_TAU)
        hi = lg.astype(BF16)
        lo = (lg - hi.astype(F32)).astype(BF16)
        cum = _dot(tri, hi) + _dot(tri, lo)
        last = cum[0:1, :] if reverse else cum[L - 1:L, :]
        q_dec = q * jnp.exp(cum)
        k_dec = (k * jnp.exp(-cum)).astype(BF16)
        k_last = (k * jnp.exp(last - cum)).astype(BF16)
        decay = jnp.exp(last)
        for h in range(GLA_HEADS):
            head = (lane >= h * GLA_DK) & (lane < (h + 1) * GLA_DK)
            qh = jnp.where(head, q_dec, 0.0).astype(BF16)
            vh = v_ref[0, rows, h * GLA_DV:(h + 1) * GLA_DV]
            a = jnp.where(keep, _dot_nt(qh, k_dec), 0.0)
            st = st_ref[h]
            o = _dot(a.astype(BF16), vh) + _dot_nt(qh, st.astype(BF16))
            o_ref[0, rows, h * GLA_DV:(h + 1) * GLA_DV] = o
            st_ref[h] = st * decay + _dot_tn(vh, k_last)


def gla(proj, gw, gb, *, reverse, tm=256):
    b, n, _ = proj.shape
    nb = n // tm
    kern = functools.partial(_gla_kernel, reverse=reverse, nchunk=tm // GLA_CHUNK)
    blk = (lambda i: nb - 1 - i) if reverse else (lambda i: i)
    hd = GLA_HEADS * GLA_DK
    return pl.pallas_call(
        kern,
        grid=(b, nb),
        in_specs=[
            pl.BlockSpec((1, tm, hd), lambda bi, i: (bi, blk(i), COL_BQ // hd)),
            pl.BlockSpec((1, tm, hd), lambda bi, i: (bi, blk(i), COL_BK // hd)),
            pl.BlockSpec((1, tm, 512), lambda bi, i: (bi, blk(i), COL_BV // 512)),
            pl.BlockSpec((1, tm, 128), lambda bi, i: (bi, blk(i), COL_BZ // 128)),
            pl.BlockSpec((128, hd), lambda bi, i: (0, 0)),
            pl.BlockSpec((1, hd), lambda bi, i: (0, 0)),
        ],
        out_specs=pl.BlockSpec((1, tm, GLA_HEADS * GLA_DV), lambda bi, i: (bi, blk(i), 0)),
        out_shape=jax.ShapeDtypeStruct((b, n, GLA_HEADS * GLA_DV), F32),
        scratch_shapes=[pltpu.VMEM((GLA_HEADS, GLA_DV, hd), F32)],
        compiler_params=_cparams(("parallel", "arbitrary")),
        name="gla_bwd" if reverse else "gla_fwd",
    )(proj, proj, proj, proj, gw, gb)


def _merge_kernel(x_ref, ya_ref, of_ref, ob_ref, r_ref, yc_ref, gz_ref, gn_ref, wb_ref, wo_ref,
                  o_ref):
    gn = gn_ref[...]
    o = of_ref[0] + ob_ref[0]
    r = r_ref[0].astype(F32)
    yb = []
    for h in range(GLA_HEADS):
        sl = slice(h * GLA_DV, (h + 1) * GLA_DV)
        yb.append(_rms(o[:, sl], gn) * jax.nn.silu(r[:, sl]))
    yb = jnp.concatenate(yb, axis=1).astype(BF16)

    pa = _dot(ya_ref[0], wb_ref[0])
    pb = _dot(yb, wb_ref[1])
    pc = _dot_tn(yc_ref[0], wb_ref[2])

    gz = gz_ref[0].astype(F32)
    merged = (jax.nn.sigmoid(gz[:, 0:D_MODEL]) * pa
              + jax.nn.sigmoid(gz[:, D_MODEL:2 * D_MODEL]) * pb
              + jax.nn.sigmoid(gz[:, 2 * D_MODEL:3 * D_MODEL]) * pc)
    o_ref[0] = x_ref[0] + _dot(merged.astype(BF16), wo_ref[...])


def merge(x, ya, of, ob, proj, yc, gn, wb, wo, *, tm=256):
    b, n, _ = x.shape
    tok = lambda bi, i: (bi, i, 0)
    return pl.pallas_call(
        _merge_kernel,
        grid=(b, n // tm),
        in_specs=[
            pl.BlockSpec((1, tm, D_MODEL), tok),
            pl.BlockSpec((1, tm, BRANCH_W), tok),
            pl.BlockSpec((1, tm, BRANCH_W), tok),
            pl.BlockSpec((1, tm, BRANCH_W), tok),
            pl.BlockSpec((1, tm, 512), lambda bi, i: (bi, i, COL_BR // 512)),
            pl.BlockSpec((1, BRANCH_W, tm), lambda bi, i: (bi, 0, i)),
            pl.BlockSpec((1, tm, N_BRANCH * D_MODEL), lambda bi, i: (bi, i, COL_GZ // (N_BRANCH * D_MODEL))),
            pl.BlockSpec((1, GLA_DV), lambda bi, i: (0, 0)),
            pl.BlockSpec((N_BRANCH, BRANCH_W, D_MODEL), lambda bi, i: (0, 0, 0)),
            pl.BlockSpec((D_MODEL, D_MODEL), lambda bi, i: (0, 0)),
        ],
        out_specs=pl.BlockSpec((1, tm, D_MODEL), tok),
        out_shape=jax.ShapeDtypeStruct((b, n, D_MODEL), F32),
        compiler_params=_cparams(("parallel", "parallel")),
        name="merge",
    )(x, ya, of, ob, proj, yc, proj, gn, wb, wo)


TOPX = PEER_TOPK + 1
TOPX_PAD = 24


def _top_values(vals, count):
    out = []
    for _ in range(count):
        m = jnp.max(vals, axis=0, keepdims=True)
        out.append(m)
        vals = jnp.where(vals == m, NEG_INF, vals)
    return out


def _route_kernel(x_ref, g_ref, wq_ref, sk_ref, hn_ref, s1_ref, e1_ref, th_ref, e0_ref):
    tb = x_ref.shape[0]
    hn = _rms(x_ref[...], g_ref[...])
    hn_t = hn.T.astype(BF16)
    hn_ref[...] = hn_t
    q_t = _dot(wq_ref[...], hn_t).astype(BF16)
    pad = jnp.full((TOPX_PAD - TOPX, tb), NEG_INF, F32)
    for h in range(PEER_HEADS):
        r0 = (2 * h) * PEER_HALF
        s0 = _dot(sk_ref[h, 0], q_t[r0:r0 + PEER_HALF])
        s1 = _dot(sk_ref[h, 1], q_t[r0 + PEER_HALF:r0 + 2 * PEER_HALF])
        top0 = _top_values(s0, TOPX)
        top1 = jnp.concatenate(_top_values(s1, TOPX) + [pad], axis=0)
        cand = jnp.concatenate([t0 + top1 for t0 in top0], axis=0)
        best = _top_values(cand, TOPX)
        thresh = 0.5 * (best[PEER_TOPK - 1] + best[PEER_TOPK])
        m0, m1 = top0[0], top1[0:1]
        z = jnp.zeros_like(m0)
        for c in best[:PEER_TOPK]:
            z = z + jnp.exp(c - best[0])
        e1 = jnp.exp(s1 - m1) / z
        th = thresh - s0
        e0 = jnp.exp(s0 - m0)
        for c in range(tb // 128):
            cs = slice(c * 128, (c + 1) * 128)
            s1_ref[h, c] = s1[:, cs]
            e1_ref[h, c] = e1[:, cs]
            th_ref[h, c] = th[:, cs]
            e0_ref[h, c] = e0[:, cs]


def peer_route(x2d, g, wq_t, subkeys, *, tb=256):
    t = x2d.shape[0]
    hk = jax.ShapeDtypeStruct((PEER_HEADS, t // 128, PEER_KEYS, 128), F32)
    hk_spec = pl.BlockSpec((PEER_HEADS, tb // 128, PEER_KEYS, 128), lambda i: (0, i, 0, 0))
    return pl.pallas_call(
        _route_kernel,
        grid=(t // tb,),
        in_specs=[
            pl.BlockSpec((tb, D_MODEL), lambda i: (i, 0)),
            pl.BlockSpec((1, D_MODEL), lambda i: (0, 0)),
            pl.BlockSpec((PEER_HEADS * PEER_QDIM, D_MODEL), lambda i: (0, 0)),
            pl.BlockSpec((PEER_HEADS, 2, PEER_KEYS, PEER_HALF), lambda i: (0, 0, 0, 0)),
        ],
        out_specs=[pl.BlockSpec((D_MODEL, tb), lambda i: (0, i)), hk_spec, hk_spec, hk_spec, hk_spec],
        out_shape=[jax.ShapeDtypeStruct((D_MODEL, t), BF16), hk, hk, hk, hk],
        compiler_params=_cparams(("parallel",)),
        name="peer_route",
    )(x2d, g, wq_t, subkeys)


DENSE_ROWS = 32


def _dense_kernel(x_ref, hn_ref, u_ref, vt_ref, s1_ref, e1_ref, th_ref, e0_ref, fg_ref, o_ref,
                  wg_ref, acc_ref, *, ne, ib, final_norm):
    e = pl.program_id(1)
    tb = hn_ref.shape[1]

    @pl.when(e == 0)
    def _():
        acc_ref[...] = jnp.zeros(acc_ref.shape, F32)

    a = _dot(u_ref[...], hn_ref[...])
    gelu_c = np.float32(np.sqrt(0.5))
    for il in range(ib):
        for tc in range(tb // 128):
            cols = slice(tc * 128, (tc + 1) * 128)
            for jc in range(PEER_KEYS // DENSE_ROWS):
                keys = slice(jc * DENSE_ROWS, (jc + 1) * DENSE_ROWS)
                rows = slice(il * PEER_KEYS + jc * DENSE_ROWS, il * PEER_KEYS + (jc + 1) * DENSE_ROWS)
                w = jnp.zeros((DENSE_ROWS, 128), F32)
                for h in range(PEER_HEADS):
                    sel = s1_ref[h, tc, keys, :] >= th_ref[h, tc, il:il + 1, :]
                    w = w + jnp.where(sel, e1_ref[h, tc, keys, :], 0.0) * e0_ref[h, tc, il:il + 1, :]
                ac = a[rows, cols]
                act = 0.5 * ac * (1.0 + lax.erf(ac * gelu_c))
                wg_ref[rows, cols] = (w * act).astype(BF16)
    acc_ref[...] += _dot(vt_ref[...], wg_ref[...])

    @pl.when(e == ne - 1)
    def _():
        y = x_ref[...] + acc_ref[...].T
        if final_norm:
            y = _rms(y, fg_ref[...])
        o_ref[...] = y


def peer_dense(x2d, hn_t, u, v_t, s1, e1, th, e0, final_g, *, final_norm, tb=512, ib=8):
    t = x2d.shape[0]
    eb = ib * PEER_KEYS
    ne = PEER_EXPERTS // eb
    kern = functools.partial(_dense_kernel, ne=ne, ib=ib, final_norm=final_norm)
    full = pl.BlockSpec((PEER_HEADS, tb // 128, PEER_KEYS, 128), lambda i, e: (0, i, 0, 0))
    part = pl.BlockSpec((PEER_HEADS, tb // 128, ib, 128), lambda i, e: (0, i, e, 0))
    return pl.pallas_call(
        kern,
        grid=(t // tb, ne),
        in_specs=[
            pl.BlockSpec((tb, D_MODEL), lambda i, e: (i, 0)),
            pl.BlockSpec((D_MODEL, tb), lambda i, e: (0, i)),
            pl.BlockSpec((eb, D_MODEL), lambda i, e: (e, 0)),
            pl.BlockSpec((D_MODEL, eb), lambda i, e: (0, e)),
            full, full, part, part,
            pl.BlockSpec((1, D_MODEL), lambda i, e: (0, 0)),
        ],
        out_specs=pl.BlockSpec((tb, D_MODEL), lambda i, e: (i, 0)),
        out_shape=jax.ShapeDtypeStruct((t, D_MODEL), F32),
        scratch_shapes=[pltpu.VMEM((eb, tb), BF16), pltpu.VMEM((D_MODEL, tb), F32)],
        compiler_params=_cparams(("parallel", "arbitrary")),
        name="peer_dense",
    )(x2d, hn_t, u, v_t, s1, e1, th, e0, final_g)


def _rel_bucket(rel):
    nb = REL_BUCKETS // 2
    max_exact = nb // 2
    n = jnp.abs(rel)
    nf = jnp.maximum(n, 1).astype(jnp.float32)
    large = max_exact + (jnp.log(nf / max_exact) / math.log(REL_MAX_DIST / max_exact)
                         * (nb - max_exact)).astype(jnp.int32)
    large = jnp.minimum(large, nb - 1)
    return jnp.where(rel > 0, nb, 0) + jnp.where(n < max_exact, n, large)


def _bias_band(rel_table, tq, tk):
    r = tk // tq
    qpos = jnp.arange(tq)
    tiles = []
    for d in range(-r - 1, 3):
        rel = (d * tq + jnp.arange(tk))[:, None] - qpos[None, :]
        onehot = jax.nn.one_hot(_rel_bucket(rel), REL_BUCKETS, dtype=F32)
        tiles.append(jnp.einsum("kqb,bh->hkq", onehot, rel_table.astype(F32),
                                precision=lax.Precision.HIGHEST))
    return jnp.stack(tiles) * LOG2E


def _rope_tables(n):
    rows = n // GRID_W
    row_id = jnp.repeat(jnp.arange(rows), GRID_W).astype(jnp.float32)
    col_id = (jnp.arange(rows * GRID_W) % GRID_W).astype(jnp.float32)
    sec = GQ_HEAD_DIM // 2
    inv = ROPE_THETA ** (-jnp.arange(0, sec, 2, dtype=jnp.float32) / sec)
    cr, sr = jnp.cos(row_id[:, None] * inv), jnp.sin(row_id[:, None] * inv)
    cc, sc = jnp.cos(col_id[:, None] * inv), jnp.sin(col_id[:, None] * inv)
    cos = jnp.concatenate([cr, cr, cc, cc], axis=1)
    sin = jnp.concatenate([-sr, sr, -sc, sc], axis=1)
    return jnp.concatenate([cos, cos], axis=1), jnp.concatenate([sin, sin], axis=1)


def _prep_weights(w_in, gla_gate_w, gla_gate_b, gq_qk_g, w_branch, w_out, peer_wq, peer_subkeys,
                  peer_u, peer_v):
    wi = jnp.concatenate(
        [w_in[:, :, ORIG_GZ:], w_in[:, :, :ORIG_BZ], w_in[:, :, ORIG_CQ:ORIG_GZ],
         w_in[:, :, ORIG_BZ:ORIG_CQ],
         jnp.zeros((DEPTH, D_MODEL, PROJ_W - ORIG_W), w_in.dtype)], axis=2).astype(BF16)
    hd = GLA_HEADS * GLA_DK
    gw = jnp.zeros((DEPTH, 2, 128, hd), F32)
    gw = gw.at[:, 0, 0:GLA_GATE_RANK].set(gla_gate_w[:, 0])
    gw = gw.at[:, 1, GLA_GATE_RANK:2 * GLA_GATE_RANK].set(gla_gate_w[:, 1])
    return dict(
        w_in=wi, gw=gw.astype(BF16), gb=gla_gate_b.astype(F32)[:, :, None, :],
        gq=jnp.tile(gq_qk_g[:, 0], (1, GQ_HEADS))[:, None, :],
        gk=jnp.tile(gq_qk_g[:, 1], (1, GQ_KV_HEADS))[:, None, :],
        wb=w_branch.astype(BF16), wo=w_out.astype(BF16),
        wq_t=jnp.swapaxes(peer_wq, 1, 2).astype(BF16), sk=peer_subkeys.astype(BF16),
        u=peer_u.astype(BF16), v_t=jnp.swapaxes(peer_v, 1, 2).astype(BF16))


def _encoder(x, w, rel_bias, norm1_g, da_lambda, da_subln_g, gla_norm_g, norm2_g, final_g, seg_ones):
    b, n, _ = x.shape
    t = b * n
    band = _bias_band(rel_bias, DA_TQ, DA_TK)
    cos2, sin2 = _rope_tables(n)
    for l in range(DEPTH):
        proj = in_proj(x.reshape(t, D_MODEL), norm1_g[l][None, :], w["w_in"][l]).reshape(b, n, PROJ_W)
        lam_init = 0.8 - 0.6 * math.exp(-0.3 * l)
        lp = da_lambda[l].astype(F32)
        lam = (jnp.exp(jnp.sum(lp[0] * lp[1])) - jnp.exp(jnp.sum(lp[2] * lp[3])) + lam_init).reshape(1)
        ya = diff_attn(proj, band, lam, da_subln_g[l][None, :], lam_init=lam_init, tq=DA_TQ, tk=DA_TK)
        qt, kr, vt = gqa_prep(proj, cos2, sin2, w["gq"][l], w["gk"][l], seg_ones)
        yc = gqa_attn(qt, kr, vt, tq=GQ_TQ, tk=GQ_TK)
        of = gla(proj, w["gw"][l, 0], w["gb"][l, 0], reverse=False)
        ob = gla(proj, w["gw"][l, 1], w["gb"][l, 1], reverse=True)
        x1 = merge(x, ya, of, ob, proj, yc, gla_norm_g[l][None, :], w["wb"][l], w["wo"][l])
        x1 = x1.reshape(t, D_MODEL)
        hn_t, s1, e1, th, e0 = peer_route(x1, norm2_g[l][None, :], w["wq_t"][l], w["sk"][l])
        x = peer_dense(x1, hn_t, w["u"][l], w["v_t"][l], s1, e1, th, e0, final_g[None, :],
                       final_norm=(l == DEPTH - 1)).reshape(b, n, D_MODEL)
    return x


def kernel(x_prompt, x_sample, rel_bias, norm1_g, w_in, da_lambda, da_subln_g, gla_gate_w, gla_gate_b,
           gla_norm_g, gq_qk_g, w_branch, w_out, norm2_g, peer_wq, peer_subkeys, peer_u, peer_v, final_g):
    w = _prep_weights(w_in, gla_gate_w, gla_gate_b, gq_qk_g, w_branch, w_out, peer_wq, peer_subkeys,
                      peer_u, peer_v)
    seg = np.arange(512) // GQ_HEAD_DIM
    seg_ones = jnp.asarray(seg[:, None] == seg[None, :], dtype=BF16)
    args = (w, rel_bias, norm1_g, da_lambda, da_subln_g, gla_norm_g, norm2_g, final_g, seg_ones)
    return (_encoder(x_prompt, *args), _encoder(x_sample, *args))
```

```python
import functools
import math

import numpy as np
import jax
import jax.numpy as jnp
from jax import lax
from jax.experimental import pallas as pl
from jax.experimental.pallas import tpu as pltpu

D_MODEL = 1024
DEPTH = 2
GRID_W = 64
NORM_EPS = 1e-6
N_BRANCH = 3
BRANCH_W = 512

DA_HEADS = 4
DA_HEAD_DIM = 64
DA_V_DIM = 128
REL_BUCKETS = 32
REL_MAX_DIST = 128

GLA_HEADS = 4
GLA_DK = 64
GLA_DV = 128
GLA_GATE_RANK = 16
GLA_TAU = 16.0
GLA_CHUNK = 64

GQ_HEADS = 8
GQ_KV_HEADS = 2
GQ_GROUP = GQ_HEADS // GQ_KV_HEADS
GQ_HEAD_DIM = 64
ROPE_THETA = 10000.0

PEER_HEADS = 8
PEER_KEYS = 128
PEER_EXPERTS = PEER_KEYS * PEER_KEYS
PEER_QDIM = 256
PEER_HALF = 128
PEER_TOPK = 16

COL_GZ = 0
COL_AQ, COL_AK, COL_AV = 3072, 3584, 4096
COL_BQ, COL_BK, COL_BV, COL_BR = 4608, 4864, 5120, 5632
COL_CQ, COL_CK, COL_CV = 6144, 6656, 6784
COL_BZ = 6912
PROJ_W = 7168
ORIG_BZ = 3072
ORIG_CQ = ORIG_BZ + 2 * GLA_GATE_RANK
ORIG_GZ = ORIG_CQ + 768
ORIG_W = ORIG_GZ + N_BRANCH * D_MODEL

DA_TQ, DA_TK = 1024, 512
GQ_TQ, GQ_TK = 512, 512

VMEM_LIMIT = 56 * 1024 * 1024

F32 = jnp.float32
BF16 = jnp.bfloat16
NEG_INF = float("-inf")
LOG2E = math.log2(math.e)


def _cparams(sem):
    return pltpu.CompilerParams(dimension_semantics=sem, vmem_limit_bytes=VMEM_LIMIT)


def _dot(a, b):
    return jnp.dot(a, b, preferred_element_type=F32)


def _dot_nt(a, b):
    return lax.dot_general(a, b, (((1,), (1,)), ((), ())), preferred_element_type=F32)


def _dot_tn(a, b):
    return lax.dot_general(a, b, (((0,), (0,)), ((), ())), preferred_element_type=F32)


def _rms(x, g):
    return x * lax.rsqrt(jnp.mean(x * x, axis=-1, keepdims=True) + NORM_EPS) * g


def _in_proj_kernel(x_ref, g_ref, w_ref, o_ref, hn_ref):
    @pl.when(pl.program_id(1) == 0)
    def _():
        hn_ref[...] = _rms(x_ref[...], g_ref[...]).astype(BF16)

    o_ref[...] = _dot(hn_ref[...], w_ref[...]).astype(o_ref.dtype)


def in_proj(x2d, g, w, *, tm=512, tn=1024):
    t = x2d.shape[0]
    nw = w.shape[1]
    return pl.pallas_call(
        _in_proj_kernel,
        grid=(t // tm, nw // tn),
        in_specs=[
            pl.BlockSpec((tm, D_MODEL), lambda i, j: (i, 0)),
            pl.BlockSpec((1, D_MODEL), lambda i, j: (0, 0)),
            pl.BlockSpec((D_MODEL, tn), lambda i, j: (0, j)),
        ],
        out_specs=pl.BlockSpec((tm, tn), lambda i, j: (i, j)),
        out_shape=jax.ShapeDtypeStruct((t, nw), BF16),
        scratch_shapes=[pltpu.VMEM((tm, D_MODEL), BF16)],
        compiler_params=_cparams(("parallel", "arbitrary")),
        name="in_proj",
    )(x2d, g, w)


def _flash_pipe_step(k, vt, bias, qt_ref, s_cur, s_prev, p_cur, p_prev, al_cur, al_prev,
                     m_ref, l_ref, acc_ref):
    s_cur[...] = _dot(k, qt_ref[...])
    acc_ref[...] = al_cur[...] * acc_ref[...] + _dot(vt, p_cur[...])
    s = s_prev[...]
    if bias is not None:
        s = s + jnp.concatenate([bias] * (s.shape[1] // bias.shape[1]), axis=1)
    m_prev = m_ref[...]
    m_next = jnp.maximum(m_prev, jnp.max(s, axis=0, keepdims=True))
    p = jnp.exp2(s - m_next)
    alpha = jnp.exp2(m_prev - m_next)
    l_ref[...] = alpha * l_ref[...] + jnp.sum(p, axis=0, keepdims=True)
    m_ref[...] = m_next
    al_prev[...] = alpha
    p_prev[...] = p.astype(BF16)


def _flash_pipe(kt, nk, build_qt, k_ref, get_vt, bias_ref, qt_ref, s_refs, p_refs, al_refs,
                m_ref, l_ref, lfin_ref, acc_ref):
    @pl.when(kt == 0)
    def _():
        build_qt()
        for r in (s_refs[1], p_refs[0], al_refs[0], acc_ref):
            r[...] = jnp.zeros(r.shape, r.dtype)

    @pl.when(kt <= 1)
    def _():
        m_ref[...] = jnp.full(m_ref.shape, NEG_INF, F32)
        l_ref[...] = jnp.zeros(l_ref.shape, F32)

    @pl.when(kt == 2)
    def _():
        acc_ref[...] = jnp.zeros(acc_ref.shape, F32)

    @pl.when(kt == nk + 1)
    def _():
        lfin_ref[...] = l_ref[...]

    for cur in (0, 1):
        @pl.when(kt % 2 == cur)
        def _(cur=cur):
            prev = 1 - cur
            bias = None if bias_ref is None else bias_ref[0, 0]
            _flash_pipe_step(k_ref[0], get_vt(), bias, qt_ref, s_refs[cur], s_refs[prev],
                             p_refs[cur], p_refs[prev], al_refs[cur], al_refs[prev], m_ref, l_ref, acc_ref)


def _flash_scratch(tk, cols, dv):
    return [pltpu.VMEM((tk, cols), F32), pltpu.VMEM((tk, cols), F32),
            pltpu.VMEM((tk, cols), BF16), pltpu.VMEM((tk, cols), BF16),
            pltpu.VMEM((1, cols), F32), pltpu.VMEM((1, cols), F32),
            pltpu.VMEM((1, cols), F32), pltpu.VMEM((1, cols), F32), pltpu.VMEM((1, cols), F32),
            pltpu.VMEM((dv, cols), F32)]


def _dattn_kernel(lam_ref, q_ref, k_ref, v_ref, b_ref, g_ref, o_ref, qt_ref,
                  s0_ref, s1_ref, p0_ref, p1_ref, al0_ref, al1_ref, m_ref, l_ref, lfin_ref, acc_ref,
                  *, tq, nk, out_scale):
    kt = pl.program_id(3)

    def build_qt():
        q = q_ref[0].astype(F32) * (LOG2E * DA_HEAD_DIM ** -0.5)
        lane = lax.broadcasted_iota(jnp.int32, q.shape, 1)
        qt_ref[:, 0:tq] = jnp.where(lane < DA_HEAD_DIM, q, 0.0).T.astype(BF16)
        qt_ref[:, tq:2 * tq] = jnp.where(lane >= DA_HEAD_DIM, q, 0.0).T.astype(BF16)

    _flash_pipe(kt, nk, build_qt, k_ref, lambda: v_ref[0].astype(F32).T.astype(BF16), b_ref, qt_ref,
                (s0_ref, s1_ref), (p0_ref, p1_ref), (al0_ref, al1_ref), m_ref, l_ref, lfin_ref, acc_ref)

    @pl.when(kt == nk + 1)
    def _():
        o = acc_ref[...] / lfin_ref[...]
        o = (o[:, 0:tq] - lam_ref[0] * o[:, tq:2 * tq]).T
        o_ref[0] = (_rms(o, g_ref[...]) * out_scale).astype(o_ref.dtype)


def diff_attn(proj, bias_band, lam, sub_g, *, lam_init, tq, tk):
    b, n, _ = proj.shape
    _, rq, rk = _band_units(tq, tk)
    nk = n // tk
    kern = functools.partial(_dattn_kernel, tq=tq, nk=nk, out_scale=1.0 - lam_init)
    tile = lambda ki, lag: jnp.clip(ki - lag, 0, nk - 1)

    def bias_map(bi, h, qi, ki):
        return (jnp.clip(rk * tile(ki, 1) - rq * qi, -rk - 1, rq + 1) + rk + 1, h, 0, 0)

    return pl.pallas_call(
        kern,
        grid=(b, DA_HEADS, n // tq, nk + 2),
        in_specs=[
            pl.BlockSpec(memory_space=pltpu.SMEM),
            pl.BlockSpec((1, tq, 128), lambda bi, h, qi, ki: (bi, qi, COL_AQ // 128 + h)),
            pl.BlockSpec((1, tk, 128), lambda bi, h, qi, ki: (bi, tile(ki, 0), COL_AK // 128 + h)),
            pl.BlockSpec((1, tk, 128), lambda bi, h, qi, ki: (bi, tile(ki, 2), COL_AV // 128 + h)),
            pl.BlockSpec((1, 1, tk, tq), bias_map),
            pl.BlockSpec((1, DA_V_DIM), lambda bi, h, qi, ki: (0, 0)),
        ],
        out_specs=pl.BlockSpec((1, tq, DA_V_DIM), lambda bi, h, qi, ki: (bi, qi, h)),
        out_shape=jax.ShapeDtypeStruct((b, n, DA_HEADS * DA_V_DIM), BF16),
        scratch_shapes=[pltpu.VMEM((128, 2 * tq), BF16)] + _flash_scratch(tk, 2 * tq, DA_V_DIM),
        compiler_params=_cparams(("parallel", "parallel", "parallel", "arbitrary")),
        name="diff_attn",
    )(lam, proj, proj, proj, bias_band, sub_g)


def _swap16(x):
    w = x.shape[1]
    lane = lax.broadcasted_iota(jnp.int32, x.shape, 1)
    return jnp.where(lane % 32 < 16, pltpu.roll(x, w - 16, 1), pltpu.roll(x, 16, 1))


def _seg_rms(x, seg_ones, g):
    sq = x * x
    hi = sq.astype(BF16)
    lo = (sq - hi.astype(F32)).astype(BF16)
    ss = _dot(hi, seg_ones) + _dot(lo, seg_ones)
    return x * lax.rsqrt(ss * (1.0 / GQ_HEAD_DIM) + NORM_EPS) * g


def _gqa_prep_kernel(q_ref, k_ref, v_ref, cos_ref, sin_ref, gq_ref, gk_ref, ones_ref,
                     qo_ref, ko_ref, vo_ref):
    cos = cos_ref[...]
    sin = sin_ref[...]
    cos4 = jnp.concatenate([cos] * 4, axis=1)
    sin4 = jnp.concatenate([sin] * 4, axis=1)
    ones = ones_ref[...]
    q = _seg_rms(q_ref[0].astype(F32), ones, gq_ref[...])
    k = _seg_rms(k_ref[0].astype(F32), ones[0:128, 0:128], gk_ref[...])
    q = (q * cos4 + _swap16(q) * sin4) * (LOG2E * GQ_HEAD_DIM ** -0.5)
    qo_ref[0] = q.T.astype(BF16)
    ko_ref[0] = (k * cos + _swap16(k) * sin).astype(BF16)
    vo_ref[0] = v_ref[0].astype(F32).T.astype(BF16)


def gqa_prep(proj, cos2, sin2, gq, gk, seg_ones, *, tm=512):
    b, n, _ = proj.shape
    return pl.pallas_call(
        _gqa_prep_kernel,
        grid=(b, n // tm),
        in_specs=[
            pl.BlockSpec((1, tm, 512), lambda bi, i: (bi, i, COL_CQ // 512)),
            pl.BlockSpec((1, tm, 128), lambda bi, i: (bi, i, COL_CK // 128)),
            pl.BlockSpec((1, tm, 128), lambda bi, i: (bi, i, COL_CV // 128)),
            pl.BlockSpec((tm, 128), lambda bi, i: (i, 0)),
            pl.BlockSpec((tm, 128), lambda bi, i: (i, 0)),
            pl.BlockSpec((1, 512), lambda bi, i: (0, 0)),
            pl.BlockSpec((1, 128), lambda bi, i: (0, 0)),
            pl.BlockSpec((512, 512), lambda bi, i: (0, 0)),
        ],
        out_specs=[
            pl.BlockSpec((1, GQ_HEADS * GQ_HEAD_DIM, tm), lambda bi, i: (bi, 0, i)),
            pl.BlockSpec((1, tm, GQ_KV_HEADS * GQ_HEAD_DIM), lambda bi, i: (bi, i, 0)),
            pl.BlockSpec((1, GQ_KV_HEADS * GQ_HEAD_DIM, tm), lambda bi, i: (bi, 0, i)),
        ],
        out_shape=[
            jax.ShapeDtypeStruct((b, GQ_HEADS * GQ_HEAD_DIM, n), BF16),
            jax.ShapeDtypeStruct((b, n, GQ_KV_HEADS * GQ_HEAD_DIM), BF16),
            jax.ShapeDtypeStruct((b, GQ_KV_HEADS * GQ_HEAD_DIM, n), BF16),
        ],
        compiler_params=_cparams(("parallel", "parallel")),
        name="gqa_prep",
    )(proj, proj, proj, cos2, sin2, gq, gk, seg_ones)


def _gqa_kernel(q_ref, k_ref, vt_ref, o_ref, qt_ref, s0_ref, s1_ref, p0_ref, p1_ref, al0_ref, al1_ref,
                m_ref, l_ref, lfin_ref, acc_ref, *, tq, nk):
    kt = pl.program_id(3)
    hd = GQ_HEAD_DIM

    def build_qt():
        qt_ref[...] = jnp.zeros(qt_ref.shape, BF16)
        base = pl.multiple_of(pl.program_id(1) * hd, hd)
        for r in range(GQ_GROUP):
            qt_ref[pl.ds(base, hd), r * tq:(r + 1) * tq] = q_ref[0, r * hd:(r + 1) * hd, :]

    _flash_pipe(kt, nk, build_qt, k_ref, lambda: vt_ref[0], None, qt_ref,
                (s0_ref, s1_ref), (p0_ref, p1_ref), (al0_ref, al1_ref), m_ref, l_ref, lfin_ref, acc_ref)

    @pl.when(kt == nk + 1)
    def _():
        o = (acc_ref[...] / lfin_ref[...]).astype(o_ref.dtype)
        for r in range(GQ_GROUP):
            o_ref[0, r * hd:(r + 1) * hd, :] = o[:, r * tq:(r + 1) * tq]


def gqa_attn(qt, k, vt, *, tq, tk):
    b, _, n = qt.shape
    gw = GQ_GROUP * GQ_HEAD_DIM
    kvw = GQ_KV_HEADS * GQ_HEAD_DIM
    nk = n // tk
    cols = GQ_GROUP * tq
    kern = functools.partial(_gqa_kernel, tq=tq, nk=nk)
    tile = lambda ki, lag: jnp.clip(ki - lag, 0, nk - 1)
    return pl.pallas_call(
        kern,
        grid=(b, GQ_KV_HEADS, n // tq, nk + 2),
        in_specs=[
            pl.BlockSpec((1, gw, tq), lambda bi, g, qi, ki: (bi, g, qi)),
            pl.BlockSpec((1, tk, kvw), lambda bi, g, qi, ki: (bi, tile(ki, 0), 0)),
            pl.BlockSpec((1, GQ_HEAD_DIM, tk), lambda bi, g, qi, ki: (bi, g, tile(ki, 2))),
        ],
        out_specs=pl.BlockSpec((1, gw, tq), lambda bi, g, qi, ki: (bi, g, qi)),
        out_shape=jax.ShapeDtypeStruct((b, GQ_HEADS * GQ_HEAD_DIM, n), BF16),
        scratch_shapes=[pltpu.VMEM((kvw, cols), BF16)] + _flash_scratch(tk, cols, GQ_HEAD_DIM),
        compiler_params=_cparams(("parallel", "parallel", "parallel", "arbitrary")),
        name="gqa_attn",
    )(qt, k, vt)


def _gla_kernel(q_ref, k_ref, v_ref, z_ref, gw_ref, gb_ref, o_ref, st_ref, *, reverse, nchunk):
    L = GLA_CHUNK
    hd = GLA_HEADS * GLA_DK

    @pl.when(pl.program_id(1) == 0)
    def _():
        st_ref[...] = jnp.zeros(st_ref.shape, F32)

    row = lax.broadcasted_iota(jnp.int32, (L, L), 0)
    col = lax.broadcasted_iota(jnp.int32, (L, L), 1)
    if reverse:
        tri = (col >= row).astype(BF16)
        keep = col > row
    else:
        tri = (col <= row).astype(BF16)
        keep = col <= row
    lane = lax.broadcasted_iota(jnp.int32, (L, hd), 1)

    order = range(nchunk - 1, -1, -1) if reverse else range(nchunk)
    for c in order:
        rows = pl.ds(c * L, L)
        q = q_ref[0, rows, :].astype(F32) * (GLA_DK ** -0.5)
        k = k_ref[0, rows, :].astype(F32)
        logit = _dot(z_ref[0, rows, :], gw_ref[...]) + gb_ref[...]
        lg = jax.nn.log_sigmoid(logit) * (1.0 / GLA_TAU)
        hi = lg.astype(BF16)
        lo = (lg - hi.astype(F32)).astype(BF16)
        cum = _dot(tri, hi) + _dot(tri, lo)
        last = cum[0:1, :] if reverse else cum[L - 1:L, :]
        q_dec = q * jnp.exp(cum)
        k_dec = (k * jnp.exp(-cum)).astype(BF16)
        k_last = (k * jnp.exp(last - cum)).astype(BF16)
        decay = jnp.exp(last)
        for h in range(GLA_HEADS):
            head = (lane >= h * GLA_DK) & (lane < (h + 1) * GLA_DK)
            qh = jnp.where(head, q_dec, 0.0).astype(BF16)
            vh = v_ref[0, rows, h * GLA_DV:(h + 1) * GLA_DV]
            a = jnp.where(keep, _dot_nt(qh, k_dec), 0.0)
            st = st_ref[h]
            o = _dot(a.astype(BF16), vh) + _dot_nt(qh, st.astype(BF16))
            o_ref[0, rows, h * GLA_DV:(h + 1) * GLA_DV] = o
            st_ref[h] = st * decay + _dot_tn(vh, k_last)


def gla(proj, gw, gb, *, reverse, tm=256):
    b, n, _ = proj.shape
    nb = n // tm
    kern = functools.partial(_gla_kernel, reverse=reverse, nchunk=tm // GLA_CHUNK)
    blk = (lambda i: nb - 1 - i) if reverse else (lambda i: i)
    hd = GLA_HEADS * GLA_DK
    return pl.pallas_call(
        kern,
        grid=(b, nb),
        in_specs=[
            pl.BlockSpec((1, tm, hd), lambda bi, i: (bi, blk(i), COL_BQ // hd)),
            pl.BlockSpec((1, tm, hd), lambda bi, i: (bi, blk(i), COL_BK // hd)),
            pl.BlockSpec((1, tm, 512), lambda bi, i: (bi, blk(i), COL_BV // 512)),
            pl.BlockSpec((1, tm, 128), lambda bi, i: (bi, blk(i), COL_BZ // 128)),
            pl.BlockSpec((128, hd), lambda bi, i: (0, 0)),
            pl.BlockSpec((1, hd), lambda bi, i: (0, 0)),
        ],
        out_specs=pl.BlockSpec((1, tm, GLA_HEADS * GLA_DV), lambda bi, i: (bi, blk(i), 0)),
        out_shape=jax.ShapeDtypeStruct((b, n, GLA_HEADS * GLA_DV), F32),
        scratch_shapes=[pltpu.VMEM((GLA_HEADS, GLA_DV, hd), F32)],
        compiler_params=_cparams(("parallel", "arbitrary")),
        name="gla_bwd" if reverse else "gla_fwd",
    )(proj, proj, proj, proj, gw, gb)


def _merge_kernel(x_ref, ya_ref, of_ref, ob_ref, r_ref, yc_ref, gz_ref, gn_ref, wb_ref, wo_ref,
                  o_ref):
    gn = gn_ref[...]
    o = of_ref[0] + ob_ref[0]
    r = r_ref[0].astype(F32)
    yb = []
    for h in range(GLA_HEADS):
        sl = slice(h * GLA_DV, (h + 1) * GLA_DV)
        yb.append(_rms(o[:, sl], gn) * jax.nn.silu(r[:, sl]))
    yb = jnp.concatenate(yb, axis=1).astype(BF16)

    pa = _dot(ya_ref[0], wb_ref[0])
    pb = _dot(yb, wb_ref[1])
    pc = _dot_tn(yc_ref[0], wb_ref[2])

    gz = gz_ref[0].astype(F32)
    merged = (jax.nn.sigmoid(gz[:, 0:D_MODEL]) * pa
              + jax.nn.sigmoid(gz[:, D_MODEL:2 * D_MODEL]) * pb
              + jax.nn.sigmoid(gz[:, 2 * D_MODEL:3 * D_MODEL]) * pc)
    o_ref[0] = x_ref[0] + _dot(merged.astype(BF16), wo_ref[...])


def merge(x, ya, of, ob, proj, yc, gn, wb, wo, *, tm=256):
    b, n, _ = x.shape
    tok = lambda bi, i: (bi, i, 0)
    return pl.pallas_call(
        _merge_kernel,
        grid=(b, n // tm),
        in_specs=[
            pl.BlockSpec((1, tm, D_MODEL), tok),
            pl.BlockSpec((1, tm, BRANCH_W), tok),
            pl.BlockSpec((1, tm, BRANCH_W), tok),
            pl.BlockSpec((1, tm, BRANCH_W), tok),
            pl.BlockSpec((1, tm, 512), lambda bi, i: (bi, i, COL_BR // 512)),
            pl.BlockSpec((1, BRANCH_W, tm), lambda bi, i: (bi, 0, i)),
            pl.BlockSpec((1, tm, N_BRANCH * D_MODEL), lambda bi, i: (bi, i, COL_GZ // (N_BRANCH * D_MODEL))),
            pl.BlockSpec((1, GLA_DV), lambda bi, i: (0, 0)),
            pl.BlockSpec((N_BRANCH, BRANCH_W, D_MODEL), lambda bi, i: (0, 0, 0)),
            pl.BlockSpec((D_MODEL, D_MODEL), lambda bi, i: (0, 0)),
        ],
        out_specs=pl.BlockSpec((1, tm, D_MODEL), tok),
        out_shape=jax.ShapeDtypeStruct((b, n, D_MODEL), F32),
        compiler_params=_cparams(("parallel", "parallel")),
        name="merge",
    )(x, ya, of, ob, proj, yc, proj, gn, wb, wo)


TOPX = PEER_TOPK + 1
TOPX_PAD = 24


def _top_values(vals, count):
    out = []
    for _ in range(count):
        m = jnp.max(vals, axis=0, keepdims=True)
        out.append(m)
        vals = jnp.where(vals == m, NEG_INF, vals)
    return out


def _route_kernel(x_ref, g_ref, wq_ref, sk_ref, hn_ref, s1_ref, e1_ref, th_ref, e0_ref):
    tb = x_ref.shape[0]
    hn = _rms(x_ref[...], g_ref[...])
    hn_t = hn.T.astype(BF16)
    hn_ref[...] = hn_t
    q_t = _dot(wq_ref[...], hn_t).astype(BF16)
    pad = jnp.full((TOPX_PAD - TOPX, tb), NEG_INF, F32)
    for h in range(PEER_HEADS):
        r0 = (2 * h) * PEER_HALF
        s0 = _dot(sk_ref[h, 0], q_t[r0:r0 + PEER_HALF])
        s1 = _dot(sk_ref[h, 1], q_t[r0 + PEER_HALF:r0 + 2 * PEER_HALF])
        top0 = _top_values(s0, TOPX)
        top1 = jnp.concatenate(_top_values(s1, TOPX) + [pad], axis=0)
        cand = jnp.concatenate([t0 + top1 for t0 in top0], axis=0)
        best = _top_values(cand, TOPX)
        thresh = 0.5 * (best[PEER_TOPK - 1] + best[PEER_TOPK])
        m0, m1 = top0[0], top1[0:1]
        z = jnp.zeros_like(m0)
        for c in best[:PEER_TOPK]:
            z = z + jnp.exp(c - best[0])
        e1 = jnp.exp(s1 - m1) / z
        th = thresh - s0
        e0 = jnp.exp(s0 - m0)
        for c in range(tb // 128):
            cs = slice(c * 128, (c + 1) * 128)
            s1_ref[h, c] = s1[:, cs]
            e1_ref[h, c] = e1[:, cs]
            th_ref[h, c] = th[:, cs]
            e0_ref[h, c] = e0[:, cs]


def peer_route(x2d, g, wq_t, subkeys, *, tb=256):
    t = x2d.shape[0]
    hk = jax.ShapeDtypeStruct((PEER_HEADS, t // 128, PEER_KEYS, 128), F32)
    hk_spec = pl.BlockSpec((PEER_HEADS, tb // 128, PEER_KEYS, 128), lambda i: (0, i, 0, 0))
    return pl.pallas_call(
        _route_kernel,
        grid=(t // tb,),
        in_specs=[
            pl.BlockSpec((tb, D_MODEL), lambda i: (i, 0)),
            pl.BlockSpec((1, D_MODEL), lambda i: (0, 0)),
            pl.BlockSpec((PEER_HEADS * PEER_QDIM, D_MODEL), lambda i: (0, 0)),
            pl.BlockSpec((PEER_HEADS, 2, PEER_KEYS, PEER_HALF), lambda i: (0, 0, 0, 0)),
        ],
        out_specs=[pl.BlockSpec((D_MODEL, tb), lambda i: (0, i)), hk_spec, hk_spec, hk_spec, hk_spec],
        out_shape=[jax.ShapeDtypeStruct((D_MODEL, t), BF16), hk, hk, hk, hk],
        compiler_params=_cparams(("parallel",)),
        name="peer_route",
    )(x2d, g, wq_t, subkeys)


DENSE_ROWS = 32


def _dense_kernel(x_ref, hn_ref, u_ref, vt_ref, s1_ref, e1_ref, th_ref, e0_ref, fg_ref, o_ref,
                  wg_ref, acc_ref, *, ne, ib, final_norm):
    e = pl.program_id(1)
    tb = hn_ref.shape[1]

    @pl.when(e == 0)
    def _():
        acc_ref[...] = jnp.zeros(acc_ref.shape, F32)

    a = _dot(u_ref[...], hn_ref[...])
    gelu_c = np.float32(np.sqrt(0.5))
    for il in range(ib):
        for tc in range(tb // 128):
            cols = slice(tc * 128, (tc + 1) * 128)
            for jc in range(PEER_KEYS // DENSE_ROWS):
                keys = slice(jc * DENSE_ROWS, (jc + 1) * DENSE_ROWS)
                rows = slice(il * PEER_KEYS + jc * DENSE_ROWS, il * PEER_KEYS + (jc + 1) * DENSE_ROWS)
                w = jnp.zeros((DENSE_ROWS, 128), F32)
                for h in range(PEER_HEADS):
                    sel = s1_ref[h, tc, keys, :] >= th_ref[h, tc, il:il + 1, :]
                    w = w + jnp.where(sel, e1_ref[h, tc, keys, :], 0.0) * e0_ref[h, tc, il:il + 1, :]
                ac = a[rows, cols]
                act = 0.5 * ac * (1.0 + lax.erf(ac * gelu_c))
                wg_ref[rows, cols] = (w * act).astype(BF16)
    acc_ref[...] += _dot(vt_ref[...], wg_ref[...])

    @pl.when(e == ne - 1)
    def _():
        y = x_ref[...] + acc_ref[...].T
        if final_norm:
            y = _rms(y, fg_ref[...])
        o_ref[...] = y


def peer_dense(x2d, hn_t, u, v_t, s1, e1, th, e0, final_g, *, final_norm, tb=512, ib=8):
    t = x2d.shape[0]
    eb = ib * PEER_KEYS
    ne = PEER_EXPERTS // eb
    kern = functools.partial(_dense_kernel, ne=ne, ib=ib, final_norm=final_norm)
    full = pl.BlockSpec((PEER_HEADS, tb // 128, PEER_KEYS, 128), lambda i, e: (0, i, 0, 0))
    part = pl.BlockSpec((PEER_HEADS, tb // 128, ib, 128), lambda i, e: (0, i, e, 0))
    return pl.pallas_call(
        kern,
        grid=(t // tb, ne),
        in_specs=[
            pl.BlockSpec((tb, D_MODEL), lambda i, e: (i, 0)),
            pl.BlockSpec((D_MODEL, tb), lambda i, e: (0, i)),
            pl.BlockSpec((eb, D_MODEL), lambda i, e: (e, 0)),
            pl.BlockSpec((D_MODEL, eb), lambda i, e: (0, e)),
            full, full, part, part,
            pl.BlockSpec((1, D_MODEL), lambda i, e: (0, 0)),
        ],
        out_specs=pl.BlockSpec((tb, D_MODEL), lambda i, e: (i, 0)),
        out_shape=jax.ShapeDtypeStruct((t, D_MODEL), F32),
        scratch_shapes=[pltpu.VMEM((eb, tb), BF16), pltpu.VMEM((D_MODEL, tb), F32)],
        compiler_params=_cparams(("parallel", "arbitrary")),
        name="peer_dense",
    )(x2d, hn_t, u, v_t, s1, e1, th, e0, final_g)


def _rel_bucket(rel):
    nb = REL_BUCKETS // 2
    max_exact = nb // 2
    n = jnp.abs(rel)
    nf = jnp.maximum(n, 1).astype(jnp.float32)
    large = max_exact + (jnp.log(nf / max_exact) / math.log(REL_MAX_DIST / max_exact)
                         * (nb - max_exact)).astype(jnp.int32)
    large = jnp.minimum(large, nb - 1)
    return jnp.where(rel > 0, nb, 0) + jnp.where(n < max_exact, n, large)


def _band_units(tq, tk):
    u = min(tq, tk)
    return u, tq // u, tk // u


def _bias_band(rel_table, tq, tk):
    u, rq, rk = _band_units(tq, tk)
    qpos = jnp.arange(tq)
    tiles = []
    for d in range(-rk - 1, rq + 2):
        rel = (d * u + jnp.arange(tk))[:, None] - qpos[None, :]
        onehot = jax.nn.one_hot(_rel_bucket(rel), REL_BUCKETS, dtype=F32)
        tiles.append(jnp.einsum("kqb,bh->hkq", onehot, rel_table.astype(F32),
                                precision=lax.Precision.HIGHEST))
    return jnp.stack(tiles) * LOG2E


def _rope_tables(n):
    rows = n // GRID_W
    row_id = jnp.repeat(jnp.arange(rows), GRID_W).astype(jnp.float32)
    col_id = (jnp.arange(rows * GRID_W) % GRID_W).astype(jnp.float32)
    sec = GQ_HEAD_DIM // 2
    inv = ROPE_THETA ** (-jnp.arange(0, sec, 2, dtype=jnp.float32) / sec)
    cr, sr = jnp.cos(row_id[:, None] * inv), jnp.sin(row_id[:, None] * inv)
    cc, sc = jnp.cos(col_id[:, None] * inv), jnp.sin(col_id[:, None] * inv)
    cos = jnp.concatenate([cr, cr, cc, cc], axis=1)
    sin = jnp.concatenate([-sr, sr, -sc, sc], axis=1)
    return jnp.concatenate([cos, cos], axis=1), jnp.concatenate([sin, sin], axis=1)


def _prep_weights(w_in, gla_gate_w, gla_gate_b, gq_qk_g, w_branch, w_out, peer_wq, peer_subkeys,
                  peer_u, peer_v):
    wi = jnp.concatenate(
        [w_in[:, :, ORIG_GZ:], w_in[:, :, :ORIG_BZ], w_in[:, :, ORIG_CQ:ORIG_GZ],
         w_in[:, :, ORIG_BZ:ORIG_CQ],
         jnp.zeros((DEPTH, D_MODEL, PROJ_W - ORIG_W), w_in.dtype)], axis=2).astype(BF16)
    hd = GLA_HEADS * GLA_DK
    gw = jnp.zeros((DEPTH, 2, 128, hd), F32)
    gw = gw.at[:, 0, 0:GLA_GATE_RANK].set(gla_gate_w[:, 0])
    gw = gw.at[:, 1, GLA_GATE_RANK:2 * GLA_GATE_RANK].set(gla_gate_w[:, 1])
    return dict(
        w_in=wi, gw=gw.astype(BF16), gb=gla_gate_b.astype(F32)[:, :, None, :],
        gq=jnp.tile(gq_qk_g[:, 0], (1, GQ_HEADS))[:, None, :],
        gk=jnp.tile(gq_qk_g[:, 1], (1, GQ_KV_HEADS))[:, None, :],
        wb=w_branch.astype(BF16), wo=w_out.astype(BF16),
        wq_t=jnp.swapaxes(peer_wq, 1, 2).astype(BF16), sk=peer_subkeys.astype(BF16),
        u=peer_u.astype(BF16), v_t=jnp.swapaxes(peer_v, 1, 2).astype(BF16))


def _encoder(x, w, rel_bias, norm1_g, da_lambda, da_subln_g, gla_norm_g, norm2_g, final_g, seg_ones):
    b, n, _ = x.shape
    t = b * n
    band = _bias_band(rel_bias, DA_TQ, DA_TK)
    cos2, sin2 = _rope_tables(n)
    for l in range(DEPTH):
        proj = in_proj(x.reshape(t, D_MODEL), norm1_g[l][None, :], w["w_in"][l]).reshape(b, n, PROJ_W)
        lam_init = 0.8 - 0.6 * math.exp(-0.3 * l)
        lp = da_lambda[l].astype(F32)
        lam = (jnp.exp(jnp.sum(lp[0] * lp[1])) - jnp.exp(jnp.sum(lp[2] * lp[3])) + lam_init).reshape(1)
        ya = diff_attn(proj, band, lam, da_subln_g[l][None, :], lam_init=lam_init, tq=DA_TQ, tk=DA_TK)
        qt, kr, vt = gqa_prep(proj, cos2, sin2, w["gq"][l], w["gk"][l], seg_ones)
        yc = gqa_attn(qt, kr, vt, tq=GQ_TQ, tk=GQ_TK)
        of = gla(proj, w["gw"][l, 0], w["gb"][l, 0], reverse=False)
        ob = gla(proj, w["gw"][l, 1], w["gb"][l, 1], reverse=True)
        x1 = merge(x, ya, of, ob, proj, yc, gla_norm_g[l][None, :], w["wb"][l], w["wo"][l])
        x1 = x1.reshape(t, D_MODEL)
        hn_t, s1, e1, th, e0 = peer_route(x1, norm2_g[l][None, :], w["wq_t"][l], w["sk"][l])
        x = peer_dense(x1, hn_t, w["u"][l], w["v_t"][l], s1, e1, th, e0, final_g[None, :],
                       final_norm=(l == DEPTH - 1)).reshape(b, n, D_MODEL)
    return x


def kernel(x_prompt, x_sample, rel_bias, norm1_g, w_in, da_lambda, da_subln_g, gla_gate_w, gla_gate_b,
           gla_norm_g, gq_qk_g, w_branch, w_out, norm2_g, peer_wq, peer_subkeys, peer_u, peer_v, final_g):
    w = _prep_weights(w_in, gla_gate_w, gla_gate_b, gq_qk_g, w_branch, w_out, peer_wq, peer_subkeys,
                      peer_u, peer_v)
    seg = np.arange(512) // GQ_HEAD_DIM
    seg_ones = jnp.asarray(seg[:, None] == seg[None, :], dtype=BF16)
    args = (w, rel_bias, norm1_g, da_lambda, da_subln_g, gla_norm_g, norm2_g, final_g, seg_ones)
    return (_encoder(x_prompt, *args), _encoder(x_sample, *args))
```

```python
import functools
import math

import numpy as np
import jax
import jax.numpy as jnp
from jax import lax
from jax.experimental import pallas as pl
from jax.experimental.pallas import tpu as pltpu

D_MODEL = 1024
DEPTH = 2
GRID_W = 64
NORM_EPS = 1e-6
N_BRANCH = 3
BRANCH_W = 512

DA_HEADS = 4
DA_HEAD_DIM = 64
DA_V_DIM = 128
REL_BUCKETS = 32
REL_MAX_DIST = 128

GLA_HEADS = 4
GLA_DK = 64
GLA_DV = 128
GLA_GATE_RANK = 16
GLA_TAU = 16.0
GLA_CHUNK = 64

GQ_HEADS = 8
GQ_KV_HEADS = 2
GQ_GROUP = GQ_HEADS // GQ_KV_HEADS
GQ_HEAD_DIM = 64
ROPE_THETA = 10000.0

PEER_HEADS = 8
PEER_KEYS = 128
PEER_EXPERTS = PEER_KEYS * PEER_KEYS
PEER_QDIM = 256
PEER_HALF = 128
PEER_TOPK = 16

COL_GZ = 0
COL_AQ, COL_AK, COL_AV = 3072, 3584, 4096
COL_BQ, COL_BK, COL_BV, COL_BR = 4608, 4864, 5120, 5632
COL_CQ, COL_CK, COL_CV = 6144, 6656, 6784
COL_BZ = 6912
PROJ_W = 7168
ORIG_BZ = 3072
ORIG_CQ = ORIG_BZ + 2 * GLA_GATE_RANK
ORIG_GZ = ORIG_CQ + 768
ORIG_W = ORIG_GZ + N_BRANCH * D_MODEL

DA_TQ, DA_TK = 1024, 512
GQ_TQ, GQ_TK = 512, 512

VMEM_LIMIT = 56 * 1024 * 1024

F32 = jnp.float32
BF16 = jnp.bfloat16
NEG_INF = float("-inf")
LOG2E = math.log2(math.e)


def _cparams(sem):
    return pltpu.CompilerParams(dimension_semantics=sem, vmem_limit_bytes=VMEM_LIMIT)


def _dot(a, b):
    return jnp.dot(a, b, preferred_element_type=F32)


def _dot_nt(a, b):
    return lax.dot_general(a, b, (((1,), (1,)), ((), ())), preferred_element_type=F32)


def _dot_tn(a, b):
    return lax.dot_general(a, b, (((0,), (0,)), ((), ())), preferred_element_type=F32)


def _rms(x, g):
    return x * lax.rsqrt(jnp.mean(x * x, axis=-1, keepdims=True) + NORM_EPS) * g


def _in_proj_kernel(x_ref, g_ref, w_ref, o_ref, hn_ref):
    @pl.when(pl.program_id(1) == 0)
    def _():
        hn_ref[...] = _rms(x_ref[...], g_ref[...]).astype(BF16)

    o_ref[...] = _dot(hn_ref[...], w_ref[...]).astype(o_ref.dtype)


def in_proj(x2d, g, w, *, tm=512, tn=1024):
    t = x2d.shape[0]
    nw = w.shape[1]
    return pl.pallas_call(
        _in_proj_kernel,
        grid=(t // tm, nw // tn),
        in_specs=[
            pl.BlockSpec((tm, D_MODEL), lambda i, j: (i, 0)),
            pl.BlockSpec((1, D_MODEL), lambda i, j: (0, 0)),
            pl.BlockSpec((D_MODEL, tn), lambda i, j: (0, j)),
        ],
        out_specs=pl.BlockSpec((tm, tn), lambda i, j: (i, j)),
        out_shape=jax.ShapeDtypeStruct((t, nw), BF16),
        scratch_shapes=[pltpu.VMEM((tm, D_MODEL), BF16)],
        compiler_params=_cparams(("parallel", "arbitrary")),
        name="in_proj",
    )(x2d, g, w)


def _flash_pipe_step(k, vt, bias, qt_ref, cur, prev, m_ref, l_ref, acc_ref):
    s_cur, p_cur, al_cur, mt_cur = cur
    s_prev, p_prev, al_prev, mt_prev = prev
    s = _dot(k, qt_ref[...])
    if bias is not None:
        s = s + jnp.concatenate([bias] * (s.shape[1] // bias.shape[1]), axis=1)
    s_cur[...] = s
    mt_cur[...] = jnp.max(s, axis=0, keepdims=True)
    acc_ref[...] = al_cur[...] * acc_ref[...] + _dot(vt, p_cur[...])
    m_prev = m_ref[...]
    m_next = jnp.maximum(m_prev, mt_prev[...])
    p = jnp.exp2(s_prev[...] - m_next)
    alpha = jnp.exp2(m_prev - m_next)
    l_ref[...] = alpha * l_ref[...] + jnp.sum(p, axis=0, keepdims=True)
    m_ref[...] = m_next
    al_prev[...] = alpha
    p_prev[...] = p.astype(BF16)


def _flash_pipe(kt, nk, build_qt, k_ref, get_vt, bias_ref, qt_ref, slots, m_ref, l_ref, lfin_ref, acc_ref):
    @pl.when(kt == 0)
    def _():
        build_qt()
        (_, p0, al0, _), (s1, _, _, mt1) = slots
        for r in (s1, mt1, p0, al0, acc_ref):
            r[...] = jnp.zeros(r.shape, r.dtype)

    @pl.when(kt <= 1)
    def _():
        m_ref[...] = jnp.full(m_ref.shape, NEG_INF, F32)
        l_ref[...] = jnp.zeros(l_ref.shape, F32)

    @pl.when(kt == 2)
    def _():
        acc_ref[...] = jnp.zeros(acc_ref.shape, F32)

    @pl.when(kt == nk + 1)
    def _():
        lfin_ref[...] = l_ref[...]

    for c in (0, 1):
        @pl.when(kt % 2 == c)
        def _(c=c):
            bias = None if bias_ref is None else bias_ref[0, 0]
            _flash_pipe_step(k_ref[0], get_vt(), bias, qt_ref, slots[c], slots[1 - c], m_ref, l_ref, acc_ref)


def _flash_scratch(tk, cols, dv):
    slot = [pltpu.VMEM((tk, cols), F32), pltpu.VMEM((tk, cols), BF16),
            pltpu.VMEM((1, cols), F32), pltpu.VMEM((1, cols), F32)]
    return slot + slot + [pltpu.VMEM((1, cols), F32), pltpu.VMEM((1, cols), F32), pltpu.VMEM((1, cols), F32),
                          pltpu.VMEM((dv, cols), F32)]


def _dattn_kernel(lam_ref, q_ref, k_ref, v_ref, b_ref, g_ref, o_ref, qt_ref,
                  s0_ref, p0_ref, al0_ref, mt0_ref, s1_ref, p1_ref, al1_ref, mt1_ref,
                  m_ref, l_ref, lfin_ref, acc_ref, *, tq, nk, out_scale):
    kt = pl.program_id(3)

    def build_qt():
        q = q_ref[0].astype(F32) * (LOG2E * DA_HEAD_DIM ** -0.5)
        lane = lax.broadcasted_iota(jnp.int32, q.shape, 1)
        qt_ref[:, 0:tq] = jnp.where(lane < DA_HEAD_DIM, q, 0.0).T.astype(BF16)
        qt_ref[:, tq:2 * tq] = jnp.where(lane >= DA_HEAD_DIM, q, 0.0).T.astype(BF16)

    slots = ((s0_ref, p0_ref, al0_ref, mt0_ref), (s1_ref, p1_ref, al1_ref, mt1_ref))
    _flash_pipe(kt, nk, build_qt, k_ref, lambda: v_ref[0].astype(F32).T.astype(BF16), b_ref, qt_ref,
                slots, m_ref, l_ref, lfin_ref, acc_ref)

    @pl.when(kt == nk + 1)
    def _():
        o = acc_ref[...] / lfin_ref[...]
        o = (o[:, 0:tq] - lam_ref[0] * o[:, tq:2 * tq]).T
        o_ref[0] = (_rms(o, g_ref[...]) * out_scale).astype(o_ref.dtype)


def diff_attn(proj, bias_band, lam, sub_g, *, lam_init, tq, tk):
    b, n, _ = proj.shape
    _, rq, rk = _band_units(tq, tk)
    nk = n // tk
    kern = functools.partial(_dattn_kernel, tq=tq, nk=nk, out_scale=1.0 - lam_init)
    tile = lambda ki, lag: jnp.clip(ki - lag, 0, nk - 1)

    def bias_map(bi, h, qi, ki):
        return (jnp.clip(rk * tile(ki, 0) - rq * qi, -rk - 1, rq + 1) + rk + 1, h, 0, 0)

    return pl.pallas_call(
        kern,
        grid=(b, DA_HEADS, n // tq, nk + 2),
        in_specs=[
            pl.BlockSpec(memory_space=pltpu.SMEM),
            pl.BlockSpec((1, tq, 128), lambda bi, h, qi, ki: (bi, qi, COL_AQ // 128 + h)),
            pl.BlockSpec((1, tk, 128), lambda bi, h, qi, ki: (bi, tile(ki, 0), COL_AK // 128 + h)),
            pl.BlockSpec((1, tk, 128), lambda bi, h, qi, ki: (bi, tile(ki, 2), COL_AV // 128 + h)),
            pl.BlockSpec((1, 1, tk, tq), bias_map),
            pl.BlockSpec((1, DA_V_DIM), lambda bi, h, qi, ki: (0, 0)),
        ],
        out_specs=pl.BlockSpec((1, tq, DA_V_DIM), lambda bi, h, qi, ki: (bi, qi, h)),
        out_shape=jax.ShapeDtypeStruct((b, n, DA_HEADS * DA_V_DIM), BF16),
        scratch_shapes=[pltpu.VMEM((128, 2 * tq), BF16)] + _flash_scratch(tk, 2 * tq, DA_V_DIM),
        compiler_params=_cparams(("parallel", "parallel", "parallel", "arbitrary")),
        name="diff_attn",
    )(lam, proj, proj, proj, bias_band, sub_g)


def _swap16(x):
    w = x.shape[1]
    lane = lax.broadcasted_iota(jnp.int32, x.shape, 1)
    return jnp.where(lane % 32 < 16, pltpu.roll(x, w - 16, 1), pltpu.roll(x, 16, 1))


def _seg_rms(x, seg_ones, g):
    sq = x * x
    hi = sq.astype(BF16)
    lo = (sq - hi.astype(F32)).astype(BF16)
    ss = _dot(hi, seg_ones) + _dot(lo, seg_ones)
    return x * lax.rsqrt(ss * (1.0 / GQ_HEAD_DIM) + NORM_EPS) * g


def _gqa_prep_kernel(q_ref, k_ref, v_ref, cos_ref, sin_ref, gq_ref, gk_ref, ones_ref,
                     qo_ref, ko_ref, vo_ref):
    cos = cos_ref[...]
    sin = sin_ref[...]
    cos4 = jnp.concatenate([cos] * 4, axis=1)
    sin4 = jnp.concatenate([sin] * 4, axis=1)
    ones = ones_ref[...]
    q = _seg_rms(q_ref[0].astype(F32), ones, gq_ref[...])
    k = _seg_rms(k_ref[0].astype(F32), ones[0:128, 0:128], gk_ref[...])
    q = (q * cos4 + _swap16(q) * sin4) * (LOG2E * GQ_HEAD_DIM ** -0.5)
    qo_ref[0] = q.T.astype(BF16)
    ko_ref[0] = (k * cos + _swap16(k) * sin).astype(BF16)
    vo_ref[0] = v_ref[0].astype(F32).T.astype(BF16)


def gqa_prep(proj, cos2, sin2, gq, gk, seg_ones, *, tm=512):
    b, n, _ = proj.shape
    return pl.pallas_call(
        _gqa_prep_kernel,
        grid=(b, n // tm),
        in_specs=[
            pl.BlockSpec((1, tm, 512), lambda bi, i: (bi, i, COL_CQ // 512)),
            pl.BlockSpec((1, tm, 128), lambda bi, i: (bi, i, COL_CK // 128)),
            pl.BlockSpec((1, tm, 128), lambda bi, i: (bi, i, COL_CV // 128)),
            pl.BlockSpec((tm, 128), lambda bi, i: (i, 0)),
            pl.BlockSpec((tm, 128), lambda bi, i: (i, 0)),
            pl.BlockSpec((1, 512), lambda bi, i: (0, 0)),
            pl.BlockSpec((1, 128), lambda bi, i: (0, 0)),
            pl.BlockSpec((512, 512), lambda bi, i: (0, 0)),
        ],
        out_specs=[
            pl.BlockSpec((1, GQ_HEADS * GQ_HEAD_DIM, tm), lambda bi, i: (bi, 0, i)),
            pl.BlockSpec((1, tm, GQ_KV_HEADS * GQ_HEAD_DIM), lambda bi, i: (bi, i, 0)),
            pl.BlockSpec((1, GQ_KV_HEADS * GQ_HEAD_DIM, tm), lambda bi, i: (bi, 0, i)),
        ],
        out_shape=[
            jax.ShapeDtypeStruct((b, GQ_HEADS * GQ_HEAD_DIM, n), BF16),
            jax.ShapeDtypeStruct((b, n, GQ_KV_HEADS * GQ_HEAD_DIM), BF16),
            jax.ShapeDtypeStruct((b, GQ_KV_HEADS * GQ_HEAD_DIM, n), BF16),
        ],
        compiler_params=_cparams(("parallel", "parallel")),
        name="gqa_prep",
    )(proj, proj, proj, cos2, sin2, gq, gk, seg_ones)


def _gqa_kernel(q_ref, k_ref, vt_ref, o_ref, qt_ref,
                s0_ref, p0_ref, al0_ref, mt0_ref, s1_ref, p1_ref, al1_ref, mt1_ref,
                m_ref, l_ref, lfin_ref, acc_ref, *, tq, nk):
    kt = pl.program_id(3)
    hd = GQ_HEAD_DIM

    def build_qt():
        qt_ref[...] = jnp.zeros(qt_ref.shape, BF16)
        base = pl.multiple_of(pl.program_id(1) * hd, hd)
        for r in range(GQ_GROUP):
            qt_ref[pl.ds(base, hd), r * tq:(r + 1) * tq] = q_ref[0, r * hd:(r + 1) * hd, :]

    slots = ((s0_ref, p0_ref, al0_ref, mt0_ref), (s1_ref, p1_ref, al1_ref, mt1_ref))
    _flash_pipe(kt, nk, build_qt, k_ref, lambda: vt_ref[0], None, qt_ref, slots, m_ref, l_ref, lfin_ref,
                acc_ref)

    @pl.when(kt == nk + 1)
    def _():
        o = (acc_ref[...] / lfin_ref[...]).astype(o_ref.dtype)
        for r in range(GQ_GROUP):
            o_ref[0, r * hd:(r + 1) * hd, :] = o[:, r * tq:(r + 1) * tq]


def gqa_attn(qt, k, vt, *, tq, tk):
    b, _, n = qt.shape
    gw = GQ_GROUP * GQ_HEAD_DIM
    kvw = GQ_KV_HEADS * GQ_HEAD_DIM
    nk = n // tk
    cols = GQ_GROUP * tq
    kern = functools.partial(_gqa_kernel, tq=tq, nk=nk)
    tile = lambda ki, lag: jnp.clip(ki - lag, 0, nk - 1)
    return pl.pallas_call(
        kern,
        grid=(b, GQ_KV_HEADS, n // tq, nk + 2),
        in_specs=[
            pl.BlockSpec((1, gw, tq), lambda bi, g, qi, ki: (bi, g, qi)),
            pl.BlockSpec((1, tk, kvw), lambda bi, g, qi, ki: (bi, tile(ki, 0), 0)),
            pl.BlockSpec((1, GQ_HEAD_DIM, tk), lambda bi, g, qi, ki: (bi, g, tile(ki, 2))),
        ],
        out_specs=pl.BlockSpec((1, gw, tq), lambda bi, g, qi, ki: (bi, g, qi)),
        out_shape=jax.ShapeDtypeStruct((b, GQ_HEADS * GQ_HEAD_DIM, n), BF16),
        scratch_shapes=[pltpu.VMEM((kvw, cols), BF16)] + _flash_scratch(tk, cols, GQ_HEAD_DIM),
        compiler_params=_cparams(("parallel", "parallel", "parallel", "arbitrary")),
        name="gqa_attn",
    )(qt, k, vt)


def _gla_kernel(q_ref, k_ref, v_ref, z_ref, gw_ref, gb_ref, o_ref, st_ref, *, reverse, nchunk):
    L = GLA_CHUNK
    hd = GLA_HEADS * GLA_DK

    @pl.when(pl.program_id(1) == 0)
    def _():
        st_ref[...] = jnp.zeros(st_ref.shape, F32)

    row = lax.broadcasted_iota(jnp.int32, (L, L), 0)
    col = lax.broadcasted_iota(jnp.int32, (L, L), 1)
    if reverse:
        tri = (col >= row).astype(BF16)
        keep = col > row
    else:
        tri = (col <= row).astype(BF16)
        keep = col <= row
    lane = lax.broadcasted_iota(jnp.int32, (L, hd), 1)

    order = range(nchunk - 1, -1, -1) if reverse else range(nchunk)
    for c in order:
        rows = pl.ds(c * L, L)
        q = q_ref[0, rows, :].astype(F32) * (GLA_DK ** -0.5)
        k = k_ref[0, rows, :].astype(F32)
        logit = _dot(z_ref[0, rows, :], gw_ref[...]) + gb_ref[...]
        lg = jax.nn.log_sigmoid(logit) * (1.0 / GLA_TAU)
        hi = lg.astype(BF16)
        lo = (lg - hi.astype(F32)).astype(BF16)
        cum = _dot(tri, hi) + _dot(tri, lo)
        last = cum[0:1, :] if reverse else cum[L - 1:L, :]
        q_dec = q * jnp.exp(cum)
        k_dec = (k * jnp.exp(-cum)).astype(BF16)
        k_last = (k * jnp.exp(last - cum)).astype(BF16)
        decay = jnp.exp(last)
        for h in range(GLA_HEADS):
            head = (lane >= h * GLA_DK) & (lane < (h + 1) * GLA_DK)
            qh = jnp.where(head, q_dec, 0.0).astype(BF16)
            vh = v_ref[0, rows, h * GLA_DV:(h + 1) * GLA_DV]
            a = jnp.where(keep, _dot_nt(qh, k_dec), 0.0)
            st = st_ref[h]
            o = _dot(a.astype(BF16), vh) + _dot_nt(qh, st.astype(BF16))
            o_ref[0, rows, h * GLA_DV:(h + 1) * GLA_DV] = o
            st_ref[h] = st * decay + _dot_tn(vh, k_last)


def gla(proj, gw, gb, *, reverse, tm=256):
    b, n, _ = proj.shape
    nb = n // tm
    kern = functools.partial(_gla_kernel, reverse=reverse, nchunk=tm // GLA_CHUNK)
    blk = (lambda i: nb - 1 - i) if reverse else (lambda i: i)
    hd = GLA_HEADS * GLA_DK
    return pl.pallas_call(
        kern,
        grid=(b, nb),
        in_specs=[
            pl.BlockSpec((1, tm, hd), lambda bi, i: (bi, blk(i), COL_BQ // hd)),
            pl.BlockSpec((1, tm, hd), lambda bi, i: (bi, blk(i), COL_BK // hd)),
            pl.BlockSpec((1, tm, 512), lambda bi, i: (bi, blk(i), COL_BV // 512)),
            pl.BlockSpec((1, tm, 128), lambda bi, i: (bi, blk(i), COL_BZ // 128)),
            pl.BlockSpec((128, hd), lambda bi, i: (0, 0)),
            pl.BlockSpec((1, hd), lambda bi, i: (0, 0)),
        ],
        out_specs=pl.BlockSpec((1, tm, GLA_HEADS * GLA_DV), lambda bi, i: (bi, blk(i), 0)),
        out_shape=jax.ShapeDtypeStruct((b, n, GLA_HEADS * GLA_DV), F32),
        scratch_shapes=[pltpu.VMEM((GLA_HEADS, GLA_DV, hd), F32)],
        compiler_params=_cparams(("parallel", "arbitrary")),
        name="gla_bwd" if reverse else "gla_fwd",
    )(proj, proj, proj, proj, gw, gb)


def _merge_kernel(x_ref, ya_ref, of_ref, ob_ref, r_ref, yc_ref, gz_ref, gn_ref, wb_ref, wo_ref,
                  o_ref):
    gn = gn_ref[...]
    o = of_ref[0] + ob_ref[0]
    r = r_ref[0].astype(F32)
    yb = []
    for h in range(GLA_HEADS):
        sl = slice(h * GLA_DV, (h + 1) * GLA_DV)
        yb.append(_rms(o[:, sl], gn) * jax.nn.silu(r[:, sl]))
    yb = jnp.concatenate(yb, axis=1).astype(BF16)

    pa = _dot(ya_ref[0], wb_ref[0])
    pb = _dot(yb, wb_ref[1])
    pc = _dot_tn(yc_ref[0], wb_ref[2])

    gz = gz_ref[0].astype(F32)
    merged = (jax.nn.sigmoid(gz[:, 0:D_MODEL]) * pa
              + jax.nn.sigmoid(gz[:, D_MODEL:2 * D_MODEL]) * pb
              + jax.nn.sigmoid(gz[:, 2 * D_MODEL:3 * D_MODEL]) * pc)
    o_ref[0] = x_ref[0] + _dot(merged.astype(BF16), wo_ref[...])


def merge(x, ya, of, ob, proj, yc, gn, wb, wo, *, tm=256):
    b, n, _ = x.shape
    tok = lambda bi, i: (bi, i, 0)
    return pl.pallas_call(
        _merge_kernel,
        grid=(b, n // tm),
        in_specs=[
            pl.BlockSpec((1, tm, D_MODEL), tok),
            pl.BlockSpec((1, tm, BRANCH_W), tok),
            pl.BlockSpec((1, tm, BRANCH_W), tok),
            pl.BlockSpec((1, tm, BRANCH_W), tok),
            pl.BlockSpec((1, tm, 512), lambda bi, i: (bi, i, COL_BR // 512)),
            pl.BlockSpec((1, BRANCH_W, tm), lambda bi, i: (bi, 0, i)),
            pl.BlockSpec((1, tm, N_BRANCH * D_MODEL), lambda bi, i: (bi, i, COL_GZ // (N_BRANCH * D_MODEL))),
            pl.BlockSpec((1, GLA_DV), lambda bi, i: (0, 0)),
            pl.BlockSpec((N_BRANCH, BRANCH_W, D_MODEL), lambda bi, i: (0, 0, 0)),
            pl.BlockSpec((D_MODEL, D_MODEL), lambda bi, i: (0, 0)),
        ],
        out_specs=pl.BlockSpec((1, tm, D_MODEL), tok),
        out_shape=jax.ShapeDtypeStruct((b, n, D_MODEL), F32),
        compiler_params=_cparams(("parallel", "parallel")),
        name="merge",
    )(x, ya, of, ob, proj, yc, proj, gn, wb, wo)


TOPX = PEER_TOPK + 1
TOPX_PAD = 24


def _top_values(vals, count):
    out = []
    for _ in range(count):
        m = jnp.max(vals, axis=0, keepdims=True)
        out.append(m)
        vals = jnp.where(vals == m, NEG_INF, vals)
    return out


def _route_kernel(x_ref, g_ref, wq_ref, sk_ref, hn_ref, s1_ref, e1_ref, th_ref, e0_ref):
    tb = x_ref.shape[0]
    hn = _rms(x_ref[...], g_ref[...])
    hn_t = hn.T.astype(BF16)
    hn_ref[...] = hn_t
    q_t = _dot(wq_ref[...], hn_t).astype(BF16)
    pad = jnp.full((TOPX_PAD - TOPX, tb), NEG_INF, F32)
    for h in range(PEER_HEADS):
        r0 = (2 * h) * PEER_HALF
        s0 = _dot(sk_ref[h, 0], q_t[r0:r0 + PEER_HALF])
        s1 = _dot(sk_ref[h, 1], q_t[r0 + PEER_HALF:r0 + 2 * PEER_HALF])
        top0 = _top_values(s0, TOPX)
        top1 = jnp.concatenate(_top_values(s1, TOPX) + [pad], axis=0)
        cand = jnp.concatenate([t0 + top1 for t0 in top0], axis=0)
        best = _top_values(cand, TOPX)
        thresh = 0.5 * (best[PEER_TOPK - 1] + best[PEER_TOPK])
        m0, m1 = top0[0], top1[0:1]
        z = jnp.zeros_like(m0)
        for c in best[:PEER_TOPK]:
            z = z + jnp.exp(c - best[0])
        e1 = jnp.exp(s1 - m1) / z
        th = thresh - s0
        e0 = jnp.exp(s0 - m0)
        for c in range(tb // 128):
            cs = slice(c * 128, (c + 1) * 128)
            s1_ref[h, c] = s1[:, cs]
            e1_ref[h, c] = e1[:, cs]
            th_ref[h, c] = th[:, cs]
            e0_ref[h, c] = e0[:, cs]


def peer_route(x2d, g, wq_t, subkeys, *, tb=256):
    t = x2d.shape[0]
    hk = jax.ShapeDtypeStruct((PEER_HEADS, t // 128, PEER_KEYS, 128), F32)
    hk_spec = pl.BlockSpec((PEER_HEADS, tb // 128, PEER_KEYS, 128), lambda i: (0, i, 0, 0))
    return pl.pallas_call(
        _route_kernel,
        grid=(t // tb,),
        in_specs=[
            pl.BlockSpec((tb, D_MODEL), lambda i: (i, 0)),
            pl.BlockSpec((1, D_MODEL), lambda i: (0, 0)),
            pl.BlockSpec((PEER_HEADS * PEER_QDIM, D_MODEL), lambda i: (0, 0)),
            pl.BlockSpec((PEER_HEADS, 2, PEER_KEYS, PEER_HALF), lambda i: (0, 0, 0, 0)),
        ],
        out_specs=[pl.BlockSpec((D_MODEL, tb), lambda i: (0, i)), hk_spec, hk_spec, hk_spec, hk_spec],
        out_shape=[jax.ShapeDtypeStruct((D_MODEL, t), BF16), hk, hk, hk, hk],
        compiler_params=_cparams(("parallel",)),
        name="peer_route",
    )(x2d, g, wq_t, subkeys)


DENSE_ROWS = 32


def _dense_kernel(x_ref, hn_ref, u_ref, vt_ref, s1_ref, e1_ref, th_ref, e0_ref, fg_ref, o_ref,
                  wg_ref, acc_ref, *, ne, ib, final_norm):
    e = pl.program_id(1)
    tb = hn_ref.shape[1]

    @pl.when(e == 0)
    def _():
        acc_ref[...] = jnp.zeros(acc_ref.shape, F32)

    a = _dot(u_ref[...], hn_ref[...])
    gelu_c = np.float32(np.sqrt(0.5))
    for il in range(ib):
        for tc in range(tb // 128):
            cols = slice(tc * 128, (tc + 1) * 128)
            for jc in range(PEER_KEYS // DENSE_ROWS):
                keys = slice(jc * DENSE_ROWS, (jc + 1) * DENSE_ROWS)
                rows = slice(il * PEER_KEYS + jc * DENSE_ROWS, il * PEER_KEYS + (jc + 1) * DENSE_ROWS)
                w = jnp.zeros((DENSE_ROWS, 128), F32)
                for h in range(PEER_HEADS):
                    sel = s1_ref[h, tc, keys, :] >= th_ref[h, tc, il:il + 1, :]
                    w = w + jnp.where(sel, e1_ref[h, tc, keys, :], 0.0) * e0_ref[h, tc, il:il + 1, :]
                ac = a[rows, cols]
                act = 0.5 * ac * (1.0 + lax.erf(ac * gelu_c))
                wg_ref[rows, cols] = (w * act).astype(BF16)
    acc_ref[...] += _dot(vt_ref[...], wg_ref[...])

    @pl.when(e == ne - 1)
    def _():
        y = x_ref[...] + acc_ref[...].T
        if final_norm:
            y = _rms(y, fg_ref[...])
        o_ref[...] = y


def peer_dense(x2d, hn_t, u, v_t, s1, e1, th, e0, final_g, *, final_norm, tb=512, ib=8):
    t = x2d.shape[0]
    eb = ib * PEER_KEYS
    ne = PEER_EXPERTS // eb
    kern = functools.partial(_dense_kernel, ne=ne, ib=ib, final_norm=final_norm)
    full = pl.BlockSpec((PEER_HEADS, tb // 128, PEER_KEYS, 128), lambda i, e: (0, i, 0, 0))
    part = pl.BlockSpec((PEER_HEADS, tb // 128, ib, 128), lambda i, e: (0, i, e, 0))
    return pl.pallas_call(
        kern,
        grid=(t // tb, ne),
        in_specs=[
            pl.BlockSpec((tb, D_MODEL), lambda i, e: (i, 0)),
            pl.BlockSpec((D_MODEL, tb), lambda i, e: (0, i)),
            pl.BlockSpec((eb, D_MODEL), lambda i, e: (e, 0)),
            pl.BlockSpec((D_MODEL, eb), lambda i, e: (0, e)),
            full, full, part, part,
            pl.BlockSpec((1, D_MODEL), lambda i, e: (0, 0)),
        ],
        out_specs=pl.BlockSpec((tb, D_MODEL), lambda i, e: (i, 0)),
        out_shape=jax.ShapeDtypeStruct((t, D_MODEL), F32),
        scratch_shapes=[pltpu.VMEM((eb, tb), BF16), pltpu.VMEM((D_MODEL, tb), F32)],
        compiler_params=_cparams(("parallel", "arbitrary")),
        name="peer_dense",
    )(x2d, hn_t, u, v_t, s1, e1, th, e0, final_g)


def _rel_bucket(rel):
    nb = REL_BUCKETS // 2
    max_exact = nb // 2
    n = jnp.abs(rel)
    nf = jnp.maximum(n, 1).astype(jnp.float32)
    large = max_exact + (jnp.log(nf / max_exact) / math.log(REL_MAX_DIST / max_exact)
                         * (nb - max_exact)).astype(jnp.int32)
    large = jnp.minimum(large, nb - 1)
    return jnp.where(rel > 0, nb, 0) + jnp.where(n < max_exact, n, large)


def _band_units(tq, tk):
    u = min(tq, tk)
    return u, tq // u, tk // u


def _bias_band(rel_table, tq, tk):
    u, rq, rk = _band_units(tq, tk)
    qpos = jnp.arange(tq)
    tiles = []
    for d in range(-rk - 1, rq + 2):
        rel = (d * u + jnp.arange(tk))[:, None] - qpos[None, :]
        onehot = jax.nn.one_hot(_rel_bucket(rel), REL_BUCKETS, dtype=F32)
        tiles.append(jnp.einsum("kqb,bh->hkq", onehot, rel_table.astype(F32),
                                precision=lax.Precision.HIGHEST))
    return jnp.stack(tiles) * LOG2E


def _rope_tables(n):
    rows = n // GRID_W
    row_id = jnp.repeat(jnp.arange(rows), GRID_W).astype(jnp.float32)
    col_id = (jnp.arange(rows * GRID_W) % GRID_W).astype(jnp.float32)
    sec = GQ_HEAD_DIM // 2
    inv = ROPE_THETA ** (-jnp.arange(0, sec, 2, dtype=jnp.float32) / sec)
    cr, sr = jnp.cos(row_id[:, None] * inv), jnp.sin(row_id[:, None] * inv)
    cc, sc = jnp.cos(col_id[:, None] * inv), jnp.sin(col_id[:, None] * inv)
    cos = jnp.concatenate([cr, cr, cc, cc], axis=1)
    sin = jnp.concatenate([-sr, sr, -sc, sc], axis=1)
    return jnp.concatenate([cos, cos], axis=1), jnp.concatenate([sin, sin], axis=1)


def _prep_weights(w_in, gla_gate_w, gla_gate_b, gq_qk_g, w_branch, w_out, peer_wq, peer_subkeys,
                  peer_u, peer_v):
    wi = jnp.concatenate(
        [w_in[:, :, ORIG_GZ:], w_in[:, :, :ORIG_BZ], w_in[:, :, ORIG_CQ:ORIG_GZ],
         w_in[:, :, ORIG_BZ:ORIG_CQ],
         jnp.zeros((DEPTH, D_MODEL, PROJ_W - ORIG_W), w_in.dtype)], axis=2).astype(BF16)
    hd = GLA_HEADS * GLA_DK
    gw = jnp.zeros((DEPTH, 2, 128, hd), F32)
    gw = gw.at[:, 0, 0:GLA_GATE_RANK].set(gla_gate_w[:, 0])
    gw = gw.at[:, 1, GLA_GATE_RANK:2 * GLA_GATE_RANK].set(gla_gate_w[:, 1])
    return dict(
        w_in=wi, gw=gw.astype(BF16), gb=gla_gate_b.astype(F32)[:, :, None, :],
        gq=jnp.tile(gq_qk_g[:, 0], (1, GQ_HEADS))[:, None, :],
        gk=jnp.tile(gq_qk_g[:, 1], (1, GQ_KV_HEADS))[:, None, :],
        wb=w_branch.astype(BF16), wo=w_out.astype(BF16),
        wq_t=jnp.swapaxes(peer_wq, 1, 2).astype(BF16), sk=peer_subkeys.astype(BF16),
        u=peer_u.astype(BF16), v_t=jnp.swapaxes(peer_v, 1, 2).astype(BF16))


def _encoder(x, w, rel_bias, norm1_g, da_lambda, da_subln_g, gla_norm_g, norm2_g, final_g, seg_ones):
    b, n, _ = x.shape
    t = b * n
    band = _bias_band(rel_bias, DA_TQ, DA_TK)
    cos2, sin2 = _rope_tables(n)
    for l in range(DEPTH):
        proj = in_proj(x.reshape(t, D_MODEL), norm1_g[l][None, :], w["w_in"][l]).reshape(b, n, PROJ_W)
        lam_init = 0.8 - 0.6 * math.exp(-0.3 * l)
        lp = da_lambda[l].astype(F32)
        lam = (jnp.exp(jnp.sum(lp[0] * lp[1])) - jnp.exp(jnp.sum(lp[2] * lp[3])) + lam_init).reshape(1)
        ya = diff_attn(proj, band, lam, da_subln_g[l][None, :], lam_init=lam_init, tq=DA_TQ, tk=DA_TK)
        qt, kr, vt = gqa_prep(proj, cos2, sin2, w["gq"][l], w["gk"][l], seg_ones)
        yc = gqa_attn(qt, kr, vt, tq=GQ_TQ, tk=GQ_TK)
        of = gla(proj, w["gw"][l, 0], w["gb"][l, 0], reverse=False)
        ob = gla(proj, w["gw"][l, 1], w["gb"][l, 1], reverse=True)
        x1 = merge(x, ya, of, ob, proj, yc, gla_norm_g[l][None, :], w["wb"][l], w["wo"][l])
        x1 = x1.reshape(t, D_MODEL)
        hn_t, s1, e1, th, e0 = peer_route(x1, norm2_g[l][None, :], w["wq_t"][l], w["sk"][l])
        x = peer_dense(x1, hn_t, w["u"][l], w["v_t"][l], s1, e1, th, e0, final_g[None, :],
                       final_norm=(l == DEPTH - 1)).reshape(b, n, D_MODEL)
    return x


def kernel(x_prompt, x_sample, rel_bias, norm1_g, w_in, da_lambda, da_subln_g, gla_gate_w, gla_gate_b,
           gla_norm_g, gq_qk_g, w_branch, w_out, norm2_g, peer_wq, peer_subkeys, peer_u, peer_v, final_g):
    w = _prep_weights(w_in, gla_gate_w, gla_gate_b, gq_qk_g, w_branch, w_out, peer_wq, peer_subkeys,
                      peer_u, peer_v)
    seg = np.arange(512) // GQ_HEAD_DIM
    seg_ones = jnp.asarray(seg[:, None] == seg[None, :], dtype=BF16)
    args = (w, rel_bias, norm1_g, da_lambda, da_subln_g, gla_norm_g, norm2_g, final_g, seg_ones)
    return (_encoder(x_prompt, *args), _encoder(x_sample, *args))
```

```python
import functools
import math

import numpy as np
import jax
import jax.numpy as jnp
from jax import lax
from jax.experimental import pallas as pl
from jax.experimental.pallas import tpu as pltpu

D_MODEL = 1024
DEPTH = 2
GRID_W = 64
NORM_EPS = 1e-6
N_BRANCH = 3
BRANCH_W = 512

DA_HEADS = 4
DA_HEAD_DIM = 64
DA_V_DIM = 128
REL_BUCKETS = 32
REL_MAX_DIST = 128

GLA_HEADS = 4
GLA_DK = 64
GLA_DV = 128
GLA_GATE_RANK = 16
GLA_TAU = 16.0
GLA_CHUNK = 64

GQ_HEADS = 8
GQ_KV_HEADS = 2
GQ_GROUP = GQ_HEADS // GQ_KV_HEADS
GQ_HEAD_DIM = 64
ROPE_THETA = 10000.0

PEER_HEADS = 8
PEER_KEYS = 128
PEER_EXPERTS = PEER_KEYS * PEER_KEYS
PEER_QDIM = 256
PEER_HALF = 128
PEER_TOPK = 16

COL_GZ = 0
COL_AQ, COL_AK, COL_AV = 3072, 3584, 4096
COL_BQ, COL_BK, COL_BV, COL_BR = 4608, 4864, 5120, 5632
COL_CQ, COL_CK, COL_CV = 6144, 6656, 6784
COL_BZ = 6912
PROJ_W = 7168
ORIG_BZ = 3072
ORIG_CQ = ORIG_BZ + 2 * GLA_GATE_RANK
ORIG_GZ = ORIG_CQ + 768
ORIG_W = ORIG_GZ + N_BRANCH * D_MODEL

DA_TQ, DA_TK = 1024, 512
GQ_TQ, GQ_TK = 512, 512

VMEM_LIMIT = 56 * 1024 * 1024

F32 = jnp.float32
BF16 = jnp.bfloat16
NEG_INF = float("-inf")
LOG2E = math.log2(math.e)


def _cparams(sem):
    return pltpu.CompilerParams(dimension_semantics=sem, vmem_limit_bytes=VMEM_LIMIT)


def _dot(a, b):
    return jnp.dot(a, b, preferred_element_type=F32)


def _dot_nt(a, b):
    return lax.dot_general(a, b, (((1,), (1,)), ((), ())), preferred_element_type=F32)


def _dot_tn(a, b):
    return lax.dot_general(a, b, (((0,), (0,)), ((), ())), preferred_element_type=F32)


def _rms(x, g):
    return x * lax.rsqrt(jnp.mean(x * x, axis=-1, keepdims=True) + NORM_EPS) * g


def _in_proj_kernel(x_ref, g_ref, w_ref, o_ref, hn_ref):
    @pl.when(pl.program_id(1) == 0)
    def _():
        hn_ref[...] = _rms(x_ref[...], g_ref[...]).astype(BF16)

    o_ref[...] = _dot(hn_ref[...], w_ref[...]).astype(o_ref.dtype)


def in_proj(x2d, g, w, *, tm=512, tn=1024):
    t = x2d.shape[0]
    nw = w.shape[1]
    return pl.pallas_call(
        _in_proj_kernel,
        grid=(t // tm, nw // tn),
        in_specs=[
            pl.BlockSpec((tm, D_MODEL), lambda i, j: (i, 0)),
            pl.BlockSpec((1, D_MODEL), lambda i, j: (0, 0)),
            pl.BlockSpec((D_MODEL, tn), lambda i, j: (0, j)),
        ],
        out_specs=pl.BlockSpec((tm, tn), lambda i, j: (i, j)),
        out_shape=jax.ShapeDtypeStruct((t, nw), BF16),
        scratch_shapes=[pltpu.VMEM((tm, D_MODEL), BF16)],
        compiler_params=_cparams(("parallel", "arbitrary")),
        name="in_proj",
    )(x2d, g, w)


SUM_ROWS = 16


def _flash_pipe_step(k, vt, bias, qt_ref, cur, prev, m_ref, acc_ref):
    s_cur, p_cur, al_cur, mt_cur = cur
    s_prev, p_prev, al_prev, mt_prev = prev
    s = _dot(k, qt_ref[...])
    if bias is not None:
        s = s + jnp.concatenate([bias] * (s.shape[1] // bias.shape[1]), axis=1)
    s_cur[...] = s
    mt_cur[...] = jnp.max(s, axis=0, keepdims=True)
    acc_ref[...] = al_cur[...] * acc_ref[...] + _dot(vt, p_cur[...])
    m_prev = m_ref[...]
    m_next = jnp.maximum(m_prev, mt_prev[...])
    p_prev[...] = jnp.exp2(s_prev[...] - m_next).astype(BF16)
    al_prev[...] = jnp.exp2(m_prev - m_next)
    m_ref[...] = m_next


def _flash_pipe(kt, nk, build_qt, k_ref, get_vt, bias_ref, qt_ref, slots, m_ref, acc_ref):
    @pl.when(kt == 0)
    def _():
        build_qt()
        (_, p0, al0, _), (s1, _, _, mt1) = slots
        for r in (s1, mt1, p0, al0, acc_ref):
            r[...] = jnp.zeros(r.shape, r.dtype)

    @pl.when(kt <= 1)
    def _():
        m_ref[...] = jnp.full(m_ref.shape, NEG_INF, F32)

    @pl.when(kt == 2)
    def _():
        acc_ref[...] = jnp.zeros(acc_ref.shape, F32)

    for c in (0, 1):
        @pl.when(kt % 2 == c)
        def _(c=c):
            bias = None if bias_ref is None else bias_ref[0, 0]
            _flash_pipe_step(k_ref[0], get_vt(), bias, qt_ref, slots[c], slots[1 - c], m_ref, acc_ref)


def _flash_scratch(tk, cols, dv):
    slot = [pltpu.VMEM((tk, cols), F32), pltpu.VMEM((tk, cols), BF16),
            pltpu.VMEM((1, cols), F32), pltpu.VMEM((1, cols), F32)]
    return slot + slot + [pltpu.VMEM((1, cols), F32), pltpu.VMEM((dv + SUM_ROWS, cols), F32)]


def _dattn_kernel(lam_ref, q_ref, k_ref, v_ref, b_ref, g_ref, o_ref, qt_ref,
                  s0_ref, p0_ref, al0_ref, mt0_ref, s1_ref, p1_ref, al1_ref, mt1_ref,
                  m_ref, acc_ref, *, tq, nk, out_scale):
    kt = pl.program_id(3)

    def build_qt():
        q = q_ref[0].astype(F32) * (LOG2E * DA_HEAD_DIM ** -0.5)
        lane = lax.broadcasted_iota(jnp.int32, q.shape, 1)
        qt_ref[:, 0:tq] = jnp.where(lane < DA_HEAD_DIM, q, 0.0).T.astype(BF16)
        qt_ref[:, tq:2 * tq] = jnp.where(lane >= DA_HEAD_DIM, q, 0.0).T.astype(BF16)

    def get_vt():
        vt = v_ref[0].astype(F32).T.astype(BF16)
        return jnp.concatenate([vt, jnp.ones((SUM_ROWS, vt.shape[1]), BF16)], axis=0)

    slots = ((s0_ref, p0_ref, al0_ref, mt0_ref), (s1_ref, p1_ref, al1_ref, mt1_ref))
    _flash_pipe(kt, nk, build_qt, k_ref, get_vt, b_ref, qt_ref, slots, m_ref, acc_ref)

    @pl.when(kt == nk + 1)
    def _():
        o = acc_ref[0:DA_V_DIM, :] / acc_ref[DA_V_DIM:DA_V_DIM + 1, :]
        o = (o[:, 0:tq] - lam_ref[0] * o[:, tq:2 * tq]).T
        o_ref[0] = (_rms(o, g_ref[...]) * out_scale).astype(o_ref.dtype)


def diff_attn(proj, bias_band, lam, sub_g, *, lam_init, tq, tk):
    b, n, _ = proj.shape
    _, rq, rk = _band_units(tq, tk)
    nk = n // tk
    kern = functools.partial(_dattn_kernel, tq=tq, nk=nk, out_scale=1.0 - lam_init)
    tile = lambda ki, lag: jnp.clip(ki - lag, 0, nk - 1)

    def bias_map(bi, h, qi, ki):
        return (jnp.clip(rk * tile(ki, 0) - rq * qi, -rk - 1, rq + 1) + rk + 1, h, 0, 0)

    return pl.pallas_call(
        kern,
        grid=(b, DA_HEADS, n // tq, nk + 2),
        in_specs=[
            pl.BlockSpec(memory_space=pltpu.SMEM),
            pl.BlockSpec((1, tq, 128), lambda bi, h, qi, ki: (bi, qi, COL_AQ // 128 + h)),
            pl.BlockSpec((1, tk, 128), lambda bi, h, qi, ki: (bi, tile(ki, 0), COL_AK // 128 + h)),
            pl.BlockSpec((1, tk, 128), lambda bi, h, qi, ki: (bi, tile(ki, 2), COL_AV // 128 + h)),
            pl.BlockSpec((1, 1, tk, tq), bias_map),
            pl.BlockSpec((1, DA_V_DIM), lambda bi, h, qi, ki: (0, 0)),
        ],
        out_specs=pl.BlockSpec((1, tq, DA_V_DIM), lambda bi, h, qi, ki: (bi, qi, h)),
        out_shape=jax.ShapeDtypeStruct((b, n, DA_HEADS * DA_V_DIM), BF16),
        scratch_shapes=[pltpu.VMEM((128, 2 * tq), BF16)] + _flash_scratch(tk, 2 * tq, DA_V_DIM),
        compiler_params=_cparams(("parallel", "parallel", "parallel", "arbitrary")),
        name="diff_attn",
    )(lam, proj, proj, proj, bias_band, sub_g)


def _swap16(x):
    w = x.shape[1]
    lane = lax.broadcasted_iota(jnp.int32, x.shape, 1)
    return jnp.where(lane % 32 < 16, pltpu.roll(x, w - 16, 1), pltpu.roll(x, 16, 1))


def _seg_rms(x, seg_ones, g):
    sq = x * x
    hi = sq.astype(BF16)
    lo = (sq - hi.astype(F32)).astype(BF16)
    ss = _dot(hi, seg_ones) + _dot(lo, seg_ones)
    return x * lax.rsqrt(ss * (1.0 / GQ_HEAD_DIM) + NORM_EPS) * g


def _gqa_prep_kernel(q_ref, k_ref, v_ref, cos_ref, sin_ref, gq_ref, gk_ref, ones_ref,
                     qo_ref, ko_ref, vo_ref):
    cos = cos_ref[...]
    sin = sin_ref[...]
    cos4 = jnp.concatenate([cos] * 4, axis=1)
    sin4 = jnp.concatenate([sin] * 4, axis=1)
    ones = ones_ref[...]
    q = _seg_rms(q_ref[0].astype(F32), ones, gq_ref[...])
    k = _seg_rms(k_ref[0].astype(F32), ones[0:128, 0:128], gk_ref[...])
    q = (q * cos4 + _swap16(q) * sin4) * (LOG2E * GQ_HEAD_DIM ** -0.5)
    qo_ref[0] = q.T.astype(BF16)
    ko_ref[0] = (k * cos + _swap16(k) * sin).astype(BF16)
    vt = v_ref[0].astype(F32).T.astype(BF16)
    hd = GQ_HEAD_DIM
    ones = jnp.ones((SUM_ROWS, vt.shape[1]), BF16)
    for h in range(GQ_KV_HEADS):
        vo_ref[0, h, 0:hd, :] = vt[h * hd:(h + 1) * hd]
        vo_ref[0, h, hd:hd + SUM_ROWS, :] = ones


def gqa_prep(proj, cos2, sin2, gq, gk, seg_ones, *, tm=512):
    b, n, _ = proj.shape
    return pl.pallas_call(
        _gqa_prep_kernel,
        grid=(b, n // tm),
        in_specs=[
            pl.BlockSpec((1, tm, 512), lambda bi, i: (bi, i, COL_CQ // 512)),
            pl.BlockSpec((1, tm, 128), lambda bi, i: (bi, i, COL_CK // 128)),
            pl.BlockSpec((1, tm, 128), lambda bi, i: (bi, i, COL_CV // 128)),
            pl.BlockSpec((tm, 128), lambda bi, i: (i, 0)),
            pl.BlockSpec((tm, 128), lambda bi, i: (i, 0)),
            pl.BlockSpec((1, 512), lambda bi, i: (0, 0)),
            pl.BlockSpec((1, 128), lambda bi, i: (0, 0)),
            pl.BlockSpec((512, 512), lambda bi, i: (0, 0)),
        ],
        out_specs=[
            pl.BlockSpec((1, GQ_HEADS * GQ_HEAD_DIM, tm), lambda bi, i: (bi, 0, i)),
            pl.BlockSpec((1, tm, GQ_KV_HEADS * GQ_HEAD_DIM), lambda bi, i: (bi, i, 0)),
            pl.BlockSpec((1, GQ_KV_HEADS, GQ_HEAD_DIM + SUM_ROWS, tm), lambda bi, i: (bi, 0, 0, i)),
        ],
        out_shape=[
            jax.ShapeDtypeStruct((b, GQ_HEADS * GQ_HEAD_DIM, n), BF16),
            jax.ShapeDtypeStruct((b, n, GQ_KV_HEADS * GQ_HEAD_DIM), BF16),
            jax.ShapeDtypeStruct((b, GQ_KV_HEADS, GQ_HEAD_DIM + SUM_ROWS, n), BF16),
        ],
        compiler_params=_cparams(("parallel", "parallel")),
        name="gqa_prep",
    )(proj, proj, proj, cos2, sin2, gq, gk, seg_ones)


def _gqa_kernel(q_ref, k_ref, vt_ref, o_ref, qt_ref,
                s0_ref, p0_ref, al0_ref, mt0_ref, s1_ref, p1_ref, al1_ref, mt1_ref,
                m_ref, acc_ref, *, tq, nk):
    kt = pl.program_id(3)
    hd = GQ_HEAD_DIM

    def build_qt():
        qt_ref[...] = jnp.zeros(qt_ref.shape, BF16)
        base = pl.multiple_of(pl.program_id(1) * hd, hd)
        for r in range(GQ_GROUP):
            qt_ref[pl.ds(base, hd), r * tq:(r + 1) * tq] = q_ref[0, r * hd:(r + 1) * hd, :]

    slots = ((s0_ref, p0_ref, al0_ref, mt0_ref), (s1_ref, p1_ref, al1_ref, mt1_ref))
    _flash_pipe(kt, nk, build_qt, k_ref, lambda: vt_ref[0, 0], None, qt_ref, slots, m_ref, acc_ref)

    @pl.when(kt == nk + 1)
    def _():
        o = (acc_ref[0:hd, :] / acc_ref[hd:hd + 1, :]).astype(o_ref.dtype)
        for r in range(GQ_GROUP):
            o_ref[0, r * hd:(r + 1) * hd, :] = o[:, r * tq:(r + 1) * tq]


def gqa_attn(qt, k, vt, *, tq, tk):
    b, _, n = qt.shape
    gw = GQ_GROUP * GQ_HEAD_DIM
    kvw = GQ_KV_HEADS * GQ_HEAD_DIM
    nk = n // tk
    cols = GQ_GROUP * tq
    kern = functools.partial(_gqa_kernel, tq=tq, nk=nk)
    tile = lambda ki, lag: jnp.clip(ki - lag, 0, nk - 1)
    return pl.pallas_call(
        kern,
        grid=(b, GQ_KV_HEADS, n // tq, nk + 2),
        in_specs=[
            pl.BlockSpec((1, gw, tq), lambda bi, g, qi, ki: (bi, g, qi)),
            pl.BlockSpec((1, tk, kvw), lambda bi, g, qi, ki: (bi, tile(ki, 0), 0)),
            pl.BlockSpec((1, 1, GQ_HEAD_DIM + SUM_ROWS, tk), lambda bi, g, qi, ki: (bi, g, 0, tile(ki, 2))),
        ],
        out_specs=pl.BlockSpec((1, gw, tq), lambda bi, g, qi, ki: (bi, g, qi)),
        out_shape=jax.ShapeDtypeStruct((b, GQ_HEADS * GQ_HEAD_DIM, n), BF16),
        scratch_shapes=[pltpu.VMEM((kvw, cols), BF16)] + _flash_scratch(tk, cols, GQ_HEAD_DIM),
        compiler_params=_cparams(("parallel", "parallel", "parallel", "arbitrary")),
        name="gqa_attn",
    )(qt, k, vt)


def _gla_kernel(q_ref, k_ref, v_ref, z_ref, gw_ref, gb_ref, o_ref, st_ref, *, reverse, nchunk):
    L = GLA_CHUNK
    hd = GLA_HEADS * GLA_DK

    @pl.when(pl.program_id(1) == 0)
    def _():
        st_ref[...] = jnp.zeros(st_ref.shape, F32)

    row = lax.broadcasted_iota(jnp.int32, (L, L), 0)
    col = lax.broadcasted_iota(jnp.int32, (L, L), 1)
    if reverse:
        tri = (col >= row).astype(BF16)
        keep = col > row
    else:
        tri = (col <= row).astype(BF16)
        keep = col <= row
    lane = lax.broadcasted_iota(jnp.int32, (L, hd), 1)

    order = range(nchunk - 1, -1, -1) if reverse else range(nchunk)
    for c in order:
        rows = pl.ds(c * L, L)
        q = q_ref[0, rows, :].astype(F32) * (GLA_DK ** -0.5)
        k = k_ref[0, rows, :].astype(F32)
        logit = _dot(z_ref[0, rows, :], gw_ref[...]) + gb_ref[...]
        lg = jax.nn.log_sigmoid(logit) * (1.0 / GLA_TAU)
        hi = lg.astype(BF16)
        lo = (lg - hi.astype(F32)).astype(BF16)
        cum = _dot(tri, hi) + _dot(tri, lo)
        last = cum[0:1, :] if reverse else cum[L - 1:L, :]
        q_dec = q * jnp.exp(cum)
        k_dec = (k * jnp.exp(-cum)).astype(BF16)
        k_last = (k * jnp.exp(last - cum)).astype(BF16)
        decay = jnp.exp(last)
        for h in range(GLA_HEADS):
            head = (lane >= h * GLA_DK) & (lane < (h + 1) * GLA_DK)
            qh = jnp.where(head, q_dec, 0.0).astype(BF16)
            vh = v_ref[0, rows, h * GLA_DV:(h + 1) * GLA_DV]
            a = jnp.where(keep, _dot_nt(qh, k_dec), 0.0)
            st = st_ref[h]
            o = _dot(a.astype(BF16), vh) + _dot_nt(qh, st.astype(BF16))
            o_ref[0, rows, h * GLA_DV:(h + 1) * GLA_DV] = o
            st_ref[h] = st * decay + _dot_tn(vh, k_last)


def gla(proj, gw, gb, *, reverse, tm=256):
    b, n, _ = proj.shape
    nb = n // tm
    kern = functools.partial(_gla_kernel, reverse=reverse, nchunk=tm // GLA_CHUNK)
    blk = (lambda i: nb - 1 - i) if reverse else (lambda i: i)
    hd = GLA_HEADS * GLA_DK
    return pl.pallas_call(
        kern,
        grid=(b, nb),
        in_specs=[
            pl.BlockSpec((1, tm, hd), lambda bi, i: (bi, blk(i), COL_BQ // hd)),
            pl.BlockSpec((1, tm, hd), lambda bi, i: (bi, blk(i), COL_BK // hd)),
            pl.BlockSpec((1, tm, 512), lambda bi, i: (bi, blk(i), COL_BV // 512)),
            pl.BlockSpec((1, tm, 128), lambda bi, i: (bi, blk(i), COL_BZ // 128)),
            pl.BlockSpec((128, hd), lambda bi, i: (0, 0)),
            pl.BlockSpec((1, hd), lambda bi, i: (0, 0)),
        ],
        out_specs=pl.BlockSpec((1, tm, GLA_HEADS * GLA_DV), lambda bi, i: (bi, blk(i), 0)),
        out_shape=jax.ShapeDtypeStruct((b, n, GLA_HEADS * GLA_DV), F32),
        scratch_shapes=[pltpu.VMEM((GLA_HEADS, GLA_DV, hd), F32)],
        compiler_params=_cparams(("parallel", "arbitrary")),
        name="gla_bwd" if reverse else "gla_fwd",
    )(proj, proj, proj, proj, gw, gb)


def _merge_kernel(x_ref, ya_ref, of_ref, ob_ref, r_ref, yc_ref, gz_ref, gn_ref, wb_ref, wo_ref,
                  o_ref):
    gn = gn_ref[...]
    o = of_ref[0] + ob_ref[0]
    r = r_ref[0].astype(F32)
    yb = []
    for h in range(GLA_HEADS):
        sl = slice(h * GLA_DV, (h + 1) * GLA_DV)
        yb.append(_rms(o[:, sl], gn) * jax.nn.silu(r[:, sl]))
    yb = jnp.concatenate(yb, axis=1).astype(BF16)

    pa = _dot(ya_ref[0], wb_ref[0])
    pb = _dot(yb, wb_ref[1])
    pc = _dot_tn(yc_ref[0], wb_ref[2])

    gz = gz_ref[0].astype(F32)
    merged = (jax.nn.sigmoid(gz[:, 0:D_MODEL]) * pa
              + jax.nn.sigmoid(gz[:, D_MODEL:2 * D_MODEL]) * pb
              + jax.nn.sigmoid(gz[:, 2 * D_MODEL:3 * D_MODEL]) * pc)
    o_ref[0] = x_ref[0] + _dot(merged.astype(BF16), wo_ref[...])


def merge(x, ya, of, ob, proj, yc, gn, wb, wo, *, tm=256):
    b, n, _ = x.shape
    tok = lambda bi, i: (bi, i, 0)
    return pl.pallas_call(
        _merge_kernel,
        grid=(b, n // tm),
        in_specs=[
            pl.BlockSpec((1, tm, D_MODEL), tok),
            pl.BlockSpec((1, tm, BRANCH_W), tok),
            pl.BlockSpec((1, tm, BRANCH_W), tok),
            pl.BlockSpec((1, tm, BRANCH_W), tok),
            pl.BlockSpec((1, tm, 512), lambda bi, i: (bi, i, COL_BR // 512)),
            pl.BlockSpec((1, BRANCH_W, tm), lambda bi, i: (bi, 0, i)),
            pl.BlockSpec((1, tm, N_BRANCH * D_MODEL), lambda bi, i: (bi, i, COL_GZ // (N_BRANCH * D_MODEL))),
            pl.BlockSpec((1, GLA_DV), lambda bi, i: (0, 0)),
            pl.BlockSpec((N_BRANCH, BRANCH_W, D_MODEL), lambda bi, i: (0, 0, 0)),
            pl.BlockSpec((D_MODEL, D_MODEL), lambda bi, i: (0, 0)),
        ],
        out_specs=pl.BlockSpec((1, tm, D_MODEL), tok),
        out_shape=jax.ShapeDtypeStruct((b, n, D_MODEL), F32),
        compiler_params=_cparams(("parallel", "parallel")),
        name="merge",
    )(x, ya, of, ob, proj, yc, proj, gn, wb, wo)


TOPX = PEER_TOPK + 1
TOPX_PAD = 24


def _top_values(vals, count):
    out = []
    for _ in range(count):
        m = jnp.max(vals, axis=0, keepdims=True)
        out.append(m)
        vals = jnp.where(vals == m, NEG_INF, vals)
    return out


def _route_kernel(x_ref, g_ref, wq_ref, sk_ref, hn_ref, s1_ref, e1_ref, th_ref, e0_ref):
    tb = x_ref.shape[0]
    hn = _rms(x_ref[...], g_ref[...])
    hn_t = hn.T.astype(BF16)
    hn_ref[...] = hn_t
    q_t = _dot(wq_ref[...], hn_t).astype(BF16)
    pad = jnp.full((TOPX_PAD - TOPX, tb), NEG_INF, F32)
    for h in range(PEER_HEADS):
        r0 = (2 * h) * PEER_HALF
        s0 = _dot(sk_ref[h, 0], q_t[r0:r0 + PEER_HALF])
        s1 = _dot(sk_ref[h, 1], q_t[r0 + PEER_HALF:r0 + 2 * PEER_HALF])
        top0 = _top_values(s0, TOPX)
        top1 = jnp.concatenate(_top_values(s1, TOPX) + [pad], axis=0)
        cand = jnp.concatenate([t0 + top1 for t0 in top0], axis=0)
        best = _top_values(cand, TOPX)
        thresh = 0.5 * (best[PEER_TOPK - 1] + best[PEER_TOPK])
        m0, m1 = top0[0], top1[0:1]
        z = jnp.zeros_like(m0)
        for c in best[:PEER_TOPK]:
            z = z + jnp.exp(c - best[0])
        e1 = jnp.exp(s1 - m1) / z
        th = thresh - s0
        e0 = jnp.exp(s0 - m0)
        for c in range(tb // 128):
            cs = slice(c * 128, (c + 1) * 128)
            s1_ref[h, c] = s1[:, cs]
            e1_ref[h, c] = e1[:, cs]
            th_ref[h, c] = th[:, cs]
            e0_ref[h, c] = e0[:, cs]


def peer_route(x2d, g, wq_t, subkeys, *, tb=256):
    t = x2d.shape[0]
    hk = jax.ShapeDtypeStruct((PEER_HEADS, t // 128, PEER_KEYS, 128), F32)
    hk_spec = pl.BlockSpec((PEER_HEADS, tb // 128, PEER_KEYS, 128), lambda i: (0, i, 0, 0))
    return pl.pallas_call(
        _route_kernel,
        grid=(t // tb,),
        in_specs=[
            pl.BlockSpec((tb, D_MODEL), lambda i: (i, 0)),
            pl.BlockSpec((1, D_MODEL), lambda i: (0, 0)),
            pl.BlockSpec((PEER_HEADS * PEER_QDIM, D_MODEL), lambda i: (0, 0)),
            pl.BlockSpec((PEER_HEADS, 2, PEER_KEYS, PEER_HALF), lambda i: (0, 0, 0, 0)),
        ],
        out_specs=[pl.BlockSpec((D_MODEL, tb), lambda i: (0, i)), hk_spec, hk_spec, hk_spec, hk_spec],
        out_shape=[jax.ShapeDtypeStruct((D_MODEL, t), BF16), hk, hk, hk, hk],
        compiler_params=_cparams(("parallel",)),
        name="peer_route",
    )(x2d, g, wq_t, subkeys)


DENSE_ROWS = 32


def _dense_kernel(x_ref, hn_ref, u_ref, vt_ref, s1_ref, e1_ref, th_ref, e0_ref, fg_ref, o_ref,
                  wg_ref, acc_ref, *, ne, ib, final_norm):
    e = pl.program_id(1)
    tb = hn_ref.shape[1]

    @pl.when(e == 0)
    def _():
        acc_ref[...] = jnp.zeros(acc_ref.shape, F32)

    a = _dot(u_ref[...], hn_ref[...])
    gelu_c = np.float32(np.sqrt(0.5))
    for il in range(ib):
        for tc in range(tb // 128):
            cols = slice(tc * 128, (tc + 1) * 128)
            for jc in range(PEER_KEYS // DENSE_ROWS):
                keys = slice(jc * DENSE_ROWS, (jc + 1) * DENSE_ROWS)
                rows = slice(il * PEER_KEYS + jc * DENSE_ROWS, il * PEER_KEYS + (jc + 1) * DENSE_ROWS)
                w = jnp.zeros((DENSE_ROWS, 128), F32)
                for h in range(PEER_HEADS):
                    sel = s1_ref[h, tc, keys, :] >= th_ref[h, tc, il:il + 1, :]
                    w = w + jnp.where(sel, e1_ref[h, tc, keys, :], 0.0) * e0_ref[h, tc, il:il + 1, :]
                ac = a[rows, cols]
                act = 0.5 * ac * (1.0 + lax.erf(ac * gelu_c))
                wg_ref[rows, cols] = (w * act).astype(BF16)
    acc_ref[...] += _dot(vt_ref[...], wg_ref[...])

    @pl.when(e == ne - 1)
    def _():
        y = x_ref[...] + acc_ref[...].T
        if final_norm:
            y = _rms(y, fg_ref[...])
        o_ref[...] = y


def peer_dense(x2d, hn_t, u, v_t, s1, e1, th, e0, final_g, *, final_norm, tb=512, ib=8):
    t = x2d.shape[0]
    eb = ib * PEER_KEYS
    ne = PEER_EXPERTS // eb
    kern = functools.partial(_dense_kernel, ne=ne, ib=ib, final_norm=final_norm)
    full = pl.BlockSpec((PEER_HEADS, tb // 128, PEER_KEYS, 128), lambda i, e: (0, i, 0, 0))
    part = pl.BlockSpec((PEER_HEADS, tb // 128, ib, 128), lambda i, e: (0, i, e, 0))
    return pl.pallas_call(
        kern,
        grid=(t // tb, ne),
        in_specs=[
            pl.BlockSpec((tb, D_MODEL), lambda i, e: (i, 0)),
            pl.BlockSpec((D_MODEL, tb), lambda i, e: (0, i)),
            pl.BlockSpec((eb, D_MODEL), lambda i, e: (e, 0)),
            pl.BlockSpec((D_MODEL, eb), lambda i, e: (0, e)),
            full, full, part, part,
            pl.BlockSpec((1, D_MODEL), lambda i, e: (0, 0)),
        ],
        out_specs=pl.BlockSpec((tb, D_MODEL), lambda i, e: (i, 0)),
        out_shape=jax.ShapeDtypeStruct((t, D_MODEL), F32),
        scratch_shapes=[pltpu.VMEM((eb, tb), BF16), pltpu.VMEM((D_MODEL, tb), F32)],
        compiler_params=_cparams(("parallel", "arbitrary")),
        name="peer_dense",
    )(x2d, hn_t, u, v_t, s1, e1, th, e0, final_g)


def _rel_bucket(rel):
    nb = REL_BUCKETS // 2
    max_exact = nb // 2
    n = jnp.abs(rel)
    nf = jnp.maximum(n, 1).astype(jnp.float32)
    large = max_exact + (jnp.log(nf / max_exact) / math.log(REL_MAX_DIST / max_exact)
                         * (nb - max_exact)).astype(jnp.int32)
    large = jnp.minimum(large, nb - 1)
    return jnp.where(rel > 0, nb, 0) + jnp.where(n < max_exact, n, large)


def _band_units(tq, tk):
    u = min(tq, tk)
    return u, tq // u, tk // u


def _bias_band(rel_table, tq, tk):
    u, rq, rk = _band_units(tq, tk)
    qpos = jnp.arange(tq)
    tiles = []
    for d in range(-rk - 1, rq + 2):
        rel = (d * u + jnp.arange(tk))[:, None] - qpos[None, :]
        onehot = jax.nn.one_hot(_rel_bucket(rel), REL_BUCKETS, dtype=F32)
        tiles.append(jnp.einsum("kqb,bh->hkq", onehot, rel_table.astype(F32),
                                precision=lax.Precision.HIGHEST))
    return jnp.stack(tiles) * LOG2E


def _rope_tables(n):
    rows = n // GRID_W
    row_id = jnp.repeat(jnp.arange(rows), GRID_W).astype(jnp.float32)
    col_id = (jnp.arange(rows * GRID_W) % GRID_W).astype(jnp.float32)
    sec = GQ_HEAD_DIM // 2
    inv = ROPE_THETA ** (-jnp.arange(0, sec, 2, dtype=jnp.float32) / sec)
    cr, sr = jnp.cos(row_id[:, None] * inv), jnp.sin(row_id[:, None] * inv)
    cc, sc = jnp.cos(col_id[:, None] * inv), jnp.sin(col_id[:, None] * inv)
    cos = jnp.concatenate([cr, cr, cc, cc], axis=1)
    sin = jnp.concatenate([-sr, sr, -sc, sc], axis=1)
    return jnp.concatenate([cos, cos], axis=1), jnp.concatenate([sin, sin], axis=1)


def _prep_weights(w_in, gla_gate_w, gla_gate_b, gq_qk_g, w_branch, w_out, peer_wq, peer_subkeys,
                  peer_u, peer_v):
    wi = jnp.concatenate(
        [w_in[:, :, ORIG_GZ:], w_in[:, :, :ORIG_BZ], w_in[:, :, ORIG_CQ:ORIG_GZ],
         w_in[:, :, ORIG_BZ:ORIG_CQ],
         jnp.zeros((DEPTH, D_MODEL, PROJ_W - ORIG_W), w_in.dtype)], axis=2).astype(BF16)
    hd = GLA_HEADS * GLA_DK
    gw = jnp.zeros((DEPTH, 2, 128, hd), F32)
    gw = gw.at[:, 0, 0:GLA_GATE_RANK].set(gla_gate_w[:, 0])
    gw = gw.at[:, 1, GLA_GATE_RANK:2 * GLA_GATE_RANK].set(gla_gate_w[:, 1])
    return dict(
        w_in=wi, gw=gw.astype(BF16), gb=gla_gate_b.astype(F32)[:, :, None, :],
        gq=jnp.tile(gq_qk_g[:, 0], (1, GQ_HEADS))[:, None, :],
        gk=jnp.tile(gq_qk_g[:, 1], (1, GQ_KV_HEADS))[:, None, :],
        wb=w_branch.astype(BF16), wo=w_out.astype(BF16),
        wq_t=jnp.swapaxes(peer_wq, 1, 2).astype(BF16), sk=peer_subkeys.astype(BF16),
        u=peer_u.astype(BF16), v_t=jnp.swapaxes(peer_v, 1, 2).astype(BF16))


def _encoder(x, w, rel_bias, norm1_g, da_lambda, da_subln_g, gla_norm_g, norm2_g, final_g, seg_ones):
    b, n, _ = x.shape
    t = b * n
    band = _bias_band(rel_bias, DA_TQ, DA_TK)
    cos2, sin2 = _rope_tables(n)
    for l in range(DEPTH):
        proj = in_proj(x.reshape(t, D_MODEL), norm1_g[l][None, :], w["w_in"][l]).reshape(b, n, PROJ_W)
        lam_init = 0.8 - 0.6 * math.exp(-0.3 * l)
        lp = da_lambda[l].astype(F32)
        lam = (jnp.exp(jnp.sum(lp[0] * lp[1])) - jnp.exp(jnp.sum(lp[2] * lp[3])) + lam_init).reshape(1)
        ya = diff_attn(proj, band, lam, da_subln_g[l][None, :], lam_init=lam_init, tq=DA_TQ, tk=DA_TK)
        qt, kr, vt = gqa_prep(proj, cos2, sin2, w["gq"][l], w["gk"][l], seg_ones)
        yc = gqa_attn(qt, kr, vt, tq=GQ_TQ, tk=GQ_TK)
        of = gla(proj, w["gw"][l, 0], w["gb"][l, 0], reverse=False)
        ob = gla(proj, w["gw"][l, 1], w["gb"][l, 1], reverse=True)
        x1 = merge(x, ya, of, ob, proj, yc, gla_norm_g[l][None, :], w["wb"][l], w["wo"][l])
        x1 = x1.reshape(t, D_MODEL)
        hn_t, s1, e1, th, e0 = peer_route(x1, norm2_g[l][None, :], w["wq_t"][l], w["sk"][l])
        x = peer_dense(x1, hn_t, w["u"][l], w["v_t"][l], s1, e1, th, e0, final_g[None, :],
                       final_norm=(l == DEPTH - 1)).reshape(b, n, D_MODEL)
    return x


def kernel(x_prompt, x_sample, rel_bias, norm1_g, w_in, da_lambda, da_subln_g, gla_gate_w, gla_gate_b,
           gla_norm_g, gq_qk_g, w_branch, w_out, norm2_g, peer_wq, peer_subkeys, peer_u, peer_v, final_g):
    w = _prep_weights(w_in, gla_gate_w, gla_gate_b, gq_qk_g, w_branch, w_out, peer_wq, peer_subkeys,
                      peer_u, peer_v)
    seg = np.arange(512) // GQ_HEAD_DIM
    seg_ones = jnp.asarray(seg[:, None] == seg[None, :], dtype=BF16)
    args = (w, rel_bias, norm1_g, da_lambda, da_subln_g, gla_norm_g, norm2_g, final_g, seg_ones)
    return (_encoder(x_prompt, *args), _encoder(x_sample, *args))
```

```python
import functools
import math

import numpy as np
import jax
import jax.numpy as jnp
from jax import lax
from jax.experimental import pallas as pl
from jax.experimental.pallas import tpu as pltpu

D_MODEL = 1024
DEPTH = 2
GRID_W = 64
NORM_EPS = 1e-6
N_BRANCH = 3
BRANCH_W = 512

DA_HEADS = 4
DA_HEAD_DIM = 64
DA_V_DIM = 128
REL_BUCKETS = 32
REL_MAX_DIST = 128

GLA_HEADS = 4
GLA_DK = 64
GLA_DV = 128
GLA_GATE_RANK = 16
GLA_TAU = 16.0
GLA_CHUNK = 64

GQ_HEADS = 8
GQ_KV_HEADS = 2
GQ_GROUP = GQ_HEADS // GQ_KV_HEADS
GQ_HEAD_DIM = 64
ROPE_THETA = 10000.0

PEER_HEADS = 8
PEER_KEYS = 128
PEER_EXPERTS = PEER_KEYS * PEER_KEYS
PEER_QDIM = 256
PEER_HALF = 128
PEER_TOPK = 16

COL_GZ = 0
COL_AQ, COL_AK, COL_AV = 3072, 3584, 4096
COL_BQ, COL_BK, COL_BV, COL_BR = 4608, 4864, 5120, 5632
COL_CQ, COL_CK, COL_CV = 6144, 6656, 6784
COL_BZ = 6912
PROJ_W = 7168
ORIG_BZ = 3072
ORIG_CQ = ORIG_BZ + 2 * GLA_GATE_RANK
ORIG_GZ = ORIG_CQ + 768
ORIG_W = ORIG_GZ + N_BRANCH * D_MODEL

DA_TQ, DA_TK = 1024, 512
GQ_TQ, GQ_TK = 512, 512

VMEM_LIMIT = 56 * 1024 * 1024

F32 = jnp.float32
BF16 = jnp.bfloat16
NEG_INF = float("-inf")
LOG2E = math.log2(math.e)


def _cparams(sem):
    return pltpu.CompilerParams(dimension_semantics=sem, vmem_limit_bytes=VMEM_LIMIT)


def _dot(a, b):
    return jnp.dot(a, b, preferred_element_type=F32)


def _dot_nt(a, b):
    return lax.dot_general(a, b, (((1,), (1,)), ((), ())), preferred_element_type=F32)


def _dot_tn(a, b):
    return lax.dot_general(a, b, (((0,), (0,)), ((), ())), preferred_element_type=F32)


def _rms(x, g):
    return x * lax.rsqrt(jnp.mean(x * x, axis=-1, keepdims=True) + NORM_EPS) * g


def _in_proj_kernel(x_ref, g_ref, w_ref, o_ref, hn_ref):
    @pl.when(pl.program_id(1) == 0)
    def _():
        hn_ref[...] = _rms(x_ref[...], g_ref[...]).astype(BF16)

    o_ref[...] = _dot(hn_ref[...], w_ref[...]).astype(o_ref.dtype)


def in_proj(x2d, g, w, *, tm=2048, tn=1024):
    t = x2d.shape[0]
    nw = w.shape[1]
    return pl.pallas_call(
        _in_proj_kernel,
        grid=(t // tm, nw // tn),
        in_specs=[
            pl.BlockSpec((tm, D_MODEL), lambda i, j: (i, 0)),
            pl.BlockSpec((1, D_MODEL), lambda i, j: (0, 0)),
            pl.BlockSpec((D_MODEL, tn), lambda i, j: (0, j)),
        ],
        out_specs=pl.BlockSpec((tm, tn), lambda i, j: (i, j)),
        out_shape=jax.ShapeDtypeStruct((t, nw), BF16),
        scratch_shapes=[pltpu.VMEM((tm, D_MODEL), BF16)],
        compiler_params=_cparams(("parallel", "arbitrary")),
        name="in_proj",
    )(x2d, g, w)


SUM_ROWS = 16


def _flash_pipe_step(k, vt, bias, qt_ref, cur, prev, m_ref, acc_ref):
    s_cur, p_cur, al_cur, mt_cur = cur
    s_prev, p_prev, al_prev, mt_prev = prev
    s = _dot(k, qt_ref[...])
    if bias is not None:
        s = s + jnp.concatenate([bias] * (s.shape[1] // bias.shape[1]), axis=1)
    s_cur[...] = s
    mt_cur[...] = jnp.max(s, axis=0, keepdims=True)
    acc_ref[...] = al_cur[...] * acc_ref[...] + _dot(vt, p_cur[...])
    m_prev = m_ref[...]
    m_next = jnp.maximum(m_prev, mt_prev[...])
    p_prev[...] = jnp.exp2(s_prev[...] - m_next).astype(BF16)
    al_prev[...] = jnp.exp2(m_prev - m_next)
    m_ref[...] = m_next


def _flash_pipe(kt, nk, build_qt, k_ref, get_vt, bias_ref, qt_ref, slots, m_ref, acc_ref):
    @pl.when(kt == 0)
    def _():
        build_qt()
        (_, p0, al0, _), (s1, _, _, mt1) = slots
        for r in (s1, mt1, p0, al0, acc_ref):
            r[...] = jnp.zeros(r.shape, r.dtype)

    @pl.when(kt <= 1)
    def _():
        m_ref[...] = jnp.full(m_ref.shape, NEG_INF, F32)

    @pl.when(kt == 2)
    def _():
        acc_ref[...] = jnp.zeros(acc_ref.shape, F32)

    for c in (0, 1):
        @pl.when(kt % 2 == c)
        def _(c=c):
            bias = None if bias_ref is None else bias_ref[0, 0]
            _flash_pipe_step(k_ref[0], get_vt(), bias, qt_ref, slots[c], slots[1 - c], m_ref, acc_ref)


def _flash_scratch(tk, cols, dv):
    slot = [pltpu.VMEM((tk, cols), F32), pltpu.VMEM((tk, cols), BF16),
            pltpu.VMEM((1, cols), F32), pltpu.VMEM((1, cols), F32)]
    return slot + slot + [pltpu.VMEM((1, cols), F32), pltpu.VMEM((dv + SUM_ROWS, cols), F32)]


def _dattn_kernel(lam_ref, q_ref, k_ref, v_ref, b_ref, g_ref, o_ref, qt_ref,
                  s0_ref, p0_ref, al0_ref, mt0_ref, s1_ref, p1_ref, al1_ref, mt1_ref,
                  m_ref, acc_ref, *, tq, nk, out_scale):
    kt = pl.program_id(3)

    def build_qt():
        q = q_ref[0].astype(F32) * (LOG2E * DA_HEAD_DIM ** -0.5)
        lane = lax.broadcasted_iota(jnp.int32, q.shape, 1)
        qt_ref[:, 0:tq] = jnp.where(lane < DA_HEAD_DIM, q, 0.0).T.astype(BF16)
        qt_ref[:, tq:2 * tq] = jnp.where(lane >= DA_HEAD_DIM, q, 0.0).T.astype(BF16)

    def get_vt():
        vt = v_ref[0].astype(F32).T.astype(BF16)
        return jnp.concatenate([vt, jnp.ones((SUM_ROWS, vt.shape[1]), BF16)], axis=0)

    slots = ((s0_ref, p0_ref, al0_ref, mt0_ref), (s1_ref, p1_ref, al1_ref, mt1_ref))
    _flash_pipe(kt, nk, build_qt, k_ref, get_vt, b_ref, qt_ref, slots, m_ref, acc_ref)

    @pl.when(kt == nk + 1)
    def _():
        o = acc_ref[0:DA_V_DIM, :] / acc_ref[DA_V_DIM:DA_V_DIM + 1, :]
        o = (o[:, 0:tq] - lam_ref[0] * o[:, tq:2 * tq]).T
        o_ref[0] = (_rms(o, g_ref[...]) * out_scale).astype(o_ref.dtype)


def diff_attn(proj, bias_band, lam, sub_g, *, lam_init, tq, tk):
    b, n, _ = proj.shape
    _, rq, rk = _band_units(tq, tk)
    nk = n // tk
    kern = functools.partial(_dattn_kernel, tq=tq, nk=nk, out_scale=1.0 - lam_init)
    tile = lambda ki, lag: jnp.clip(ki - lag, 0, nk - 1)

    def bias_map(bi, h, qi, ki):
        return (jnp.clip(rk * tile(ki, 0) - rq * qi, -rk - 1, rq + 1) + rk + 1, h, 0, 0)

    return pl.pallas_call(
        kern,
        grid=(b, DA_HEADS, n // tq, nk + 2),
        in_specs=[
            pl.BlockSpec(memory_space=pltpu.SMEM),
            pl.BlockSpec((1, tq, 128), lambda bi, h, qi, ki: (bi, qi, COL_AQ // 128 + h)),
            pl.BlockSpec((1, tk, 128), lambda bi, h, qi, ki: (bi, tile(ki, 0), COL_AK // 128 + h)),
            pl.BlockSpec((1, tk, 128), lambda bi, h, qi, ki: (bi, tile(ki, 2), COL_AV // 128 + h)),
            pl.BlockSpec((1, 1, tk, tq), bias_map),
            pl.BlockSpec((1, DA_V_DIM), lambda bi, h, qi, ki: (0, 0)),
        ],
        out_specs=pl.BlockSpec((1, tq, DA_V_DIM), lambda bi, h, qi, ki: (bi, qi, h)),
        out_shape=jax.ShapeDtypeStruct((b, n, DA_HEADS * DA_V_DIM), BF16),
        scratch_shapes=[pltpu.VMEM((128, 2 * tq), BF16)] + _flash_scratch(tk, 2 * tq, DA_V_DIM),
        compiler_params=_cparams(("parallel", "parallel", "parallel", "arbitrary")),
        name="diff_attn",
    )(lam, proj, proj, proj, bias_band, sub_g)


def _swap16(x):
    w = x.shape[1]
    lane = lax.broadcasted_iota(jnp.int32, x.shape, 1)
    return jnp.where(lane % 32 < 16, pltpu.roll(x, w - 16, 1), pltpu.roll(x, 16, 1))


def _seg_rms(x, seg_ones, g):
    sq = x * x
    hi = sq.astype(BF16)
    lo = (sq - hi.astype(F32)).astype(BF16)
    ss = _dot(hi, seg_ones) + _dot(lo, seg_ones)
    return x * lax.rsqrt(ss * (1.0 / GQ_HEAD_DIM) + NORM_EPS) * g


def _gqa_prep_kernel(q_ref, k_ref, v_ref, cos_ref, sin_ref, gq_ref, gk_ref, ones_ref,
                     qo_ref, ko_ref, vo_ref):
    cos = cos_ref[...]
    sin = sin_ref[...]
    cos4 = jnp.concatenate([cos] * 4, axis=1)
    sin4 = jnp.concatenate([sin] * 4, axis=1)
    ones = ones_ref[...]
    q = _seg_rms(q_ref[0].astype(F32), ones, gq_ref[...])
    k = _seg_rms(k_ref[0].astype(F32), ones[0:128, 0:128], gk_ref[...])
    q = (q * cos4 + _swap16(q) * sin4) * (LOG2E * GQ_HEAD_DIM ** -0.5)
    qo_ref[0] = q.T.astype(BF16)
    ko_ref[0] = (k * cos + _swap16(k) * sin).astype(BF16)
    vt = v_ref[0].astype(F32).T.astype(BF16)
    hd = GQ_HEAD_DIM
    ones = jnp.ones((SUM_ROWS, vt.shape[1]), BF16)
    for h in range(GQ_KV_HEADS):
        vo_ref[0, h, 0:hd, :] = vt[h * hd:(h + 1) * hd]
        vo_ref[0, h, hd:hd + SUM_ROWS, :] = ones


def gqa_prep(proj, cos2, sin2, gq, gk, seg_ones, *, tm=512):
    b, n, _ = proj.shape
    return pl.pallas_call(
        _gqa_prep_kernel,
        grid=(b, n // tm),
        in_specs=[
            pl.BlockSpec((1, tm, 512), lambda bi, i: (bi, i, COL_CQ // 512)),
            pl.BlockSpec((1, tm, 128), lambda bi, i: (bi, i, COL_CK // 128)),
            pl.BlockSpec((1, tm, 128), lambda bi, i: (bi, i, COL_CV // 128)),
            pl.BlockSpec((tm, 128), lambda bi, i: (i, 0)),
            pl.BlockSpec((tm, 128), lambda bi, i: (i, 0)),
            pl.BlockSpec((1, 512), lambda bi, i: (0, 0)),
            pl.BlockSpec((1, 128), lambda bi, i: (0, 0)),
            pl.BlockSpec((512, 512), lambda bi, i: (0, 0)),
        ],
        out_specs=[
            pl.BlockSpec((1, GQ_HEADS * GQ_HEAD_DIM, tm), lambda bi, i: (bi, 0, i)),
            pl.BlockSpec((1, tm, GQ_KV_HEADS * GQ_HEAD_DIM), lambda bi, i: (bi, i, 0)),
            pl.BlockSpec((1, GQ_KV_HEADS, GQ_HEAD_DIM + SUM_ROWS, tm), lambda bi, i: (bi, 0, 0, i)),
        ],
        out_shape=[
            jax.ShapeDtypeStruct((b, GQ_HEADS * GQ_HEAD_DIM, n), BF16),
            jax.ShapeDtypeStruct((b, n, GQ_KV_HEADS * GQ_HEAD_DIM), BF16),
            jax.ShapeDtypeStruct((b, GQ_KV_HEADS, GQ_HEAD_DIM + SUM_ROWS, n), BF16),
        ],
        compiler_params=_cparams(("parallel", "parallel")),
        name="gqa_prep",
    )(proj, proj, proj, cos2, sin2, gq, gk, seg_ones)


def _gqa_kernel(q_ref, k_ref, vt_ref, o_ref, qt_ref,
                s0_ref, p0_ref, al0_ref, mt0_ref, s1_ref, p1_ref, al1_ref, mt1_ref,
                m_ref, acc_ref, *, tq, nk):
    kt = pl.program_id(3)
    hd = GQ_HEAD_DIM

    def build_qt():
        qt_ref[...] = jnp.zeros(qt_ref.shape, BF16)
        base = pl.multiple_of(pl.program_id(1) * hd, hd)
        for r in range(GQ_GROUP):
            qt_ref[pl.ds(base, hd), r * tq:(r + 1) * tq] = q_ref[0, r * hd:(r + 1) * hd, :]

    slots = ((s0_ref, p0_ref, al0_ref, mt0_ref), (s1_ref, p1_ref, al1_ref, mt1_ref))
    _flash_pipe(kt, nk, build_qt, k_ref, lambda: vt_ref[0, 0], None, qt_ref, slots, m_ref, acc_ref)

    @pl.when(kt == nk + 1)
    def _():
        o = (acc_ref[0:hd, :] / acc_ref[hd:hd + 1, :]).astype(o_ref.dtype)
        for r in range(GQ_GROUP):
            o_ref[0, r * hd:(r + 1) * hd, :] = o[:, r * tq:(r + 1) * tq]


def gqa_attn(qt, k, vt, *, tq, tk):
    b, _, n = qt.shape
    gw = GQ_GROUP * GQ_HEAD_DIM
    kvw = GQ_KV_HEADS * GQ_HEAD_DIM
    nk = n // tk
    cols = GQ_GROUP * tq
    kern = functools.partial(_gqa_kernel, tq=tq, nk=nk)
    tile = lambda ki, lag: jnp.clip(ki - lag, 0, nk - 1)
    return pl.pallas_call(
        kern,
        grid=(b, GQ_KV_HEADS, n // tq, nk + 2),
        in_specs=[
            pl.BlockSpec((1, gw, tq), lambda bi, g, qi, ki: (bi, g, qi)),
            pl.BlockSpec((1, tk, kvw), lambda bi, g, qi, ki: (bi, tile(ki, 0), 0)),
            pl.BlockSpec((1, 1, GQ_HEAD_DIM + SUM_ROWS, tk), lambda bi, g, qi, ki: (bi, g, 0, tile(ki, 2))),
        ],
        out_specs=pl.BlockSpec((1, gw, tq), lambda bi, g, qi, ki: (bi, g, qi)),
        out_shape=jax.ShapeDtypeStruct((b, GQ_HEADS * GQ_HEAD_DIM, n), BF16),
        scratch_shapes=[pltpu.VMEM((kvw, cols), BF16)] + _flash_scratch(tk, cols, GQ_HEAD_DIM),
        compiler_params=_cparams(("parallel", "parallel", "parallel", "arbitrary")),
        name="gqa_attn",
    )(qt, k, vt)


def _gla_kernel(q_ref, k_ref, v_ref, z_ref, gw_ref, gb_ref, o_ref, st_ref, *, reverse, nchunk):
    L = GLA_CHUNK
    hd = GLA_HEADS * GLA_DK

    @pl.when(pl.program_id(1) == 0)
    def _():
        st_ref[...] = jnp.zeros(st_ref.shape, F32)

    row = lax.broadcasted_iota(jnp.int32, (L, L), 0)
    col = lax.broadcasted_iota(jnp.int32, (L, L), 1)
    if reverse:
        tri = (col >= row).astype(BF16)
        keep = col > row
    else:
        tri = (col <= row).astype(BF16)
        keep = col <= row
    lane = lax.broadcasted_iota(jnp.int32, (L, hd), 1)

    order = range(nchunk - 1, -1, -1) if reverse else range(nchunk)
    for c in order:
        rows = pl.ds(c * L, L)
        q = q_ref[0, rows, :].astype(F32) * (GLA_DK ** -0.5)
        k = k_ref[0, rows, :].astype(F32)
        logit = _dot(z_ref[0, rows, :], gw_ref[...]) + gb_ref[...]
        lg = jax.nn.log_sigmoid(logit) * (1.0 / GLA_TAU)
        hi = lg.astype(BF16)
        lo = (lg - hi.astype(F32)).astype(BF16)
        cum = _dot(tri, hi) + _dot(tri, lo)
        last = cum[0:1, :] if reverse else cum[L - 1:L, :]
        q_dec = q * jnp.exp(cum)
        k_dec = (k * jnp.exp(-cum)).astype(BF16)
        k_last = (k * jnp.exp(last - cum)).astype(BF16)
        decay = jnp.exp(last)
        for h in range(GLA_HEADS):
            head = (lane >= h * GLA_DK) & (lane < (h + 1) * GLA_DK)
            qh = jnp.where(head, q_dec, 0.0).astype(BF16)
            vh = v_ref[0, rows, h * GLA_DV:(h + 1) * GLA_DV]
            a = jnp.where(keep, _dot_nt(qh, k_dec), 0.0)
            st = st_ref[h]
            o = _dot(a.astype(BF16), vh) + _dot_nt(qh, st.astype(BF16))
            o_ref[0, rows, h * GLA_DV:(h + 1) * GLA_DV] = o
            st_ref[h] = st * decay + _dot_tn(vh, k_last)


def gla(proj, gw, gb, *, reverse, tm=256):
    b, n, _ = proj.shape
    nb = n // tm
    kern = functools.partial(_gla_kernel, reverse=reverse, nchunk=tm // GLA_CHUNK)
    blk = (lambda i: nb - 1 - i) if reverse else (lambda i: i)
    hd = GLA_HEADS * GLA_DK
    return pl.pallas_call(
        kern,
        grid=(b, nb),
        in_specs=[
            pl.BlockSpec((1, tm, hd), lambda bi, i: (bi, blk(i), COL_BQ // hd)),
            pl.BlockSpec((1, tm, hd), lambda bi, i: (bi, blk(i), COL_BK // hd)),
            pl.BlockSpec((1, tm, 512), lambda bi, i: (bi, blk(i), COL_BV // 512)),
            pl.BlockSpec((1, tm, 128), lambda bi, i: (bi, blk(i), COL_BZ // 128)),
            pl.BlockSpec((128, hd), lambda bi, i: (0, 0)),
            pl.BlockSpec((1, hd), lambda bi, i: (0, 0)),
        ],
        out_specs=pl.BlockSpec((1, tm, GLA_HEADS * GLA_DV), lambda bi, i: (bi, blk(i), 0)),
        out_shape=jax.ShapeDtypeStruct((b, n, GLA_HEADS * GLA_DV), F32),
        scratch_shapes=[pltpu.VMEM((GLA_HEADS, GLA_DV, hd), F32)],
        compiler_params=_cparams(("parallel", "arbitrary")),
        name="gla_bwd" if reverse else "gla_fwd",
    )(proj, proj, proj, proj, gw, gb)


def _merge_kernel(x_ref, ya_ref, of_ref, ob_ref, r_ref, yc_ref, gz_ref, gn_ref, wb_ref, wo_ref,
                  o_ref):
    gn = gn_ref[...]
    o = of_ref[0] + ob_ref[0]
    r = r_ref[0].astype(F32)
    yb = []
    for h in range(GLA_HEADS):
        sl = slice(h * GLA_DV, (h + 1) * GLA_DV)
        yb.append(_rms(o[:, sl], gn) * jax.nn.silu(r[:, sl]))
    yb = jnp.concatenate(yb, axis=1).astype(BF16)

    pa = _dot(ya_ref[0], wb_ref[0])
    pb = _dot(yb, wb_ref[1])
    pc = _dot_tn(yc_ref[0], wb_ref[2])

    gz = gz_ref[0].astype(F32)
    merged = (jax.nn.sigmoid(gz[:, 0:D_MODEL]) * pa
              + jax.nn.sigmoid(gz[:, D_MODEL:2 * D_MODEL]) * pb
              + jax.nn.sigmoid(gz[:, 2 * D_MODEL:3 * D_MODEL]) * pc)
    o_ref[0] = x_ref[0] + _dot(merged.astype(BF16), wo_ref[...])


def merge(x, ya, of, ob, proj, yc, gn, wb, wo, *, tm=256):
    b, n, _ = x.shape
    tok = lambda bi, i: (bi, i, 0)
    return pl.pallas_call(
        _merge_kernel,
        grid=(b, n // tm),
        in_specs=[
            pl.BlockSpec((1, tm, D_MODEL), tok),
            pl.BlockSpec((1, tm, BRANCH_W), tok),
            pl.BlockSpec((1, tm, BRANCH_W), tok),
            pl.BlockSpec((1, tm, BRANCH_W), tok),
            pl.BlockSpec((1, tm, 512), lambda bi, i: (bi, i, COL_BR // 512)),
            pl.BlockSpec((1, BRANCH_W, tm), lambda bi, i: (bi, 0, i)),
            pl.BlockSpec((1, tm, N_BRANCH * D_MODEL), lambda bi, i: (bi, i, COL_GZ // (N_BRANCH * D_MODEL))),
            pl.BlockSpec((1, GLA_DV), lambda bi, i: (0, 0)),
            pl.BlockSpec((N_BRANCH, BRANCH_W, D_MODEL), lambda bi, i: (0, 0, 0)),
            pl.BlockSpec((D_MODEL, D_MODEL), lambda bi, i: (0, 0)),
        ],
        out_specs=pl.BlockSpec((1, tm, D_MODEL), tok),
        out_shape=jax.ShapeDtypeStruct((b, n, D_MODEL), F32),
        compiler_params=_cparams(("parallel", "parallel")),
        name="merge",
    )(x, ya, of, ob, proj, yc, proj, gn, wb, wo)


TOPX = PEER_TOPK + 1
TOPX_PAD = 24


def _top_values(vals, count):
    out = []
    for _ in range(count):
        m = jnp.max(vals, axis=0, keepdims=True)
        out.append(m)
        vals = jnp.where(vals == m, NEG_INF, vals)
    return out


def _route_kernel(x_ref, g_ref, wq_ref, sk_ref, hn_ref, s1_ref, e1_ref, th_ref, e0_ref):
    tb = x_ref.shape[0]
    hn = _rms(x_ref[...], g_ref[...])
    hn_t = hn.T.astype(BF16)
    hn_ref[...] = hn_t
    q_t = _dot(wq_ref[...], hn_t).astype(BF16)
    pad = jnp.full((TOPX_PAD - TOPX, tb), NEG_INF, F32)
    for h in range(PEER_HEADS):
        r0 = (2 * h) * PEER_HALF
        s0 = _dot(sk_ref[h, 0], q_t[r0:r0 + PEER_HALF])
        s1 = _dot(sk_ref[h, 1], q_t[r0 + PEER_HALF:r0 + 2 * PEER_HALF])
        top0 = _top_values(s0, TOPX)
        top1 = _top_values(s1, TOPX)
        slab0 = jnp.concatenate(top0 + [pad], axis=0)
        slab1 = jnp.concatenate(top1 + [pad], axis=0)
        half = slab1[0:8]
        cand = jnp.concatenate([top0[0] + slab1] + [top0[a] + half for a in range(1, 8)]
                               + [slab0[8:TOPX_PAD] + top1[0]], axis=0)
        best = _top_values(cand, TOPX)
        thresh = 0.5 * (best[PEER_TOPK - 1] + best[PEER_TOPK])
        m0, m1 = top0[0], top1[0]
        z = jnp.zeros_like(m0)
        for c in best[:PEER_TOPK]:
            z = z + jnp.exp(c - best[0])
        e1 = jnp.exp(s1 - m1) / z
        th = thresh - s0
        e0 = jnp.exp(s0 - m0)
        for c in range(tb // 128):
            cs = slice(c * 128, (c + 1) * 128)
            s1_ref[h, c] = s1[:, cs]
            e1_ref[h, c] = e1[:, cs]
            th_ref[h, c] = th[:, cs]
            e0_ref[h, c] = e0[:, cs]


def peer_route(x2d, g, wq_t, subkeys, *, tb=256):
    t = x2d.shape[0]
    hk = jax.ShapeDtypeStruct((PEER_HEADS, t // 128, PEER_KEYS, 128), F32)
    hk_spec = pl.BlockSpec((PEER_HEADS, tb // 128, PEER_KEYS, 128), lambda i: (0, i, 0, 0))
    return pl.pallas_call(
        _route_kernel,
        grid=(t // tb,),
        in_specs=[
            pl.BlockSpec((tb, D_MODEL), lambda i: (i, 0)),
            pl.BlockSpec((1, D_MODEL), lambda i: (0, 0)),
            pl.BlockSpec((PEER_HEADS * PEER_QDIM, D_MODEL), lambda i: (0, 0)),
            pl.BlockSpec((PEER_HEADS, 2, PEER_KEYS, PEER_HALF), lambda i: (0, 0, 0, 0)),
        ],
        out_specs=[pl.BlockSpec((D_MODEL, tb), lambda i: (0, i)), hk_spec, hk_spec, hk_spec, hk_spec],
        out_shape=[jax.ShapeDtypeStruct((D_MODEL, t), BF16), hk, hk, hk, hk],
        compiler_params=_cparams(("parallel",)),
        name="peer_route",
    )(x2d, g, wq_t, subkeys)


DENSE_ROWS = 32

def _dense_kernel(x_ref, hn_ref, u_ref, vt_ref, s1_ref, e1_ref, th_ref, e0_ref, fg_ref, o_ref,
                  wg_ref, acc_ref, *, ne, ib, final_norm):
    e = pl.program_id(1)
    tb = hn_ref.shape[1]

    @pl.when(e == 0)
    def _():
        acc_ref[...] = jnp.zeros(acc_ref.shape, F32)

    a = _dot(u_ref[...], hn_ref[...])
    gelu_c = np.float32(np.sqrt(0.5))
    for il in range(ib):
        for tc in range(tb // 128):
            cols = slice(tc * 128, (tc + 1) * 128)
            for jc in range(PEER_KEYS // DENSE_ROWS):
                keys = slice(jc * DENSE_ROWS, (jc + 1) * DENSE_ROWS)
                rows = slice(il * PEER_KEYS + jc * DENSE_ROWS, il * PEER_KEYS + (jc + 1) * DENSE_ROWS)
                w = jnp.zeros((DENSE_ROWS, 128), F32)
                for h in range(PEER_HEADS):
                    sel = s1_ref[h, tc, keys, :] >= th_ref[h, tc, il:il + 1, :]
                    w = w + jnp.where(sel, e1_ref[h, tc, keys, :], 0.0) * e0_ref[h, tc, il:il + 1, :]
                ac = a[rows, cols]
                act = 0.5 * ac * (1.0 + lax.erf(ac * gelu_c))
                wg_ref[rows, cols] = (w * act).astype(BF16)
    acc_ref[...] += _dot(vt_ref[...], wg_ref[...])

    @pl.when(e == ne - 1)
    def _():
        y = x_ref[...] + acc_ref[...].T
        if final_norm:
            y = _rms(y, fg_ref[...])
        o_ref[...] = y


def peer_dense(x2d, hn_t, u, v_t, s1, e1, th, e0, final_g, *, final_norm, tb=512, ib=8):
    t = x2d.shape[0]
    eb = ib * PEER_KEYS
    ne = PEER_EXPERTS // eb
    kern = functools.partial(_dense_kernel, ne=ne, ib=ib, final_norm=final_norm)
    full = pl.BlockSpec((PEER_HEADS, tb // 128, PEER_KEYS, 128), lambda i, e: (0, i, 0, 0))
    part = pl.BlockSpec((PEER_HEADS, tb // 128, ib, 128), lambda i, e: (0, i, e, 0))
    return pl.pallas_call(
        kern,
        grid=(t // tb, ne),
        in_specs=[
            pl.BlockSpec((tb, D_MODEL), lambda i, e: (i, 0)),
            pl.BlockSpec((D_MODEL, tb), lambda i, e: (0, i)),
            pl.BlockSpec((eb, D_MODEL), lambda i, e: (e, 0)),
            pl.BlockSpec((D_MODEL, eb), lambda i, e: (0, e)),
            full, full, part, part,
            pl.BlockSpec((1, D_MODEL), lambda i, e: (0, 0)),
        ],
        out_specs=pl.BlockSpec((tb, D_MODEL), lambda i, e: (i, 0)),
        out_shape=jax.ShapeDtypeStruct((t, D_MODEL), F32),
        scratch_shapes=[pltpu.VMEM((eb, tb), BF16), pltpu.VMEM((D_MODEL, tb), F32)],
        compiler_params=_cparams(("parallel", "arbitrary")),
        name="peer_dense",
    )(x2d, hn_t, u, v_t, s1, e1, th, e0, final_g)


def _rel_bucket(rel):
    nb = REL_BUCKETS // 2
    max_exact = nb // 2
    n = jnp.abs(rel)
    nf = jnp.maximum(n, 1).astype(jnp.float32)
    large = max_exact + (jnp.log(nf / max_exact) / math.log(REL_MAX_DIST / max_exact)
                         * (nb - max_exact)).astype(jnp.int32)
    large = jnp.minimum(large, nb - 1)
    return jnp.where(rel > 0, nb, 0) + jnp.where(n < max_exact, n, large)


def _band_units(tq, tk):
    u = min(tq, tk)
    return u, tq // u, tk // u


def _bias_band(rel_table, tq, tk):
    u, rq, rk = _band_units(tq, tk)
    qpos = jnp.arange(tq)
    tiles = []
    for d in range(-rk - 1, rq + 2):
        rel = (d * u + jnp.arange(tk))[:, None] - qpos[None, :]
        onehot = jax.nn.one_hot(_rel_bucket(rel), REL_BUCKETS, dtype=F32)
        tiles.append(jnp.einsum("kqb,bh->hkq", onehot, rel_table.astype(F32),
                                precision=lax.Precision.HIGHEST))
    return jnp.stack(tiles) * LOG2E


def _rope_tables(n):
    rows = n // GRID_W
    row_id = jnp.repeat(jnp.arange(rows), GRID_W).astype(jnp.float32)
    col_id = (jnp.arange(rows * GRID_W) % GRID_W).astype(jnp.float32)
    sec = GQ_HEAD_DIM // 2
    inv = ROPE_THETA ** (-jnp.arange(0, sec, 2, dtype=jnp.float32) / sec)
    cr, sr = jnp.cos(row_id[:, None] * inv), jnp.sin(row_id[:, None] * inv)
    cc, sc = jnp.cos(col_id[:, None] * inv), jnp.sin(col_id[:, None] * inv)
    cos = jnp.concatenate([cr, cr, cc, cc], axis=1)
    sin = jnp.concatenate([-sr, sr, -sc, sc], axis=1)
    return jnp.concatenate([cos, cos], axis=1), jnp.concatenate([sin, sin], axis=1)


def _prep_weights(w_in, gla_gate_w, gla_gate_b, gq_qk_g, w_branch, w_out, peer_wq, peer_subkeys,
                  peer_u, peer_v):
    wi = jnp.concatenate(
        [w_in[:, :, ORIG_GZ:], w_in[:, :, :ORIG_BZ], w_in[:, :, ORIG_CQ:ORIG_GZ],
         w_in[:, :, ORIG_BZ:ORIG_CQ],
         jnp.zeros((DEPTH, D_MODEL, PROJ_W - ORIG_W), w_in.dtype)], axis=2).astype(BF16)
    hd = GLA_HEADS * GLA_DK
    gw = jnp.zeros((DEPTH, 2, 128, hd), F32)
    gw = gw.at[:, 0, 0:GLA_GATE_RANK].set(gla_gate_w[:, 0])
    gw = gw.at[:, 1, GLA_GATE_RANK:2 * GLA_GATE_RANK].set(gla_gate_w[:, 1])
    return dict(
        w_in=wi, gw=gw.astype(BF16), gb=gla_gate_b.astype(F32)[:, :, None, :],
        gq=jnp.tile(gq_qk_g[:, 0], (1, GQ_HEADS))[:, None, :],
        gk=jnp.tile(gq_qk_g[:, 1], (1, GQ_KV_HEADS))[:, None, :],
        wb=w_branch.astype(BF16), wo=w_out.astype(BF16),
        wq_t=jnp.swapaxes(peer_wq, 1, 2).astype(BF16), sk=peer_subkeys.astype(BF16),
        u=peer_u.astype(BF16), v_t=jnp.swapaxes(peer_v, 1, 2).astype(BF16))


def _encoder(x, w, rel_bias, norm1_g, da_lambda, da_subln_g, gla_norm_g, norm2_g, final_g, seg_ones):
    b, n, _ = x.shape
    t = b * n
    band = _bias_band(rel_bias, DA_TQ, DA_TK)
    cos2, sin2 = _rope_tables(n)
    for l in range(DEPTH):
        proj = in_proj(x.reshape(t, D_MODEL), norm1_g[l][None, :], w["w_in"][l]).reshape(b, n, PROJ_W)
        lam_init = 0.8 - 0.6 * math.exp(-0.3 * l)
        lp = da_lambda[l].astype(F32)
        lam = (jnp.exp(jnp.sum(lp[0] * lp[1])) - jnp.exp(jnp.sum(lp[2] * lp[3])) + lam_init).reshape(1)
        ya = diff_attn(proj, band, lam, da_subln_g[l][None, :], lam_init=lam_init, tq=DA_TQ, tk=DA_TK)
        qt, kr, vt = gqa_prep(proj, cos2, sin2, w["gq"][l], w["gk"][l], seg_ones)
        yc = gqa_attn(qt, kr, vt, tq=GQ_TQ, tk=GQ_TK)
        of = gla(proj, w["gw"][l, 0], w["gb"][l, 0], reverse=False)
        ob = gla(proj, w["gw"][l, 1], w["gb"][l, 1], reverse=True)
        x1 = merge(x, ya, of, ob, proj, yc, gla_norm_g[l][None, :], w["wb"][l], w["wo"][l])
        x1 = x1.reshape(t, D_MODEL)
        hn_t, s1, e1, th, e0 = peer_route(x1, norm2_g[l][None, :], w["wq_t"][l], w["sk"][l])
        x = peer_dense(x1, hn_t, w["u"][l], w["v_t"][l], s1, e1, th, e0, final_g[None, :],
                       final_norm=(l == DEPTH - 1)).reshape(b, n, D_MODEL)
    return x


def kernel(x_prompt, x_sample, rel_bias, norm1_g, w_in, da_lambda, da_subln_g, gla_gate_w, gla_gate_b,
           gla_norm_g, gq_qk_g, w_branch, w_out, norm2_g, peer_wq, peer_subkeys, peer_u, peer_v, final_g):
    w = _prep_weights(w_in, gla_gate_w, gla_gate_b, gq_qk_g, w_branch, w_out, peer_wq, peer_subkeys,
                      peer_u, peer_v)
    seg = np.arange(512) // GQ_HEAD_DIM
    seg_ones = jnp.asarray(seg[:, None] == seg[None, :], dtype=BF16)
    args = (w, rel_bias, norm1_g, da_lambda, da_subln_g, gla_norm_g, norm2_g, final_g, seg_ones)
    return (_encoder(x_prompt, *args), _encoder(x_sample, *args))
```

```python
import functools
import math

import numpy as np
import jax
import jax.numpy as jnp
from jax import lax
from jax.experimental import pallas as pl
from jax.experimental.pallas import tpu as pltpu

D_MODEL = 1024
DEPTH = 2
GRID_W = 64
NORM_EPS = 1e-6
N_BRANCH = 3
BRANCH_W = 512

DA_HEADS = 4
DA_HEAD_DIM = 64
DA_V_DIM = 128
REL_BUCKETS = 32
REL_MAX_DIST = 128

GLA_HEADS = 4
GLA_DK = 64
GLA_DV = 128
GLA_GATE_RANK = 16
GLA_TAU = 16.0
GLA_CHUNK = 64

GQ_HEADS = 8
GQ_KV_HEADS = 2
GQ_GROUP = GQ_HEADS // GQ_KV_HEADS
GQ_HEAD_DIM = 64
ROPE_THETA = 10000.0

PEER_HEADS = 8
PEER_KEYS = 128
PEER_EXPERTS = PEER_KEYS * PEER_KEYS
PEER_QDIM = 256
PEER_HALF = 128
PEER_TOPK = 16

COL_GZ = 0
COL_AQ, COL_AK, COL_AV = 3072, 3584, 4096
COL_BQ, COL_BK, COL_BV, COL_BR = 4608, 4864, 5120, 5632
COL_CQ, COL_CK, COL_CV = 6144, 6656, 6784
COL_BZ = 6912
PROJ_W = 7168
ORIG_BZ = 3072
ORIG_CQ = ORIG_BZ + 2 * GLA_GATE_RANK
ORIG_GZ = ORIG_CQ + 768
ORIG_W = ORIG_GZ + N_BRANCH * D_MODEL

DA_TQ, DA_TK = 1024, 512
GQ_TQ, GQ_TK = 512, 512

VMEM_LIMIT = 56 * 1024 * 1024

F32 = jnp.float32
BF16 = jnp.bfloat16
NEG_INF = float("-inf")
LOG2E = math.log2(math.e)


def _cparams(sem):
    return pltpu.CompilerParams(dimension_semantics=sem, vmem_limit_bytes=VMEM_LIMIT)


def _dot(a, b):
    return jnp.dot(a, b, preferred_element_type=F32)


def _dot_nt(a, b):
    return lax.dot_general(a, b, (((1,), (1,)), ((), ())), preferred_element_type=F32)


def _dot_tn(a, b):
    return lax.dot_general(a, b, (((0,), (0,)), ((), ())), preferred_element_type=F32)


def _rms(x, g):
    return x * lax.rsqrt(jnp.mean(x * x, axis=-1, keepdims=True) + NORM_EPS) * g


def _in_proj_kernel(x_ref, g_ref, w_ref, o_ref, hn_ref):
    @pl.when(pl.program_id(1) == 0)
    def _():
        hn_ref[...] = _rms(x_ref[...], g_ref[...]).astype(BF16)

    o_ref[...] = _dot(hn_ref[...], w_ref[...]).astype(o_ref.dtype)


def in_proj(x2d, g, w, *, tm=2048, tn=1024):
    t = x2d.shape[0]
    nw = w.shape[1]
    return pl.pallas_call(
        _in_proj_kernel,
        grid=(t // tm, nw // tn),
        in_specs=[
            pl.BlockSpec((tm, D_MODEL), lambda i, j: (i, 0)),
            pl.BlockSpec((1, D_MODEL), lambda i, j: (0, 0)),
            pl.BlockSpec((D_MODEL, tn), lambda i, j: (0, j)),
        ],
        out_specs=pl.BlockSpec((tm, tn), lambda i, j: (i, j)),
        out_shape=jax.ShapeDtypeStruct((t, nw), BF16),
        scratch_shapes=[pltpu.VMEM((tm, D_MODEL), BF16)],
        compiler_params=_cparams(("parallel", "arbitrary")),
        name="in_proj",
    )(x2d, g, w)


SUM_ROWS = 16


def _flash_pipe_step(k, vt, bias, qt_ref, cur, prev, m_ref, acc_ref, *, bf16_exp):
    s_cur, p_cur, al_cur, mt_cur = cur
    s_prev, p_prev, al_prev, mt_prev = prev
    s = _dot(k, qt_ref[...])
    if bias is not None:
        s = s + jnp.concatenate([bias] * (s.shape[1] // bias.shape[1]), axis=1)
    s_cur[...] = s
    mt_cur[...] = jnp.max(s, axis=0, keepdims=True)
    acc_ref[...] = al_cur[...] * acc_ref[...] + _dot(vt, p_cur[...])
    m_prev = m_ref[...]
    m_next = jnp.maximum(m_prev, mt_prev[...])
    d = s_prev[...] - m_next
    p_prev[...] = jnp.exp2(d.astype(BF16)) if bf16_exp else jnp.exp2(d).astype(BF16)
    al_prev[...] = jnp.exp2(m_prev - m_next)
    m_ref[...] = m_next


def _flash_pipe(kt, nk, build_qt, k_ref, get_vt, bias_ref, qt_ref, slots, m_ref, acc_ref, *, bf16_exp):
    @pl.when(kt == 0)
    def _():
        build_qt()
        (_, p0, al0, _), (s1, _, _, mt1) = slots
        for r in (s1, mt1, p0, al0, acc_ref):
            r[...] = jnp.zeros(r.shape, r.dtype)

    @pl.when(kt <= 1)
    def _():
        m_ref[...] = jnp.full(m_ref.shape, NEG_INF, F32)

    @pl.when(kt == 2)
    def _():
        acc_ref[...] = jnp.zeros(acc_ref.shape, F32)

    for c in (0, 1):
        @pl.when(kt % 2 == c)
        def _(c=c):
            bias = None if bias_ref is None else bias_ref[0, 0]
            _flash_pipe_step(k_ref[0], get_vt(), bias, qt_ref, slots[c], slots[1 - c], m_ref, acc_ref,
                             bf16_exp=bf16_exp)


def _flash_scratch(tk, cols, dv):
    slot = [pltpu.VMEM((tk, cols), F32), pltpu.VMEM((tk, cols), BF16),
            pltpu.VMEM((1, cols), F32), pltpu.VMEM((1, cols), F32)]
    return slot + slot + [pltpu.VMEM((1, cols), F32), pltpu.VMEM((dv + SUM_ROWS, cols), F32)]


def _dattn_kernel(lam_ref, q_ref, k_ref, v_ref, b_ref, g_ref, o_ref, qt_ref,
                  s0_ref, p0_ref, al0_ref, mt0_ref, s1_ref, p1_ref, al1_ref, mt1_ref,
                  m_ref, acc_ref, *, tq, nk, out_scale):
    kt = pl.program_id(3)

    def build_qt():
        q = q_ref[0].astype(F32) * (LOG2E * DA_HEAD_DIM ** -0.5)
        lane = lax.broadcasted_iota(jnp.int32, q.shape, 1)
        qt_ref[:, 0:tq] = jnp.where(lane < DA_HEAD_DIM, q, 0.0).T.astype(BF16)
        qt_ref[:, tq:2 * tq] = jnp.where(lane >= DA_HEAD_DIM, q, 0.0).T.astype(BF16)

    def get_vt():
        vt = v_ref[0].astype(F32).T.astype(BF16)
        return jnp.concatenate([vt, jnp.ones((SUM_ROWS, vt.shape[1]), BF16)], axis=0)

    slots = ((s0_ref, p0_ref, al0_ref, mt0_ref), (s1_ref, p1_ref, al1_ref, mt1_ref))
    _flash_pipe(kt, nk, build_qt, k_ref, get_vt, b_ref, qt_ref, slots, m_ref, acc_ref, bf16_exp=True)

    @pl.when(kt == nk + 1)
    def _():
        o = acc_ref[0:DA_V_DIM, :] / acc_ref[DA_V_DIM:DA_V_DIM + 1, :]
        o = (o[:, 0:tq] - lam_ref[0] * o[:, tq:2 * tq]).T
        o_ref[0] = (_rms(o, g_ref[...]) * out_scale).astype(o_ref.dtype)


def diff_attn(proj, bias_band, lam, sub_g, *, lam_init, tq, tk):
    b, n, _ = proj.shape
    _, rq, rk = _band_units(tq, tk)
    nk = n // tk
    kern = functools.partial(_dattn_kernel, tq=tq, nk=nk, out_scale=1.0 - lam_init)
    tile = lambda ki, lag: jnp.clip(ki - lag, 0, nk - 1)

    def bias_map(bi, h, qi, ki):
        return (jnp.clip(rk * tile(ki, 0) - rq * qi, -rk - 1, rq + 1) + rk + 1, h, 0, 0)

    return pl.pallas_call(
        kern,
        grid=(b, DA_HEADS, n // tq, nk + 2),
        in_specs=[
            pl.BlockSpec(memory_space=pltpu.SMEM),
            pl.BlockSpec((1, tq, 128), lambda bi, h, qi, ki: (bi, qi, COL_AQ // 128 + h)),
            pl.BlockSpec((1, tk, 128), lambda bi, h, qi, ki: (bi, tile(ki, 0), COL_AK // 128 + h)),
            pl.BlockSpec((1, tk, 128), lambda bi, h, qi, ki: (bi, tile(ki, 2), COL_AV // 128 + h)),
            pl.BlockSpec((1, 1, tk, tq), bias_map),
            pl.BlockSpec((1, DA_V_DIM), lambda bi, h, qi, ki: (0, 0)),
        ],
        out_specs=pl.BlockSpec((1, tq, DA_V_DIM), lambda bi, h, qi, ki: (bi, qi, h)),
        out_shape=jax.ShapeDtypeStruct((b, n, DA_HEADS * DA_V_DIM), BF16),
        scratch_shapes=[pltpu.VMEM((128, 2 * tq), BF16)] + _flash_scratch(tk, 2 * tq, DA_V_DIM),
        compiler_params=_cparams(("parallel", "parallel", "parallel", "arbitrary")),
        name="diff_attn",
    )(lam, proj, proj, proj, bias_band, sub_g)


def _swap16(x):
    w = x.shape[1]
    lane = lax.broadcasted_iota(jnp.int32, x.shape, 1)
    return jnp.where(lane % 32 < 16, pltpu.roll(x, w - 16, 1), pltpu.roll(x, 16, 1))


def _seg_rms(x, seg_ones, g):
    sq = x * x
    hi = sq.astype(BF16)
    lo = (sq - hi.astype(F32)).astype(BF16)
    ss = _dot(hi, seg_ones) + _dot(lo, seg_ones)
    return x * lax.rsqrt(ss * (1.0 / GQ_HEAD_DIM) + NORM_EPS) * g


def _gqa_prep_kernel(q_ref, k_ref, v_ref, cos_ref, sin_ref, gq_ref, gk_ref, ones_ref,
                     qo_ref, ko_ref, vo_ref):
    cos = cos_ref[...]
    sin = sin_ref[...]
    cos4 = jnp.concatenate([cos] * 4, axis=1)
    sin4 = jnp.concatenate([sin] * 4, axis=1)
    ones = ones_ref[...]
    q = _seg_rms(q_ref[0].astype(F32), ones, gq_ref[...])
    k = _seg_rms(k_ref[0].astype(F32), ones[0:128, 0:128], gk_ref[...])
    q = (q * cos4 + _swap16(q) * sin4) * (LOG2E * GQ_HEAD_DIM ** -0.5)
    qo_ref[0] = q.T.astype(BF16)
    ko_ref[0] = (k * cos + _swap16(k) * sin).astype(BF16)
    vt = v_ref[0].astype(F32).T.astype(BF16)
    hd = GQ_HEAD_DIM
    ones = jnp.ones((SUM_ROWS, vt.shape[1]), BF16)
    for h in range(GQ_KV_HEADS):
        vo_ref[0, h, 0:hd, :] = vt[h * hd:(h + 1) * hd]
        vo_ref[0, h, hd:hd + SUM_ROWS, :] = ones


def gqa_prep(proj, cos2, sin2, gq, gk, seg_ones, *, tm=512):
    b, n, _ = proj.shape
    return pl.pallas_call(
        _gqa_prep_kernel,
        grid=(b, n // tm),
        in_specs=[
            pl.BlockSpec((1, tm, 512), lambda bi, i: (bi, i, COL_CQ // 512)),
            pl.BlockSpec((1, tm, 128), lambda bi, i: (bi, i, COL_CK // 128)),
            pl.BlockSpec((1, tm, 128), lambda bi, i: (bi, i, COL_CV // 128)),
            pl.BlockSpec((tm, 128), lambda bi, i: (i, 0)),
            pl.BlockSpec((tm, 128), lambda bi, i: (i, 0)),
            pl.BlockSpec((1, 512), lambda bi, i: (0, 0)),
            pl.BlockSpec((1, 128), lambda bi, i: (0, 0)),
            pl.BlockSpec((512, 512), lambda bi, i: (0, 0)),
        ],
        out_specs=[
            pl.BlockSpec((1, GQ_HEADS * GQ_HEAD_DIM, tm), lambda bi, i: (bi, 0, i)),
            pl.BlockSpec((1, tm, GQ_KV_HEADS * GQ_HEAD_DIM), lambda bi, i: (bi, i, 0)),
            pl.BlockSpec((1, GQ_KV_HEADS, GQ_HEAD_DIM + SUM_ROWS, tm), lambda bi, i: (bi, 0, 0, i)),
        ],
        out_shape=[
            jax.ShapeDtypeStruct((b, GQ_HEADS * GQ_HEAD_DIM, n), BF16),
            jax.ShapeDtypeStruct((b, n, GQ_KV_HEADS * GQ_HEAD_DIM), BF16),
            jax.ShapeDtypeStruct((b, GQ_KV_HEADS, GQ_HEAD_DIM + SUM_ROWS, n), BF16),
        ],
        compiler_params=_cparams(("parallel", "parallel")),
        name="gqa_prep",
    )(proj, proj, proj, cos2, sin2, gq, gk, seg_ones)


def _gqa_kernel(q_ref, k_ref, vt_ref, o_ref, qt_ref,
                s0_ref, p0_ref, al0_ref, mt0_ref, s1_ref, p1_ref, al1_ref, mt1_ref,
                m_ref, acc_ref, *, tq, nk):
    kt = pl.program_id(3)
    hd = GQ_HEAD_DIM

    def build_qt():
        qt_ref[...] = jnp.zeros(qt_ref.shape, BF16)
        base = pl.multiple_of(pl.program_id(1) * hd, hd)
        for r in range(GQ_GROUP):
            qt_ref[pl.ds(base, hd), r * tq:(r + 1) * tq] = q_ref[0, r * hd:(r + 1) * hd, :]

    slots = ((s0_ref, p0_ref, al0_ref, mt0_ref), (s1_ref, p1_ref, al1_ref, mt1_ref))
    _flash_pipe(kt, nk, build_qt, k_ref, lambda: vt_ref[0, 0], None, qt_ref, slots, m_ref, acc_ref,
                bf16_exp=False)

    @pl.when(kt == nk + 1)
    def _():
        o = (acc_ref[0:hd, :] / acc_ref[hd:hd + 1, :]).astype(o_ref.dtype)
        for r in range(GQ_GROUP):
            o_ref[0, r * hd:(r + 1) * hd, :] = o[:, r * tq:(r + 1) * tq]


def gqa_attn(qt, k, vt, *, tq, tk):
    b, _, n = qt.shape
    gw = GQ_GROUP * GQ_HEAD_DIM
    kvw = GQ_KV_HEADS * GQ_HEAD_DIM
    nk = n // tk
    cols = GQ_GROUP * tq
    kern = functools.partial(_gqa_kernel, tq=tq, nk=nk)
    tile = lambda ki, lag: jnp.clip(ki - lag, 0, nk - 1)
    return pl.pallas_call(
        kern,
        grid=(b, GQ_KV_HEADS, n // tq, nk + 2),
        in_specs=[
            pl.BlockSpec((1, gw, tq), lambda bi, g, qi, ki: (bi, g, qi)),
            pl.BlockSpec((1, tk, kvw), lambda bi, g, qi, ki: (bi, tile(ki, 0), 0)),
            pl.BlockSpec((1, 1, GQ_HEAD_DIM + SUM_ROWS, tk), lambda bi, g, qi, ki: (bi, g, 0, tile(ki, 2))),
        ],
        out_specs=pl.BlockSpec((1, gw, tq), lambda bi, g, qi, ki: (bi, g, qi)),
        out_shape=jax.ShapeDtypeStruct((b, GQ_HEADS * GQ_HEAD_DIM, n), BF16),
        scratch_shapes=[pltpu.VMEM((kvw, cols), BF16)] + _flash_scratch(tk, cols, GQ_HEAD_DIM),
        compiler_params=_cparams(("parallel", "parallel", "parallel", "arbitrary")),
        name="gqa_attn",
    )(qt, k, vt)


def _gla_kernel(q_ref, k_ref, v_ref, z_ref, gw_ref, gb_ref, o_ref, st_ref, *, reverse, nchunk):
    L = GLA_CHUNK
    hd = GLA_HEADS * GLA_DK

    @pl.when(pl.program_id(1) == 0)
    def _():
        st_ref[...] = jnp.zeros(st_ref.shape, F32)

    row = lax.broadcasted_iota(jnp.int32, (L, L), 0)
    col = lax.broadcasted_iota(jnp.int32, (L, L), 1)
    if reverse:
        tri = (col >= row).astype(BF16)
        keep = col > row
    else:
        tri = (col <= row).astype(BF16)
        keep = col <= row
    lane = lax.broadcasted_iota(jnp.int32, (L, hd), 1)

    order = range(nchunk - 1, -1, -1) if reverse else range(nchunk)
    for c in order:
        rows = pl.ds(c * L, L)
        q = q_ref[0, rows, :].astype(F32) * (GLA_DK ** -0.5)
        k = k_ref[0, rows, :].astype(F32)
        logit = _dot(z_ref[0, rows, :], gw_ref[...]) + gb_ref[...]
        lg = jax.nn.log_sigmoid(logit) * (1.0 / GLA_TAU)
        hi = lg.astype(BF16)
        lo = (lg - hi.astype(F32)).astype(BF16)
        cum = _dot(tri, hi) + _dot(tri, lo)
        last = cum[0:1, :] if reverse else cum[L - 1:L, :]
        q_dec = q * jnp.exp(cum)
        k_dec = (k * jnp.exp(-cum)).astype(BF16)
        k_last = (k * jnp.exp(last - cum)).astype(BF16)
        decay = jnp.exp(last)
        for h in range(GLA_HEADS):
            head = (lane >= h * GLA_DK) & (lane < (h + 1) * GLA_DK)
            qh = jnp.where(head, q_dec, 0.0).astype(BF16)
            vh = v_ref[0, rows, h * GLA_DV:(h + 1) * GLA_DV]
            a = jnp.where(keep, _dot_nt(qh, k_dec), 0.0)
            st = st_ref[h]
            o = _dot(a.astype(BF16), vh) + _dot_nt(qh, st.astype(BF16))
            o_ref[0, rows, h * GLA_DV:(h + 1) * GLA_DV] = o
            st_ref[h] = st * decay + _dot_tn(vh, k_last)


def gla(proj, gw, gb, *, reverse, tm=256):
    b, n, _ = proj.shape
    nb = n // tm
    kern = functools.partial(_gla_kernel, reverse=reverse, nchunk=tm // GLA_CHUNK)
    blk = (lambda i: nb - 1 - i) if reverse else (lambda i: i)
    hd = GLA_HEADS * GLA_DK
    return pl.pallas_call(
        kern,
        grid=(b, nb),
        in_specs=[
            pl.BlockSpec((1, tm, hd), lambda bi, i: (bi, blk(i), COL_BQ // hd)),
            pl.BlockSpec((1, tm, hd), lambda bi, i: (bi, blk(i), COL_BK // hd)),
            pl.BlockSpec((1, tm, 512), lambda bi, i: (bi, blk(i), COL_BV // 512)),
            pl.BlockSpec((1, tm, 128), lambda bi, i: (bi, blk(i), COL_BZ // 128)),
            pl.BlockSpec((128, hd), lambda bi, i: (0, 0)),
            pl.BlockSpec((1, hd), lambda bi, i: (0, 0)),
        ],
        out_specs=pl.BlockSpec((1, tm, GLA_HEADS * GLA_DV), lambda bi, i: (bi, blk(i), 0)),
        out_shape=jax.ShapeDtypeStruct((b, n, GLA_HEADS * GLA_DV), F32),
        scratch_shapes=[pltpu.VMEM((GLA_HEADS, GLA_DV, hd), F32)],
        compiler_params=_cparams(("parallel", "arbitrary")),
        name="gla_bwd" if reverse else "gla_fwd",
    )(proj, proj, proj, proj, gw, gb)


def _merge_kernel(x_ref, ya_ref, of_ref, ob_ref, r_ref, yc_ref, gz_ref, gn_ref, wb_ref, wo_ref,
                  o_ref):
    gn = gn_ref[...]
    o = of_ref[0] + ob_ref[0]
    r = r_ref[0].astype(F32)
    yb = []
    for h in range(GLA_HEADS):
        sl = slice(h * GLA_DV, (h + 1) * GLA_DV)
        yb.append(_rms(o[:, sl], gn) * jax.nn.silu(r[:, sl]))
    yb = jnp.concatenate(yb, axis=1).astype(BF16)

    pa = _dot(ya_ref[0], wb_ref[0])
    pb = _dot(yb, wb_ref[1])
    pc = _dot_tn(yc_ref[0], wb_ref[2])

    gz = gz_ref[0].astype(F32)
    merged = (jax.nn.sigmoid(gz[:, 0:D_MODEL]) * pa
              + jax.nn.sigmoid(gz[:, D_MODEL:2 * D_MODEL]) * pb
              + jax.nn.sigmoid(gz[:, 2 * D_MODEL:3 * D_MODEL]) * pc)
    o_ref[0] = x_ref[0] + _dot(merged.astype(BF16), wo_ref[...])


def merge(x, ya, of, ob, proj, yc, gn, wb, wo, *, tm=256):
    b, n, _ = x.shape
    tok = lambda bi, i: (bi, i, 0)
    return pl.pallas_call(
        _merge_kernel,
        grid=(b, n // tm),
        in_specs=[
            pl.BlockSpec((1, tm, D_MODEL), tok),
            pl.BlockSpec((1, tm, BRANCH_W), tok),
            pl.BlockSpec((1, tm, BRANCH_W), tok),
            pl.BlockSpec((1, tm, BRANCH_W), tok),
            pl.BlockSpec((1, tm, 512), lambda bi, i: (bi, i, COL_BR // 512)),
            pl.BlockSpec((1, BRANCH_W, tm), lambda bi, i: (bi, 0, i)),
            pl.BlockSpec((1, tm, N_BRANCH * D_MODEL), lambda bi, i: (bi, i, COL_GZ // (N_BRANCH * D_MODEL))),
            pl.BlockSpec((1, GLA_DV), lambda bi, i: (0, 0)),
            pl.BlockSpec((N_BRANCH, BRANCH_W, D_MODEL), lambda bi, i: (0, 0, 0)),
            pl.BlockSpec((D_MODEL, D_MODEL), lambda bi, i: (0, 0)),
        ],
        out_specs=pl.BlockSpec((1, tm, D_MODEL), tok),
        out_shape=jax.ShapeDtypeStruct((b, n, D_MODEL), F32),
        compiler_params=_cparams(("parallel", "parallel")),
        name="merge",
    )(x, ya, of, ob, proj, yc, proj, gn, wb, wo)


TOPX = PEER_TOPK + 1
TOPX_PAD = 24


def _top_values(vals, count):
    out = []
    for _ in range(count):
        m = jnp.max(vals, axis=0, keepdims=True)
        out.append(m)
        vals = jnp.where(vals == m, NEG_INF, vals)
    return out


def _route_kernel(x_ref, g_ref, wq_ref, sk_ref, hn_ref, s1_ref, e1_ref, th_ref, e0_ref):
    tb = x_ref.shape[0]
    hn = _rms(x_ref[...], g_ref[...])
    hn_t = hn.T.astype(BF16)
    hn_ref[...] = hn_t
    q_t = _dot(wq_ref[...], hn_t).astype(BF16)
    pad = jnp.full((TOPX_PAD - TOPX, tb), NEG_INF, F32)
    for h in range(PEER_HEADS):
        r0 = (2 * h) * PEER_HALF
        s0 = _dot(sk_ref[h, 0], q_t[r0:r0 + PEER_HALF])
        s1 = _dot(sk_ref[h, 1], q_t[r0 + PEER_HALF:r0 + 2 * PEER_HALF])
        top0 = _top_values(s0, TOPX)
        top1 = _top_values(s1, TOPX)
        slab0 = jnp.concatenate(top0 + [pad], axis=0)
        slab1 = jnp.concatenate(top1 + [pad], axis=0)
        half = slab1[0:8]
        cand = jnp.concatenate([top0[0] + slab1] + [top0[a] + half for a in range(1, 8)]
                               + [slab0[8:TOPX_PAD] + top1[0]], axis=0)
        best = _top_values(cand, TOPX)
        thresh = 0.5 * (best[PEER_TOPK - 1] + best[PEER_TOPK])
        m0, m1 = top0[0], top1[0]
        z = jnp.zeros_like(m0)
        for c in best[:PEER_TOPK]:
            z = z + jnp.exp(c - best[0])
        e1 = jnp.exp(s1 - m1) / z
        th = thresh - s0
        e0 = jnp.exp(s0 - m0)
        for c in range(tb // 128):
            cs = slice(c * 128, (c + 1) * 128)
            s1_ref[h, c] = s1[:, cs]
            e1_ref[h, c] = e1[:, cs]
            th_ref[h, c] = th[:, cs]
            e0_ref[h, c] = e0[:, cs]


def peer_route(x2d, g, wq_t, subkeys, *, tb=256):
    t = x2d.shape[0]
    hk = jax.ShapeDtypeStruct((PEER_HEADS, t // 128, PEER_KEYS, 128), F32)
    hk_spec = pl.BlockSpec((PEER_HEADS, tb // 128, PEER_KEYS, 128), lambda i: (0, i, 0, 0))
    return pl.pallas_call(
        _route_kernel,
        grid=(t // tb,),
        in_specs=[
            pl.BlockSpec((tb, D_MODEL), lambda i: (i, 0)),
            pl.BlockSpec((1, D_MODEL), lambda i: (0, 0)),
            pl.BlockSpec((PEER_HEADS * PEER_QDIM, D_MODEL), lambda i: (0, 0)),
            pl.BlockSpec((PEER_HEADS, 2, PEER_KEYS, PEER_HALF), lambda i: (0, 0, 0, 0)),
        ],
        out_specs=[pl.BlockSpec((D_MODEL, tb), lambda i: (0, i)), hk_spec, hk_spec, hk_spec, hk_spec],
        out_shape=[jax.ShapeDtypeStruct((D_MODEL, t), BF16), hk, hk, hk, hk],
        compiler_params=_cparams(("parallel",)),
        name="peer_route",
    )(x2d, g, wq_t, subkeys)


DENSE_ROWS = 128

def _dense_kernel(x_ref, hn_ref, u_ref, vt_ref, s1_ref, e1_ref, th_ref, e0_ref, fg_ref, zero_ref, o_ref,
                  wg_ref, acc_ref, *, ne, ib, final_norm):
    e = pl.program_id(1)
    tb = hn_ref.shape[1]

    @pl.when(e == 0)
    def _():
        acc_ref[...] = jnp.zeros(acc_ref.shape, F32)

    a = _dot(u_ref[...], hn_ref[...])
    gelu_c = np.float32(np.sqrt(0.5))
    zero = zero_ref[...]
    for il in range(ib):
        for tc in range(tb // 128):
            cols = slice(tc * 128, (tc + 1) * 128)
            for jc in range(PEER_KEYS // DENSE_ROWS):
                keys = slice(jc * DENSE_ROWS, (jc + 1) * DENSE_ROWS)
                rows = slice(il * PEER_KEYS + jc * DENSE_ROWS, il * PEER_KEYS + (jc + 1) * DENSE_ROWS)
                tiles = (DENSE_ROWS // 8, 8, 128)
                w = jnp.zeros(tiles, F32)
                for h in range(PEER_HEADS):
                    th = (th_ref[h, tc, il:il + 1, :] + zero)[None]
                    e0 = (e0_ref[h, tc, il:il + 1, :] + zero)[None]
                    sel = s1_ref[h, tc, keys, :].reshape(tiles) >= th
                    w = w + jnp.where(sel, e1_ref[h, tc, keys, :].reshape(tiles), 0.0) * e0
                ac = a[rows, cols]
                act = 0.5 * ac * (1.0 + lax.erf(ac * gelu_c))
                wg_ref[rows, cols] = (w.reshape(DENSE_ROWS, 128) * act).astype(BF16)
    acc_ref[...] += _dot(vt_ref[...], wg_ref[...])

    @pl.when(e == ne - 1)
    def _():
        y = x_ref[...] + acc_ref[...].T
        if final_norm:
            y = _rms(y, fg_ref[...])
        o_ref[...] = y


def peer_dense(x2d, hn_t, u, v_t, s1, e1, th, e0, final_g, *, final_norm, tb=512, ib=8):
    t = x2d.shape[0]
    eb = ib * PEER_KEYS
    ne = PEER_EXPERTS // eb
    kern = functools.partial(_dense_kernel, ne=ne, ib=ib, final_norm=final_norm)
    full = pl.BlockSpec((PEER_HEADS, tb // 128, PEER_KEYS, 128), lambda i, e: (0, i, 0, 0))
    part = pl.BlockSpec((PEER_HEADS, tb // 128, ib, 128), lambda i, e: (0, i, e, 0))
    return pl.pallas_call(
        kern,
        grid=(t // tb, ne),
        in_specs=[
            pl.BlockSpec((tb, D_MODEL), lambda i, e: (i, 0)),
            pl.BlockSpec((D_MODEL, tb), lambda i, e: (0, i)),
            pl.BlockSpec((eb, D_MODEL), lambda i, e: (e, 0)),
            pl.BlockSpec((D_MODEL, eb), lambda i, e: (0, e)),
            full, full, part, part,
            pl.BlockSpec((1, D_MODEL), lambda i, e: (0, 0)),
            pl.BlockSpec((8, 128), lambda i, e: (0, 0)),
        ],
        out_specs=pl.BlockSpec((tb, D_MODEL), lambda i, e: (i, 0)),
        out_shape=jax.ShapeDtypeStruct((t, D_MODEL), F32),
        scratch_shapes=[pltpu.VMEM((eb, tb), BF16), pltpu.VMEM((D_MODEL, tb), F32)],
        compiler_params=_cparams(("parallel", "arbitrary")),
        name="peer_dense",
    )(x2d, hn_t, u, v_t, s1, e1, th, e0, final_g, jnp.zeros((8, 128), F32))


def _rel_bucket(rel):
    nb = REL_BUCKETS // 2
    max_exact = nb // 2
    n = jnp.abs(rel)
    nf = jnp.maximum(n, 1).astype(jnp.float32)
    large = max_exact + (jnp.log(nf / max_exact) / math.log(REL_MAX_DIST / max_exact)
                         * (nb - max_exact)).astype(jnp.int32)
    large = jnp.minimum(large, nb - 1)
    return jnp.where(rel > 0, nb, 0) + jnp.where(n < max_exact, n, large)


def _band_units(tq, tk):
    u = min(tq, tk)
    return u, tq // u, tk // u


def _bias_band(rel_table, tq, tk):
    u, rq, rk = _band_units(tq, tk)
    qpos = jnp.arange(tq)
    tiles = []
    for d in range(-rk - 1, rq + 2):
        rel = (d * u + jnp.arange(tk))[:, None] - qpos[None, :]
        onehot = jax.nn.one_hot(_rel_bucket(rel), REL_BUCKETS, dtype=F32)
        tiles.append(jnp.einsum("kqb,bh->hkq", onehot, rel_table.astype(F32),
                                precision=lax.Precision.HIGHEST))
    return jnp.stack(tiles) * LOG2E


def _rope_tables(n):
    rows = n // GRID_W
    row_id = jnp.repeat(jnp.arange(rows), GRID_W).astype(jnp.float32)
    col_id = (jnp.arange(rows * GRID_W) % GRID_W).astype(jnp.float32)
    sec = GQ_HEAD_DIM // 2
    inv = ROPE_THETA ** (-jnp.arange(0, sec, 2, dtype=jnp.float32) / sec)
    cr, sr = jnp.cos(row_id[:, None] * inv), jnp.sin(row_id[:, None] * inv)
    cc, sc = jnp.cos(col_id[:, None] * inv), jnp.sin(col_id[:, None] * inv)
    cos = jnp.concatenate([cr, cr, cc, cc], axis=1)
    sin = jnp.concatenate([-sr, sr, -sc, sc], axis=1)
    return jnp.concatenate([cos, cos], axis=1), jnp.concatenate([sin, sin], axis=1)


def _prep_weights(w_in, gla_gate_w, gla_gate_b, gq_qk_g, w_branch, w_out, peer_wq, peer_subkeys,
                  peer_u, peer_v):
    wi = jnp.concatenate(
        [w_in[:, :, ORIG_GZ:], w_in[:, :, :ORIG_BZ], w_in[:, :, ORIG_CQ:ORIG_GZ],
         w_in[:, :, ORIG_BZ:ORIG_CQ],
         jnp.zeros((DEPTH, D_MODEL, PROJ_W - ORIG_W), w_in.dtype)], axis=2).astype(BF16)
    hd = GLA_HEADS * GLA_DK
    gw = jnp.zeros((DEPTH, 2, 128, hd), F32)
    gw = gw.at[:, 0, 0:GLA_GATE_RANK].set(gla_gate_w[:, 0])
    gw = gw.at[:, 1, GLA_GATE_RANK:2 * GLA_GATE_RANK].set(gla_gate_w[:, 1])
    return dict(
        w_in=wi, gw=gw.astype(BF16), gb=gla_gate_b.astype(F32)[:, :, None, :],
        gq=jnp.tile(gq_qk_g[:, 0], (1, GQ_HEADS))[:, None, :],
        gk=jnp.tile(gq_qk_g[:, 1], (1, GQ_KV_HEADS))[:, None, :],
        wb=w_branch.astype(BF16), wo=w_out.astype(BF16),
        wq_t=jnp.swapaxes(peer_wq, 1, 2).astype(BF16), sk=peer_subkeys.astype(BF16),
        u=peer_u.astype(BF16), v_t=jnp.swapaxes(peer_v, 1, 2).astype(BF16))


def _encoder(x, w, rel_bias, norm1_g, da_lambda, da_subln_g, gla_norm_g, norm2_g, final_g, seg_ones):
    b, n, _ = x.shape
    t = b * n
    band = _bias_band(rel_bias, DA_TQ, DA_TK)
    cos2, sin2 = _rope_tables(n)
    for l in range(DEPTH):
        proj = in_proj(x.reshape(t, D_MODEL), norm1_g[l][None, :], w["w_in"][l]).reshape(b, n, PROJ_W)
        lam_init = 0.8 - 0.6 * math.exp(-0.3 * l)
        lp = da_lambda[l].astype(F32)
        lam = (jnp.exp(jnp.sum(lp[0] * lp[1])) - jnp.exp(jnp.sum(lp[2] * lp[3])) + lam_init).reshape(1)
        ya = diff_attn(proj, band, lam, da_subln_g[l][None, :], lam_init=lam_init, tq=DA_TQ, tk=DA_TK)
        qt, kr, vt = gqa_prep(proj, cos2, sin2, w["gq"][l], w["gk"][l], seg_ones)
        yc = gqa_attn(qt, kr, vt, tq=GQ_TQ, tk=GQ_TK)
        of = gla(proj, w["gw"][l, 0], w["gb"][l, 0], reverse=False)
        ob = gla(proj, w["gw"][l, 1], w["gb"][l, 1], reverse=True)
        x1 = merge(x, ya, of, ob, proj, yc, gla_norm_g[l][None, :], w["wb"][l], w["wo"][l])
        x1 = x1.reshape(t, D_MODEL)
        hn_t, s1, e1, th, e0 = peer_route(x1, norm2_g[l][None, :], w["wq_t"][l], w["sk"][l])
        x = peer_dense(x1, hn_t, w["u"][l], w["v_t"][l], s1, e1, th, e0, final_g[None, :],
                       final_norm=(l == DEPTH - 1)).reshape(b, n, D_MODEL)
    return x


def kernel(x_prompt, x_sample, rel_bias, norm1_g, w_in, da_lambda, da_subln_g, gla_gate_w, gla_gate_b,
           gla_norm_g, gq_qk_g, w_branch, w_out, norm2_g, peer_wq, peer_subkeys, peer_u, peer_v, final_g):
    w = _prep_weights(w_in, gla_gate_w, gla_gate_b, gq_qk_g, w_branch, w_out, peer_wq, peer_subkeys,
                      peer_u, peer_v)
    seg = np.arange(512) // GQ_HEAD_DIM
    seg_ones = jnp.asarray(seg[:, None] == seg[None, :], dtype=BF16)
    args = (w, rel_bias, norm1_g, da_lambda, da_subln_g, gla_norm_g, norm2_g, final_g, seg_ones)
    return (_encoder(x_prompt, *args), _encoder(x_sample, *args))
```

```python
import functools
import math

import numpy as np
import jax
import jax.numpy as jnp
from jax import lax
from jax.experimental import pallas as pl
from jax.experimental.pallas import tpu as pltpu

D_MODEL = 1024
DEPTH = 2
GRID_W = 64
NORM_EPS = 1e-6
N_BRANCH = 3
BRANCH_W = 512

DA_HEADS = 4
DA_HEAD_DIM = 64
DA_V_DIM = 128
REL_BUCKETS = 32
REL_MAX_DIST = 128

GLA_HEADS = 4
GLA_DK = 64
GLA_DV = 128
GLA_GATE_RANK = 16
GLA_TAU = 16.0
GLA_CHUNK = 64

GQ_HEADS = 8
GQ_KV_HEADS = 2
GQ_GROUP = GQ_HEADS // GQ_KV_HEADS
GQ_HEAD_DIM = 64
ROPE_THETA = 10000.0

PEER_HEADS = 8
PEER_KEYS = 128
PEER_EXPERTS = PEER_KEYS * PEER_KEYS
PEER_QDIM = 256
PEER_HALF = 128
PEER_TOPK = 16

COL_GZ = 0
COL_AQ, COL_AK, COL_AV = 3072, 3584, 4096
COL_BQ, COL_BK, COL_BV, COL_BR = 4608, 4864, 5120, 5632
COL_CQ, COL_CK, COL_CV = 6144, 6656, 6784
COL_BZ = 6912
PROJ_W = 7168
ORIG_BZ = 3072
ORIG_CQ = ORIG_BZ + 2 * GLA_GATE_RANK
ORIG_GZ = ORIG_CQ + 768
ORIG_W = ORIG_GZ + N_BRANCH * D_MODEL

DA_TQ, DA_TK = 1024, 512
GQ_TQ, GQ_TK = 512, 512

VMEM_LIMIT = 56 * 1024 * 1024

F32 = jnp.float32
BF16 = jnp.bfloat16
NEG_INF = float("-inf")
LOG2E = math.log2(math.e)


def _cparams(sem):
    return pltpu.CompilerParams(dimension_semantics=sem, vmem_limit_bytes=VMEM_LIMIT)


def _dot(a, b):
    return jnp.dot(a, b, preferred_element_type=F32)


def _dot_nt(a, b):
    return lax.dot_general(a, b, (((1,), (1,)), ((), ())), preferred_element_type=F32)


def _dot_tn(a, b):
    return lax.dot_general(a, b, (((0,), (0,)), ((), ())), preferred_element_type=F32)


def _rms(x, g):
    return x * lax.rsqrt(jnp.mean(x * x, axis=-1, keepdims=True) + NORM_EPS) * g


def _in_proj_kernel(x_ref, g_ref, w_ref, o_ref, hn_ref):
    @pl.when(pl.program_id(1) == 0)
    def _():
        hn_ref[...] = _rms(x_ref[...], g_ref[...]).astype(BF16)

    o_ref[...] = _dot(hn_ref[...], w_ref[...]).astype(o_ref.dtype)


def in_proj(x2d, g, w, *, tm=2048, tn=1024):
    t = x2d.shape[0]
    nw = w.shape[1]
    return pl.pallas_call(
        _in_proj_kernel,
        grid=(t // tm, nw // tn),
        in_specs=[
            pl.BlockSpec((tm, D_MODEL), lambda i, j: (i, 0)),
            pl.BlockSpec((1, D_MODEL), lambda i, j: (0, 0)),
            pl.BlockSpec((D_MODEL, tn), lambda i, j: (0, j)),
        ],
        out_specs=pl.BlockSpec((tm, tn), lambda i, j: (i, j)),
        out_shape=jax.ShapeDtypeStruct((t, nw), BF16),
        scratch_shapes=[pltpu.VMEM((tm, D_MODEL), BF16)],
        compiler_params=_cparams(("parallel", "arbitrary")),
        name="in_proj",
    )(x2d, g, w)


SUM_ROWS = 16


def _flash_pipe_step(k, vt, bias, qt_ref, cur, prev, m_ref, acc_ref):
    s_cur, p_cur, al_cur, mt_cur = cur
    s_prev, p_prev, al_prev, mt_prev = prev
    s = _dot(k, qt_ref[...])
    if bias is not None:
        s = s + jnp.concatenate([bias] * (s.shape[1] // bias.shape[1]), axis=1)
    s_cur[...] = s
    mt_cur[...] = jnp.max(s, axis=0, keepdims=True)
    acc_ref[...] = al_cur[...] * acc_ref[...] + _dot(vt, p_cur[...])
    m_prev = m_ref[...]
    m_next = jnp.maximum(m_prev, mt_prev[...])
    p_prev[...] = jnp.exp2(s_prev[...] - m_next).astype(BF16)
    al_prev[...] = jnp.exp2(m_prev - m_next)
    m_ref[...] = m_next


def _pipe_tile(t, lag, nq, nk):
    j = jnp.clip(t - lag, 0, nq * nk - 1)
    return j // nk, j % nk


def _flash_pipe(t, nq, nk, build_qt, k_ref, get_vt, bias_ref, qt_ref, slots, m_ref, acc_ref, finish):
    _, kt_a = _pipe_tile(t, 0, nq, nk)
    _, kt_b = _pipe_tile(t, 1, nq, nk)
    _, kt_c = _pipe_tile(t, 2, nq, nk)

    @pl.when(t == 0)
    def _():
        (_, p0, al0, _), (s1, _, _, mt1) = slots
        for r in (s1, mt1, p0, al0, acc_ref):
            r[...] = jnp.zeros(r.shape, r.dtype)

    @pl.when(kt_a == 0)
    def _():
        build_qt()

    @pl.when(kt_b == 0)
    def _():
        m_ref[...] = jnp.full(m_ref.shape, NEG_INF, F32)

    for c in (0, 1):
        @pl.when(t % 2 == c)
        def _(c=c):
            bias = None if bias_ref is None else bias_ref[0, 0]
            _flash_pipe_step(k_ref[0], get_vt(), bias, qt_ref, slots[c], slots[1 - c], m_ref, acc_ref)

    @pl.when((t >= 2) & (kt_c == nk - 1))
    def _():
        finish()


def _flash_scratch(tk, cols, dv):
    slot = [pltpu.VMEM((tk, cols), F32), pltpu.VMEM((tk, cols), BF16),
            pltpu.VMEM((1, cols), F32), pltpu.VMEM((1, cols), F32)]
    return slot + slot + [pltpu.VMEM((1, cols), F32), pltpu.VMEM((dv + SUM_ROWS, cols), F32)]


def _dattn_kernel(lam_ref, q_ref, k_ref, v_ref, b_ref, g_ref, o_ref, qt_ref,
                  s0_ref, p0_ref, al0_ref, mt0_ref, s1_ref, p1_ref, al1_ref, mt1_ref,
                  m_ref, acc_ref, *, tq, nq, nk, out_scale):

    def build_qt():
        q = q_ref[0].astype(F32) * (LOG2E * DA_HEAD_DIM ** -0.5)
        lane = lax.broadcasted_iota(jnp.int32, q.shape, 1)
        qt_ref[:, 0:tq] = jnp.where(lane < DA_HEAD_DIM, q, 0.0).T.astype(BF16)
        qt_ref[:, tq:2 * tq] = jnp.where(lane >= DA_HEAD_DIM, q, 0.0).T.astype(BF16)

    def get_vt():
        vt = v_ref[0].astype(F32).T.astype(BF16)
        return jnp.concatenate([vt, jnp.ones((SUM_ROWS, vt.shape[1]), BF16)], axis=0)

    def finish():
        o = acc_ref[0:DA_V_DIM, :] / acc_ref[DA_V_DIM:DA_V_DIM + 1, :]
        o = (o[:, 0:tq] - lam_ref[0] * o[:, tq:2 * tq]).T
        o_ref[0] = (_rms(o, g_ref[...]) * out_scale).astype(o_ref.dtype)

    slots = ((s0_ref, p0_ref, al0_ref, mt0_ref), (s1_ref, p1_ref, al1_ref, mt1_ref))
    _flash_pipe(pl.program_id(2), nq, nk, build_qt, k_ref, get_vt, b_ref, qt_ref, slots, m_ref, acc_ref,
                finish)


def diff_attn(proj, bias_band, lam, sub_g, *, lam_init, tq, tk):
    b, n, _ = proj.shape
    _, rq, rk = _band_units(tq, tk)
    nq, nk = n // tq, n // tk
    kern = functools.partial(_dattn_kernel, tq=tq, nq=nq, nk=nk, out_scale=1.0 - lam_init)
    tile = functools.partial(_pipe_tile, nq=nq, nk=nk)

    def bias_map(bi, h, t):
        qi, ki = tile(t, 0)
        return (jnp.clip(rk * ki - rq * qi, -rk - 1, rq + 1) + rk + 1, h, 0, 0)

    return pl.pallas_call(
        kern,
        grid=(b, DA_HEADS, nq * nk + 2),
        in_specs=[
            pl.BlockSpec(memory_space=pltpu.SMEM),
            pl.BlockSpec((1, tq, 128), lambda bi, h, t: (bi, tile(t, 0)[0], COL_AQ // 128 + h)),
            pl.BlockSpec((1, tk, 128), lambda bi, h, t: (bi, tile(t, 0)[1], COL_AK // 128 + h)),
            pl.BlockSpec((1, tk, 128), lambda bi, h, t: (bi, tile(t, 2)[1], COL_AV // 128 + h)),
            pl.BlockSpec((1, 1, tk, tq), bias_map),
            pl.BlockSpec((1, DA_V_DIM), lambda bi, h, t: (0, 0)),
        ],
        out_specs=pl.BlockSpec((1, tq, DA_V_DIM), lambda bi, h, t: (bi, tile(t, 2)[0], h)),
        out_shape=jax.ShapeDtypeStruct((b, n, DA_HEADS * DA_V_DIM), BF16),
        scratch_shapes=[pltpu.VMEM((128, 2 * tq), BF16)] + _flash_scratch(tk, 2 * tq, DA_V_DIM),
        compiler_params=_cparams(("parallel", "parallel", "arbitrary")),
        name="diff_attn",
    )(lam, proj, proj, proj, bias_band, sub_g)


def _swap16(x):
    w = x.shape[1]
    lane = lax.broadcasted_iota(jnp.int32, x.shape, 1)
    return jnp.where(lane % 32 < 16, pltpu.roll(x, w - 16, 1), pltpu.roll(x, 16, 1))


def _seg_rms(x, seg_ones, g):
    sq = x * x
    hi = sq.astype(BF16)
    lo = (sq - hi.astype(F32)).astype(BF16)
    ss = _dot(hi, seg_ones) + _dot(lo, seg_ones)
    return x * lax.rsqrt(ss * (1.0 / GQ_HEAD_DIM) + NORM_EPS) * g


def _gqa_prep_kernel(q_ref, k_ref, v_ref, cos_ref, sin_ref, gq_ref, gk_ref, ones_ref,
                     qo_ref, ko_ref, vo_ref):
    cos = cos_ref[...]
    sin = sin_ref[...]
    cos4 = jnp.concatenate([cos] * 4, axis=1)
    sin4 = jnp.concatenate([sin] * 4, axis=1)
    ones = ones_ref[...]
    q = _seg_rms(q_ref[0].astype(F32), ones, gq_ref[...])
    k = _seg_rms(k_ref[0].astype(F32), ones[0:128, 0:128], gk_ref[...])
    q = (q * cos4 + _swap16(q) * sin4) * (LOG2E * GQ_HEAD_DIM ** -0.5)
    qo_ref[0] = q.T.astype(BF16)
    ko_ref[0] = (k * cos + _swap16(k) * sin).astype(BF16)
    vt = v_ref[0].astype(F32).T.astype(BF16)
    hd = GQ_HEAD_DIM
    ones = jnp.ones((SUM_ROWS, vt.shape[1]), BF16)
    for h in range(GQ_KV_HEADS):
        vo_ref[0, h, 0:hd, :] = vt[h * hd:(h + 1) * hd]
        vo_ref[0, h, hd:hd + SUM_ROWS, :] = ones


def gqa_prep(proj, cos2, sin2, gq, gk, seg_ones, *, tm=512):
    b, n, _ = proj.shape
    return pl.pallas_call(
        _gqa_prep_kernel,
        grid=(b, n // tm),
        in_specs=[
            pl.BlockSpec((1, tm, 512), lambda bi, i: (bi, i, COL_CQ // 512)),
            pl.BlockSpec((1, tm, 128), lambda bi, i: (bi, i, COL_CK // 128)),
            pl.BlockSpec((1, tm, 128), lambda bi, i: (bi, i, COL_CV // 128)),
            pl.BlockSpec((tm, 128), lambda bi, i: (i, 0)),
            pl.BlockSpec((tm, 128), lambda bi, i: (i, 0)),
            pl.BlockSpec((1, 512), lambda bi, i: (0, 0)),
            pl.BlockSpec((1, 128), lambda bi, i: (0, 0)),
            pl.BlockSpec((512, 512), lambda bi, i: (0, 0)),
        ],
        out_specs=[
            pl.BlockSpec((1, GQ_HEADS * GQ_HEAD_DIM, tm), lambda bi, i: (bi, 0, i)),
            pl.BlockSpec((1, tm, GQ_KV_HEADS * GQ_HEAD_DIM), lambda bi, i: (bi, i, 0)),
            pl.BlockSpec((1, GQ_KV_HEADS, GQ_HEAD_DIM + SUM_ROWS, tm), lambda bi, i: (bi, 0, 0, i)),
        ],
        out_shape=[
            jax.ShapeDtypeStruct((b, GQ_HEADS * GQ_HEAD_DIM, n), BF16),
            jax.ShapeDtypeStruct((b, n, GQ_KV_HEADS * GQ_HEAD_DIM), BF16),
            jax.ShapeDtypeStruct((b, GQ_KV_HEADS, GQ_HEAD_DIM + SUM_ROWS, n), BF16),
        ],
        compiler_params=_cparams(("parallel", "parallel")),
        name="gqa_prep",
    )(proj, proj, proj, cos2, sin2, gq, gk, seg_ones)


def _gqa_kernel(q_ref, k_ref, vt_ref, o_ref, qt_ref,
                s0_ref, p0_ref, al0_ref, mt0_ref, s1_ref, p1_ref, al1_ref, mt1_ref,
                m_ref, acc_ref, *, tq, nq, nk):
    hd = GQ_HEAD_DIM

    def build_qt():
        qt_ref[...] = jnp.zeros(qt_ref.shape, BF16)
        base = pl.multiple_of(pl.program_id(1) * hd, hd)
        for r in range(GQ_GROUP):
            qt_ref[pl.ds(base, hd), r * tq:(r + 1) * tq] = q_ref[0, r * hd:(r + 1) * hd, :]

    def finish():
        o = (acc_ref[0:hd, :] / acc_ref[hd:hd + 1, :]).astype(o_ref.dtype)
        for r in range(GQ_GROUP):
            o_ref[0, r * hd:(r + 1) * hd, :] = o[:, r * tq:(r + 1) * tq]

    slots = ((s0_ref, p0_ref, al0_ref, mt0_ref), (s1_ref, p1_ref, al1_ref, mt1_ref))
    _flash_pipe(pl.program_id(2), nq, nk, build_qt, k_ref, lambda: vt_ref[0, 0], None, qt_ref, slots,
                m_ref, acc_ref, finish)


def gqa_attn(qt, k, vt, *, tq, tk):
    b, _, n = qt.shape
    gw = GQ_GROUP * GQ_HEAD_DIM
    kvw = GQ_KV_HEADS * GQ_HEAD_DIM
    nq, nk = n // tq, n // tk
    cols = GQ_GROUP * tq
    kern = functools.partial(_gqa_kernel, tq=tq, nq=nq, nk=nk)
    tile = functools.partial(_pipe_tile, nq=nq, nk=nk)
    return pl.pallas_call(
        kern,
        grid=(b, GQ_KV_HEADS, nq * nk + 2),
        in_specs=[
            pl.BlockSpec((1, gw, tq), lambda bi, g, t: (bi, g, tile(t, 0)[0])),
            pl.BlockSpec((1, tk, kvw), lambda bi, g, t: (bi, tile(t, 0)[1], 0)),
            pl.BlockSpec((1, 1, GQ_HEAD_DIM + SUM_ROWS, tk), lambda bi, g, t: (bi, g, 0, tile(t, 2)[1])),
        ],
        out_specs=pl.BlockSpec((1, gw, tq), lambda bi, g, t: (bi, g, tile(t, 2)[0])),
        out_shape=jax.ShapeDtypeStruct((b, GQ_HEADS * GQ_HEAD_DIM, n), BF16),
        scratch_shapes=[pltpu.VMEM((kvw, cols), BF16)] + _flash_scratch(tk, cols, GQ_HEAD_DIM),
        compiler_params=_cparams(("parallel", "parallel", "arbitrary")),
        name="gqa_attn",
    )(qt, k, vt)


def _gla_kernel(q_ref, k_ref, v_ref, z_ref, gw_ref, gb_ref, o_ref, st_ref, *, reverse, nchunk):
    L = GLA_CHUNK
    hd = GLA_HEADS * GLA_DK

    @pl.when(pl.program_id(1) == 0)
    def _():
        st_ref[...] = jnp.zeros(st_ref.shape, F32)

    row = lax.broadcasted_iota(jnp.int32, (L, L), 0)
    col = lax.broadcasted_iota(jnp.int32, (L, L), 1)
    if reverse:
        tri = (col >= row).astype(BF16)
        keep = col > row
    else:
        tri = (col <= row).astype(BF16)
        keep = col <= row
    lane = lax.broadcasted_iota(jnp.int32, (L, hd), 1)

    order = range(nchunk - 1, -1, -1) if reverse else range(nchunk)
    for c in order:
        rows = pl.ds(c * L, L)
        q = q_ref[0, rows, :].astype(F32) * (GLA_DK ** -0.5)
        k = k_ref[0, rows, :].astype(F32)
        logit = _dot(z_ref[0, rows, :], gw_ref[...]) + gb_ref[...]
        lg = jax.nn.log_sigmoid(logit) * (1.0 / GLA_TAU)
        hi = lg.astype(BF16)
        lo = (lg - hi.astype(F32)).astype(BF16)
        cum = _dot(tri, hi) + _dot(tri, lo)
        last = cum[0:1, :] if reverse else cum[L - 1:L, :]
        q_dec = q * jnp.exp(cum)
        k_dec = (k * jnp.exp(-cum)).astype(BF16)
        k_last = (k * jnp.exp(last - cum)).astype(BF16)
        decay = jnp.exp(last)
        for h in range(GLA_HEADS):
            head = (lane >= h * GLA_DK) & (lane < (h + 1) * GLA_DK)
            qh = jnp.where(head, q_dec, 0.0).astype(BF16)
            vh = v_ref[0, rows, h * GLA_DV:(h + 1) * GLA_DV]
            a = jnp.where(keep, _dot_nt(qh, k_dec), 0.0)
            st = st_ref[h]
            o = _dot(a.astype(BF16), vh) + _dot_nt(qh, st.astype(BF16))
            o_ref[0, rows, h * GLA_DV:(h + 1) * GLA_DV] = o
            st_ref[h] = st * decay + _dot_tn(vh, k_last)


def gla(proj, gw, gb, *, reverse, tm=256):
    b, n, _ = proj.shape
    nb = n // tm
    kern = functools.partial(_gla_kernel, reverse=reverse, nchunk=tm // GLA_CHUNK)
    blk = (lambda i: nb - 1 - i) if reverse else (lambda i: i)
    hd = GLA_HEADS * GLA_DK
    return pl.pallas_call(
        kern,
        grid=(b, nb),
        in_specs=[
            pl.BlockSpec((1, tm, hd), lambda bi, i: (bi, blk(i), COL_BQ // hd)),
            pl.BlockSpec((1, tm, hd), lambda bi, i: (bi, blk(i), COL_BK // hd)),
            pl.BlockSpec((1, tm, 512), lambda bi, i: (bi, blk(i), COL_BV // 512)),
            pl.BlockSpec((1, tm, 128), lambda bi, i: (bi, blk(i), COL_BZ // 128)),
            pl.BlockSpec((128, hd), lambda bi, i: (0, 0)),
            pl.BlockSpec((1, hd), lambda bi, i: (0, 0)),
        ],
        out_specs=pl.BlockSpec((1, tm, GLA_HEADS * GLA_DV), lambda bi, i: (bi, blk(i), 0)),
        out_shape=jax.ShapeDtypeStruct((b, n, GLA_HEADS * GLA_DV), F32),
        scratch_shapes=[pltpu.VMEM((GLA_HEADS, GLA_DV, hd), F32)],
        compiler_params=_cparams(("parallel", "arbitrary")),
        name="gla_bwd" if reverse else "gla_fwd",
    )(proj, proj, proj, proj, gw, gb)


def _merge_kernel(x_ref, ya_ref, of_ref, ob_ref, r_ref, yc_ref, gz_ref, gn_ref, wb_ref, wo_ref,
                  o_ref):
    gn = gn_ref[...]
    o = of_ref[0] + ob_ref[0]
    r = r_ref[0].astype(F32)
    yb = []
    for h in range(GLA_HEADS):
        sl = slice(h * GLA_DV, (h + 1) * GLA_DV)
        yb.append(_rms(o[:, sl], gn) * jax.nn.silu(r[:, sl]))
    yb = jnp.concatenate(yb, axis=1).astype(BF16)

    pa = _dot(ya_ref[0], wb_ref[0])
    pb = _dot(yb, wb_ref[1])
    pc = _dot_tn(yc_ref[0], wb_ref[2])

    gz = gz_ref[0].astype(F32)
    merged = (jax.nn.sigmoid(gz[:, 0:D_MODEL]) * pa
              + jax.nn.sigmoid(gz[:, D_MODEL:2 * D_MODEL]) * pb
              + jax.nn.sigmoid(gz[:, 2 * D_MODEL:3 * D_MODEL]) * pc)
    o_ref[0] = x_ref[0] + _dot(merged.astype(BF16), wo_ref[...])


def merge(x, ya, of, ob, proj, yc, gn, wb, wo, *, tm=256):
    b, n, _ = x.shape
    tok = lambda bi, i: (bi, i, 0)
    return pl.pallas_call(
        _merge_kernel,
        grid=(b, n // tm),
        in_specs=[
            pl.BlockSpec((1, tm, D_MODEL), tok),
            pl.BlockSpec((1, tm, BRANCH_W), tok),
            pl.BlockSpec((1, tm, BRANCH_W), tok),
            pl.BlockSpec((1, tm, BRANCH_W), tok),
            pl.BlockSpec((1, tm, 512), lambda bi, i: (bi, i, COL_BR // 512)),
            pl.BlockSpec((1, BRANCH_W, tm), lambda bi, i: (bi, 0, i)),
            pl.BlockSpec((1, tm, N_BRANCH * D_MODEL), lambda bi, i: (bi, i, COL_GZ // (N_BRANCH * D_MODEL))),
            pl.BlockSpec((1, GLA_DV), lambda bi, i: (0, 0)),
            pl.BlockSpec((N_BRANCH, BRANCH_W, D_MODEL), lambda bi, i: (0, 0, 0)),
            pl.BlockSpec((D_MODEL, D_MODEL), lambda bi, i: (0, 0)),
        ],
        out_specs=pl.BlockSpec((1, tm, D_MODEL), tok),
        out_shape=jax.ShapeDtypeStruct((b, n, D_MODEL), F32),
        compiler_params=_cparams(("parallel", "parallel")),
        name="merge",
    )(x, ya, of, ob, proj, yc, proj, gn, wb, wo)


TOPX = PEER_TOPK + 1
TOPX_PAD = 24


def _top_values(vals, count):
    out = []
    for _ in range(count):
        m = jnp.max(vals, axis=0, keepdims=True)
        out.append(m)
        vals = jnp.where(vals == m, NEG_INF, vals)
    return out


def _route_kernel(x_ref, g_ref, wq_ref, sk_ref, hn_ref, s1_ref, e1_ref, th_ref, e0_ref):
    tb = x_ref.shape[0]
    hn = _rms(x_ref[...], g_ref[...])
    hn_t = hn.T.astype(BF16)
    hn_ref[...] = hn_t
    q_t = _dot(wq_ref[...], hn_t).astype(BF16)
    pad = jnp.full((TOPX_PAD - TOPX, tb), NEG_INF, F32)
    for h in range(PEER_HEADS):
        r0 = (2 * h) * PEER_HALF
        s0 = _dot(sk_ref[h, 0], q_t[r0:r0 + PEER_HALF])
        s1 = _dot(sk_ref[h, 1], q_t[r0 + PEER_HALF:r0 + 2 * PEER_HALF])
        top0 = _top_values(s0, TOPX)
        top1 = _top_values(s1, TOPX)
        slab0 = jnp.concatenate(top0 + [pad], axis=0)
        slab1 = jnp.concatenate(top1 + [pad], axis=0)
        half = slab1[0:8]
        cand = jnp.concatenate([top0[0] + slab1] + [top0[a] + half for a in range(1, 8)]
                               + [slab0[8:TOPX_PAD] + top1[0]], axis=0)
        best = _top_values(cand, TOPX)
        thresh = 0.5 * (best[PEER_TOPK - 1] + best[PEER_TOPK])
        m0, m1 = top0[0], top1[0]
        z = jnp.zeros_like(m0)
        for c in best[:PEER_TOPK]:
            z = z + jnp.exp(c - best[0])
        e1 = jnp.exp(s1 - m1) / z
        th = thresh - s0
        e0 = jnp.exp(s0 - m0)
        for c in range(tb // 128):
            cs = slice(c * 128, (c + 1) * 128)
            s1_ref[h, c] = s1[:, cs]
            e1_ref[h, c] = e1[:, cs]
            th_ref[h, c] = th[:, cs]
            e0_ref[h, c] = e0[:, cs]


def peer_route(x2d, g, wq_t, subkeys, *, tb=256):
    t = x2d.shape[0]
    hk = jax.ShapeDtypeStruct((PEER_HEADS, t // 128, PEER_KEYS, 128), F32)
    hk_spec = pl.BlockSpec((PEER_HEADS, tb // 128, PEER_KEYS, 128), lambda i: (0, i, 0, 0))
    return pl.pallas_call(
        _route_kernel,
        grid=(t // tb,),
        in_specs=[
            pl.BlockSpec((tb, D_MODEL), lambda i: (i, 0)),
            pl.BlockSpec((1, D_MODEL), lambda i: (0, 0)),
            pl.BlockSpec((PEER_HEADS * PEER_QDIM, D_MODEL), lambda i: (0, 0)),
            pl.BlockSpec((PEER_HEADS, 2, PEER_KEYS, PEER_HALF), lambda i: (0, 0, 0, 0)),
        ],
        out_specs=[pl.BlockSpec((D_MODEL, tb), lambda i: (0, i)), hk_spec, hk_spec, hk_spec, hk_spec],
        out_shape=[jax.ShapeDtypeStruct((D_MODEL, t), BF16), hk, hk, hk, hk],
        compiler_params=_cparams(("parallel",)),
        name="peer_route",
    )(x2d, g, wq_t, subkeys)


DENSE_ROWS = 32


def _dense_kernel(x_ref, hn_ref, u_ref, vt_ref, s1_ref, e1_ref, th_ref, e0_ref, fg_ref, o_ref,
                  wg_ref, acc_ref, *, ne, ib, final_norm):
    e = pl.program_id(1)
    tb = hn_ref.shape[1]

    @pl.when(e == 0)
    def _():
        acc_ref[...] = jnp.zeros(acc_ref.shape, F32)

    a = _dot(u_ref[...], hn_ref[...])
    gelu_c = np.float32(np.sqrt(0.5))
    for il in range(ib):
        for tc in range(tb // 128):
            cols = slice(tc * 128, (tc + 1) * 128)
            for jc in range(PEER_KEYS // DENSE_ROWS):
                keys = slice(jc * DENSE_ROWS, (jc + 1) * DENSE_ROWS)
                rows = slice(il * PEER_KEYS + jc * DENSE_ROWS, il * PEER_KEYS + (jc + 1) * DENSE_ROWS)
                w = jnp.zeros((DENSE_ROWS, 128), F32)
                for h in range(PEER_HEADS):
                    sel = s1_ref[h, tc, keys, :] >= th_ref[h, tc, il:il + 1, :]
                    w = w + jnp.where(sel, e1_ref[h, tc, keys, :], 0.0) * e0_ref[h, tc, il:il + 1, :]
                ac = a[rows, cols]
                act = 0.5 * ac * (1.0 + lax.erf(ac * gelu_c))
                wg_ref[rows, cols] = (w * act).astype(BF16)
    acc_ref[...] += _dot(vt_ref[...], wg_ref[...])

    @pl.when(e == ne - 1)
    def _():
        y = x_ref[...] + acc_ref[...].T
        if final_norm:
            y = _rms(y, fg_ref[...])
        o_ref[...] = y


def peer_dense(x2d, hn_t, u, v_t, s1, e1, th, e0, final_g, *, final_norm, tb=512, ib=8):
    t = x2d.shape[0]
    eb = ib * PEER_KEYS
    ne = PEER_EXPERTS // eb
    kern = functools.partial(_dense_kernel, ne=ne, ib=ib, final_norm=final_norm)
    full = pl.BlockSpec((PEER_HEADS, tb // 128, PEER_KEYS, 128), lambda i, e: (0, i, 0, 0))
    part = pl.BlockSpec((PEER_HEADS, tb // 128, ib, 128), lambda i, e: (0, i, e, 0))
    return pl.pallas_call(
        kern,
        grid=(t // tb, ne),
        in_specs=[
            pl.BlockSpec((tb, D_MODEL), lambda i, e: (i, 0)),
            pl.BlockSpec((D_MODEL, tb), lambda i, e: (0, i)),
            pl.BlockSpec((eb, D_MODEL), lambda i, e: (e, 0)),
            pl.BlockSpec((D_MODEL, eb), lambda i, e: (0, e)),
            full, full, part, part,
            pl.BlockSpec((1, D_MODEL), lambda i, e: (0, 0)),
        ],
        out_specs=pl.BlockSpec((tb, D_MODEL), lambda i, e: (i, 0)),
        out_shape=jax.ShapeDtypeStruct((t, D_MODEL), F32),
        scratch_shapes=[pltpu.VMEM((eb, tb), BF16), pltpu.VMEM((D_MODEL, tb), F32)],
        compiler_params=_cparams(("parallel", "arbitrary")),
        name="peer_dense",
    )(x2d, hn_t, u, v_t, s1, e1, th, e0, final_g)


def _rel_bucket(rel):
    nb = REL_BUCKETS // 2
    max_exact = nb // 2
    n = jnp.abs(rel)
    nf = jnp.maximum(n, 1).astype(jnp.float32)
    large = max_exact + (jnp.log(nf / max_exact) / math.log(REL_MAX_DIST / max_exact)
                         * (nb - max_exact)).astype(jnp.int32)
    large = jnp.minimum(large, nb - 1)
    return jnp.where(rel > 0, nb, 0) + jnp.where(n < max_exact, n, large)


def _band_units(tq, tk):
    u = min(tq, tk)
    return u, tq // u, tk // u


def _bias_band(rel_table, tq, tk):
    u, rq, rk = _band_units(tq, tk)
    qpos = jnp.arange(tq)
    tiles = []
    for d in range(-rk - 1, rq + 2):
        rel = (d * u + jnp.arange(tk))[:, None] - qpos[None, :]
        onehot = jax.nn.one_hot(_rel_bucket(rel), REL_BUCKETS, dtype=F32)
        tiles.append(jnp.einsum("kqb,bh->hkq", onehot, rel_table.astype(F32),
                                precision=lax.Precision.HIGHEST))
    return jnp.stack(tiles) * LOG2E


def _rope_tables(n):
    rows = n // GRID_W
    row_id = jnp.repeat(jnp.arange(rows), GRID_W).astype(jnp.float32)
    col_id = (jnp.arange(rows * GRID_W) % GRID_W).astype(jnp.float32)
    sec = GQ_HEAD_DIM // 2
    inv = ROPE_THETA ** (-jnp.arange(0, sec, 2, dtype=jnp.float32) / sec)
    cr, sr = jnp.cos(row_id[:, None] * inv), jnp.sin(row_id[:, None] * inv)
    cc, sc = jnp.cos(col_id[:, None] * inv), jnp.sin(col_id[:, None] * inv)
    cos = jnp.concatenate([cr, cr, cc, cc], axis=1)
    sin = jnp.concatenate([-sr, sr, -sc, sc], axis=1)
    return jnp.concatenate([cos, cos], axis=1), jnp.concatenate([sin, sin], axis=1)


def _prep_weights(w_in, gla_gate_w, gla_gate_b, gq_qk_g, w_branch, w_out, peer_wq, peer_subkeys,
                  peer_u, peer_v):
    wi = jnp.concatenate(
        [w_in[:, :, ORIG_GZ:], w_in[:, :, :ORIG_BZ], w_in[:, :, ORIG_CQ:ORIG_GZ],
         w_in[:, :, ORIG_BZ:ORIG_CQ],
         jnp.zeros((DEPTH, D_MODEL, PROJ_W - ORIG_W), w_in.dtype)], axis=2).astype(BF16)
    hd = GLA_HEADS * GLA_DK
    gw = jnp.zeros((DEPTH, 2, 128, hd), F32)
    gw = gw.at[:, 0, 0:GLA_GATE_RANK].set(gla_gate_w[:, 0])
    gw = gw.at[:, 1, GLA_GATE_RANK:2 * GLA_GATE_RANK].set(gla_gate_w[:, 1])
    return dict(
        w_in=wi, gw=gw.astype(BF16), gb=gla_gate_b.astype(F32)[:, :, None, :],
        gq=jnp.tile(gq_qk_g[:, 0], (1, GQ_HEADS))[:, None, :],
        gk=jnp.tile(gq_qk_g[:, 1], (1, GQ_KV_HEADS))[:, None, :],
        wb=w_branch.astype(BF16), wo=w_out.astype(BF16),
        wq_t=jnp.swapaxes(peer_wq, 1, 2).astype(BF16), sk=peer_subkeys.astype(BF16),
        u=peer_u.astype(BF16), v_t=jnp.swapaxes(peer_v, 1, 2).astype(BF16))


def _encoder(x, w, rel_bias, norm1_g, da_lambda, da_subln_g, gla_norm_g, norm2_g, final_g, seg_ones):
    b, n, _ = x.shape
    t = b * n
    band = _bias_band(rel_bias, DA_TQ, DA_TK)
    cos2, sin2 = _rope_tables(n)
    for l in range(DEPTH):
        proj = in_proj(x.reshape(t, D_MODEL), norm1_g[l][None, :], w["w_in"][l]).reshape(b, n, PROJ_W)
        lam_init = 0.8 - 0.6 * math.exp(-0.3 * l)
        lp = da_lambda[l].astype(F32)
        lam = (jnp.exp(jnp.sum(lp[0] * lp[1])) - jnp.exp(jnp.sum(lp[2] * lp[3])) + lam_init).reshape(1)
        ya = diff_attn(proj, band, lam, da_subln_g[l][None, :], lam_init=lam_init, tq=DA_TQ, tk=DA_TK)
        qt, kr, vt = gqa_prep(proj, cos2, sin2, w["gq"][l], w["gk"][l], seg_ones)
        yc = gqa_attn(qt, kr, vt, tq=GQ_TQ, tk=GQ_TK)
        of = gla(proj, w["gw"][l, 0], w["gb"][l, 0], reverse=False)
        ob = gla(proj, w["gw"][l, 1], w["gb"][l, 1], reverse=True)
        x1 = merge(x, ya, of, ob, proj, yc, gla_norm_g[l][None, :], w["wb"][l], w["wo"][l])
        x1 = x1.reshape(t, D_MODEL)
        hn_t, s1, e1, th, e0 = peer_route(x1, norm2_g[l][None, :], w["wq_t"][l], w["sk"][l])
        x = peer_dense(x1, hn_t, w["u"][l], w["v_t"][l], s1, e1, th, e0, final_g[None, :],
                       final_norm=(l == DEPTH - 1)).reshape(b, n, D_MODEL)
    return x


def kernel(x_prompt, x_sample, rel_bias, norm1_g, w_in, da_lambda, da_subln_g, gla_gate_w, gla_gate_b,
           gla_norm_g, gq_qk_g, w_branch, w_out, norm2_g, peer_wq, peer_subkeys, peer_u, peer_v, final_g):
    w = _prep_weights(w_in, gla_gate_w, gla_gate_b, gq_qk_g, w_branch, w_out, peer_wq, peer_subkeys,
                      peer_u, peer_v)
    seg = np.arange(512) // GQ_HEAD_DIM
    seg_ones = jnp.asarray(seg[:, None] == seg[None, :], dtype=BF16)
    args = (w, rel_bias, norm1_g, da_lambda, da_subln_g, gla_norm_g, norm2_g, final_g, seg_ones)
    return (_encoder(x_prompt, *args), _encoder(x_sample, *args))
```

```python
import functools
import math

import numpy as np
import jax
import jax.numpy as jnp
from jax import lax
from jax.experimental import pallas as pl
from jax.experimental.pallas import tpu as pltpu

D_MODEL = 1024
DEPTH = 2
GRID_W = 64
NORM_EPS = 1e-6
N_BRANCH = 3
BRANCH_W = 512

DA_HEADS = 4
DA_HEAD_DIM = 64
DA_V_DIM = 128
REL_BUCKETS = 32
REL_MAX_DIST = 128

GLA_HEADS = 4
GLA_DK = 64
GLA_DV = 128
GLA_GATE_RANK = 16
GLA_TAU = 16.0
GLA_CHUNK = 64

GQ_HEADS = 8
GQ_KV_HEADS = 2
GQ_GROUP = GQ_HEADS // GQ_KV_HEADS
GQ_HEAD_DIM = 64
ROPE_THETA = 10000.0

PEER_HEADS = 8
PEER_KEYS = 128
PEER_EXPERTS = PEER_KEYS * PEER_KEYS
PEER_QDIM = 256
PEER_HALF = 128
PEER_TOPK = 16

COL_GZ = 0
COL_AQ, COL_AK, COL_AV = 3072, 3584, 4096
COL_BQ, COL_BK, COL_BV, COL_BR = 4608, 4864, 5120, 5632
COL_CQ, COL_CK, COL_CV = 6144, 6656, 6784
COL_BZ = 6912
PROJ_W = 7168
ORIG_BZ = 3072
ORIG_CQ = ORIG_BZ + 2 * GLA_GATE_RANK
ORIG_GZ = ORIG_CQ + 768
ORIG_W = ORIG_GZ + N_BRANCH * D_MODEL

DA_TQ, DA_TK = 1024, 1024
GQ_TQ, GQ_TK = 512, 1024

VMEM_LIMIT = 56 * 1024 * 1024

F32 = jnp.float32
BF16 = jnp.bfloat16
NEG_INF = float("-inf")
LOG2E = math.log2(math.e)


def _cparams(sem):
    return pltpu.CompilerParams(dimension_semantics=sem, vmem_limit_bytes=VMEM_LIMIT)


def _dot(a, b):
    return jnp.dot(a, b, preferred_element_type=F32)


def _dot_nt(a, b):
    return lax.dot_general(a, b, (((1,), (1,)), ((), ())), preferred_element_type=F32)


def _dot_tn(a, b):
    return lax.dot_general(a, b, (((0,), (0,)), ((), ())), preferred_element_type=F32)


def _rms(x, g):
    return x * lax.rsqrt(jnp.mean(x * x, axis=-1, keepdims=True) + NORM_EPS) * g


def _in_proj_kernel(x_ref, g_ref, w_ref, o_ref, hn_ref):
    @pl.when(pl.program_id(1) == 0)
    def _():
        hn_ref[...] = _rms(x_ref[...], g_ref[...]).astype(BF16)

    o_ref[...] = _dot(hn_ref[...], w_ref[...]).astype(o_ref.dtype)


def in_proj(x2d, g, w, *, tm=2048, tn=1024):
    t = x2d.shape[0]
    nw = w.shape[1]
    return pl.pallas_call(
        _in_proj_kernel,
        grid=(t // tm, nw // tn),
        in_specs=[
            pl.BlockSpec((tm, D_MODEL), lambda i, j: (i, 0)),
            pl.BlockSpec((1, D_MODEL), lambda i, j: (0, 0)),
            pl.BlockSpec((D_MODEL, tn), lambda i, j: (0, j)),
        ],
        out_specs=pl.BlockSpec((tm, tn), lambda i, j: (i, j)),
        out_shape=jax.ShapeDtypeStruct((t, nw), BF16),
        scratch_shapes=[pltpu.VMEM((tm, D_MODEL), BF16)],
        compiler_params=_cparams(("parallel", "arbitrary")),
        name="in_proj",
    )(x2d, g, w)


SUM_ROWS = 16


def _flash_pipe_step(k, vt, bias, qt_ref, cur, prev, m_ref, acc_ref):
    s_cur, p_cur, al_cur, mt_cur = cur
    s_prev, p_prev, al_prev, mt_prev = prev
    s = _dot(k, qt_ref[...])
    if bias is not None:
        s = s + jnp.concatenate([bias] * (s.shape[1] // bias.shape[1]), axis=1)
    s_cur[...] = s
    mt_cur[...] = jnp.max(s, axis=0, keepdims=True)
    acc_ref[...] = al_cur[...] * acc_ref[...] + _dot(vt, p_cur[...])
    m_prev = m_ref[...]
    m_next = jnp.maximum(m_prev, mt_prev[...])
    p_prev[...] = jnp.exp2(s_prev[...] - m_next).astype(BF16)
    al_prev[...] = jnp.exp2(m_prev - m_next)
    m_ref[...] = m_next


def _pipe_tile(t, lag, nq, nk):
    j = jnp.clip(t - lag, 0, nq * nk - 1)
    return j // nk, j % nk


def _flash_pipe(t, nq, nk, build_qt, k_ref, get_vt, bias_ref, qt_ref, slots, m_ref, acc_ref, finish):
    _, kt_a = _pipe_tile(t, 0, nq, nk)
    _, kt_b = _pipe_tile(t, 1, nq, nk)
    _, kt_c = _pipe_tile(t, 2, nq, nk)

    @pl.when(t == 0)
    def _():
        (_, p0, al0, _), (s1, _, _, mt1) = slots
        for r in (s1, mt1, p0, al0, acc_ref):
            r[...] = jnp.zeros(r.shape, r.dtype)

    @pl.when(kt_a == 0)
    def _():
        build_qt()

    @pl.when(kt_b == 0)
    def _():
        m_ref[...] = jnp.full(m_ref.shape, NEG_INF, F32)

    for c in (0, 1):
        @pl.when(t % 2 == c)
        def _(c=c):
            bias = None if bias_ref is None else bias_ref[0, 0]
            _flash_pipe_step(k_ref[0], get_vt(), bias, qt_ref, slots[c], slots[1 - c], m_ref, acc_ref)

    @pl.when((t >= 2) & (kt_c == nk - 1))
    def _():
        finish()


def _flash_scratch(tk, cols, dv):
    slot = [pltpu.VMEM((tk, cols), F32), pltpu.VMEM((tk, cols), BF16),
            pltpu.VMEM((1, cols), F32), pltpu.VMEM((1, cols), F32)]
    return slot + slot + [pltpu.VMEM((1, cols), F32), pltpu.VMEM((dv + SUM_ROWS, cols), F32)]


def _dattn_kernel(lam_ref, q_ref, k_ref, v_ref, b_ref, g_ref, o_ref, qt_ref,
                  s0_ref, p0_ref, al0_ref, mt0_ref, s1_ref, p1_ref, al1_ref, mt1_ref,
                  m_ref, acc_ref, *, tq, nq, nk, out_scale):

    def build_qt():
        q = q_ref[0].astype(F32) * (LOG2E * DA_HEAD_DIM ** -0.5)
        lane = lax.broadcasted_iota(jnp.int32, q.shape, 1)
        qt_ref[:, 0:tq] = jnp.where(lane < DA_HEAD_DIM, q, 0.0).T.astype(BF16)
        qt_ref[:, tq:2 * tq] = jnp.where(lane >= DA_HEAD_DIM, q, 0.0).T.astype(BF16)

    def get_vt():
        vt = v_ref[0].astype(F32).T.astype(BF16)
        return jnp.concatenate([vt, jnp.ones((SUM_ROWS, vt.shape[1]), BF16)], axis=0)

    def finish():
        o = acc_ref[0:DA_V_DIM, :] / acc_ref[DA_V_DIM:DA_V_DIM + 1, :]
        o = (o[:, 0:tq] - lam_ref[0] * o[:, tq:2 * tq]).T
        o_ref[0] = (_rms(o, g_ref[...]) * out_scale).astype(o_ref.dtype)

    slots = ((s0_ref, p0_ref, al0_ref, mt0_ref), (s1_ref, p1_ref, al1_ref, mt1_ref))
    _flash_pipe(pl.program_id(2), nq, nk, build_qt, k_ref, get_vt, b_ref, qt_ref, slots, m_ref, acc_ref,
                finish)


def diff_attn(proj, bias_band, lam, sub_g, *, lam_init, tq, tk):
    b, n, _ = proj.shape
    _, rq, rk = _band_units(tq, tk)
    nq, nk = n // tq, n // tk
    kern = functools.partial(_dattn_kernel, tq=tq, nq=nq, nk=nk, out_scale=1.0 - lam_init)
    tile = functools.partial(_pipe_tile, nq=nq, nk=nk)

    def bias_map(bi, h, t):
        qi, ki = tile(t, 0)
        return (jnp.clip(rk * ki - rq * qi, -rk - 1, rq + 1) + rk + 1, h, 0, 0)

    return pl.pallas_call(
        kern,
        grid=(b, DA_HEADS, nq * nk + 2),
        in_specs=[
            pl.BlockSpec(memory_space=pltpu.SMEM),
            pl.BlockSpec((1, tq, 128), lambda bi, h, t: (bi, tile(t, 0)[0], COL_AQ // 128 + h)),
            pl.BlockSpec((1, tk, 128), lambda bi, h, t: (bi, tile(t, 0)[1], COL_AK // 128 + h)),
            pl.BlockSpec((1, tk, 128), lambda bi, h, t: (bi, tile(t, 2)[1], COL_AV // 128 + h)),
            pl.BlockSpec((1, 1, tk, tq), bias_map),
            pl.BlockSpec((1, DA_V_DIM), lambda bi, h, t: (0, 0)),
        ],
        out_specs=pl.BlockSpec((1, tq, DA_V_DIM), lambda bi, h, t: (bi, tile(t, 2)[0], h)),
        out_shape=jax.ShapeDtypeStruct((b, n, DA_HEADS * DA_V_DIM), BF16),
        scratch_shapes=[pltpu.VMEM((128, 2 * tq), BF16)] + _flash_scratch(tk, 2 * tq, DA_V_DIM),
        compiler_params=_cparams(("parallel", "parallel", "arbitrary")),
        name="diff_attn",
    )(lam, proj, proj, proj, bias_band, sub_g)


def _swap16(x):
    w = x.shape[1]
    lane = lax.broadcasted_iota(jnp.int32, x.shape, 1)
    return jnp.where(lane % 32 < 16, pltpu.roll(x, w - 16, 1), pltpu.roll(x, 16, 1))


def _seg_rms(x, seg_ones, g):
    sq = x * x
    hi = sq.astype(BF16)
    lo = (sq - hi.astype(F32)).astype(BF16)
    ss = _dot(hi, seg_ones) + _dot(lo, seg_ones)
    return x * lax.rsqrt(ss * (1.0 / GQ_HEAD_DIM) + NORM_EPS) * g


def _gqa_prep_kernel(q_ref, k_ref, v_ref, cos_ref, sin_ref, gq_ref, gk_ref, ones_ref,
                     qo_ref, ko_ref, vo_ref):
    cos = cos_ref[...]
    sin = sin_ref[...]
    cos4 = jnp.concatenate([cos] * 4, axis=1)
    sin4 = jnp.concatenate([sin] * 4, axis=1)
    ones = ones_ref[...]
    q = _seg_rms(q_ref[0].astype(F32), ones, gq_ref[...])
    k = _seg_rms(k_ref[0].astype(F32), ones[0:128, 0:128], gk_ref[...])
    q = (q * cos4 + _swap16(q) * sin4) * (LOG2E * GQ_HEAD_DIM ** -0.5)
    qo_ref[0] = q.T.astype(BF16)
    ko_ref[0] = (k * cos + _swap16(k) * sin).astype(BF16)
    vt = v_ref[0].astype(F32).T.astype(BF16)
    hd = GQ_HEAD_DIM
    ones = jnp.ones((SUM_ROWS, vt.shape[1]), BF16)
    for h in range(GQ_KV_HEADS):
        vo_ref[0, h, 0:hd, :] = vt[h * hd:(h + 1) * hd]
        vo_ref[0, h, hd:hd + SUM_ROWS, :] = ones


def gqa_prep(proj, cos2, sin2, gq, gk, seg_ones, *, tm=512):
    b, n, _ = proj.shape
    return pl.pallas_call(
        _gqa_prep_kernel,
        grid=(b, n // tm),
        in_specs=[
            pl.BlockSpec((1, tm, 512), lambda bi, i: (bi, i, COL_CQ // 512)),
            pl.BlockSpec((1, tm, 128), lambda bi, i: (bi, i, COL_CK // 128)),
            pl.BlockSpec((1, tm, 128), lambda bi, i: (bi, i, COL_CV // 128)),
            pl.BlockSpec((tm, 128), lambda bi, i: (i, 0)),
            pl.BlockSpec((tm, 128), lambda bi, i: (i, 0)),
            pl.BlockSpec((1, 512), lambda bi, i: (0, 0)),
            pl.BlockSpec((1, 128), lambda bi, i: (0, 0)),
            pl.BlockSpec((512, 512), lambda bi, i: (0, 0)),
        ],
        out_specs=[
            pl.BlockSpec((1, GQ_HEADS * GQ_HEAD_DIM, tm), lambda bi, i: (bi, 0, i)),
            pl.BlockSpec((1, tm, GQ_KV_HEADS * GQ_HEAD_DIM), lambda bi, i: (bi, i, 0)),
            pl.BlockSpec((1, GQ_KV_HEADS, GQ_HEAD_DIM + SUM_ROWS, tm), lambda bi, i: (bi, 0, 0, i)),
        ],
        out_shape=[
            jax.ShapeDtypeStruct((b, GQ_HEADS * GQ_HEAD_DIM, n), BF16),
            jax.ShapeDtypeStruct((b, n, GQ_KV_HEADS * GQ_HEAD_DIM), BF16),
            jax.ShapeDtypeStruct((b, GQ_KV_HEADS, GQ_HEAD_DIM + SUM_ROWS, n), BF16),
        ],
        compiler_params=_cparams(("parallel", "parallel")),
        name="gqa_prep",
    )(proj, proj, proj, cos2, sin2, gq, gk, seg_ones)


def _gqa_kernel(q_ref, k_ref, vt_ref, o_ref, qt_ref,
                s0_ref, p0_ref, al0_ref, mt0_ref, s1_ref, p1_ref, al1_ref, mt1_ref,
                m_ref, acc_ref, *, tq, nq, nk):
    hd = GQ_HEAD_DIM

    def build_qt():
        qt_ref[...] = jnp.zeros(qt_ref.shape, BF16)
        base = pl.multiple_of(pl.program_id(1) * hd, hd)
        for r in range(GQ_GROUP):
            qt_ref[pl.ds(base, hd), r * tq:(r + 1) * tq] = q_ref[0, r * hd:(r + 1) * hd, :]

    def finish():
        o = (acc_ref[0:hd, :] / acc_ref[hd:hd + 1, :]).astype(o_ref.dtype)
        for r in range(GQ_GROUP):
            o_ref[0, r * hd:(r + 1) * hd, :] = o[:, r * tq:(r + 1) * tq]

    slots = ((s0_ref, p0_ref, al0_ref, mt0_ref), (s1_ref, p1_ref, al1_ref, mt1_ref))
    _flash_pipe(pl.program_id(2), nq, nk, build_qt, k_ref, lambda: vt_ref[0, 0], None, qt_ref, slots,
                m_ref, acc_ref, finish)


def gqa_attn(qt, k, vt, *, tq, tk):
    b, _, n = qt.shape
    gw = GQ_GROUP * GQ_HEAD_DIM
    kvw = GQ_KV_HEADS * GQ_HEAD_DIM
    nq, nk = n // tq, n // tk
    cols = GQ_GROUP * tq
    kern = functools.partial(_gqa_kernel, tq=tq, nq=nq, nk=nk)
    tile = functools.partial(_pipe_tile, nq=nq, nk=nk)
    return pl.pallas_call(
        kern,
        grid=(b, GQ_KV_HEADS, nq * nk + 2),
        in_specs=[
            pl.BlockSpec((1, gw, tq), lambda bi, g, t: (bi, g, tile(t, 0)[0])),
            pl.BlockSpec((1, tk, kvw), lambda bi, g, t: (bi, tile(t, 0)[1], 0)),
            pl.BlockSpec((1, 1, GQ_HEAD_DIM + SUM_ROWS, tk), lambda bi, g, t: (bi, g, 0, tile(t, 2)[1])),
        ],
        out_specs=pl.BlockSpec((1, gw, tq), lambda bi, g, t: (bi, g, tile(t, 2)[0])),
        out_shape=jax.ShapeDtypeStruct((b, GQ_HEADS * GQ_HEAD_DIM, n), BF16),
        scratch_shapes=[pltpu.VMEM((kvw, cols), BF16)] + _flash_scratch(tk, cols, GQ_HEAD_DIM),
        compiler_params=_cparams(("parallel", "parallel", "arbitrary")),
        name="gqa_attn",
    )(qt, k, vt)


def _gla_kernel(q_ref, k_ref, v_ref, z_ref, gw_ref, gb_ref, o_ref, st_ref, *, reverse, nchunk):
    L = GLA_CHUNK
    hd = GLA_HEADS * GLA_DK

    @pl.when(pl.program_id(1) == 0)
    def _():
        st_ref[...] = jnp.zeros(st_ref.shape, F32)

    row = lax.broadcasted_iota(jnp.int32, (L, L), 0)
    col = lax.broadcasted_iota(jnp.int32, (L, L), 1)
    if reverse:
        tri = (col >= row).astype(BF16)
        keep = col > row
    else:
        tri = (col <= row).astype(BF16)
        keep = col <= row
    lane = lax.broadcasted_iota(jnp.int32, (L, hd), 1)

    order = range(nchunk - 1, -1, -1) if reverse else range(nchunk)
    for c in order:
        rows = pl.ds(c * L, L)
        q = q_ref[0, rows, :].astype(F32) * (GLA_DK ** -0.5)
        k = k_ref[0, rows, :].astype(F32)
        logit = _dot(z_ref[0, rows, :], gw_ref[...]) + gb_ref[...]
        lg = jax.nn.log_sigmoid(logit) * (1.0 / GLA_TAU)
        hi = lg.astype(BF16)
        lo = (lg - hi.astype(F32)).astype(BF16)
        cum = _dot(tri, hi) + _dot(tri, lo)
        last = cum[0:1, :] if reverse else cum[L - 1:L, :]
        q_dec = q * jnp.exp(cum)
        k_dec = (k * jnp.exp(-cum)).astype(BF16)
        k_last = (k * jnp.exp(last - cum)).astype(BF16)
        decay = jnp.exp(last)
        for h in range(GLA_HEADS):
            head = (lane >= h * GLA_DK) & (lane < (h + 1) * GLA_DK)
            qh = jnp.where(head, q_dec, 0.0).astype(BF16)
            vh = v_ref[0, rows, h * GLA_DV:(h + 1) * GLA_DV]
            a = jnp.where(keep, _dot_nt(qh, k_dec), 0.0)
            st = st_ref[h]
            o = _dot(a.astype(BF16), vh) + _dot_nt(qh, st.astype(BF16))
            o_ref[0, rows, h * GLA_DV:(h + 1) * GLA_DV] = o
            st_ref[h] = st * decay + _dot_tn(vh, k_last)


def gla(proj, gw, gb, *, reverse, tm=256):
    b, n, _ = proj.shape
    nb = n // tm
    kern = functools.partial(_gla_kernel, reverse=reverse, nchunk=tm // GLA_CHUNK)
    blk = (lambda i: nb - 1 - i) if reverse else (lambda i: i)
    hd = GLA_HEADS * GLA_DK
    return pl.pallas_call(
        kern,
        grid=(b, nb),
        in_specs=[
            pl.BlockSpec((1, tm, hd), lambda bi, i: (bi, blk(i), COL_BQ // hd)),
            pl.BlockSpec((1, tm, hd), lambda bi, i: (bi, blk(i), COL_BK // hd)),
            pl.BlockSpec((1, tm, 512), lambda bi, i: (bi, blk(i), COL_BV // 512)),
            pl.BlockSpec((1, tm, 128), lambda bi, i: (bi, blk(i), COL_BZ // 128)),
            pl.BlockSpec((128, hd), lambda bi, i: (0, 0)),
            pl.BlockSpec((1, hd), lambda bi, i: (0, 0)),
        ],
        out_specs=pl.BlockSpec((1, tm, GLA_HEADS * GLA_DV), lambda bi, i: (bi, blk(i), 0)),
        out_shape=jax.ShapeDtypeStruct((b, n, GLA_HEADS * GLA_DV), F32),
        scratch_shapes=[pltpu.VMEM((GLA_HEADS, GLA_DV, hd), F32)],
        compiler_params=_cparams(("parallel", "arbitrary")),
        name="gla_bwd" if reverse else "gla_fwd",
    )(proj, proj, proj, proj, gw, gb)


def _merge_kernel(x_ref, ya_ref, of_ref, ob_ref, r_ref, yc_ref, gz_ref, gn_ref, wb_ref, wo_ref,
                  o_ref):
    gn = gn_ref[...]
    o = of_ref[0] + ob_ref[0]
    r = r_ref[0].astype(F32)
    yb = []
    for h in range(GLA_HEADS):
        sl = slice(h * GLA_DV, (h + 1) * GLA_DV)
        yb.append(_rms(o[:, sl], gn) * jax.nn.silu(r[:, sl]))
    yb = jnp.concatenate(yb, axis=1).astype(BF16)

    pa = _dot(ya_ref[0], wb_ref[0])
    pb = _dot(yb, wb_ref[1])
    pc = _dot_tn(yc_ref[0], wb_ref[2])

    gz = gz_ref[0].astype(F32)
    merged = (jax.nn.sigmoid(gz[:, 0:D_MODEL]) * pa
              + jax.nn.sigmoid(gz[:, D_MODEL:2 * D_MODEL]) * pb
              + jax.nn.sigmoid(gz[:, 2 * D_MODEL:3 * D_MODEL]) * pc)
    o_ref[0] = x_ref[0] + _dot(merged.astype(BF16), wo_ref[...])


def merge(x, ya, of, ob, proj, yc, gn, wb, wo, *, tm=256):
    b, n, _ = x.shape
    tok = lambda bi, i: (bi, i, 0)
    return pl.pallas_call(
        _merge_kernel,
        grid=(b, n // tm),
        in_specs=[
            pl.BlockSpec((1, tm, D_MODEL), tok),
            pl.BlockSpec((1, tm, BRANCH_W), tok),
            pl.BlockSpec((1, tm, BRANCH_W), tok),
            pl.BlockSpec((1, tm, BRANCH_W), tok),
            pl.BlockSpec((1, tm, 512), lambda bi, i: (bi, i, COL_BR // 512)),
            pl.BlockSpec((1, BRANCH_W, tm), lambda bi, i: (bi, 0, i)),
            pl.BlockSpec((1, tm, N_BRANCH * D_MODEL), lambda bi, i: (bi, i, COL_GZ // (N_BRANCH * D_MODEL))),
            pl.BlockSpec((1, GLA_DV), lambda bi, i: (0, 0)),
            pl.BlockSpec((N_BRANCH, BRANCH_W, D_MODEL), lambda bi, i: (0, 0, 0)),
            pl.BlockSpec((D_MODEL, D_MODEL), lambda bi, i: (0, 0)),
        ],
        out_specs=pl.BlockSpec((1, tm, D_MODEL), tok),
        out_shape=jax.ShapeDtypeStruct((b, n, D_MODEL), F32),
        compiler_params=_cparams(("parallel", "parallel")),
        name="merge",
    )(x, ya, of, ob, proj, yc, proj, gn, wb, wo)


TOPX = PEER_TOPK + 1
TOPX_PAD = 24


def _top_values(vals, count):
    out = []
    for _ in range(count):
        m = jnp.max(vals, axis=0, keepdims=True)
        out.append(m)
        vals = jnp.where(vals == m, NEG_INF, vals)
    return out


def _route_kernel(x_ref, g_ref, wq_ref, sk_ref, hn_ref, s1_ref, e1_ref, th_ref, e0_ref):
    tb = x_ref.shape[0]
    hn = _rms(x_ref[...], g_ref[...])
    hn_t = hn.T.astype(BF16)
    hn_ref[...] = hn_t
    q_t = _dot(wq_ref[...], hn_t).astype(BF16)
    pad = jnp.full((TOPX_PAD - TOPX, tb), NEG_INF, F32)
    for h in range(PEER_HEADS):
        r0 = (2 * h) * PEER_HALF
        s0 = _dot(sk_ref[h, 0], q_t[r0:r0 + PEER_HALF])
        s1 = _dot(sk_ref[h, 1], q_t[r0 + PEER_HALF:r0 + 2 * PEER_HALF])
        top0 = _top_values(s0, TOPX)
        top1 = _top_values(s1, TOPX)
        slab0 = jnp.concatenate(top0 + [pad], axis=0)
        slab1 = jnp.concatenate(top1 + [pad], axis=0)
        half = slab1[0:8]
        cand = jnp.concatenate([top0[0] + slab1] + [top0[a] + half for a in range(1, 8)]
                               + [slab0[8:TOPX_PAD] + top1[0]], axis=0)
        best = _top_values(cand, TOPX)
        thresh = 0.5 * (best[PEER_TOPK - 1] + best[PEER_TOPK])
        m0, m1 = top0[0], top1[0]
        z = jnp.zeros_like(m0)
        for c in best[:PEER_TOPK]:
            z = z + jnp.exp(c - best[0])
        e1 = jnp.exp(s1 - m1) / z
        th = thresh - s0
        e0 = jnp.exp(s0 - m0)
        for c in range(tb // 128):
            cs = slice(c * 128, (c + 1) * 128)
            s1_ref[h, c] = s1[:, cs]
            e1_ref[h, c] = e1[:, cs]
            th_ref[h, c] = th[:, cs]
            e0_ref[h, c] = e0[:, cs]


def peer_route(x2d, g, wq_t, subkeys, *, tb=256):
    t = x2d.shape[0]
    hk = jax.ShapeDtypeStruct((PEER_HEADS, t // 128, PEER_KEYS, 128), F32)
    hk_spec = pl.BlockSpec((PEER_HEADS, tb // 128, PEER_KEYS, 128), lambda i: (0, i, 0, 0))
    return pl.pallas_call(
        _route_kernel,
        grid=(t // tb,),
        in_specs=[
            pl.BlockSpec((tb, D_MODEL), lambda i: (i, 0)),
            pl.BlockSpec((1, D_MODEL), lambda i: (0, 0)),
            pl.BlockSpec((PEER_HEADS * PEER_QDIM, D_MODEL), lambda i: (0, 0)),
            pl.BlockSpec((PEER_HEADS, 2, PEER_KEYS, PEER_HALF), lambda i: (0, 0, 0, 0)),
        ],
        out_specs=[pl.BlockSpec((D_MODEL, tb), lambda i: (0, i)), hk_spec, hk_spec, hk_spec, hk_spec],
        out_shape=[jax.ShapeDtypeStruct((D_MODEL, t), BF16), hk, hk, hk, hk],
        compiler_params=_cparams(("parallel",)),
        name="peer_route",
    )(x2d, g, wq_t, subkeys)


DENSE_ROWS = 32


def _dense_kernel(x_ref, hn_ref, u_ref, vt_ref, s1_ref, e1_ref, th_ref, e0_ref, fg_ref, o_ref,
                  wg_ref, acc_ref, *, ne, ib, final_norm):
    e = pl.program_id(1)
    tb = hn_ref.shape[1]

    @pl.when(e == 0)
    def _():
        acc_ref[...] = jnp.zeros(acc_ref.shape, F32)

    a = _dot(u_ref[...], hn_ref[...])
    gelu_c = np.float32(np.sqrt(0.5))
    for il in range(ib):
        for tc in range(tb // 128):
            cols = slice(tc * 128, (tc + 1) * 128)
            for jc in range(PEER_KEYS // DENSE_ROWS):
                keys = slice(jc * DENSE_ROWS, (jc + 1) * DENSE_ROWS)
                rows = slice(il * PEER_KEYS + jc * DENSE_ROWS, il * PEER_KEYS + (jc + 1) * DENSE_ROWS)
                w = jnp.zeros((DENSE_ROWS, 128), F32)
                for h in range(PEER_HEADS):
                    sel = s1_ref[h, tc, keys, :] >= th_ref[h, tc, il:il + 1, :]
                    w = w + jnp.where(sel, e1_ref[h, tc, keys, :], 0.0) * e0_ref[h, tc, il:il + 1, :]
                ac = a[rows, cols]
                act = 0.5 * ac * (1.0 + lax.erf(ac * gelu_c))
                wg_ref[rows, cols] = (w * act).astype(BF16)
    acc_ref[...] += _dot(vt_ref[...], wg_ref[...])

    @pl.when(e == ne - 1)
    def _():
        y = x_ref[...] + acc_ref[...].T
        if final_norm:
            y = _rms(y, fg_ref[...])
        o_ref[...] = y


def peer_dense(x2d, hn_t, u, v_t, s1, e1, th, e0, final_g, *, final_norm, tb=512, ib=8):
    t = x2d.shape[0]
    eb = ib * PEER_KEYS
    ne = PEER_EXPERTS // eb
    kern = functools.partial(_dense_kernel, ne=ne, ib=ib, final_norm=final_norm)
    full = pl.BlockSpec((PEER_HEADS, tb // 128, PEER_KEYS, 128), lambda i, e: (0, i, 0, 0))
    part = pl.BlockSpec((PEER_HEADS, tb // 128, ib, 128), lambda i, e: (0, i, e, 0))
    return pl.pallas_call(
        kern,
        grid=(t // tb, ne),
        in_specs=[
            pl.BlockSpec((tb, D_MODEL), lambda i, e: (i, 0)),
            pl.BlockSpec((D_MODEL, tb), lambda i, e: (0, i)),
            pl.BlockSpec((eb, D_MODEL), lambda i, e: (e, 0)),
            pl.BlockSpec((D_MODEL, eb), lambda i, e: (0, e)),
            full, full, part, part,
            pl.BlockSpec((1, D_MODEL), lambda i, e: (0, 0)),
        ],
        out_specs=pl.BlockSpec((tb, D_MODEL), lambda i, e: (i, 0)),
        out_shape=jax.ShapeDtypeStruct((t, D_MODEL), F32),
        scratch_shapes=[pltpu.VMEM((eb, tb), BF16), pltpu.VMEM((D_MODEL, tb), F32)],
        compiler_params=_cparams(("parallel", "arbitrary")),
        name="peer_dense",
    )(x2d, hn_t, u, v_t, s1, e1, th, e0, final_g)


def _rel_bucket(rel):
    nb = REL_BUCKETS // 2
    max_exact = nb // 2
    n = jnp.abs(rel)
    nf = jnp.maximum(n, 1).astype(jnp.float32)
    large = max_exact + (jnp.log(nf / max_exact) / math.log(REL_MAX_DIST / max_exact)
                         * (nb - max_exact)).astype(jnp.int32)
    large = jnp.minimum(large, nb - 1)
    return jnp.where(rel > 0, nb, 0) + jnp.where(n < max_exact, n, large)


def _band_units(tq, tk):
    u = min(tq, tk)
    return u, tq // u, tk // u


def _bias_band(rel_table, tq, tk):
    u, rq, rk = _band_units(tq, tk)
    qpos = jnp.arange(tq)
    tiles = []
    for d in range(-rk - 1, rq + 2):
        rel = (d * u + jnp.arange(tk))[:, None] - qpos[None, :]
        onehot = jax.nn.one_hot(_rel_bucket(rel), REL_BUCKETS, dtype=F32)
        tiles.append(jnp.einsum("kqb,bh->hkq", onehot, rel_table.astype(F32),
                                precision=lax.Precision.HIGHEST))
    return jnp.stack(tiles) * LOG2E


def _rope_tables(n):
    rows = n // GRID_W
    row_id = jnp.repeat(jnp.arange(rows), GRID_W).astype(jnp.float32)
    col_id = (jnp.arange(rows * GRID_W) % GRID_W).astype(jnp.float32)
    sec = GQ_HEAD_DIM // 2
    inv = ROPE_THETA ** (-jnp.arange(0, sec, 2, dtype=jnp.float32) / sec)
    cr, sr = jnp.cos(row_id[:, None] * inv), jnp.sin(row_id[:, None] * inv)
    cc, sc = jnp.cos(col_id[:, None] * inv), jnp.sin(col_id[:, None] * inv)
    cos = jnp.concatenate([cr, cr, cc, cc], axis=1)
    sin = jnp.concatenate([-sr, sr, -sc, sc], axis=1)
    return jnp.concatenate([cos, cos], axis=1), jnp.concatenate([sin, sin], axis=1)


def _prep_weights(w_in, gla_gate_w, gla_gate_b, gq_qk_g, w_branch, w_out, peer_wq, peer_subkeys,
                  peer_u, peer_v):
    wi = jnp.concatenate(
        [w_in[:, :, ORIG_GZ:], w_in[:, :, :ORIG_BZ], w_in[:, :, ORIG_CQ:ORIG_GZ],
         w_in[:, :, ORIG_BZ:ORIG_CQ],
         jnp.zeros((DEPTH, D_MODEL, PROJ_W - ORIG_W), w_in.dtype)], axis=2).astype(BF16)
    hd = GLA_HEADS * GLA_DK
    gw = jnp.zeros((DEPTH, 2, 128, hd), F32)
    gw = gw.at[:, 0, 0:GLA_GATE_RANK].set(gla_gate_w[:, 0])
    gw = gw.at[:, 1, GLA_GATE_RANK:2 * GLA_GATE_RANK].set(gla_gate_w[:, 1])
    return dict(
        w_in=wi, gw=gw.astype(BF16), gb=gla_gate_b.astype(F32)[:, :, None, :],
        gq=jnp.tile(gq_qk_g[:, 0], (1, GQ_HEADS))[:, None, :],
        gk=jnp.tile(gq_qk_g[:, 1], (1, GQ_KV_HEADS))[:, None, :],
        wb=w_branch.astype(BF16), wo=w_out.astype(BF16),
        wq_t=jnp.swapaxes(peer_wq, 1, 2).astype(BF16), sk=peer_subkeys.astype(BF16),
        u=peer_u.astype(BF16), v_t=jnp.swapaxes(peer_v, 1, 2).astype(BF16))


def _encoder(x, w, rel_bias, norm1_g, da_lambda, da_subln_g, gla_norm_g, norm2_g, final_g, seg_ones):
    b, n, _ = x.shape
    t = b * n
    band = _bias_band(rel_bias, DA_TQ, DA_TK)
    cos2, sin2 = _rope_tables(n)
    for l in range(DEPTH):
        proj = in_proj(x.reshape(t, D_MODEL), norm1_g[l][None, :], w["w_in"][l]).reshape(b, n, PROJ_W)
        lam_init = 0.8 - 0.6 * math.exp(-0.3 * l)
        lp = da_lambda[l].astype(F32)
        lam = (jnp.exp(jnp.sum(lp[0] * lp[1])) - jnp.exp(jnp.sum(lp[2] * lp[3])) + lam_init).reshape(1)
        ya = diff_attn(proj, band, lam, da_subln_g[l][None, :], lam_init=lam_init, tq=DA_TQ, tk=DA_TK)
        qt, kr, vt = gqa_prep(proj, cos2, sin2, w["gq"][l], w["gk"][l], seg_ones)
        yc = gqa_attn(qt, kr, vt, tq=GQ_TQ, tk=GQ_TK)
        of = gla(proj, w["gw"][l, 0], w["gb"][l, 0], reverse=False)
        ob = gla(proj, w["gw"][l, 1], w["gb"][l, 1], reverse=True)
        x1 = merge(x, ya, of, ob, proj, yc, gla_norm_g[l][None, :], w["wb"][l], w["wo"][l])
        x1 = x1.reshape(t, D_MODEL)
        hn_t, s1, e1, th, e0 = peer_route(x1, norm2_g[l][None, :], w["wq_t"][l], w["sk"][l])
        x = peer_dense(x1, hn_t, w["u"][l], w["v_t"][l], s1, e1, th, e0, final_g[None, :],
                       final_norm=(l == DEPTH - 1)).reshape(b, n, D_MODEL)
    return x


def kernel(x_prompt, x_sample, rel_bias, norm1_g, w_in, da_lambda, da_subln_g, gla_gate_w, gla_gate_b,
           gla_norm_g, gq_qk_g, w_branch, w_out, norm2_g, peer_wq, peer_subkeys, peer_u, peer_v, final_g):
    w = _prep_weights(w_in, gla_gate_w, gla_gate_b, gq_qk_g, w_branch, w_out, peer_wq, peer_subkeys,
                      peer_u, peer_v)
    seg = np.arange(512) // GQ_HEAD_DIM
    seg_ones = jnp.asarray(seg[:, None] == seg[None, :], dtype=BF16)
    args = (w, rel_bias, norm1_g, da_lambda, da_subln_g, gla_norm_g, norm2_g, final_g, seg_ones)
    return (_encoder(x_prompt, *args), _encoder(x_sample, *args))
```

```python
import functools
import math

import numpy as np
import jax
import jax.numpy as jnp
from jax import lax
from jax.experimental import pallas as pl
from jax.experimental.pallas import tpu as pltpu

D_MODEL = 1024
DEPTH = 2
GRID_W = 64
NORM_EPS = 1e-6
N_BRANCH = 3
BRANCH_W = 512

DA_HEADS = 4
DA_HEAD_DIM = 64
DA_V_DIM = 128
REL_BUCKETS = 32
REL_MAX_DIST = 128

GLA_HEADS = 4
GLA_DK = 64
GLA_DV = 128
GLA_GATE_RANK = 16
GLA_TAU = 16.0
GLA_CHUNK = 64

GQ_HEADS = 8
GQ_KV_HEADS = 2
GQ_GROUP = GQ_HEADS // GQ_KV_HEADS
GQ_HEAD_DIM = 64
ROPE_THETA = 10000.0

PEER_HEADS = 8
PEER_KEYS = 128
PEER_EXPERTS = PEER_KEYS * PEER_KEYS
PEER_QDIM = 256
PEER_HALF = 128
PEER_TOPK = 16

COL_GZ = 0
COL_AQ, COL_AK, COL_AV = 3072, 3584, 4096
COL_BQ, COL_BK, COL_BV, COL_BR = 4608, 4864, 5120, 5632
COL_CQ, COL_CK, COL_CV = 6144, 6656, 6784
COL_BZ = 6912
PROJ_W = 7168
ORIG_BZ = 3072
ORIG_CQ = ORIG_BZ + 2 * GLA_GATE_RANK
ORIG_GZ = ORIG_CQ + 768
ORIG_W = ORIG_GZ + N_BRANCH * D_MODEL

DA_TQ, DA_TK = 1024, 1024
GQ_TQ, GQ_TK = 1024, 512

VMEM_LIMIT = 56 * 1024 * 1024

F32 = jnp.float32
BF16 = jnp.bfloat16
NEG_INF = float("-inf")
LOG2E = math.log2(math.e)


def _cparams(sem):
    return pltpu.CompilerParams(dimension_semantics=sem, vmem_limit_bytes=VMEM_LIMIT)


def _dot(a, b):
    return jnp.dot(a, b, preferred_element_type=F32)


def _dot_nt(a, b):
    return lax.dot_general(a, b, (((1,), (1,)), ((), ())), preferred_element_type=F32)


def _dot_tn(a, b):
    return lax.dot_general(a, b, (((0,), (0,)), ((), ())), preferred_element_type=F32)


def _rms(x, g):
    return x * lax.rsqrt(jnp.mean(x * x, axis=-1, keepdims=True) + NORM_EPS) * g


def _in_proj_kernel(x_ref, g_ref, w_ref, o_ref, hn_ref):
    @pl.when(pl.program_id(1) == 0)
    def _():
        hn_ref[...] = _rms(x_ref[...], g_ref[...]).astype(BF16)

    o_ref[...] = _dot(hn_ref[...], w_ref[...]).astype(o_ref.dtype)


def in_proj(x2d, g, w, *, tm=2048, tn=1024):
    t = x2d.shape[0]
    nw = w.shape[1]
    return pl.pallas_call(
        _in_proj_kernel,
        grid=(t // tm, nw // tn),
        in_specs=[
            pl.BlockSpec((tm, D_MODEL), lambda i, j: (i, 0)),
            pl.BlockSpec((1, D_MODEL), lambda i, j: (0, 0)),
            pl.BlockSpec((D_MODEL, tn), lambda i, j: (0, j)),
        ],
        out_specs=pl.BlockSpec((tm, tn), lambda i, j: (i, j)),
        out_shape=jax.ShapeDtypeStruct((t, nw), BF16),
        scratch_shapes=[pltpu.VMEM((tm, D_MODEL), BF16)],
        compiler_params=_cparams(("parallel", "arbitrary")),
        name="in_proj",
    )(x2d, g, w)


SUM_ROWS = 16


def _flash_pipe_step(k, vt, bias, qt_ref, cur, prev, m_ref, acc_ref):
    s_cur, p_cur, al_cur, mt_cur = cur
    s_prev, p_prev, al_prev, mt_prev = prev
    s = _dot(k, qt_ref[...])
    if bias is not None:
        s = s + jnp.concatenate([bias] * (s.shape[1] // bias.shape[1]), axis=1)
    s_cur[...] = s
    mt_cur[...] = jnp.max(s, axis=0, keepdims=True)
    acc_ref[...] = al_cur[...] * acc_ref[...] + _dot(vt, p_cur[...])
    m_prev = m_ref[...]
    m_next = jnp.maximum(m_prev, mt_prev[...])
    p_prev[...] = jnp.exp2(s_prev[...] - m_next).astype(BF16)
    al_prev[...] = jnp.exp2(m_prev - m_next)
    m_ref[...] = m_next


def _pipe_tile(t, lag, nq, nk):
    j = jnp.clip(t - lag, 0, nq * nk - 1)
    return j // nk, j % nk


def _flash_pipe(t, nq, nk, build_qt, k_ref, get_vt, bias_ref, qt_ref, slots, m_ref, acc_ref, finish):
    _, kt_a = _pipe_tile(t, 0, nq, nk)
    _, kt_b = _pipe_tile(t, 1, nq, nk)
    _, kt_c = _pipe_tile(t, 2, nq, nk)

    @pl.when(t == 0)
    def _():
        (_, p0, al0, _), (s1, _, _, mt1) = slots
        for r in (s1, mt1, p0, al0, acc_ref):
            r[...] = jnp.zeros(r.shape, r.dtype)

    @pl.when(kt_a == 0)
    def _():
        build_qt()

    @pl.when(kt_b == 0)
    def _():
        m_ref[...] = jnp.full(m_ref.shape, NEG_INF, F32)

    for c in (0, 1):
        @pl.when(t % 2 == c)
        def _(c=c):
            bias = None if bias_ref is None else bias_ref[0, 0]
            _flash_pipe_step(k_ref[0], get_vt(), bias, qt_ref, slots[c], slots[1 - c], m_ref, acc_ref)

    @pl.when((t >= 2) & (kt_c == nk - 1))
    def _():
        finish()


def _flash_scratch(tk, cols, dv):
    slot = [pltpu.VMEM((tk, cols), F32), pltpu.VMEM((tk, cols), BF16),
            pltpu.VMEM((1, cols), F32), pltpu.VMEM((1, cols), F32)]
    return slot + slot + [pltpu.VMEM((1, cols), F32), pltpu.VMEM((dv + SUM_ROWS, cols), F32)]


def _dattn_kernel(lam_ref, q_ref, k_ref, v_ref, b_ref, g_ref, o_ref, qt_ref,
                  s0_ref, p0_ref, al0_ref, mt0_ref, s1_ref, p1_ref, al1_ref, mt1_ref,
                  m_ref, acc_ref, *, tq, nq, nk, out_scale):

    def build_qt():
        q = q_ref[0].astype(F32) * (LOG2E * DA_HEAD_DIM ** -0.5)
        lane = lax.broadcasted_iota(jnp.int32, q.shape, 1)
        qt_ref[:, 0:tq] = jnp.where(lane < DA_HEAD_DIM, q, 0.0).T.astype(BF16)
        qt_ref[:, tq:2 * tq] = jnp.where(lane >= DA_HEAD_DIM, q, 0.0).T.astype(BF16)

    def get_vt():
        vt = v_ref[0].astype(F32).T.astype(BF16)
        return jnp.concatenate([vt, jnp.ones((SUM_ROWS, vt.shape[1]), BF16)], axis=0)

    def finish():
        o = acc_ref[0:DA_V_DIM, :] / acc_ref[DA_V_DIM:DA_V_DIM + 1, :]
        o = (o[:, 0:tq] - lam_ref[0] * o[:, tq:2 * tq]).T
        o_ref[0] = (_rms(o, g_ref[...]) * out_scale).astype(o_ref.dtype)

    slots = ((s0_ref, p0_ref, al0_ref, mt0_ref), (s1_ref, p1_ref, al1_ref, mt1_ref))
    _flash_pipe(pl.program_id(2), nq, nk, build_qt, k_ref, get_vt, b_ref, qt_ref, slots, m_ref, acc_ref,
                finish)


def diff_attn(proj, bias_band, lam, sub_g, *, lam_init, tq, tk):
    b, n, _ = proj.shape
    _, rq, rk = _band_units(tq, tk)
    nq, nk = n // tq, n // tk
    kern = functools.partial(_dattn_kernel, tq=tq, nq=nq, nk=nk, out_scale=1.0 - lam_init)
    tile = functools.partial(_pipe_tile, nq=nq, nk=nk)

    def bias_map(bi, h, t):
        qi, ki = tile(t, 0)
        return (jnp.clip(rk * ki - rq * qi, -rk - 1, rq + 1) + rk + 1, h, 0, 0)

    return pl.pallas_call(
        kern,
        grid=(b, DA_HEADS, nq * nk + 2),
        in_specs=[
            pl.BlockSpec(memory_space=pltpu.SMEM),
            pl.BlockSpec((1, tq, 128), lambda bi, h, t: (bi, tile(t, 0)[0], COL_AQ // 128 + h)),
            pl.BlockSpec((1, tk, 128), lambda bi, h, t: (bi, tile(t, 0)[1], COL_AK // 128 + h)),
            pl.BlockSpec((1, tk, 128), lambda bi, h, t: (bi, tile(t, 2)[1], COL_AV // 128 + h)),
            pl.BlockSpec((1, 1, tk, tq), bias_map),
            pl.BlockSpec((1, DA_V_DIM), lambda bi, h, t: (0, 0)),
        ],
        out_specs=pl.BlockSpec((1, tq, DA_V_DIM), lambda bi, h, t: (bi, tile(t, 2)[0], h)),
        out_shape=jax.ShapeDtypeStruct((b, n, DA_HEADS * DA_V_DIM), BF16),
        scratch_shapes=[pltpu.VMEM((128, 2 * tq), BF16)] + _flash_scratch(tk, 2 * tq, DA_V_DIM),
        compiler_params=_cparams(("parallel", "parallel", "arbitrary")),
        name="diff_attn",
    )(lam, proj, proj, proj, bias_band, sub_g)


def _swap16(x):
    w = x.shape[1]
    lane = lax.broadcasted_iota(jnp.int32, x.shape, 1)
    return jnp.where(lane % 32 < 16, pltpu.roll(x, w - 16, 1), pltpu.roll(x, 16, 1))


def _seg_rms(x, seg_ones, g):
    sq = x * x
    hi = sq.astype(BF16)
    lo = (sq - hi.astype(F32)).astype(BF16)
    ss = _dot(hi, seg_ones) + _dot(lo, seg_ones)
    return x * lax.rsqrt(ss * (1.0 / GQ_HEAD_DIM) + NORM_EPS) * g


def _gqa_prep_kernel(q_ref, k_ref, v_ref, cos_ref, sin_ref, gq_ref, gk_ref, ones_ref,
                     qo_ref, ko_ref, vo_ref):
    cos = cos_ref[...]
    sin = sin_ref[...]
    cos4 = jnp.concatenate([cos] * 4, axis=1)
    sin4 = jnp.concatenate([sin] * 4, axis=1)
    ones = ones_ref[...]
    q = _seg_rms(q_ref[0].astype(F32), ones, gq_ref[...])
    k = _seg_rms(k_ref[0].astype(F32), ones[0:128, 0:128], gk_ref[...])
    q = (q * cos4 + _swap16(q) * sin4) * (LOG2E * GQ_HEAD_DIM ** -0.5)
    qo_ref[0] = q.T.astype(BF16)
    ko_ref[0] = (k * cos + _swap16(k) * sin).astype(BF16)
    vt = v_ref[0].astype(F32).T.astype(BF16)
    hd = GQ_HEAD_DIM
    ones = jnp.ones((SUM_ROWS, vt.shape[1]), BF16)
    for h in range(GQ_KV_HEADS):
        vo_ref[0, h, 0:hd, :] = vt[h * hd:(h + 1) * hd]
        vo_ref[0, h, hd:hd + SUM_ROWS, :] = ones


def gqa_prep(proj, cos2, sin2, gq, gk, seg_ones, *, tm=512):
    b, n, _ = proj.shape
    return pl.pallas_call(
        _gqa_prep_kernel,
        grid=(b, n // tm),
        in_specs=[
            pl.BlockSpec((1, tm, 512), lambda bi, i: (bi, i, COL_CQ // 512)),
            pl.BlockSpec((1, tm, 128), lambda bi, i: (bi, i, COL_CK // 128)),
            pl.BlockSpec((1, tm, 128), lambda bi, i: (bi, i, COL_CV // 128)),
            pl.BlockSpec((tm, 128), lambda bi, i: (i, 0)),
            pl.BlockSpec((tm, 128), lambda bi, i: (i, 0)),
            pl.BlockSpec((1, 512), lambda bi, i: (0, 0)),
            pl.BlockSpec((1, 128), lambda bi, i: (0, 0)),
            pl.BlockSpec((512, 512), lambda bi, i: (0, 0)),
        ],
        out_specs=[
            pl.BlockSpec((1, GQ_HEADS * GQ_HEAD_DIM, tm), lambda bi, i: (bi, 0, i)),
            pl.BlockSpec((1, tm, GQ_KV_HEADS * GQ_HEAD_DIM), lambda bi, i: (bi, i, 0)),
            pl.BlockSpec((1, GQ_KV_HEADS, GQ_HEAD_DIM + SUM_ROWS, tm), lambda bi, i: (bi, 0, 0, i)),
        ],
        out_shape=[
            jax.ShapeDtypeStruct((b, GQ_HEADS * GQ_HEAD_DIM, n), BF16),
            jax.ShapeDtypeStruct((b, n, GQ_KV_HEADS * GQ_HEAD_DIM), BF16),
            jax.ShapeDtypeStruct((b, GQ_KV_HEADS, GQ_HEAD_DIM + SUM_ROWS, n), BF16),
        ],
        compiler_params=_cparams(("parallel", "parallel")),
        name="gqa_prep",
    )(proj, proj, proj, cos2, sin2, gq, gk, seg_ones)


def _gqa_kernel(q_ref, k_ref, vt_ref, o_ref, qt_ref,
                s0_ref, p0_ref, al0_ref, mt0_ref, s1_ref, p1_ref, al1_ref, mt1_ref,
                m_ref, acc_ref, *, tq, nq, nk):
    hd = GQ_HEAD_DIM

    def build_qt():
        qt_ref[...] = jnp.zeros(qt_ref.shape, BF16)
        base = pl.multiple_of(pl.program_id(1) * hd, hd)
        for r in range(GQ_GROUP):
            qt_ref[pl.ds(base, hd), r * tq:(r + 1) * tq] = q_ref[0, r * hd:(r + 1) * hd, :]

    def finish():
        o = (acc_ref[0:hd, :] / acc_ref[hd:hd + 1, :]).astype(o_ref.dtype)
        for r in range(GQ_GROUP):
            o_ref[0, r * hd:(r + 1) * hd, :] = o[:, r * tq:(r + 1) * tq]

    slots = ((s0_ref, p0_ref, al0_ref, mt0_ref), (s1_ref, p1_ref, al1_ref, mt1_ref))
    _flash_pipe(pl.program_id(2), nq, nk, build_qt, k_ref, lambda: vt_ref[0, 0], None, qt_ref, slots,
                m_ref, acc_ref, finish)


def gqa_attn(qt, k, vt, *, tq, tk):
    b, _, n = qt.shape
    gw = GQ_GROUP * GQ_HEAD_DIM
    kvw = GQ_KV_HEADS * GQ_HEAD_DIM
    nq, nk = n // tq, n // tk
    cols = GQ_GROUP * tq
    kern = functools.partial(_gqa_kernel, tq=tq, nq=nq, nk=nk)
    tile = functools.partial(_pipe_tile, nq=nq, nk=nk)
    return pl.pallas_call(
        kern,
        grid=(b, GQ_KV_HEADS, nq * nk + 2),
        in_specs=[
            pl.BlockSpec((1, gw, tq), lambda bi, g, t: (bi, g, tile(t, 0)[0])),
            pl.BlockSpec((1, tk, kvw), lambda bi, g, t: (bi, tile(t, 0)[1], 0)),
            pl.BlockSpec((1, 1, GQ_HEAD_DIM + SUM_ROWS, tk), lambda bi, g, t: (bi, g, 0, tile(t, 2)[1])),
        ],
        out_specs=pl.BlockSpec((1, gw, tq), lambda bi, g, t: (bi, g, tile(t, 2)[0])),
        out_shape=jax.ShapeDtypeStruct((b, GQ_HEADS * GQ_HEAD_DIM, n), BF16),
        scratch_shapes=[pltpu.VMEM((kvw, cols), BF16)] + _flash_scratch(tk, cols, GQ_HEAD_DIM),
        compiler_params=_cparams(("parallel", "parallel", "arbitrary")),
        name="gqa_attn",
    )(qt, k, vt)


def _gla_kernel(q_ref, k_ref, v_ref, z_ref, gw_ref, gb_ref, o_ref, st_ref, *, reverse, nchunk):
    L = GLA_CHUNK
    hd = GLA_HEADS * GLA_DK

    @pl.when(pl.program_id(1) == 0)
    def _():
        st_ref[...] = jnp.zeros(st_ref.shape, F32)

    row = lax.broadcasted_iota(jnp.int32, (L, L), 0)
    col = lax.broadcasted_iota(jnp.int32, (L, L), 1)
    if reverse:
        tri = (col >= row).astype(BF16)
        keep = col > row
    else:
        tri = (col <= row).astype(BF16)
        keep = col <= row
    lane = lax.broadcasted_iota(jnp.int32, (L, hd), 1)

    order = range(nchunk - 1, -1, -1) if reverse else range(nchunk)
    for c in order:
        rows = pl.ds(c * L, L)
        q = q_ref[0, rows, :].astype(F32) * (GLA_DK ** -0.5)
        k = k_ref[0, rows, :].astype(F32)
        logit = _dot(z_ref[0, rows, :], gw_ref[...]) + gb_ref[...]
        lg = jax.nn.log_sigmoid(logit) * (1.0 / GLA_TAU)
        hi = lg.astype(BF16)
        lo = (lg - hi.astype(F32)).astype(BF16)
        cum = _dot(tri, hi) + _dot(tri, lo)
        last = cum[0:1, :] if reverse else cum[L - 1:L, :]
        q_dec = q * jnp.exp(cum)
        k_dec = (k * jnp.exp(-cum)).astype(BF16)
        k_last = (k * jnp.exp(last - cum)).astype(BF16)
        decay = jnp.exp(last)
        for h in range(GLA_HEADS):
            head = (lane >= h * GLA_DK) & (lane < (h + 1) * GLA_DK)
            qh = jnp.where(head, q_dec, 0.0).astype(BF16)
            vh = v_ref[0, rows, h * GLA_DV:(h + 1) * GLA_DV]
            a = jnp.where(keep, _dot_nt(qh, k_dec), 0.0)
            st = st_ref[h]
            o = _dot(a.astype(BF16), vh) + _dot_nt(qh, st.astype(BF16))
            o_ref[0, rows, h * GLA_DV:(h + 1) * GLA_DV] = o
            st_ref[h] = st * decay + _dot_tn(vh, k_last)


def gla(proj, gw, gb, *, reverse, tm=256):
    b, n, _ = proj.shape
    nb = n // tm
    kern = functools.partial(_gla_kernel, reverse=reverse, nchunk=tm // GLA_CHUNK)
    blk = (lambda i: nb - 1 - i) if reverse else (lambda i: i)
    hd = GLA_HEADS * GLA_DK
    return pl.pallas_call(
        kern,
        grid=(b, nb),
        in_specs=[
            pl.BlockSpec((1, tm, hd), lambda bi, i: (bi, blk(i), COL_BQ // hd)),
            pl.BlockSpec((1, tm, hd), lambda bi, i: (bi, blk(i), COL_BK // hd)),
            pl.BlockSpec((1, tm, 512), lambda bi, i: (bi, blk(i), COL_BV // 512)),
            pl.BlockSpec((1, tm, 128), lambda bi, i: (bi, blk(i), COL_BZ // 128)),
            pl.BlockSpec((128, hd), lambda bi, i: (0, 0)),
            pl.BlockSpec((1, hd), lambda bi, i: (0, 0)),
        ],
        out_specs=pl.BlockSpec((1, tm, GLA_HEADS * GLA_DV), lambda bi, i: (bi, blk(i), 0)),
        out_shape=jax.ShapeDtypeStruct((b, n, GLA_HEADS * GLA_DV), F32),
        scratch_shapes=[pltpu.VMEM((GLA_HEADS, GLA_DV, hd), F32)],
        compiler_params=_cparams(("parallel", "arbitrary")),
        name="gla_bwd" if reverse else "gla_fwd",
    )(proj, proj, proj, proj, gw, gb)


def _merge_kernel(x_ref, ya_ref, of_ref, ob_ref, r_ref, yc_ref, gz_ref, gn_ref, wb_ref, wo_ref,
                  o_ref):
    gn = gn_ref[...]
    o = of_ref[0] + ob_ref[0]
    r = r_ref[0].astype(F32)
    yb = []
    for h in range(GLA_HEADS):
        sl = slice(h * GLA_DV, (h + 1) * GLA_DV)
        yb.append(_rms(o[:, sl], gn) * jax.nn.silu(r[:, sl]))
    yb = jnp.concatenate(yb, axis=1).astype(BF16)

    pa = _dot(ya_ref[0], wb_ref[0])
    pb = _dot(yb, wb_ref[1])
    pc = _dot_tn(yc_ref[0], wb_ref[2])

    gz = gz_ref[0].astype(F32)
    merged = (jax.nn.sigmoid(gz[:, 0:D_MODEL]) * pa
              + jax.nn.sigmoid(gz[:, D_MODEL:2 * D_MODEL]) * pb
              + jax.nn.sigmoid(gz[:, 2 * D_MODEL:3 * D_MODEL]) * pc)
    o_ref[0] = x_ref[0] + _dot(merged.astype(BF16), wo_ref[...])


def merge(x, ya, of, ob, proj, yc, gn, wb, wo, *, tm=256):
    b, n, _ = x.shape
    tok = lambda bi, i: (bi, i, 0)
    return pl.pallas_call(
        _merge_kernel,
        grid=(b, n // tm),
        in_specs=[
            pl.BlockSpec((1, tm, D_MODEL), tok),
            pl.BlockSpec((1, tm, BRANCH_W), tok),
            pl.BlockSpec((1, tm, BRANCH_W), tok),
            pl.BlockSpec((1, tm, BRANCH_W), tok),
            pl.BlockSpec((1, tm, 512), lambda bi, i: (bi, i, COL_BR // 512)),
            pl.BlockSpec((1, BRANCH_W, tm), lambda bi, i: (bi, 0, i)),
            pl.BlockSpec((1, tm, N_BRANCH * D_MODEL), lambda bi, i: (bi, i, COL_GZ // (N_BRANCH * D_MODEL))),
            pl.BlockSpec((1, GLA_DV), lambda bi, i: (0, 0)),
            pl.BlockSpec((N_BRANCH, BRANCH_W, D_MODEL), lambda bi, i: (0, 0, 0)),
            pl.BlockSpec((D_MODEL, D_MODEL), lambda bi, i: (0, 0)),
        ],
        out_specs=pl.BlockSpec((1, tm, D_MODEL), tok),
        out_shape=jax.ShapeDtypeStruct((b, n, D_MODEL), F32),
        compiler_params=_cparams(("parallel", "parallel")),
        name="merge",
    )(x, ya, of, ob, proj, yc, proj, gn, wb, wo)


TOPX = PEER_TOPK + 1
TOPX_PAD = 24


def _top_values(vals, count):
    out = []
    for _ in range(count):
        m = jnp.max(vals, axis=0, keepdims=True)
        out.append(m)
        vals = jnp.where(vals == m, NEG_INF, vals)
    return out


def _route_kernel(x_ref, g_ref, wq_ref, sk_ref, hn_ref, s1_ref, e1_ref, th_ref, e0_ref):
    tb = x_ref.shape[0]
    hn = _rms(x_ref[...], g_ref[...])
    hn_t = hn.T.astype(BF16)
    hn_ref[...] = hn_t
    q_t = _dot(wq_ref[...], hn_t).astype(BF16)
    pad = jnp.full((TOPX_PAD - TOPX, tb), NEG_INF, F32)
    for h in range(PEER_HEADS):
        r0 = (2 * h) * PEER_HALF
        s0 = _dot(sk_ref[h, 0], q_t[r0:r0 + PEER_HALF])
        s1 = _dot(sk_ref[h, 1], q_t[r0 + PEER_HALF:r0 + 2 * PEER_HALF])
        top0 = _top_values(s0, TOPX)
        top1 = _top_values(s1, TOPX)
        slab0 = jnp.concatenate(top0 + [pad], axis=0)
        slab1 = jnp.concatenate(top1 + [pad], axis=0)
        half = slab1[0:8]
        cand = jnp.concatenate([top0[0] + slab1] + [top0[a] + half for a in range(1, 8)]
                               + [slab0[8:TOPX_PAD] + top1[0]], axis=0)
        best = _top_values(cand, TOPX)
        thresh = 0.5 * (best[PEER_TOPK - 1] + best[PEER_TOPK])
        m0, m1 = top0[0], top1[0]
        z = jnp.zeros_like(m0)
        for c in best[:PEER_TOPK]:
            z = z + jnp.exp(c - best[0])
        e1 = jnp.exp(s1 - m1) / z
        th = thresh - s0
        e0 = jnp.exp(s0 - m0)
        for c in range(tb // 128):
            cs = slice(c * 128, (c + 1) * 128)
            s1_ref[h, c] = s1[:, cs]
            e1_ref[h, c] = e1[:, cs]
            th_ref[h, c] = th[:, cs]
            e0_ref[h, c] = e0[:, cs]


def peer_route(x2d, g, wq_t, subkeys, *, tb=256):
    t = x2d.shape[0]
    hk = jax.ShapeDtypeStruct((PEER_HEADS, t // 128, PEER_KEYS, 128), F32)
    hk_spec = pl.BlockSpec((PEER_HEADS, tb // 128, PEER_KEYS, 128), lambda i: (0, i, 0, 0))
    return pl.pallas_call(
        _route_kernel,
        grid=(t // tb,),
        in_specs=[
            pl.BlockSpec((tb, D_MODEL), lambda i: (i, 0)),
            pl.BlockSpec((1, D_MODEL), lambda i: (0, 0)),
            pl.BlockSpec((PEER_HEADS * PEER_QDIM, D_MODEL), lambda i: (0, 0)),
            pl.BlockSpec((PEER_HEADS, 2, PEER_KEYS, PEER_HALF), lambda i: (0, 0, 0, 0)),
        ],
        out_specs=[pl.BlockSpec((D_MODEL, tb), lambda i: (0, i)), hk_spec, hk_spec, hk_spec, hk_spec],
        out_shape=[jax.ShapeDtypeStruct((D_MODEL, t), BF16), hk, hk, hk, hk],
        compiler_params=_cparams(("parallel",)),
        name="peer_route",
    )(x2d, g, wq_t, subkeys)


DENSE_ROWS = 32


def _dense_kernel(x_ref, hn_ref, u_ref, vt_ref, s1_ref, e1_ref, th_ref, e0_ref, fg_ref, o_ref,
                  wg_ref, acc_ref, *, ne, ib, final_norm):
    e = pl.program_id(1)
    tb = hn_ref.shape[1]

    @pl.when(e == 0)
    def _():
        acc_ref[...] = jnp.zeros(acc_ref.shape, F32)

    a = _dot(u_ref[...], hn_ref[...])
    gelu_c = np.float32(np.sqrt(0.5))
    for il in range(ib):
        for tc in range(tb // 128):
            cols = slice(tc * 128, (tc + 1) * 128)
            for jc in range(PEER_KEYS // DENSE_ROWS):
                keys = slice(jc * DENSE_ROWS, (jc + 1) * DENSE_ROWS)
                rows = slice(il * PEER_KEYS + jc * DENSE_ROWS, il * PEER_KEYS + (jc + 1) * DENSE_ROWS)
                w = jnp.zeros((DENSE_ROWS, 128), F32)
                for h in range(PEER_HEADS):
                    sel = s1_ref[h, tc, keys, :] >= th_ref[h, tc, il:il + 1, :]
                    w = w + jnp.where(sel, e1_ref[h, tc, keys, :], 0.0) * e0_ref[h, tc, il:il + 1, :]
                ac = a[rows, cols]
                act = 0.5 * ac * (1.0 + lax.erf(ac * gelu_c))
                wg_ref[rows, cols] = (w * act).astype(BF16)
    acc_ref[...] += _dot(vt_ref[...], wg_ref[...])

    @pl.when(e == ne - 1)
    def _():
        y = x_ref[...] + acc_ref[...].T
        if final_norm:
            y = _rms(y, fg_ref[...])
        o_ref[...] = y


def peer_dense(x2d, hn_t, u, v_t, s1, e1, th, e0, final_g, *, final_norm, tb=512, ib=8):
    t = x2d.shape[0]
    eb = ib * PEER_KEYS
    ne = PEER_EXPERTS // eb
    kern = functools.partial(_dense_kernel, ne=ne, ib=ib, final_norm=final_norm)
    full = pl.BlockSpec((PEER_HEADS, tb // 128, PEER_KEYS, 128), lambda i, e: (0, i, 0, 0))
    part = pl.BlockSpec((PEER_HEADS, tb // 128, ib, 128), lambda i, e: (0, i, e, 0))
    return pl.pallas_call(
        kern,
        grid=(t // tb, ne),
        in_specs=[
            pl.BlockSpec((tb, D_MODEL), lambda i, e: (i, 0)),
            pl.BlockSpec((D_MODEL, tb), lambda i, e: (0, i)),
            pl.BlockSpec((eb, D_MODEL), lambda i, e: (e, 0)),
            pl.BlockSpec((D_MODEL, eb), lambda i, e: (0, e)),
            full, full, part, part,
            pl.BlockSpec((1, D_MODEL), lambda i, e: (0, 0)),
        ],
        out_specs=pl.BlockSpec((tb, D_MODEL), lambda i, e: (i, 0)),
        out_shape=jax.ShapeDtypeStruct((t, D_MODEL), F32),
        scratch_shapes=[pltpu.VMEM((eb, tb), BF16), pltpu.VMEM((D_MODEL, tb), F32)],
        compiler_params=_cparams(("parallel", "arbitrary")),
        name="peer_dense",
    )(x2d, hn_t, u, v_t, s1, e1, th, e0, final_g)


def _rel_bucket(rel):
    nb = REL_BUCKETS // 2
    max_exact = nb // 2
    n = jnp.abs(rel)
    nf = jnp.maximum(n, 1).astype(jnp.float32)
    large = max_exact + (jnp.log(nf / max_exact) / math.log(REL_MAX_DIST / max_exact)
                         * (nb - max_exact)).astype(jnp.int32)
    large = jnp.minimum(large, nb - 1)
    return jnp.where(rel > 0, nb, 0) + jnp.where(n < max_exact, n, large)


def _band_units(tq, tk):
    u = min(tq, tk)
    return u, tq // u, tk // u


def _bias_band(rel_table, tq, tk):
    u, rq, rk = _band_units(tq, tk)
    qpos = jnp.arange(tq)
    tiles = []
    for d in range(-rk - 1, rq + 2):
        rel = (d * u + jnp.arange(tk))[:, None] - qpos[None, :]
        onehot = jax.nn.one_hot(_rel_bucket(rel), REL_BUCKETS, dtype=F32)
        tiles.append(jnp.einsum("kqb,bh->hkq", onehot, rel_table.astype(F32),
                                precision=lax.Precision.HIGHEST))
    return jnp.stack(tiles) * LOG2E


def _rope_tables(n):
    rows = n // GRID_W
    row_id = jnp.repeat(jnp.arange(rows), GRID_W).astype(jnp.float32)
    col_id = (jnp.arange(rows * GRID_W) % GRID_W).astype(jnp.float32)
    sec = GQ_HEAD_DIM // 2
    inv = ROPE_THETA ** (-jnp.arange(0, sec, 2, dtype=jnp.float32) / sec)
    cr, sr = jnp.cos(row_id[:, None] * inv), jnp.sin(row_id[:, None] * inv)
    cc, sc = jnp.cos(col_id[:, None] * inv), jnp.sin(col_id[:, None] * inv)
    cos = jnp.concatenate([cr, cr, cc, cc], axis=1)
    sin = jnp.concatenate([-sr, sr, -sc, sc], axis=1)
    return jnp.concatenate([cos, cos], axis=1), jnp.concatenate([sin, sin], axis=1)


def _prep_weights(w_in, gla_gate_w, gla_gate_b, gq_qk_g, w_branch, w_out, peer_wq, peer_subkeys,
                  peer_u, peer_v):
    wi = jnp.concatenate(
        [w_in[:, :, ORIG_GZ:], w_in[:, :, :ORIG_BZ], w_in[:, :, ORIG_CQ:ORIG_GZ],
         w_in[:, :, ORIG_BZ:ORIG_CQ],
         jnp.zeros((DEPTH, D_MODEL, PROJ_W - ORIG_W), w_in.dtype)], axis=2).astype(BF16)
    hd = GLA_HEADS * GLA_DK
    gw = jnp.zeros((DEPTH, 2, 128, hd), F32)
    gw = gw.at[:, 0, 0:GLA_GATE_RANK].set(gla_gate_w[:, 0])
    gw = gw.at[:, 1, GLA_GATE_RANK:2 * GLA_GATE_RANK].set(gla_gate_w[:, 1])
    return dict(
        w_in=wi, gw=gw.astype(BF16), gb=gla_gate_b.astype(F32)[:, :, None, :],
        gq=jnp.tile(gq_qk_g[:, 0], (1, GQ_HEADS))[:, None, :],
        gk=jnp.tile(gq_qk_g[:, 1], (1, GQ_KV_HEADS))[:, None, :],
        wb=w_branch.astype(BF16), wo=w_out.astype(BF16),
        wq_t=jnp.swapaxes(peer_wq, 1, 2).astype(BF16), sk=peer_subkeys.astype(BF16),
        u=peer_u.astype(BF16), v_t=jnp.swapaxes(peer_v, 1, 2).astype(BF16))


def _encoder(x, w, rel_bias, norm1_g, da_lambda, da_subln_g, gla_norm_g, norm2_g, final_g, seg_ones):
    b, n, _ = x.shape
    t = b * n
    band = _bias_band(rel_bias, DA_TQ, DA_TK)
    cos2, sin2 = _rope_tables(n)
    for l in range(DEPTH):
        proj = in_proj(x.reshape(t, D_MODEL), norm1_g[l][None, :], w["w_in"][l]).reshape(b, n, PROJ_W)
        lam_init = 0.8 - 0.6 * math.exp(-0.3 * l)
        lp = da_lambda[l].astype(F32)
        lam = (jnp.exp(jnp.sum(lp[0] * lp[1])) - jnp.exp(jnp.sum(lp[2] * lp[3])) + lam_init).reshape(1)
        ya = diff_attn(proj, band, lam, da_subln_g[l][None, :], lam_init=lam_init, tq=DA_TQ, tk=DA_TK)
        qt, kr, vt = gqa_prep(proj, cos2, sin2, w["gq"][l], w["gk"][l], seg_ones)
        yc = gqa_attn(qt, kr, vt, tq=GQ_TQ, tk=GQ_TK)
        of = gla(proj, w["gw"][l, 0], w["gb"][l, 0], reverse=False)
        ob = gla(proj, w["gw"][l, 1], w["gb"][l, 1], reverse=True)
        x1 = merge(x, ya, of, ob, proj, yc, gla_norm_g[l][None, :], w["wb"][l], w["wo"][l])
        x1 = x1.reshape(t, D_MODEL)
        hn_t, s1, e1, th, e0 = peer_route(x1, norm2_g[l][None, :], w["wq_t"][l], w["sk"][l])
        x = peer_dense(x1, hn_t, w["u"][l], w["v_t"][l], s1, e1, th, e0, final_g[None, :],
                       final_norm=(l == DEPTH - 1)).reshape(b, n, D_MODEL)
    return x


def kernel(x_prompt, x_sample, rel_bias, norm1_g, w_in, da_lambda, da_subln_g, gla_gate_w, gla_gate_b,
           gla_norm_g, gq_qk_g, w_branch, w_out, norm2_g, peer_wq, peer_subkeys, peer_u, peer_v, final_g):
    w = _prep_weights(w_in, gla_gate_w, gla_gate_b, gq_qk_g, w_branch, w_out, peer_wq, peer_subkeys,
                      peer_u, peer_v)
    seg = np.arange(512) // GQ_HEAD_DIM
    seg_ones = jnp.asarray(seg[:, None] == seg[None, :], dtype=BF16)
    args = (w, rel_bias, norm1_g, da_lambda, da_subln_g, gla_norm_g, norm2_g, final_g, seg_ones)
    return (_encoder(x_prompt, *args), _encoder(x_sample, *args))
```

```python
import functools
import math

import numpy as np
import jax
import jax.numpy as jnp
from jax import lax
from jax.experimental import pallas as pl
from jax.experimental.pallas import tpu as pltpu

D_MODEL = 1024
DEPTH = 2
GRID_W = 64
NORM_EPS = 1e-6
N_BRANCH = 3
BRANCH_W = 512

DA_HEADS = 4
DA_HEAD_DIM = 64
DA_V_DIM = 128
REL_BUCKETS = 32
REL_MAX_DIST = 128

GLA_HEADS = 4
GLA_DK = 64
GLA_DV = 128
GLA_GATE_RANK = 16
GLA_TAU = 16.0
GLA_CHUNK = 64

GQ_HEADS = 8
GQ_KV_HEADS = 2
GQ_GROUP = GQ_HEADS // GQ_KV_HEADS
GQ_HEAD_DIM = 64
ROPE_THETA = 10000.0

PEER_HEADS = 8
PEER_KEYS = 128
PEER_EXPERTS = PEER_KEYS * PEER_KEYS
PEER_QDIM = 256
PEER_HALF = 128
PEER_TOPK = 16

COL_GZ = 0
COL_AQ, COL_AK, COL_AV = 3072, 3584, 4096
COL_BQ, COL_BK, COL_BV, COL_BR = 4608, 4864, 5120, 5632
COL_CQ, COL_CK, COL_CV = 6144, 6656, 6784
COL_BZ = 6912
PROJ_W = 7168
ORIG_BZ = 3072
ORIG_CQ = ORIG_BZ + 2 * GLA_GATE_RANK
ORIG_GZ = ORIG_CQ + 768
ORIG_W = ORIG_GZ + N_BRANCH * D_MODEL

DA_TQ, DA_TK = 1024, 1024
GQ_TQ, GQ_TK = 512, 512

VMEM_LIMIT = 56 * 1024 * 1024

F32 = jnp.float32
BF16 = jnp.bfloat16
NEG_INF = float("-inf")
LOG2E = math.log2(math.e)


def _cparams(sem):
    return pltpu.CompilerParams(dimension_semantics=sem, vmem_limit_bytes=VMEM_LIMIT)


def _dot(a, b):
    return jnp.dot(a, b, preferred_element_type=F32)


def _dot_nt(a, b):
    return lax.dot_general(a, b, (((1,), (1,)), ((), ())), preferred_element_type=F32)


def _dot_tn(a, b):
    return lax.dot_general(a, b, (((0,), (0,)), ((), ())), preferred_element_type=F32)


def _rms(x, g):
    return x * lax.rsqrt(jnp.mean(x * x, axis=-1, keepdims=True) + NORM_EPS) * g


def _in_proj_kernel(x_ref, g_ref, w_ref, o_ref, hn_ref):
    @pl.when(pl.program_id(1) == 0)
    def _():
        hn_ref[...] = _rms(x_ref[...], g_ref[...]).astype(BF16)

    o_ref[...] = _dot(hn_ref[...], w_ref[...]).astype(o_ref.dtype)


def in_proj(x2d, g, w, *, tm=2048, tn=1024):
    t = x2d.shape[0]
    nw = w.shape[1]
    return pl.pallas_call(
        _in_proj_kernel,
        grid=(t // tm, nw // tn),
        in_specs=[
            pl.BlockSpec((tm, D_MODEL), lambda i, j: (i, 0)),
            pl.BlockSpec((1, D_MODEL), lambda i, j: (0, 0)),
            pl.BlockSpec((D_MODEL, tn), lambda i, j: (0, j)),
        ],
        out_specs=pl.BlockSpec((tm, tn), lambda i, j: (i, j)),
        out_shape=jax.ShapeDtypeStruct((t, nw), BF16),
        scratch_shapes=[pltpu.VMEM((tm, D_MODEL), BF16)],
        compiler_params=_cparams(("parallel", "arbitrary")),
        name="in_proj",
    )(x2d, g, w)


SUM_ROWS = 16


def _flash_pipe_step(k, vt, bias, qt_ref, cur, prev, m_ref, acc_ref):
    s_cur, p_cur, al_cur, mt_cur = cur
    s_prev, p_prev, al_prev, mt_prev = prev
    s = _dot(k, qt_ref[...])
    if bias is not None:
        s = s + jnp.concatenate([bias] * (s.shape[1] // bias.shape[1]), axis=1)
    s_cur[...] = s
    mt_cur[...] = jnp.max(s, axis=0, keepdims=True)
    acc_ref[...] = al_cur[...] * acc_ref[...] + _dot(vt, p_cur[...])
    m_prev = m_ref[...]
    m_next = jnp.maximum(m_prev, mt_prev[...])
    p_prev[...] = jnp.exp2(s_prev[...] - m_next).astype(BF16)
    al_prev[...] = jnp.exp2(m_prev - m_next)
    m_ref[...] = m_next


def _pipe_tile(t, lag, nq, nk):
    j = jnp.clip(t - lag, 0, nq * nk - 1)
    return j // nk, j % nk


def _flash_pipe(t, nq, nk, build_qt, k_ref, get_vt, bias_ref, qt_ref, slots, m_ref, acc_ref, finish):
    _, kt_a = _pipe_tile(t, 0, nq, nk)
    _, kt_b = _pipe_tile(t, 1, nq, nk)
    _, kt_c = _pipe_tile(t, 2, nq, nk)

    @pl.when(t == 0)
    def _():
        (_, p0, al0, _), (s1, _, _, mt1) = slots
        for r in (s1, mt1, p0, al0, acc_ref):
            r[...] = jnp.zeros(r.shape, r.dtype)

    @pl.when(kt_a == 0)
    def _():
        build_qt()

    @pl.when(kt_b == 0)
    def _():
        m_ref[...] = jnp.full(m_ref.shape, NEG_INF, F32)

    for c in (0, 1):
        @pl.when(t % 2 == c)
        def _(c=c):
            bias = None if bias_ref is None else bias_ref[0, 0]
            _flash_pipe_step(k_ref[0], get_vt(), bias, qt_ref, slots[c], slots[1 - c], m_ref, acc_ref)

    @pl.when((t >= 2) & (kt_c == nk - 1))
    def _():
        finish()


def _flash_scratch(tk, cols, dv):
    slot = [pltpu.VMEM((tk, cols), F32), pltpu.VMEM((tk, cols), BF16),
            pltpu.VMEM((1, cols), F32), pltpu.VMEM((1, cols), F32)]
    return slot + slot + [pltpu.VMEM((1, cols), F32), pltpu.VMEM((dv + SUM_ROWS, cols), F32)]


def _dattn_kernel(lam_ref, q_ref, k_ref, v_ref, b_ref, g_ref, o_ref, qt_ref,
                  s0_ref, p0_ref, al0_ref, mt0_ref, s1_ref, p1_ref, al1_ref, mt1_ref,
                  m_ref, acc_ref, *, tq, nq, nk, out_scale):

    def build_qt():
        q = q_ref[0].astype(F32) * (LOG2E * DA_HEAD_DIM ** -0.5)
        lane = lax.broadcasted_iota(jnp.int32, q.shape, 1)
        qt_ref[:, 0:tq] = jnp.where(lane < DA_HEAD_DIM, q, 0.0).T.astype(BF16)
        qt_ref[:, tq:2 * tq] = jnp.where(lane >= DA_HEAD_DIM, q, 0.0).T.astype(BF16)

    def get_vt():
        vt = v_ref[0].astype(F32).T.astype(BF16)
        return jnp.concatenate([vt, jnp.ones((SUM_ROWS, vt.shape[1]), BF16)], axis=0)

    def finish():
        o = acc_ref[0:DA_V_DIM, :] / acc_ref[DA_V_DIM:DA_V_DIM + 1, :]
        o = (o[:, 0:tq] - lam_ref[0] * o[:, tq:2 * tq]).T
        o_ref[0] = (_rms(o, g_ref[...]) * out_scale).astype(o_ref.dtype)

    slots = ((s0_ref, p0_ref, al0_ref, mt0_ref), (s1_ref, p1_ref, al1_ref, mt1_ref))
    _flash_pipe(pl.program_id(2), nq, nk, build_qt, k_ref, get_vt, b_ref, qt_ref, slots, m_ref, acc_ref,
                finish)


def diff_attn(proj, bias_band, lam, sub_g, *, lam_init, tq, tk):
    b, n, _ = proj.shape
    _, rq, rk = _band_units(tq, tk)
    nq, nk = n // tq, n // tk
    kern = functools.partial(_dattn_kernel, tq=tq, nq=nq, nk=nk, out_scale=1.0 - lam_init)
    tile = functools.partial(_pipe_tile, nq=nq, nk=nk)

    def bias_map(bi, h, t):
        qi, ki = tile(t, 0)
        return (jnp.clip(rk * ki - rq * qi, -rk - 1, rq + 1) + rk + 1, h, 0, 0)

    return pl.pallas_call(
        kern,
        grid=(b, DA_HEADS, nq * nk + 2),
        in_specs=[
            pl.BlockSpec(memory_space=pltpu.SMEM),
            pl.BlockSpec((1, tq, 128), lambda bi, h, t: (bi, tile(t, 0)[0], COL_AQ // 128 + h)),
            pl.BlockSpec((1, tk, 128), lambda bi, h, t: (bi, tile(t, 0)[1], COL_AK // 128 + h)),
            pl.BlockSpec((1, tk, 128), lambda bi, h, t: (bi, tile(t, 2)[1], COL_AV // 128 + h)),
            pl.BlockSpec((1, 1, tk, tq), bias_map),
            pl.BlockSpec((1, DA_V_DIM), lambda bi, h, t: (0, 0)),
        ],
        out_specs=pl.BlockSpec((1, tq, DA_V_DIM), lambda bi, h, t: (bi, tile(t, 2)[0], h)),
        out_shape=jax.ShapeDtypeStruct((b, n, DA_HEADS * DA_V_DIM), BF16),
        scratch_shapes=[pltpu.VMEM((128, 2 * tq), BF16)] + _flash_scratch(tk, 2 * tq, DA_V_DIM),
        compiler_params=_cparams(("parallel", "parallel", "arbitrary")),
        name="diff_attn",
    )(lam, proj, proj, proj, bias_band, sub_g)


def _swap16(x):
    w = x.shape[1]
    lane = lax.broadcasted_iota(jnp.int32, x.shape, 1)
    return jnp.where(lane % 32 < 16, pltpu.roll(x, w - 16, 1), pltpu.roll(x, 16, 1))


def _seg_rms(x, seg_ones, g):
    sq = x * x
    hi = sq.astype(BF16)
    lo = (sq - hi.astype(F32)).astype(BF16)
    ss = _dot(hi, seg_ones) + _dot(lo, seg_ones)
    return x * lax.rsqrt(ss * (1.0 / GQ_HEAD_DIM) + NORM_EPS) * g


def _gqa_prep_kernel(q_ref, k_ref, v_ref, cos_ref, sin_ref, gq_ref, gk_ref, ones_ref,
                     qo_ref, ko_ref, vo_ref):
    cos = cos_ref[...]
    sin = sin_ref[...]
    cos4 = jnp.concatenate([cos] * 4, axis=1)
    sin4 = jnp.concatenate([sin] * 4, axis=1)
    ones = ones_ref[...]
    q = _seg_rms(q_ref[0].astype(F32), ones, gq_ref[...])
    k = _seg_rms(k_ref[0].astype(F32), ones[0:128, 0:128], gk_ref[...])
    q = (q * cos4 + _swap16(q) * sin4) * (LOG2E * GQ_HEAD_DIM ** -0.5)
    qo_ref[0] = q.T.astype(BF16)
    ko_ref[0] = (k * cos + _swap16(k) * sin).astype(BF16)
    vt = v_ref[0].astype(F32).T.astype(BF16)
    hd = GQ_HEAD_DIM
    ones = jnp.ones((SUM_ROWS, vt.shape[1]), BF16)
    for h in range(GQ_KV_HEADS):
        vo_ref[0, h, 0:hd, :] = vt[h * hd:(h + 1) * hd]
        vo_ref[0, h, hd:hd + SUM_ROWS, :] = ones


def gqa_prep(proj, cos2, sin2, gq, gk, seg_ones, *, tm=512):
    b, n, _ = proj.shape
    return pl.pallas_call(
        _gqa_prep_kernel,
        grid=(b, n // tm),
        in_specs=[
            pl.BlockSpec((1, tm, 512), lambda bi, i: (bi, i, COL_CQ // 512)),
            pl.BlockSpec((1, tm, 128), lambda bi, i: (bi, i, COL_CK // 128)),
            pl.BlockSpec((1, tm, 128), lambda bi, i: (bi, i, COL_CV // 128)),
            pl.BlockSpec((tm, 128), lambda bi, i: (i, 0)),
            pl.BlockSpec((tm, 128), lambda bi, i: (i, 0)),
            pl.BlockSpec((1, 512), lambda bi, i: (0, 0)),
            pl.BlockSpec((1, 128), lambda bi, i: (0, 0)),
            pl.BlockSpec((512, 512), lambda bi, i: (0, 0)),
        ],
        out_specs=[
            pl.BlockSpec((1, GQ_HEADS * GQ_HEAD_DIM, tm), lambda bi, i: (bi, 0, i)),
            pl.BlockSpec((1, tm, GQ_KV_HEADS * GQ_HEAD_DIM), lambda bi, i: (bi, i, 0)),
            pl.BlockSpec((1, GQ_KV_HEADS, GQ_HEAD_DIM + SUM_ROWS, tm), lambda bi, i: (bi, 0, 0, i)),
        ],
        out_shape=[
            jax.ShapeDtypeStruct((b, GQ_HEADS * GQ_HEAD_DIM, n), BF16),
            jax.ShapeDtypeStruct((b, n, GQ_KV_HEADS * GQ_HEAD_DIM), BF16),
            jax.ShapeDtypeStruct((b, GQ_KV_HEADS, GQ_HEAD_DIM + SUM_ROWS, n), BF16),
        ],
        compiler_params=_cparams(("parallel", "parallel")),
        name="gqa_prep",
    )(proj, proj, proj, cos2, sin2, gq, gk, seg_ones)


def _gqa_kernel(q_ref, k_ref, vt_ref, o_ref, qt_ref,
                s0_ref, p0_ref, al0_ref, mt0_ref, s1_ref, p1_ref, al1_ref, mt1_ref,
                m_ref, acc_ref, *, tq, nq, nk):
    hd = GQ_HEAD_DIM

    def build_qt():
        qt_ref[...] = jnp.zeros(qt_ref.shape, BF16)
        base = pl.multiple_of(pl.program_id(1) * hd, hd)
        for r in range(GQ_GROUP):
            qt_ref[pl.ds(base, hd), r * tq:(r + 1) * tq] = q_ref[0, r * hd:(r + 1) * hd, :]

    def finish():
        o = (acc_ref[0:hd, :] / acc_ref[hd:hd + 1, :]).astype(o_ref.dtype)
        for r in range(GQ_GROUP):
            o_ref[0, r * hd:(r + 1) * hd, :] = o[:, r * tq:(r + 1) * tq]

    slots = ((s0_ref, p0_ref, al0_ref, mt0_ref), (s1_ref, p1_ref, al1_ref, mt1_ref))
    _flash_pipe(pl.program_id(2), nq, nk, build_qt, k_ref, lambda: vt_ref[0, 0], None, qt_ref, slots,
                m_ref, acc_ref, finish)


def gqa_attn(qt, k, vt, *, tq, tk):
    b, _, n = qt.shape
    gw = GQ_GROUP * GQ_HEAD_DIM
    kvw = GQ_KV_HEADS * GQ_HEAD_DIM
    nq, nk = n // tq, n // tk
    cols = GQ_GROUP * tq
    kern = functools.partial(_gqa_kernel, tq=tq, nq=nq, nk=nk)
    tile = functools.partial(_pipe_tile, nq=nq, nk=nk)
    return pl.pallas_call(
        kern,
        grid=(b, GQ_KV_HEADS, nq * nk + 2),
        in_specs=[
            pl.BlockSpec((1, gw, tq), lambda bi, g, t: (bi, g, tile(t, 0)[0])),
            pl.BlockSpec((1, tk, kvw), lambda bi, g, t: (bi, tile(t, 0)[1], 0)),
            pl.BlockSpec((1, 1, GQ_HEAD_DIM + SUM_ROWS, tk), lambda bi, g, t: (bi, g, 0, tile(t, 2)[1])),
        ],
        out_specs=pl.BlockSpec((1, gw, tq), lambda bi, g, t: (bi, g, tile(t, 2)[0])),
        out_shape=jax.ShapeDtypeStruct((b, GQ_HEADS * GQ_HEAD_DIM, n), BF16),
        scratch_shapes=[pltpu.VMEM((kvw, cols), BF16)] + _flash_scratch(tk, cols, GQ_HEAD_DIM),
        compiler_params=_cparams(("parallel", "parallel", "arbitrary")),
        name="gqa_attn",
    )(qt, k, vt)


def _gla_kernel(q_ref, k_ref, v_ref, z_ref, gw_ref, gb_ref, o_ref, st_ref, *, reverse, nchunk):
    L = GLA_CHUNK
    hd = GLA_HEADS * GLA_DK

    @pl.when(pl.program_id(1) == 0)
    def _():
        st_ref[...] = jnp.zeros(st_ref.shape, F32)

    row = lax.broadcasted_iota(jnp.int32, (L, L), 0)
    col = lax.broadcasted_iota(jnp.int32, (L, L), 1)
    if reverse:
        tri = (col >= row).astype(BF16)
        keep = col > row
    else:
        tri = (col <= row).astype(BF16)
        keep = col <= row
    lane = lax.broadcasted_iota(jnp.int32, (L, hd), 1)

    order = range(nchunk - 1, -1, -1) if reverse else range(nchunk)
    for c in order:
        rows = pl.ds(c * L, L)
        q = q_ref[0, rows, :].astype(F32) * (GLA_DK ** -0.5)
        k = k_ref[0, rows, :].astype(F32)
        logit = _dot(z_ref[0, rows, :], gw_ref[...]) + gb_ref[...]
        lg = jax.nn.log_sigmoid(logit) * (1.0 / GLA_TAU)
        hi = lg.astype(BF16)
        lo = (lg - hi.astype(F32)).astype(BF16)
        cum = _dot(tri, hi) + _dot(tri, lo)
        last = cum[0:1, :] if reverse else cum[L - 1:L, :]
        q_dec = q * jnp.exp(cum)
        k_dec = (k * jnp.exp(-cum)).astype(BF16)
        k_last = (k * jnp.exp(last - cum)).astype(BF16)
        decay = jnp.exp(last)
        for h in range(GLA_HEADS):
            head = (lane >= h * GLA_DK) & (lane < (h + 1) * GLA_DK)
            qh = jnp.where(head, q_dec, 0.0).astype(BF16)
            vh = v_ref[0, rows, h * GLA_DV:(h + 1) * GLA_DV]
            a = jnp.where(keep, _dot_nt(qh, k_dec), 0.0)
            st = st_ref[h]
            o = _dot(a.astype(BF16), vh) + _dot_nt(qh, st.astype(BF16))
            o_ref[0, rows, h * GLA_DV:(h + 1) * GLA_DV] = o
            st_ref[h] = st * decay + _dot_tn(vh, k_last)


def gla(proj, gw, gb, *, reverse, tm=256):
    b, n, _ = proj.shape
    nb = n // tm
    kern = functools.partial(_gla_kernel, reverse=reverse, nchunk=tm // GLA_CHUNK)
    blk = (lambda i: nb - 1 - i) if reverse else (lambda i: i)
    hd = GLA_HEADS * GLA_DK
    return pl.pallas_call(
        kern,
        grid=(b, nb),
        in_specs=[
            pl.BlockSpec((1, tm, hd), lambda bi, i: (bi, blk(i), COL_BQ // hd)),
            pl.BlockSpec((1, tm, hd), lambda bi, i: (bi, blk(i), COL_BK // hd)),
            pl.BlockSpec((1, tm, 512), lambda bi, i: (bi, blk(i), COL_BV // 512)),
            pl.BlockSpec((1, tm, 128), lambda bi, i: (bi, blk(i), COL_BZ // 128)),
            pl.BlockSpec((128, hd), lambda bi, i: (0, 0)),
            pl.BlockSpec((1, hd), lambda bi, i: (0, 0)),
        ],
        out_specs=pl.BlockSpec((1, tm, GLA_HEADS * GLA_DV), lambda bi, i: (bi, blk(i), 0)),
        out_shape=jax.ShapeDtypeStruct((b, n, GLA_HEADS * GLA_DV), F32),
        scratch_shapes=[pltpu.VMEM((GLA_HEADS, GLA_DV, hd), F32)],
        compiler_params=_cparams(("parallel", "arbitrary")),
        name="gla_bwd" if reverse else "gla_fwd",
    )(proj, proj, proj, proj, gw, gb)


def _merge_kernel(x_ref, ya_ref, of_ref, ob_ref, r_ref, yc_ref, gz_ref, gn_ref, wb_ref, wo_ref,
                  o_ref):
    gn = gn_ref[...]
    o = of_ref[0] + ob_ref[0]
    r = r_ref[0].astype(F32)
    yb = []
    for h in range(GLA_HEADS):
        sl = slice(h * GLA_DV, (h + 1) * GLA_DV)
        yb.append(_rms(o[:, sl], gn) * jax.nn.silu(r[:, sl]))
    yb = jnp.concatenate(yb, axis=1).astype(BF16)

    pa = _dot(ya_ref[0], wb_ref[0])
    pb = _dot(yb, wb_ref[1])
    pc = _dot_tn(yc_ref[0], wb_ref[2])

    gz = gz_ref[0].astype(F32)
    merged = (jax.nn.sigmoid(gz[:, 0:D_MODEL]) * pa
              + jax.nn.sigmoid(gz[:, D_MODEL:2 * D_MODEL]) * pb
              + jax.nn.sigmoid(gz[:, 2 * D_MODEL:3 * D_MODEL]) * pc)
    o_ref[0] = x_ref[0] + _dot(merged.astype(BF16), wo_ref[...])


def merge(x, ya, of, ob, proj, yc, gn, wb, wo, *, tm=256):
    b, n, _ = x.shape
    tok = lambda bi, i: (bi, i, 0)
    return pl.pallas_call(
        _merge_kernel,
        grid=(b, n // tm),
        in_specs=[
            pl.BlockSpec((1, tm, D_MODEL), tok),
            pl.BlockSpec((1, tm, BRANCH_W), tok),
            pl.BlockSpec((1, tm, BRANCH_W), tok),
            pl.BlockSpec((1, tm, BRANCH_W), tok),
            pl.BlockSpec((1, tm, 512), lambda bi, i: (bi, i, COL_BR // 512)),
            pl.BlockSpec((1, BRANCH_W, tm), lambda bi, i: (bi, 0, i)),
            pl.BlockSpec((1, tm, N_BRANCH * D_MODEL), lambda bi, i: (bi, i, COL_GZ // (N_BRANCH * D_MODEL))),
            pl.BlockSpec((1, GLA_DV), lambda bi, i: (0, 0)),
            pl.BlockSpec((N_BRANCH, BRANCH_W, D_MODEL), lambda bi, i: (0, 0, 0)),
            pl.BlockSpec((D_MODEL, D_MODEL), lambda bi, i: (0, 0)),
        ],
        out_specs=pl.BlockSpec((1, tm, D_MODEL), tok),
        out_shape=jax.ShapeDtypeStruct((b, n, D_MODEL), F32),
        compiler_params=_cparams(("parallel", "parallel")),
        name="merge",
    )(x, ya, of, ob, proj, yc, proj, gn, wb, wo)


TOPX = PEER_TOPK + 1
TOPX_PAD = 24


def _top_values(vals, count):
    out = []
    for _ in range(count):
        m = jnp.max(vals, axis=0, keepdims=True)
        out.append(m)
        vals = jnp.where(vals == m, NEG_INF, vals)
    return out


def _route_kernel(x_ref, g_ref, wq_ref, sk_ref, hn_ref, s1_ref, e1_ref, th_ref, e0_ref):
    tb = x_ref.shape[0]
    hn = _rms(x_ref[...], g_ref[...])
    hn_t = hn.T.astype(BF16)
    hn_ref[...] = hn_t
    q_t = _dot(wq_ref[...], hn_t).astype(BF16)
    pad = jnp.full((TOPX_PAD - TOPX, tb), NEG_INF, F32)
    for h in range(PEER_HEADS):
        r0 = (2 * h) * PEER_HALF
        s0 = _dot(sk_ref[h, 0], q_t[r0:r0 + PEER_HALF])
        s1 = _dot(sk_ref[h, 1], q_t[r0 + PEER_HALF:r0 + 2 * PEER_HALF])
        top0 = _top_values(s0, TOPX)
        top1 = _top_values(s1, TOPX)
        slab0 = jnp.concatenate(top0 + [pad], axis=0)
        slab1 = jnp.concatenate(top1 + [pad], axis=0)
        half = slab1[0:8]
        cand = jnp.concatenate([top0[0] + slab1] + [top0[a] + half for a in range(1, 8)]
                               + [slab0[8:TOPX_PAD] + top1[0]], axis=0)
        best = _top_values(cand, TOPX)
        thresh = 0.5 * (best[PEER_TOPK - 1] + best[PEER_TOPK])
        m0, m1 = top0[0], top1[0]
        z = jnp.zeros_like(m0)
        for c in best[:PEER_TOPK]:
            z = z + jnp.exp(c - best[0])
        e1 = jnp.exp(s1 - m1) / z
        th = thresh - s0
        e0 = jnp.exp(s0 - m0)
        for c in range(tb // 128):
            cs = slice(c * 128, (c + 1) * 128)
            s1_ref[h, c] = s1[:, cs]
            e1_ref[h, c] = e1[:, cs]
            th_ref[h, c] = th[:, cs]
            e0_ref[h, c] = e0[:, cs]


def peer_route(x2d, g, wq_t, subkeys, *, tb=256):
    t = x2d.shape[0]
    hk = jax.ShapeDtypeStruct((PEER_HEADS, t // 128, PEER_KEYS, 128), F32)
    hk_spec = pl.BlockSpec((PEER_HEADS, tb // 128, PEER_KEYS, 128), lambda i: (0, i, 0, 0))
    return pl.pallas_call(
        _route_kernel,
        grid=(t // tb,),
        in_specs=[
            pl.BlockSpec((tb, D_MODEL), lambda i: (i, 0)),
            pl.BlockSpec((1, D_MODEL), lambda i: (0, 0)),
            pl.BlockSpec((PEER_HEADS * PEER_QDIM, D_MODEL), lambda i: (0, 0)),
            pl.BlockSpec((PEER_HEADS, 2, PEER_KEYS, PEER_HALF), lambda i: (0, 0, 0, 0)),
        ],
        out_specs=[pl.BlockSpec((D_MODEL, tb), lambda i: (0, i)), hk_spec, hk_spec, hk_spec, hk_spec],
        out_shape=[jax.ShapeDtypeStruct((D_MODEL, t), BF16), hk, hk, hk, hk],
        compiler_params=_cparams(("parallel",)),
        name="peer_route",
    )(x2d, g, wq_t, subkeys)


DENSE_ROWS = 32


def _dense_kernel(x_ref, hn_ref, u_ref, vt_ref, s1_ref, e1_ref, th_ref, e0_ref, fg_ref, o_ref,
                  wg_ref, acc_ref, *, ne, ib, final_norm):
    e = pl.program_id(1)
    tb = hn_ref.shape[1]

    @pl.when(e == 0)
    def _():
        acc_ref[...] = jnp.zeros(acc_ref.shape, F32)

    a = _dot(u_ref[...], hn_ref[...])
    gelu_c = np.float32(np.sqrt(0.5))
    for il in range(ib):
        for tc in range(tb // 128):
            cols = slice(tc * 128, (tc + 1) * 128)
            for jc in range(PEER_KEYS // DENSE_ROWS):
                keys = slice(jc * DENSE_ROWS, (jc + 1) * DENSE_ROWS)
                rows = slice(il * PEER_KEYS + jc * DENSE_ROWS, il * PEER_KEYS + (jc + 1) * DENSE_ROWS)
                w = jnp.zeros((DENSE_ROWS, 128), F32)
                for h in range(PEER_HEADS):
                    sel = s1_ref[h, tc, keys, :] >= th_ref[h, tc, il:il + 1, :]
                    w = w + jnp.where(sel, e1_ref[h, tc, keys, :], 0.0) * e0_ref[h, tc, il:il + 1, :]
                ac = a[rows, cols]
                act = 0.5 * ac * (1.0 + lax.erf(ac * gelu_c))
                wg_ref[rows, cols] = (w * act).astype(BF16)
    acc_ref[...] += _dot(vt_ref[...], wg_ref[...])

    @pl.when(e == ne - 1)
    def _():
        y = x_ref[...] + acc_ref[...].T
        if final_norm:
            y = _rms(y, fg_ref[...])
        o_ref[...] = y


def peer_dense(x2d, hn_t, u, v_t, s1, e1, th, e0, final_g, *, final_norm, tb=512, ib=8):
    t = x2d.shape[0]
    eb = ib * PEER_KEYS
    ne = PEER_EXPERTS // eb
    kern = functools.partial(_dense_kernel, ne=ne, ib=ib, final_norm=final_norm)
    full = pl.BlockSpec((PEER_HEADS, tb // 128, PEER_KEYS, 128), lambda i, e: (0, i, 0, 0))
    part = pl.BlockSpec((PEER_HEADS, tb // 128, ib, 128), lambda i, e: (0, i, e, 0))
    return pl.pallas_call(
        kern,
        grid=(t // tb, ne),
        in_specs=[
            pl.BlockSpec((tb, D_MODEL), lambda i, e: (i, 0)),
            pl.BlockSpec((D_MODEL, tb), lambda i, e: (0, i)),
            pl.BlockSpec((eb, D_MODEL), lambda i, e: (e, 0)),
            pl.BlockSpec((D_MODEL, eb), lambda i, e: (0, e)),
            full, full, part, part,
            pl.BlockSpec((1, D_MODEL), lambda i, e: (0, 0)),
        ],
        out_specs=pl.BlockSpec((tb, D_MODEL), lambda i, e: (i, 0)),
        out_shape=jax.ShapeDtypeStruct((t, D_MODEL), F32),
        scratch_shapes=[pltpu.VMEM((eb, tb), BF16), pltpu.VMEM((D_MODEL, tb), F32)],
        compiler_params=_cparams(("parallel", "arbitrary")),
        name="peer_dense",
    )(x2d, hn_t, u, v_t, s1, e1, th, e0, final_g)


def _rel_bucket(rel):
    nb = REL_BUCKETS // 2
    max_exact = nb // 2
    n = jnp.abs(rel)
    nf = jnp.maximum(n, 1).astype(jnp.float32)
    large = max_exact + (jnp.log(nf / max_exact) / math.log(REL_MAX_DIST / max_exact)
                         * (nb - max_exact)).astype(jnp.int32)
    large = jnp.minimum(large, nb - 1)
    return jnp.where(rel > 0, nb, 0) + jnp.where(n < max_exact, n, large)


def _band_units(tq, tk):
    u = min(tq, tk)
    return u, tq // u, tk // u


def _bias_band(rel_table, tq, tk):
    u, rq, rk = _band_units(tq, tk)
    qpos = jnp.arange(tq)
    tiles = []
    for d in range(-rk - 1, rq + 2):
        rel = (d * u + jnp.arange(tk))[:, None] - qpos[None, :]
        onehot = jax.nn.one_hot(_rel_bucket(rel), REL_BUCKETS, dtype=F32)
        tiles.append(jnp.einsum("kqb,bh->hkq", onehot, rel_table.astype(F32),
                                precision=lax.Precision.HIGHEST))
    return jnp.stack(tiles) * LOG2E


def _rope_tables(n):
    rows = n // GRID_W
    row_id = jnp.repeat(jnp.arange(rows), GRID_W).astype(jnp.float32)
    col_id = (jnp.arange(rows * GRID_W) % GRID_W).astype(jnp.float32)
    sec = GQ_HEAD_DIM // 2
    inv = ROPE_THETA ** (-jnp.arange(0, sec, 2, dtype=jnp.float32) / sec)
    cr, sr = jnp.cos(row_id[:, None] * inv), jnp.sin(row_id[:, None] * inv)
    cc, sc = jnp.cos(col_id[:, None] * inv), jnp.sin(col_id[:, None] * inv)
    cos = jnp.concatenate([cr, cr, cc, cc], axis=1)
    sin = jnp.concatenate([-sr, sr, -sc, sc], axis=1)
    return jnp.concatenate([cos, cos], axis=1), jnp.concatenate([sin, sin], axis=1)


def _prep_weights(w_in, gla_gate_w, gla_gate_b, gq_qk_g, w_branch, w_out, peer_wq, peer_subkeys,
                  peer_u, peer_v):
    wi = jnp.concatenate(
        [w_in[:, :, ORIG_GZ:], w_in[:, :, :ORIG_BZ], w_in[:, :, ORIG_CQ:ORIG_GZ],
         w_in[:, :, ORIG_BZ:ORIG_CQ],
         jnp.zeros((DEPTH, D_MODEL, PROJ_W - ORIG_W), w_in.dtype)], axis=2).astype(BF16)
    hd = GLA_HEADS * GLA_DK
    gw = jnp.zeros((DEPTH, 2, 128, hd), F32)
    gw = gw.at[:, 0, 0:GLA_GATE_RANK].set(gla_gate_w[:, 0])
    gw = gw.at[:, 1, GLA_GATE_RANK:2 * GLA_GATE_RANK].set(gla_gate_w[:, 1])
    return dict(
        w_in=wi, gw=gw.astype(BF16), gb=gla_gate_b.astype(F32)[:, :, None, :],
        gq=jnp.tile(gq_qk_g[:, 0], (1, GQ_HEADS))[:, None, :],
        gk=jnp.tile(gq_qk_g[:, 1], (1, GQ_KV_HEADS))[:, None, :],
        wb=w_branch.astype(BF16), wo=w_out.astype(BF16),
        wq_t=jnp.swapaxes(peer_wq, 1, 2).astype(BF16), sk=peer_subkeys.astype(BF16),
        u=peer_u.astype(BF16), v_t=jnp.swapaxes(peer_v, 1, 2).astype(BF16))


def _encoder(x, w, rel_bias, norm1_g, da_lambda, da_subln_g, gla_norm_g, norm2_g, final_g, seg_ones):
    b, n, _ = x.shape
    t = b * n
    band = _bias_band(rel_bias, DA_TQ, DA_TK)
    cos2, sin2 = _rope_tables(n)
    for l in range(DEPTH):
        proj = in_proj(x.reshape(t, D_MODEL), norm1_g[l][None, :], w["w_in"][l]).reshape(b, n, PROJ_W)
        lam_init = 0.8 - 0.6 * math.exp(-0.3 * l)
        lp = da_lambda[l].astype(F32)
        lam = (jnp.exp(jnp.sum(lp[0] * lp[1])) - jnp.exp(jnp.sum(lp[2] * lp[3])) + lam_init).reshape(1)
        ya = diff_attn(proj, band, lam, da_subln_g[l][None, :], lam_init=lam_init, tq=DA_TQ, tk=DA_TK)
        qt, kr, vt = gqa_prep(proj, cos2, sin2, w["gq"][l], w["gk"][l], seg_ones)
        yc = gqa_attn(qt, kr, vt, tq=GQ_TQ, tk=GQ_TK)
        of = gla(proj, w["gw"][l, 0], w["gb"][l, 0], reverse=False)
        ob = gla(proj, w["gw"][l, 1], w["gb"][l, 1], reverse=True)
        x1 = merge(x, ya, of, ob, proj, yc, gla_norm_g[l][None, :], w["wb"][l], w["wo"][l])
        x1 = x1.reshape(t, D_MODEL)
        hn_t, s1, e1, th, e0 = peer_route(x1, norm2_g[l][None, :], w["wq_t"][l], w["sk"][l])
        x = peer_dense(x1, hn_t, w["u"][l], w["v_t"][l], s1, e1, th, e0, final_g[None, :],
                       final_norm=(l == DEPTH - 1)).reshape(b, n, D_MODEL)
    return x


def kernel(x_prompt, x_sample, rel_bias, norm1_g, w_in, da_lambda, da_subln_g, gla_gate_w, gla_gate_b,
           gla_norm_g, gq_qk_g, w_branch, w_out, norm2_g, peer_wq, peer_subkeys, peer_u, peer_v, final_g):
    w = _prep_weights(w_in, gla_gate_w, gla_gate_b, gq_qk_g, w_branch, w_out, peer_wq, peer_subkeys,
                      peer_u, peer_v)
    seg = np.arange(512) // GQ_HEAD_DIM
    seg_ones = jnp.asarray(seg[:, None] == seg[None, :], dtype=BF16)
    args = (w, rel_bias, norm1_g, da_lambda, da_subln_g, gla_norm_g, norm2_g, final_g, seg_ones)
    return (_encoder(x_prompt, *args), _encoder(x_sample, *args))
```

```python
import functools
import math

import numpy as np
import jax
import jax.numpy as jnp
from jax import lax
from jax.experimental import pallas as pl
from jax.experimental.pallas import tpu as pltpu

D_MODEL = 1024
DEPTH = 2
GRID_W = 64
NORM_EPS = 1e-6
N_BRANCH = 3
BRANCH_W = 512

DA_HEADS = 4
DA_HEAD_DIM = 64
DA_V_DIM = 128
REL_BUCKETS = 32
REL_MAX_DIST = 128

GLA_HEADS = 4
GLA_DK = 64
GLA_DV = 128
GLA_GATE_RANK = 16
GLA_TAU = 16.0
GLA_CHUNK = 64

GQ_HEADS = 8
GQ_KV_HEADS = 2
GQ_GROUP = GQ_HEADS // GQ_KV_HEADS
GQ_HEAD_DIM = 64
ROPE_THETA = 10000.0

PEER_HEADS = 8
PEER_KEYS = 128
PEER_EXPERTS = PEER_KEYS * PEER_KEYS
PEER_QDIM = 256
PEER_HALF = 128
PEER_TOPK = 16

COL_GZ = 0
COL_AQ, COL_AK, COL_AV = 3072, 3584, 4096
COL_BQ, COL_BK, COL_BV, COL_BR = 4608, 4864, 5120, 5632
COL_CQ, COL_CK, COL_CV = 6144, 6656, 6784
COL_BZ = 6912
PROJ_W = 7168
ORIG_BZ = 3072
ORIG_CQ = ORIG_BZ + 2 * GLA_GATE_RANK
ORIG_GZ = ORIG_CQ + 768
ORIG_W = ORIG_GZ + N_BRANCH * D_MODEL

GQ_TQ, GQ_TK = 512, 512


def _da_tiles(n):
    return (1024, 1024) if n >= 8192 else (512, 512)

VMEM_LIMIT = 56 * 1024 * 1024

F32 = jnp.float32
BF16 = jnp.bfloat16
NEG_INF = float("-inf")
LOG2E = math.log2(math.e)


def _cparams(sem):
    return pltpu.CompilerParams(dimension_semantics=sem, vmem_limit_bytes=VMEM_LIMIT)


def _dot(a, b):
    return jnp.dot(a, b, preferred_element_type=F32)


def _dot_nt(a, b):
    return lax.dot_general(a, b, (((1,), (1,)), ((), ())), preferred_element_type=F32)


def _dot_tn(a, b):
    return lax.dot_general(a, b, (((0,), (0,)), ((), ())), preferred_element_type=F32)


def _rms(x, g):
    return x * lax.rsqrt(jnp.mean(x * x, axis=-1, keepdims=True) + NORM_EPS) * g


def _in_proj_kernel(x_ref, g_ref, w_ref, o_ref, hn_ref):
    @pl.when(pl.program_id(1) == 0)
    def _():
        hn_ref[...] = _rms(x_ref[...], g_ref[...]).astype(BF16)

    o_ref[...] = _dot(hn_ref[...], w_ref[...]).astype(o_ref.dtype)


def in_proj(x2d, g, w, *, tm=2048, tn=1024):
    t = x2d.shape[0]
    nw = w.shape[1]
    return pl.pallas_call(
        _in_proj_kernel,
        grid=(t // tm, nw // tn),
        in_specs=[
            pl.BlockSpec((tm, D_MODEL), lambda i, j: (i, 0)),
            pl.BlockSpec((1, D_MODEL), lambda i, j: (0, 0)),
            pl.BlockSpec((D_MODEL, tn), lambda i, j: (0, j)),
        ],
        out_specs=pl.BlockSpec((tm, tn), lambda i, j: (i, j)),
        out_shape=jax.ShapeDtypeStruct((t, nw), BF16),
        scratch_shapes=[pltpu.VMEM((tm, D_MODEL), BF16)],
        compiler_params=_cparams(("parallel", "arbitrary")),
        name="in_proj",
    )(x2d, g, w)


SUM_ROWS = 16


def _flash_pipe_step(k, vt, bias, qt_ref, cur, prev, m_ref, acc_ref):
    s_cur, p_cur, al_cur, mt_cur = cur
    s_prev, p_prev, al_prev, mt_prev = prev
    s = _dot(k, qt_ref[...])
    if bias is not None:
        s = s + jnp.concatenate([bias] * (s.shape[1] // bias.shape[1]), axis=1)
    s_cur[...] = s
    mt_cur[...] = jnp.max(s, axis=0, keepdims=True)
    acc_ref[...] = al_cur[...] * acc_ref[...] + _dot(vt, p_cur[...])
    m_prev = m_ref[...]
    m_next = jnp.maximum(m_prev, mt_prev[...])
    p_prev[...] = jnp.exp2(s_prev[...] - m_next).astype(BF16)
    al_prev[...] = jnp.exp2(m_prev - m_next)
    m_ref[...] = m_next


def _pipe_tile(t, lag, nq, nk):
    j = jnp.clip(t - lag, 0, nq * nk - 1)
    return j // nk, j % nk


def _flash_pipe(t, nq, nk, build_qt, k_ref, get_vt, bias_ref, qt_ref, slots, m_ref, acc_ref, finish):
    _, kt_a = _pipe_tile(t, 0, nq, nk)
    _, kt_b = _pipe_tile(t, 1, nq, nk)
    _, kt_c = _pipe_tile(t, 2, nq, nk)

    @pl.when(t == 0)
    def _():
        (_, p0, al0, _), (s1, _, _, mt1) = slots
        for r in (s1, mt1, p0, al0, acc_ref):
            r[...] = jnp.zeros(r.shape, r.dtype)

    @pl.when(kt_a == 0)
    def _():
        build_qt()

    @pl.when(kt_b == 0)
    def _():
        m_ref[...] = jnp.full(m_ref.shape, NEG_INF, F32)

    for c in (0, 1):
        @pl.when(t % 2 == c)
        def _(c=c):
            bias = None if bias_ref is None else bias_ref[0, 0]
            _flash_pipe_step(k_ref[0], get_vt(), bias, qt_ref, slots[c], slots[1 - c], m_ref, acc_ref)

    @pl.when((t >= 2) & (kt_c == nk - 1))
    def _():
        finish()


def _flash_scratch(tk, cols, dv):
    slot = [pltpu.VMEM((tk, cols), F32), pltpu.VMEM((tk, cols), BF16),
            pltpu.VMEM((1, cols), F32), pltpu.VMEM((1, cols), F32)]
    return slot + slot + [pltpu.VMEM((1, cols), F32), pltpu.VMEM((dv + SUM_ROWS, cols), F32)]


def _dattn_kernel(lam_ref, q_ref, k_ref, v_ref, b_ref, g_ref, o_ref, qt_ref,
                  s0_ref, p0_ref, al0_ref, mt0_ref, s1_ref, p1_ref, al1_ref, mt1_ref,
                  m_ref, acc_ref, *, tq, nq, nk, out_scale):

    def build_qt():
        q = q_ref[0].astype(F32) * (LOG2E * DA_HEAD_DIM ** -0.5)
        lane = lax.broadcasted_iota(jnp.int32, q.shape, 1)
        qt_ref[:, 0:tq] = jnp.where(lane < DA_HEAD_DIM, q, 0.0).T.astype(BF16)
        qt_ref[:, tq:2 * tq] = jnp.where(lane >= DA_HEAD_DIM, q, 0.0).T.astype(BF16)

    def get_vt():
        vt = v_ref[0].astype(F32).T.astype(BF16)
        return jnp.concatenate([vt, jnp.ones((SUM_ROWS, vt.shape[1]), BF16)], axis=0)

    def finish():
        o = acc_ref[0:DA_V_DIM, :] / acc_ref[DA_V_DIM:DA_V_DIM + 1, :]
        o = (o[:, 0:tq] - lam_ref[0] * o[:, tq:2 * tq]).T
        o_ref[0] = (_rms(o, g_ref[...]) * out_scale).astype(o_ref.dtype)

    slots = ((s0_ref, p0_ref, al0_ref, mt0_ref), (s1_ref, p1_ref, al1_ref, mt1_ref))
    _flash_pipe(pl.program_id(2), nq, nk, build_qt, k_ref, get_vt, b_ref, qt_ref, slots, m_ref, acc_ref,
                finish)


def diff_attn(proj, bias_band, lam, sub_g, *, lam_init, tq, tk):
    b, n, _ = proj.shape
    _, rq, rk = _band_units(tq, tk)
    nq, nk = n // tq, n // tk
    kern = functools.partial(_dattn_kernel, tq=tq, nq=nq, nk=nk, out_scale=1.0 - lam_init)
    tile = functools.partial(_pipe_tile, nq=nq, nk=nk)

    def bias_map(bi, h, t):
        qi, ki = tile(t, 0)
        return (jnp.clip(rk * ki - rq * qi, -rk - 1, rq + 1) + rk + 1, h, 0, 0)

    return pl.pallas_call(
        kern,
        grid=(b, DA_HEADS, nq * nk + 2),
        in_specs=[
            pl.BlockSpec(memory_space=pltpu.SMEM),
            pl.BlockSpec((1, tq, 128), lambda bi, h, t: (bi, tile(t, 0)[0], COL_AQ // 128 + h)),
            pl.BlockSpec((1, tk, 128), lambda bi, h, t: (bi, tile(t, 0)[1], COL_AK // 128 + h)),
            pl.BlockSpec((1, tk, 128), lambda bi, h, t: (bi, tile(t, 2)[1], COL_AV // 128 + h)),
            pl.BlockSpec((1, 1, tk, tq), bias_map),
            pl.BlockSpec((1, DA_V_DIM), lambda bi, h, t: (0, 0)),
        ],
        out_specs=pl.BlockSpec((1, tq, DA_V_DIM), lambda bi, h, t: (bi, tile(t, 2)[0], h)),
        out_shape=jax.ShapeDtypeStruct((b, n, DA_HEADS * DA_V_DIM), BF16),
        scratch_shapes=[pltpu.VMEM((128, 2 * tq), BF16)] + _flash_scratch(tk, 2 * tq, DA_V_DIM),
        compiler_params=_cparams(("parallel", "parallel", "arbitrary")),
        name="diff_attn",
    )(lam, proj, proj, proj, bias_band, sub_g)


def _swap16(x):
    w = x.shape[1]
    lane = lax.broadcasted_iota(jnp.int32, x.shape, 1)
    return jnp.where(lane % 32 < 16, pltpu.roll(x, w - 16, 1), pltpu.roll(x, 16, 1))


def _seg_rms(x, seg_ones, g):
    sq = x * x
    hi = sq.astype(BF16)
    lo = (sq - hi.astype(F32)).astype(BF16)
    ss = _dot(hi, seg_ones) + _dot(lo, seg_ones)
    return x * lax.rsqrt(ss * (1.0 / GQ_HEAD_DIM) + NORM_EPS) * g


def _gqa_prep_kernel(q_ref, k_ref, v_ref, cos_ref, sin_ref, gq_ref, gk_ref, ones_ref,
                     qo_ref, ko_ref, vo_ref):
    cos = cos_ref[...]
    sin = sin_ref[...]
    cos4 = jnp.concatenate([cos] * 4, axis=1)
    sin4 = jnp.concatenate([sin] * 4, axis=1)
    ones = ones_ref[...]
    q = _seg_rms(q_ref[0].astype(F32), ones, gq_ref[...])
    k = _seg_rms(k_ref[0].astype(F32), ones[0:128, 0:128], gk_ref[...])
    q = (q * cos4 + _swap16(q) * sin4) * (LOG2E * GQ_HEAD_DIM ** -0.5)
    qo_ref[0] = q.T.astype(BF16)
    ko_ref[0] = (k * cos + _swap16(k) * sin).astype(BF16)
    vt = v_ref[0].astype(F32).T.astype(BF16)
    hd = GQ_HEAD_DIM
    ones = jnp.ones((SUM_ROWS, vt.shape[1]), BF16)
    for h in range(GQ_KV_HEADS):
        vo_ref[0, h, 0:hd, :] = vt[h * hd:(h + 1) * hd]
        vo_ref[0, h, hd:hd + SUM_ROWS, :] = ones


def gqa_prep(proj, cos2, sin2, gq, gk, seg_ones, *, tm=512):
    b, n, _ = proj.shape
    return pl.pallas_call(
        _gqa_prep_kernel,
        grid=(b, n // tm),
        in_specs=[
            pl.BlockSpec((1, tm, 512), lambda bi, i: (bi, i, COL_CQ // 512)),
            pl.BlockSpec((1, tm, 128), lambda bi, i: (bi, i, COL_CK // 128)),
            pl.BlockSpec((1, tm, 128), lambda bi, i: (bi, i, COL_CV // 128)),
            pl.BlockSpec((tm, 128), lambda bi, i: (i, 0)),
            pl.BlockSpec((tm, 128), lambda bi, i: (i, 0)),
            pl.BlockSpec((1, 512), lambda bi, i: (0, 0)),
            pl.BlockSpec((1, 128), lambda bi, i: (0, 0)),
            pl.BlockSpec((512, 512), lambda bi, i: (0, 0)),
        ],
        out_specs=[
            pl.BlockSpec((1, GQ_HEADS * GQ_HEAD_DIM, tm), lambda bi, i: (bi, 0, i)),
            pl.BlockSpec((1, tm, GQ_KV_HEADS * GQ_HEAD_DIM), lambda bi, i: (bi, i, 0)),
            pl.BlockSpec((1, GQ_KV_HEADS, GQ_HEAD_DIM + SUM_ROWS, tm), lambda bi, i: (bi, 0, 0, i)),
        ],
        out_shape=[
            jax.ShapeDtypeStruct((b, GQ_HEADS * GQ_HEAD_DIM, n), BF16),
            jax.ShapeDtypeStruct((b, n, GQ_KV_HEADS * GQ_HEAD_DIM), BF16),
            jax.ShapeDtypeStruct((b, GQ_KV_HEADS, GQ_HEAD_DIM + SUM_ROWS, n), BF16),
        ],
        compiler_params=_cparams(("parallel", "parallel")),
        name="gqa_prep",
    )(proj, proj, proj, cos2, sin2, gq, gk, seg_ones)


def _gqa_kernel(q_ref, k_ref, vt_ref, o_ref, qt_ref,
                s0_ref, p0_ref, al0_ref, mt0_ref, s1_ref, p1_ref, al1_ref, mt1_ref,
                m_ref, acc_ref, *, tq, nq, nk):
    hd = GQ_HEAD_DIM

    def build_qt():
        qt_ref[...] = jnp.zeros(qt_ref.shape, BF16)
        base = pl.multiple_of(pl.program_id(1) * hd, hd)
        for r in range(GQ_GROUP):
            qt_ref[pl.ds(base, hd), r * tq:(r + 1) * tq] = q_ref[0, r * hd:(r + 1) * hd, :]

    def finish():
        o = (acc_ref[0:hd, :] / acc_ref[hd:hd + 1, :]).astype(o_ref.dtype)
        for r in range(GQ_GROUP):
            o_ref[0, r * hd:(r + 1) * hd, :] = o[:, r * tq:(r + 1) * tq]

    slots = ((s0_ref, p0_ref, al0_ref, mt0_ref), (s1_ref, p1_ref, al1_ref, mt1_ref))
    _flash_pipe(pl.program_id(2), nq, nk, build_qt, k_ref, lambda: vt_ref[0, 0], None, qt_ref, slots,
                m_ref, acc_ref, finish)


def gqa_attn(qt, k, vt, *, tq, tk):
    b, _, n = qt.shape
    gw = GQ_GROUP * GQ_HEAD_DIM
    kvw = GQ_KV_HEADS * GQ_HEAD_DIM
    nq, nk = n // tq, n // tk
    cols = GQ_GROUP * tq
    kern = functools.partial(_gqa_kernel, tq=tq, nq=nq, nk=nk)
    tile = functools.partial(_pipe_tile, nq=nq, nk=nk)
    return pl.pallas_call(
        kern,
        grid=(b, GQ_KV_HEADS, nq * nk + 2),
        in_specs=[
            pl.BlockSpec((1, gw, tq), lambda bi, g, t: (bi, g, tile(t, 0)[0])),
            pl.BlockSpec((1, tk, kvw), lambda bi, g, t: (bi, tile(t, 0)[1], 0)),
            pl.BlockSpec((1, 1, GQ_HEAD_DIM + SUM_ROWS, tk), lambda bi, g, t: (bi, g, 0, tile(t, 2)[1])),
        ],
        out_specs=pl.BlockSpec((1, gw, tq), lambda bi, g, t: (bi, g, tile(t, 2)[0])),
        out_shape=jax.ShapeDtypeStruct((b, GQ_HEADS * GQ_HEAD_DIM, n), BF16),
        scratch_shapes=[pltpu.VMEM((kvw, cols), BF16)] + _flash_scratch(tk, cols, GQ_HEAD_DIM),
        compiler_params=_cparams(("parallel", "parallel", "arbitrary")),
        name="gqa_attn",
    )(qt, k, vt)


def _gla_kernel(q_ref, k_ref, v_ref, z_ref, gw_ref, gb_ref, o_ref, st_ref, *, reverse, nchunk):
    L = GLA_CHUNK
    hd = GLA_HEADS * GLA_DK

    @pl.when(pl.program_id(1) == 0)
    def _():
        st_ref[...] = jnp.zeros(st_ref.shape, F32)

    row = lax.broadcasted_iota(jnp.int32, (L, L), 0)
    col = lax.broadcasted_iota(jnp.int32, (L, L), 1)
    if reverse:
        tri = (col >= row).astype(BF16)
        keep = col > row
    else:
        tri = (col <= row).astype(BF16)
        keep = col <= row
    lane = lax.broadcasted_iota(jnp.int32, (L, hd), 1)

    order = range(nchunk - 1, -1, -1) if reverse else range(nchunk)
    for c in order:
        rows = pl.ds(c * L, L)
        q = q_ref[0, rows, :].astype(F32) * (GLA_DK ** -0.5)
        k = k_ref[0, rows, :].astype(F32)
        logit = _dot(z_ref[0, rows, :], gw_ref[...]) + gb_ref[...]
        lg = jax.nn.log_sigmoid(logit) * (1.0 / GLA_TAU)
        hi = lg.astype(BF16)
        lo = (lg - hi.astype(F32)).astype(BF16)
        cum = _dot(tri, hi) + _dot(tri, lo)
        last = cum[0:1, :] if reverse else cum[L - 1:L, :]
        q_dec = q * jnp.exp(cum)
        k_dec = (k * jnp.exp(-cum)).astype(BF16)
        k_last = (k * jnp.exp(last - cum)).astype(BF16)
        decay = jnp.exp(last)
        for h in range(GLA_HEADS):
            head = (lane >= h * GLA_DK) & (lane < (h + 1) * GLA_DK)
            qh = jnp.where(head, q_dec, 0.0).astype(BF16)
            vh = v_ref[0, rows, h * GLA_DV:(h + 1) * GLA_DV]
            a = jnp.where(keep, _dot_nt(qh, k_dec), 0.0)
            st = st_ref[h]
            o = _dot(a.astype(BF16), vh) + _dot_nt(qh, st.astype(BF16))
            o_ref[0, rows, h * GLA_DV:(h + 1) * GLA_DV] = o
            st_ref[h] = st * decay + _dot_tn(vh, k_last)


def gla(proj, gw, gb, *, reverse, tm=256):
    b, n, _ = proj.shape
    nb = n // tm
    kern = functools.partial(_gla_kernel, reverse=reverse, nchunk=tm // GLA_CHUNK)
    blk = (lambda i: nb - 1 - i) if reverse else (lambda i: i)
    hd = GLA_HEADS * GLA_DK
    return pl.pallas_call(
        kern,
        grid=(b, nb),
        in_specs=[
            pl.BlockSpec((1, tm, hd), lambda bi, i: (bi, blk(i), COL_BQ // hd)),
            pl.BlockSpec((1, tm, hd), lambda bi, i: (bi, blk(i), COL_BK // hd)),
            pl.BlockSpec((1, tm, 512), lambda bi, i: (bi, blk(i), COL_BV // 512)),
            pl.BlockSpec((1, tm, 128), lambda bi, i: (bi, blk(i), COL_BZ // 128)),
            pl.BlockSpec((128, hd), lambda bi, i: (0, 0)),
            pl.BlockSpec((1, hd), lambda bi, i: (0, 0)),
        ],
        out_specs=pl.BlockSpec((1, tm, GLA_HEADS * GLA_DV), lambda bi, i: (bi, blk(i), 0)),
        out_shape=jax.ShapeDtypeStruct((b, n, GLA_HEADS * GLA_DV), F32),
        scratch_shapes=[pltpu.VMEM((GLA_HEADS, GLA_DV, hd), F32)],
        compiler_params=_cparams(("parallel", "arbitrary")),
        name="gla_bwd" if reverse else "gla_fwd",
    )(proj, proj, proj, proj, gw, gb)


def _merge_kernel(x_ref, ya_ref, of_ref, ob_ref, r_ref, yc_ref, gz_ref, gn_ref, wb_ref, wo_ref,
                  o_ref):
    gn = gn_ref[...]
    o = of_ref[0] + ob_ref[0]
    r = r_ref[0].astype(F32)
    yb = []
    for h in range(GLA_HEADS):
        sl = slice(h * GLA_DV, (h + 1) * GLA_DV)
        yb.append(_rms(o[:, sl], gn) * jax.nn.silu(r[:, sl]))
    yb = jnp.concatenate(yb, axis=1).astype(BF16)

    pa = _dot(ya_ref[0], wb_ref[0])
    pb = _dot(yb, wb_ref[1])
    pc = _dot_tn(yc_ref[0], wb_ref[2])

    gz = gz_ref[0].astype(F32)
    merged = (jax.nn.sigmoid(gz[:, 0:D_MODEL]) * pa
              + jax.nn.sigmoid(gz[:, D_MODEL:2 * D_MODEL]) * pb
              + jax.nn.sigmoid(gz[:, 2 * D_MODEL:3 * D_MODEL]) * pc)
    o_ref[0] = x_ref[0] + _dot(merged.astype(BF16), wo_ref[...])


def merge(x, ya, of, ob, proj, yc, gn, wb, wo, *, tm=256):
    b, n, _ = x.shape
    tok = lambda bi, i: (bi, i, 0)
    return pl.pallas_call(
        _merge_kernel,
        grid=(b, n // tm),
        in_specs=[
            pl.BlockSpec((1, tm, D_MODEL), tok),
            pl.BlockSpec((1, tm, BRANCH_W), tok),
            pl.BlockSpec((1, tm, BRANCH_W), tok),
            pl.BlockSpec((1, tm, BRANCH_W), tok),
            pl.BlockSpec((1, tm, 512), lambda bi, i: (bi, i, COL_BR // 512)),
            pl.BlockSpec((1, BRANCH_W, tm), lambda bi, i: (bi, 0, i)),
            pl.BlockSpec((1, tm, N_BRANCH * D_MODEL), lambda bi, i: (bi, i, COL_GZ // (N_BRANCH * D_MODEL))),
            pl.BlockSpec((1, GLA_DV), lambda bi, i: (0, 0)),
            pl.BlockSpec((N_BRANCH, BRANCH_W, D_MODEL), lambda bi, i: (0, 0, 0)),
            pl.BlockSpec((D_MODEL, D_MODEL), lambda bi, i: (0, 0)),
        ],
        out_specs=pl.BlockSpec((1, tm, D_MODEL), tok),
        out_shape=jax.ShapeDtypeStruct((b, n, D_MODEL), F32),
        compiler_params=_cparams(("parallel", "parallel")),
        name="merge",
    )(x, ya, of, ob, proj, yc, proj, gn, wb, wo)


TOPX = PEER_TOPK + 1
TOPX_PAD = 24


def _top_values(vals, count):
    out = []
    for _ in range(count):
        m = jnp.max(vals, axis=0, keepdims=True)
        out.append(m)
        vals = jnp.where(vals == m, NEG_INF, vals)
    return out


def _route_kernel(x_ref, g_ref, wq_ref, sk_ref, hn_ref, s1_ref, e1_ref, th_ref, e0_ref):
    tb = x_ref.shape[0]
    hn = _rms(x_ref[...], g_ref[...])
    hn_t = hn.T.astype(BF16)
    hn_ref[...] = hn_t
    q_t = _dot(wq_ref[...], hn_t).astype(BF16)
    pad = jnp.full((TOPX_PAD - TOPX, tb), NEG_INF, F32)
    for h in range(PEER_HEADS):
        r0 = (2 * h) * PEER_HALF
        s0 = _dot(sk_ref[h, 0], q_t[r0:r0 + PEER_HALF])
        s1 = _dot(sk_ref[h, 1], q_t[r0 + PEER_HALF:r0 + 2 * PEER_HALF])
        top0 = _top_values(s0, TOPX)
        top1 = _top_values(s1, TOPX)
        slab0 = jnp.concatenate(top0 + [pad], axis=0)
        slab1 = jnp.concatenate(top1 + [pad], axis=0)
        half = slab1[0:8]
        cand = jnp.concatenate([top0[0] + slab1] + [top0[a] + half for a in range(1, 8)]
                               + [slab0[8:TOPX_PAD] + top1[0]], axis=0)
        best = _top_values(cand, TOPX)
        thresh = 0.5 * (best[PEER_TOPK - 1] + best[PEER_TOPK])
        m0, m1 = top0[0], top1[0]
        z = jnp.zeros_like(m0)
        for c in best[:PEER_TOPK]:
            z = z + jnp.exp(c - best[0])
        e1 = jnp.exp(s1 - m1) / z
        th = thresh - s0
        e0 = jnp.exp(s0 - m0)
        for c in range(tb // 128):
            cs = slice(c * 128, (c + 1) * 128)
            s1_ref[h, c] = s1[:, cs]
            e1_ref[h, c] = e1[:, cs]
            th_ref[h, c] = th[:, cs]
            e0_ref[h, c] = e0[:, cs]


def peer_route(x2d, g, wq_t, subkeys, *, tb=256):
    t = x2d.shape[0]
    hk = jax.ShapeDtypeStruct((PEER_HEADS, t // 128, PEER_KEYS, 128), F32)
    hk_spec = pl.BlockSpec((PEER_HEADS, tb // 128, PEER_KEYS, 128), lambda i: (0, i, 0, 0))
    return pl.pallas_call(
        _route_kernel,
        grid=(t // tb,),
        in_specs=[
            pl.BlockSpec((tb, D_MODEL), lambda i: (i, 0)),
            pl.BlockSpec((1, D_MODEL), lambda i: (0, 0)),
            pl.BlockSpec((PEER_HEADS * PEER_QDIM, D_MODEL), lambda i: (0, 0)),
            pl.BlockSpec((PEER_HEADS, 2, PEER_KEYS, PEER_HALF), lambda i: (0, 0, 0, 0)),
        ],
        out_specs=[pl.BlockSpec((D_MODEL, tb), lambda i: (0, i)), hk_spec, hk_spec, hk_spec, hk_spec],
        out_shape=[jax.ShapeDtypeStruct((D_MODEL, t), BF16), hk, hk, hk, hk],
        compiler_params=_cparams(("parallel",)),
        name="peer_route",
    )(x2d, g, wq_t, subkeys)


DENSE_ROWS = 32


def _dense_kernel(x_ref, hn_ref, u_ref, vt_ref, s1_ref, e1_ref, th_ref, e0_ref, fg_ref, o_ref,
                  wg_ref, acc_ref, *, ne, ib, final_norm):
    e = pl.program_id(1)
    tb = hn_ref.shape[1]

    @pl.when(e == 0)
    def _():
        acc_ref[...] = jnp.zeros(acc_ref.shape, F32)

    a = _dot(u_ref[...], hn_ref[...])
    gelu_c = np.float32(np.sqrt(0.5))
    for il in range(ib):
        for tc in range(tb // 128):
            cols = slice(tc * 128, (tc + 1) * 128)
            for jc in range(PEER_KEYS // DENSE_ROWS):
                keys = slice(jc * DENSE_ROWS, (jc + 1) * DENSE_ROWS)
                rows = slice(il * PEER_KEYS + jc * DENSE_ROWS, il * PEER_KEYS + (jc + 1) * DENSE_ROWS)
                w = jnp.zeros((DENSE_ROWS, 128), F32)
                for h in range(PEER_HEADS):
                    sel = s1_ref[h, tc, keys, :] >= th_ref[h, tc, il:il + 1, :]
                    w = w + jnp.where(sel, e1_ref[h, tc, keys, :], 0.0) * e0_ref[h, tc, il:il + 1, :]
                ac = a[rows, cols]
                act = 0.5 * ac * (1.0 + lax.erf(ac * gelu_c))
                wg_ref[rows, cols] = (w * act).astype(BF16)
    acc_ref[...] += _dot(vt_ref[...], wg_ref[...])

    @pl.when(e == ne - 1)
    def _():
        y = x_ref[...] + acc_ref[...].T
        if final_norm:
            y = _rms(y, fg_ref[...])
        o_ref[...] = y


def peer_dense(x2d, hn_t, u, v_t, s1, e1, th, e0, final_g, *, final_norm, tb=512, ib=8):
    t = x2d.shape[0]
    eb = ib * PEER_KEYS
    ne = PEER_EXPERTS // eb
    kern = functools.partial(_dense_kernel, ne=ne, ib=ib, final_norm=final_norm)
    full = pl.BlockSpec((PEER_HEADS, tb // 128, PEER_KEYS, 128), lambda i, e: (0, i, 0, 0))
    part = pl.BlockSpec((PEER_HEADS, tb // 128, ib, 128), lambda i, e: (0, i, e, 0))
    return pl.pallas_call(
        kern,
        grid=(t // tb, ne),
        in_specs=[
            pl.BlockSpec((tb, D_MODEL), lambda i, e: (i, 0)),
            pl.BlockSpec((D_MODEL, tb), lambda i, e: (0, i)),
            pl.BlockSpec((eb, D_MODEL), lambda i, e: (e, 0)),
            pl.BlockSpec((D_MODEL, eb), lambda i, e: (0, e)),
            full, full, part, part,
            pl.BlockSpec((1, D_MODEL), lambda i, e: (0, 0)),
        ],
        out_specs=pl.BlockSpec((tb, D_MODEL), lambda i, e: (i, 0)),
        out_shape=jax.ShapeDtypeStruct((t, D_MODEL), F32),
        scratch_shapes=[pltpu.VMEM((eb, tb), BF16), pltpu.VMEM((D_MODEL, tb), F32)],
        compiler_params=_cparams(("parallel", "arbitrary")),
        name="peer_dense",
    )(x2d, hn_t, u, v_t, s1, e1, th, e0, final_g)


def _rel_bucket(rel):
    nb = REL_BUCKETS // 2
    max_exact = nb // 2
    n = jnp.abs(rel)
    nf = jnp.maximum(n, 1).astype(jnp.float32)
    large = max_exact + (jnp.log(nf / max_exact) / math.log(REL_MAX_DIST / max_exact)
                         * (nb - max_exact)).astype(jnp.int32)
    large = jnp.minimum(large, nb - 1)
    return jnp.where(rel > 0, nb, 0) + jnp.where(n < max_exact, n, large)


def _band_units(tq, tk):
    u = min(tq, tk)
    return u, tq // u, tk // u


def _bias_band(rel_table, tq, tk):
    u, rq, rk = _band_units(tq, tk)
    qpos = jnp.arange(tq)
    tiles = []
    for d in range(-rk - 1, rq + 2):
        rel = (d * u + jnp.arange(tk))[:, None] - qpos[None, :]
        onehot = jax.nn.one_hot(_rel_bucket(rel), REL_BUCKETS, dtype=F32)
        tiles.append(jnp.einsum("kqb,bh->hkq", onehot, rel_table.astype(F32),
                                precision=lax.Precision.HIGHEST))
    return jnp.stack(tiles) * LOG2E


def _rope_tables(n):
    rows = n // GRID_W
    row_id = jnp.repeat(jnp.arange(rows), GRID_W).astype(jnp.float32)
    col_id = (jnp.arange(rows * GRID_W) % GRID_W).astype(jnp.float32)
    sec = GQ_HEAD_DIM // 2
    inv = ROPE_THETA ** (-jnp.arange(0, sec, 2, dtype=jnp.float32) / sec)
    cr, sr = jnp.cos(row_id[:, None] * inv), jnp.sin(row_id[:, None] * inv)
    cc, sc = jnp.cos(col_id[:, None] * inv), jnp.sin(col_id[:, None] * inv)
    cos = jnp.concatenate([cr, cr, cc, cc], axis=1)
    sin = jnp.concatenate([-sr, sr, -sc, sc], axis=1)
    return jnp.concatenate([cos, cos], axis=1), jnp.concatenate([sin, sin], axis=1)


def _prep_weights(w_in, gla_gate_w, gla_gate_b, gq_qk_g, w_branch, w_out, peer_wq, peer_subkeys,
                  peer_u, peer_v):
    wi = jnp.concatenate(
        [w_in[:, :, ORIG_GZ:], w_in[:, :, :ORIG_BZ], w_in[:, :, ORIG_CQ:ORIG_GZ],
         w_in[:, :, ORIG_BZ:ORIG_CQ],
         jnp.zeros((DEPTH, D_MODEL, PROJ_W - ORIG_W), w_in.dtype)], axis=2).astype(BF16)
    hd = GLA_HEADS * GLA_DK
    gw = jnp.zeros((DEPTH, 2, 128, hd), F32)
    gw = gw.at[:, 0, 0:GLA_GATE_RANK].set(gla_gate_w[:, 0])
    gw = gw.at[:, 1, GLA_GATE_RANK:2 * GLA_GATE_RANK].set(gla_gate_w[:, 1])
    return dict(
        w_in=wi, gw=gw.astype(BF16), gb=gla_gate_b.astype(F32)[:, :, None, :],
        gq=jnp.tile(gq_qk_g[:, 0], (1, GQ_HEADS))[:, None, :],
        gk=jnp.tile(gq_qk_g[:, 1], (1, GQ_KV_HEADS))[:, None, :],
        wb=w_branch.astype(BF16), wo=w_out.astype(BF16),
        wq_t=jnp.swapaxes(peer_wq, 1, 2).astype(BF16), sk=peer_subkeys.astype(BF16),
        u=peer_u.astype(BF16), v_t=jnp.swapaxes(peer_v, 1, 2).astype(BF16))


def _encoder(x, w, rel_bias, norm1_g, da_lambda, da_subln_g, gla_norm_g, norm2_g, final_g, seg_ones):
    b, n, _ = x.shape
    t = b * n
    da_tq, da_tk = _da_tiles(n)
    band = _bias_band(rel_bias, da_tq, da_tk)
    cos2, sin2 = _rope_tables(n)
    for l in range(DEPTH):
        proj = in_proj(x.reshape(t, D_MODEL), norm1_g[l][None, :], w["w_in"][l]).reshape(b, n, PROJ_W)
        lam_init = 0.8 - 0.6 * math.exp(-0.3 * l)
        lp = da_lambda[l].astype(F32)
        lam = (jnp.exp(jnp.sum(lp[0] * lp[1])) - jnp.exp(jnp.sum(lp[2] * lp[3])) + lam_init).reshape(1)
        ya = diff_attn(proj, band, lam, da_subln_g[l][None, :], lam_init=lam_init, tq=da_tq, tk=da_tk)
        qt, kr, vt = gqa_prep(proj, cos2, sin2, w["gq"][l], w["gk"][l], seg_ones)
        yc = gqa_attn(qt, kr, vt, tq=GQ_TQ, tk=GQ_TK)
        of = gla(proj, w["gw"][l, 0], w["gb"][l, 0], reverse=False)
        ob = gla(proj, w["gw"][l, 1], w["gb"][l, 1], reverse=True)
        x1 = merge(x, ya, of, ob, proj, yc, gla_norm_g[l][None, :], w["wb"][l], w["wo"][l])
        x1 = x1.reshape(t, D_MODEL)
        hn_t, s1, e1, th, e0 = peer_route(x1, norm2_g[l][None, :], w["wq_t"][l], w["sk"][l])
        x = peer_dense(x1, hn_t, w["u"][l], w["v_t"][l], s1, e1, th, e0, final_g[None, :],
                       final_norm=(l == DEPTH - 1)).reshape(b, n, D_MODEL)
    return x


def kernel(x_prompt, x_sample, rel_bias, norm1_g, w_in, da_lambda, da_subln_g, gla_gate_w, gla_gate_b,
           gla_norm_g, gq_qk_g, w_branch, w_out, norm2_g, peer_wq, peer_subkeys, peer_u, peer_v, final_g):
    w = _prep_weights(w_in, gla_gate_w, gla_gate_b, gq_qk_g, w_branch, w_out, peer_wq, peer_subkeys,
                      peer_u, peer_v)
    seg = np.arange(512) // GQ_HEAD_DIM
    seg_ones = jnp.asarray(seg[:, None] == seg[None, :], dtype=BF16)
    args = (w, rel_bias, norm1_g, da_lambda, da_subln_g, gla_norm_g, norm2_g, final_g, seg_ones)
    return (_encoder(x_prompt, *args), _encoder(x_sample, *args))
```

```python
import functools
import math

import numpy as np
import jax
import jax.numpy as jnp
from jax import lax
from jax.experimental import pallas as pl
from jax.experimental.pallas import tpu as pltpu

D_MODEL = 1024
DEPTH = 2
GRID_W = 64
NORM_EPS = 1e-6
N_BRANCH = 3
BRANCH_W = 512

DA_HEADS = 4
DA_HEAD_DIM = 64
DA_V_DIM = 128
REL_BUCKETS = 32
REL_MAX_DIST = 128

GLA_HEADS = 4
GLA_DK = 64
GLA_DV = 128
GLA_GATE_RANK = 16
GLA_TAU = 16.0
GLA_CHUNK = 64

GQ_HEADS = 8
GQ_KV_HEADS = 2
GQ_GROUP = GQ_HEADS // GQ_KV_HEADS
GQ_HEAD_DIM = 64
ROPE_THETA = 10000.0

PEER_HEADS = 8
PEER_KEYS = 128
PEER_EXPERTS = PEER_KEYS * PEER_KEYS
PEER_QDIM = 256
PEER_HALF = 128
PEER_TOPK = 16

COL_GZ = 0
COL_AQ, COL_AK, COL_AV = 3072, 3584, 4096
COL_BQ, COL_BK, COL_BV, COL_BR = 4608, 4864, 5120, 5632
COL_CQ, COL_CK, COL_CV = 6144, 6656, 6784
COL_BZ = 6912
PROJ_W = 7168
ORIG_BZ = 3072
ORIG_CQ = ORIG_BZ + 2 * GLA_GATE_RANK
ORIG_GZ = ORIG_CQ + 768
ORIG_W = ORIG_GZ + N_BRANCH * D_MODEL

GQ_TQ, GQ_TK = 512, 512


def _da_tiles(n):
    return (1024, 1024) if n >= 8192 else (512, 512)

VMEM_LIMIT = 56 * 1024 * 1024

F32 = jnp.float32
BF16 = jnp.bfloat16
NEG_INF = float("-inf")
LOG2E = math.log2(math.e)


def _cparams(sem):
    return pltpu.CompilerParams(dimension_semantics=sem, vmem_limit_bytes=VMEM_LIMIT)


def _dot(a, b):
    return jnp.dot(a, b, preferred_element_type=F32)


def _dot_nt(a, b):
    return lax.dot_general(a, b, (((1,), (1,)), ((), ())), preferred_element_type=F32)


def _dot_tn(a, b):
    return lax.dot_general(a, b, (((0,), (0,)), ((), ())), preferred_element_type=F32)


def _rms(x, g):
    return x * lax.rsqrt(jnp.mean(x * x, axis=-1, keepdims=True) + NORM_EPS) * g


def _in_proj_kernel(x_ref, g_ref, w_ref, o_ref, hn_ref):
    @pl.when(pl.program_id(1) == 0)
    def _():
        hn_ref[...] = _rms(x_ref[...], g_ref[...]).astype(BF16)

    o_ref[...] = _dot(hn_ref[...], w_ref[...]).astype(o_ref.dtype)


def in_proj(x2d, g, w, *, tm=2048, tn=1024):
    t = x2d.shape[0]
    nw = w.shape[1]
    return pl.pallas_call(
        _in_proj_kernel,
        grid=(t // tm, nw // tn),
        in_specs=[
            pl.BlockSpec((tm, D_MODEL), lambda i, j: (i, 0)),
            pl.BlockSpec((1, D_MODEL), lambda i, j: (0, 0)),
            pl.BlockSpec((D_MODEL, tn), lambda i, j: (0, j)),
        ],
        out_specs=pl.BlockSpec((tm, tn), lambda i, j: (i, j)),
        out_shape=jax.ShapeDtypeStruct((t, nw), BF16),
        scratch_shapes=[pltpu.VMEM((tm, D_MODEL), BF16)],
        compiler_params=_cparams(("parallel", "arbitrary")),
        name="in_proj",
    )(x2d, g, w)


SUM_ROWS = 16


def _flash_pipe_step(k, vt, bias, qt_ref, cur, prev, m_ref, acc_ref):
    s_cur, p_cur, al_cur, mt_cur = cur
    s_prev, p_prev, al_prev, mt_prev = prev
    s = _dot(k, qt_ref[...])
    if bias is not None:
        s = s + jnp.concatenate([bias] * (s.shape[1] // bias.shape[1]), axis=1)
    s_cur[...] = s
    mt_cur[...] = jnp.max(s, axis=0, keepdims=True)
    acc_ref[...] = al_cur[...] * acc_ref[...] + _dot(vt, p_cur[...])
    m_prev = m_ref[...]
    m_next = jnp.maximum(m_prev, mt_prev[...])
    p_prev[...] = jnp.exp2(s_prev[...] - m_next).astype(BF16)
    al_prev[...] = jnp.exp2(m_prev - m_next)
    m_ref[...] = m_next


def _pipe_tile(t, lag, nq, nk):
    j = jnp.clip(t - lag, 0, nq * nk - 1)
    return j // nk, j % nk


def _flash_pipe(t, nq, nk, build_qt, k_ref, get_vt, bias_ref, qt_ref, slots, m_ref, acc_ref, finish):
    _, kt_a = _pipe_tile(t, 0, nq, nk)
    _, kt_b = _pipe_tile(t, 1, nq, nk)
    _, kt_c = _pipe_tile(t, 2, nq, nk)

    @pl.when(t == 0)
    def _():
        (_, p0, al0, _), (s1, _, _, mt1) = slots
        for r in (s1, mt1, p0, al0, acc_ref):
            r[...] = jnp.zeros(r.shape, r.dtype)

    @pl.when(kt_a == 0)
    def _():
        build_qt()

    @pl.when(kt_b == 0)
    def _():
        m_ref[...] = jnp.full(m_ref.shape, NEG_INF, F32)

    for c in (0, 1):
        @pl.when(t % 2 == c)
        def _(c=c):
            bias = None if bias_ref is None else bias_ref[0, 0]
            _flash_pipe_step(k_ref[0], get_vt(), bias, qt_ref, slots[c], slots[1 - c], m_ref, acc_ref)

    @pl.when((t >= 2) & (kt_c == nk - 1))
    def _():
        finish()


def _flash_scratch(tk, cols, dv):
    slot = [pltpu.VMEM((tk, cols), F32), pltpu.VMEM((tk, cols), BF16),
            pltpu.VMEM((1, cols), F32), pltpu.VMEM((1, cols), F32)]
    return slot + slot + [pltpu.VMEM((1, cols), F32), pltpu.VMEM((dv + SUM_ROWS, cols), F32)]


def _dattn_kernel(lam_ref, q_ref, k_ref, v_ref, b_ref, g_ref, o_ref, qt_ref,
                  s0_ref, p0_ref, al0_ref, mt0_ref, s1_ref, p1_ref, al1_ref, mt1_ref,
                  m_ref, acc_ref, *, tq, nq, nk, out_scale):

    def build_qt():
        q = q_ref[0].astype(F32) * (LOG2E * DA_HEAD_DIM ** -0.5)
        lane = lax.broadcasted_iota(jnp.int32, q.shape, 1)
        qt_ref[:, 0:tq] = jnp.where(lane < DA_HEAD_DIM, q, 0.0).T.astype(BF16)
        qt_ref[:, tq:2 * tq] = jnp.where(lane >= DA_HEAD_DIM, q, 0.0).T.astype(BF16)

    def get_vt():
        vt = v_ref[0].astype(F32).T.astype(BF16)
        return jnp.concatenate([vt, jnp.ones((SUM_ROWS, vt.shape[1]), BF16)], axis=0)

    def finish():
        o = acc_ref[0:DA_V_DIM, :] / acc_ref[DA_V_DIM:DA_V_DIM + 1, :]
        o = (o[:, 0:tq] - lam_ref[0] * o[:, tq:2 * tq]).T
        o_ref[0] = (_rms(o, g_ref[...]) * out_scale).astype(o_ref.dtype)

    slots = ((s0_ref, p0_ref, al0_ref, mt0_ref), (s1_ref, p1_ref, al1_ref, mt1_ref))
    _flash_pipe(pl.program_id(2), nq, nk, build_qt, k_ref, get_vt, b_ref, qt_ref, slots, m_ref, acc_ref,
                finish)


def diff_attn(proj, bias_band, lam, sub_g, *, lam_init, tq, tk):
    b, n, _ = proj.shape
    _, rq, rk = _band_units(tq, tk)
    nq, nk = n // tq, n // tk
    kern = functools.partial(_dattn_kernel, tq=tq, nq=nq, nk=nk, out_scale=1.0 - lam_init)
    tile = functools.partial(_pipe_tile, nq=nq, nk=nk)

    def bias_map(bi, h, t):
        qi, ki = tile(t, 0)
        return (jnp.clip(rk * ki - rq * qi, -rk - 1, rq + 1) + rk + 1, h, 0, 0)

    return pl.pallas_call(
        kern,
        grid=(b, DA_HEADS, nq * nk + 2),
        in_specs=[
            pl.BlockSpec(memory_space=pltpu.SMEM),
            pl.BlockSpec((1, tq, 128), lambda bi, h, t: (bi, tile(t, 0)[0], COL_AQ // 128 + h)),
            pl.BlockSpec((1, tk, 128), lambda bi, h, t: (bi, tile(t, 0)[1], COL_AK // 128 + h)),
            pl.BlockSpec((1, tk, 128), lambda bi, h, t: (bi, tile(t, 2)[1], COL_AV // 128 + h)),
            pl.BlockSpec((1, 1, tk, tq), bias_map),
            pl.BlockSpec((1, DA_V_DIM), lambda bi, h, t: (0, 0)),
        ],
        out_specs=pl.BlockSpec((1, tq, DA_V_DIM), lambda bi, h, t: (bi, tile(t, 2)[0], h)),
        out_shape=jax.ShapeDtypeStruct((b, n, DA_HEADS * DA_V_DIM), BF16),
        scratch_shapes=[pltpu.VMEM((128, 2 * tq), BF16)] + _flash_scratch(tk, 2 * tq, DA_V_DIM),
        compiler_params=_cparams(("parallel", "parallel", "arbitrary")),
        name="diff_attn",
    )(lam, proj, proj, proj, bias_band, sub_g)


def _swap16(x):
    w = x.shape[1]
    lane = lax.broadcasted_iota(jnp.int32, x.shape, 1)
    return jnp.where(lane % 32 < 16, pltpu.roll(x, w - 16, 1), pltpu.roll(x, 16, 1))


def _seg_rms(x, seg_ones, g):
    sq = x * x
    hi = sq.astype(BF16)
    lo = (sq - hi.astype(F32)).astype(BF16)
    ss = _dot(hi, seg_ones) + _dot(lo, seg_ones)
    return x * lax.rsqrt(ss * (1.0 / GQ_HEAD_DIM) + NORM_EPS) * g


def _gqa_prep_kernel(q_ref, k_ref, v_ref, cos_ref, sin_ref, gq_ref, gk_ref, ones_ref,
                     qo_ref, ko_ref, vo_ref):
    cos = cos_ref[...]
    sin = sin_ref[...]
    cos4 = jnp.concatenate([cos] * 4, axis=1)
    sin4 = jnp.concatenate([sin] * 4, axis=1)
    ones = ones_ref[...]
    q = _seg_rms(q_ref[0].astype(F32), ones, gq_ref[...])
    k = _seg_rms(k_ref[0].astype(F32), ones[0:128, 0:128], gk_ref[...])
    q = (q * cos4 + _swap16(q) * sin4) * (LOG2E * GQ_HEAD_DIM ** -0.5)
    qo_ref[0] = q.T.astype(BF16)
    ko_ref[0] = (k * cos + _swap16(k) * sin).astype(BF16)
    vt = v_ref[0].astype(F32).T.astype(BF16)
    hd = GQ_HEAD_DIM
    ones = jnp.ones((SUM_ROWS, vt.shape[1]), BF16)
    for h in range(GQ_KV_HEADS):
        vo_ref[0, h, 0:hd, :] = vt[h * hd:(h + 1) * hd]
        vo_ref[0, h, hd:hd + SUM_ROWS, :] = ones


def gqa_prep(proj, cos2, sin2, gq, gk, seg_ones, *, tm=512):
    b, n, _ = proj.shape
    return pl.pallas_call(
        _gqa_prep_kernel,
        grid=(b, n // tm),
        in_specs=[
            pl.BlockSpec((1, tm, 512), lambda bi, i: (bi, i, COL_CQ // 512)),
            pl.BlockSpec((1, tm, 128), lambda bi, i: (bi, i, COL_CK // 128)),
            pl.BlockSpec((1, tm, 128), lambda bi, i: (bi, i, COL_CV // 128)),
            pl.BlockSpec((tm, 128), lambda bi, i: (i, 0)),
            pl.BlockSpec((tm, 128), lambda bi, i: (i, 0)),
            pl.BlockSpec((1, 512), lambda bi, i: (0, 0)),
            pl.BlockSpec((1, 128), lambda bi, i: (0, 0)),
            pl.BlockSpec((512, 512), lambda bi, i: (0, 0)),
        ],
        out_specs=[
            pl.BlockSpec((1, GQ_HEADS * GQ_HEAD_DIM, tm), lambda bi, i: (bi, 0, i)),
            pl.BlockSpec((1, tm, GQ_KV_HEADS * GQ_HEAD_DIM), lambda bi, i: (bi, i, 0)),
            pl.BlockSpec((1, GQ_KV_HEADS, GQ_HEAD_DIM + SUM_ROWS, tm), lambda bi, i: (bi, 0, 0, i)),
        ],
        out_shape=[
            jax.ShapeDtypeStruct((b, GQ_HEADS * GQ_HEAD_DIM, n), BF16),
            jax.ShapeDtypeStruct((b, n, GQ_KV_HEADS * GQ_HEAD_DIM), BF16),
            jax.ShapeDtypeStruct((b, GQ_KV_HEADS, GQ_HEAD_DIM + SUM_ROWS, n), BF16),
        ],
        compiler_params=_cparams(("parallel", "parallel")),
        name="gqa_prep",
    )(proj, proj, proj, cos2, sin2, gq, gk, seg_ones)


def _gqa_kernel(q_ref, k_ref, vt_ref, o_ref, qt_ref,
                s0_ref, p0_ref, al0_ref, mt0_ref, s1_ref, p1_ref, al1_ref, mt1_ref,
                m_ref, acc_ref, *, tq, nq, nk):
    hd = GQ_HEAD_DIM

    def build_qt():
        qt_ref[...] = jnp.zeros(qt_ref.shape, BF16)
        base = pl.multiple_of(pl.program_id(1) * hd, hd)
        for r in range(GQ_GROUP):
            qt_ref[pl.ds(base, hd), r * tq:(r + 1) * tq] = q_ref[0, r * hd:(r + 1) * hd, :]

    def finish():
        o = (acc_ref[0:hd, :] / acc_ref[hd:hd + 1, :]).astype(o_ref.dtype)
        for r in range(GQ_GROUP):
            o_ref[0, r * hd:(r + 1) * hd, :] = o[:, r * tq:(r + 1) * tq]

    slots = ((s0_ref, p0_ref, al0_ref, mt0_ref), (s1_ref, p1_ref, al1_ref, mt1_ref))
    _flash_pipe(pl.program_id(2), nq, nk, build_qt, k_ref, lambda: vt_ref[0, 0], None, qt_ref, slots,
                m_ref, acc_ref, finish)


def gqa_attn(qt, k, vt, *, tq, tk):
    b, _, n = qt.shape
    gw = GQ_GROUP * GQ_HEAD_DIM
    kvw = GQ_KV_HEADS * GQ_HEAD_DIM
    nq, nk = n // tq, n // tk
    cols = GQ_GROUP * tq
    kern = functools.partial(_gqa_kernel, tq=tq, nq=nq, nk=nk)
    tile = functools.partial(_pipe_tile, nq=nq, nk=nk)
    return pl.pallas_call(
        kern,
        grid=(b, GQ_KV_HEADS, nq * nk + 2),
        in_specs=[
            pl.BlockSpec((1, gw, tq), lambda bi, g, t: (bi, g, tile(t, 0)[0])),
            pl.BlockSpec((1, tk, kvw), lambda bi, g, t: (bi, tile(t, 0)[1], 0)),
            pl.BlockSpec((1, 1, GQ_HEAD_DIM + SUM_ROWS, tk), lambda bi, g, t: (bi, g, 0, tile(t, 2)[1])),
        ],
        out_specs=pl.BlockSpec((1, gw, tq), lambda bi, g, t: (bi, g, tile(t, 2)[0])),
        out_shape=jax.ShapeDtypeStruct((b, GQ_HEADS * GQ_HEAD_DIM, n), BF16),
        scratch_shapes=[pltpu.VMEM((kvw, cols), BF16)] + _flash_scratch(tk, cols, GQ_HEAD_DIM),
        compiler_params=_cparams(("parallel", "parallel", "arbitrary")),
        name="gqa_attn",
    )(qt, k, vt)


def _gla_kernel(q_ref, k_ref, v_ref, z_ref, gw_ref, gb_ref, o_ref, st_ref, *, reverse, nchunk):
    L = GLA_CHUNK
    hd = GLA_HEADS * GLA_DK

    @pl.when(pl.program_id(1) == 0)
    def _():
        st_ref[...] = jnp.zeros(st_ref.shape, F32)

    row = lax.broadcasted_iota(jnp.int32, (L, L), 0)
    col = lax.broadcasted_iota(jnp.int32, (L, L), 1)
    if reverse:
        tri = (col >= row).astype(BF16)
        keep = col > row
    else:
        tri = (col <= row).astype(BF16)
        keep = col <= row
    lane = lax.broadcasted_iota(jnp.int32, (L, hd), 1)

    order = range(nchunk - 1, -1, -1) if reverse else range(nchunk)
    for c in order:
        rows = pl.ds(c * L, L)
        q = q_ref[0, rows, :].astype(F32) * (GLA_DK ** -0.5)
        k = k_ref[0, rows, :].astype(F32)
        logit = _dot(z_ref[0, rows, :], gw_ref[...]) + gb_ref[...]
        lg = jax.nn.log_sigmoid(logit) * (1.0 / GLA_TAU)
        hi = lg.astype(BF16)
        lo = (lg - hi.astype(F32)).astype(BF16)
        cum = _dot(tri, hi) + _dot(tri, lo)
        last = cum[0:1, :] if reverse else cum[L - 1:L, :]
        q_dec = q * jnp.exp(cum)
        k_dec = (k * jnp.exp(-cum)).astype(BF16)
        k_last = (k * jnp.exp(last - cum)).astype(BF16)
        decay = jnp.exp(last)
        for h in range(GLA_HEADS):
            head = (lane >= h * GLA_DK) & (lane < (h + 1) * GLA_DK)
            qh = jnp.where(head, q_dec, 0.0).astype(BF16)
            vh = v_ref[0, rows, h * GLA_DV:(h + 1) * GLA_DV]
            a = jnp.where(keep, _dot_nt(qh, k_dec), 0.0)
            st = st_ref[h]
            o = _dot(a.astype(BF16), vh) + _dot_nt(qh, st.astype(BF16))
            o_ref[0, rows, h * GLA_DV:(h + 1) * GLA_DV] = o
            st_ref[h] = st * decay + _dot_tn(vh, k_last)


def gla(proj, gw, gb, *, reverse, tm=512):
    b, n, _ = proj.shape
    nb = n // tm
    kern = functools.partial(_gla_kernel, reverse=reverse, nchunk=tm // GLA_CHUNK)
    blk = (lambda i: nb - 1 - i) if reverse else (lambda i: i)
    hd = GLA_HEADS * GLA_DK
    return pl.pallas_call(
        kern,
        grid=(b, nb),
        in_specs=[
            pl.BlockSpec((1, tm, hd), lambda bi, i: (bi, blk(i), COL_BQ // hd)),
            pl.BlockSpec((1, tm, hd), lambda bi, i: (bi, blk(i), COL_BK // hd)),
            pl.BlockSpec((1, tm, 512), lambda bi, i: (bi, blk(i), COL_BV // 512)),
            pl.BlockSpec((1, tm, 128), lambda bi, i: (bi, blk(i), COL_BZ // 128)),
            pl.BlockSpec((128, hd), lambda bi, i: (0, 0)),
            pl.BlockSpec((1, hd), lambda bi, i: (0, 0)),
        ],
        out_specs=pl.BlockSpec((1, tm, GLA_HEADS * GLA_DV), lambda bi, i: (bi, blk(i), 0)),
        out_shape=jax.ShapeDtypeStruct((b, n, GLA_HEADS * GLA_DV), F32),
        scratch_shapes=[pltpu.VMEM((GLA_HEADS, GLA_DV, hd), F32)],
        compiler_params=_cparams(("parallel", "arbitrary")),
        name="gla_bwd" if reverse else "gla_fwd",
    )(proj, proj, proj, proj, gw, gb)


def _merge_kernel(x_ref, ya_ref, of_ref, ob_ref, r_ref, yc_ref, gz_ref, gn_ref, wb_ref, wo_ref,
                  o_ref):
    gn = gn_ref[...]
    o = of_ref[0] + ob_ref[0]
    r = r_ref[0].astype(F32)
    yb = []
    for h in range(GLA_HEADS):
        sl = slice(h * GLA_DV, (h + 1) * GLA_DV)
        yb.append(_rms(o[:, sl], gn) * jax.nn.silu(r[:, sl]))
    yb = jnp.concatenate(yb, axis=1).astype(BF16)

    pa = _dot(ya_ref[0], wb_ref[0])
    pb = _dot(yb, wb_ref[1])
    pc = _dot_tn(yc_ref[0], wb_ref[2])

    gz = gz_ref[0].astype(F32)
    merged = (jax.nn.sigmoid(gz[:, 0:D_MODEL]) * pa
              + jax.nn.sigmoid(gz[:, D_MODEL:2 * D_MODEL]) * pb
              + jax.nn.sigmoid(gz[:, 2 * D_MODEL:3 * D_MODEL]) * pc)
    o_ref[0] = x_ref[0] + _dot(merged.astype(BF16), wo_ref[...])


def merge(x, ya, of, ob, proj, yc, gn, wb, wo, *, tm=512):
    b, n, _ = x.shape
    tok = lambda bi, i: (bi, i, 0)
    return pl.pallas_call(
        _merge_kernel,
        grid=(b, n // tm),
        in_specs=[
            pl.BlockSpec((1, tm, D_MODEL), tok),
            pl.BlockSpec((1, tm, BRANCH_W), tok),
            pl.BlockSpec((1, tm, BRANCH_W), tok),
            pl.BlockSpec((1, tm, BRANCH_W), tok),
            pl.BlockSpec((1, tm, 512), lambda bi, i: (bi, i, COL_BR // 512)),
            pl.BlockSpec((1, BRANCH_W, tm), lambda bi, i: (bi, 0, i)),
            pl.BlockSpec((1, tm, N_BRANCH * D_MODEL), lambda bi, i: (bi, i, COL_GZ // (N_BRANCH * D_MODEL))),
            pl.BlockSpec((1, GLA_DV), lambda bi, i: (0, 0)),
            pl.BlockSpec((N_BRANCH, BRANCH_W, D_MODEL), lambda bi, i: (0, 0, 0)),
            pl.BlockSpec((D_MODEL, D_MODEL), lambda bi, i: (0, 0)),
        ],
        out_specs=pl.BlockSpec((1, tm, D_MODEL), tok),
        out_shape=jax.ShapeDtypeStruct((b, n, D_MODEL), F32),
        compiler_params=_cparams(("parallel", "parallel")),
        name="merge",
    )(x, ya, of, ob, proj, yc, proj, gn, wb, wo)


TOPX = PEER_TOPK + 1
TOPX_PAD = 24


def _top_values(vals, count):
    out = []
    for _ in range(count):
        m = jnp.max(vals, axis=0, keepdims=True)
        out.append(m)
        vals = jnp.where(vals == m, NEG_INF, vals)
    return out


def _route_kernel(x_ref, g_ref, wq_ref, sk_ref, hn_ref, s1_ref, e1_ref, th_ref, e0_ref):
    tb = x_ref.shape[0]
    hn = _rms(x_ref[...], g_ref[...])
    hn_t = hn.T.astype(BF16)
    hn_ref[...] = hn_t
    q_t = _dot(wq_ref[...], hn_t).astype(BF16)
    pad = jnp.full((TOPX_PAD - TOPX, tb), NEG_INF, F32)
    for h in range(PEER_HEADS):
        r0 = (2 * h) * PEER_HALF
        s0 = _dot(sk_ref[h, 0], q_t[r0:r0 + PEER_HALF])
        s1 = _dot(sk_ref[h, 1], q_t[r0 + PEER_HALF:r0 + 2 * PEER_HALF])
        top0 = _top_values(s0, TOPX)
        top1 = _top_values(s1, TOPX)
        slab0 = jnp.concatenate(top0 + [pad], axis=0)
        slab1 = jnp.concatenate(top1 + [pad], axis=0)
        half = slab1[0:8]
        cand = jnp.concatenate([top0[0] + slab1] + [top0[a] + half for a in range(1, 8)]
                               + [slab0[8:TOPX_PAD] + top1[0]], axis=0)
        best = _top_values(cand, TOPX)
        thresh = 0.5 * (best[PEER_TOPK - 1] + best[PEER_TOPK])
        m0, m1 = top0[0], top1[0]
        z = jnp.zeros_like(m0)
        for c in best[:PEER_TOPK]:
            z = z + jnp.exp(c - best[0])
        e1 = jnp.exp(s1 - m1) / z
        th = thresh - s0
        e0 = jnp.exp(s0 - m0)
        for c in range(tb // 128):
            cs = slice(c * 128, (c + 1) * 128)
            s1_ref[h, c] = s1[:, cs]
            e1_ref[h, c] = e1[:, cs]
            th_ref[h, c] = th[:, cs]
            e0_ref[h, c] = e0[:, cs]


def peer_route(x2d, g, wq_t, subkeys, *, tb=256):
    t = x2d.shape[0]
    hk = jax.ShapeDtypeStruct((PEER_HEADS, t // 128, PEER_KEYS, 128), F32)
    hk_spec = pl.BlockSpec((PEER_HEADS, tb // 128, PEER_KEYS, 128), lambda i: (0, i, 0, 0))
    return pl.pallas_call(
        _route_kernel,
        grid=(t // tb,),
        in_specs=[
            pl.BlockSpec((tb, D_MODEL), lambda i: (i, 0)),
            pl.BlockSpec((1, D_MODEL), lambda i: (0, 0)),
            pl.BlockSpec((PEER_HEADS * PEER_QDIM, D_MODEL), lambda i: (0, 0)),
            pl.BlockSpec((PEER_HEADS, 2, PEER_KEYS, PEER_HALF), lambda i: (0, 0, 0, 0)),
        ],
        out_specs=[pl.BlockSpec((D_MODEL, tb), lambda i: (0, i)), hk_spec, hk_spec, hk_spec, hk_spec],
        out_shape=[jax.ShapeDtypeStruct((D_MODEL, t), BF16), hk, hk, hk, hk],
        compiler_params=_cparams(("parallel",)),
        name="peer_route",
    )(x2d, g, wq_t, subkeys)


DENSE_ROWS = 32


def _dense_kernel(x_ref, hn_ref, u_ref, vt_ref, s1_ref, e1_ref, th_ref, e0_ref, fg_ref, o_ref,
                  wg_ref, acc_ref, *, ne, ib, final_norm):
    e = pl.program_id(1)
    tb = hn_ref.shape[1]

    @pl.when(e == 0)
    def _():
        acc_ref[...] = jnp.zeros(acc_ref.shape, F32)

    a = _dot(u_ref[...], hn_ref[...])
    gelu_c = np.float32(np.sqrt(0.5))
    for il in range(ib):
        for tc in range(tb // 128):
            cols = slice(tc * 128, (tc + 1) * 128)
            for jc in range(PEER_KEYS // DENSE_ROWS):
                keys = slice(jc * DENSE_ROWS, (jc + 1) * DENSE_ROWS)
                rows = slice(il * PEER_KEYS + jc * DENSE_ROWS, il * PEER_KEYS + (jc + 1) * DENSE_ROWS)
                w = jnp.zeros((DENSE_ROWS, 128), F32)
                for h in range(PEER_HEADS):
                    sel = s1_ref[h, tc, keys, :] >= th_ref[h, tc, il:il + 1, :]
                    w = w + jnp.where(sel, e1_ref[h, tc, keys, :], 0.0) * e0_ref[h, tc, il:il + 1, :]
                ac = a[rows, cols]
                act = 0.5 * ac * (1.0 + lax.erf(ac * gelu_c))
                wg_ref[rows, cols] = (w * act).astype(BF16)
    acc_ref[...] += _dot(vt_ref[...], wg_ref[...])

    @pl.when(e == ne - 1)
    def _():
        y = x_ref[...] + acc_ref[...].T
        if final_norm:
            y = _rms(y, fg_ref[...])
        o_ref[...] = y


def peer_dense(x2d, hn_t, u, v_t, s1, e1, th, e0, final_g, *, final_norm, tb=512, ib=8):
    t = x2d.shape[0]
    eb = ib * PEER_KEYS
    ne = PEER_EXPERTS // eb
    kern = functools.partial(_dense_kernel, ne=ne, ib=ib, final_norm=final_norm)
    full = pl.BlockSpec((PEER_HEADS, tb // 128, PEER_KEYS, 128), lambda i, e: (0, i, 0, 0))
    part = pl.BlockSpec((PEER_HEADS, tb // 128, ib, 128), lambda i, e: (0, i, e, 0))
    return pl.pallas_call(
        kern,
        grid=(t // tb, ne),
        in_specs=[
            pl.BlockSpec((tb, D_MODEL), lambda i, e: (i, 0)),
            pl.BlockSpec((D_MODEL, tb), lambda i, e: (0, i)),
            pl.BlockSpec((eb, D_MODEL), lambda i, e: (e, 0)),
            pl.BlockSpec((D_MODEL, eb), lambda i, e: (0, e)),
            full, full, part, part,
            pl.BlockSpec((1, D_MODEL), lambda i, e: (0, 0)),
        ],
        out_specs=pl.BlockSpec((tb, D_MODEL), lambda i, e: (i, 0)),
        out_shape=jax.ShapeDtypeStruct((t, D_MODEL), F32),
        scratch_shapes=[pltpu.VMEM((eb, tb), BF16), pltpu.VMEM((D_MODEL, tb), F32)],
        compiler_params=_cparams(("parallel", "arbitrary")),
        name="peer_dense",
    )(x2d, hn_t, u, v_t, s1, e1, th, e0, final_g)


def _rel_bucket(rel):
    nb = REL_BUCKETS // 2
    max_exact = nb // 2
    n = jnp.abs(rel)
    nf = jnp.maximum(n, 1).astype(jnp.float32)
    large = max_exact + (jnp.log(nf / max_exact) / math.log(REL_MAX_DIST / max_exact)
                         * (nb - max_exact)).astype(jnp.int32)
    large = jnp.minimum(large, nb - 1)
    return jnp.where(rel > 0, nb, 0) + jnp.where(n < max_exact, n, large)


def _band_units(tq, tk):
    u = min(tq, tk)
    return u, tq // u, tk // u


def _bias_band(rel_table, tq, tk):
    u, rq, rk = _band_units(tq, tk)
    qpos = jnp.arange(tq)
    tiles = []
    for d in range(-rk - 1, rq + 2):
        rel = (d * u + jnp.arange(tk))[:, None] - qpos[None, :]
        onehot = jax.nn.one_hot(_rel_bucket(rel), REL_BUCKETS, dtype=F32)
        tiles.append(jnp.einsum("kqb,bh->hkq", onehot, rel_table.astype(F32),
                                precision=lax.Precision.HIGHEST))
    return jnp.stack(tiles) * LOG2E


def _rope_tables(n):
    rows = n // GRID_W
    row_id = jnp.repeat(jnp.arange(rows), GRID_W).astype(jnp.float32)
    col_id = (jnp.arange(rows * GRID_W) % GRID_W).astype(jnp.float32)
    sec = GQ_HEAD_DIM // 2
    inv = ROPE_THETA ** (-jnp.arange(0, sec, 2, dtype=jnp.float32) / sec)
    cr, sr = jnp.cos(row_id[:, None] * inv), jnp.sin(row_id[:, None] * inv)
    cc, sc = jnp.cos(col_id[:, None] * inv), jnp.sin(col_id[:, None] * inv)
    cos = jnp.concatenate([cr, cr, cc, cc], axis=1)
    sin = jnp.concatenate([-sr, sr, -sc, sc], axis=1)
    return jnp.concatenate([cos, cos], axis=1), jnp.concatenate([sin, sin], axis=1)


def _prep_weights(w_in, gla_gate_w, gla_gate_b, gq_qk_g, w_branch, w_out, peer_wq, peer_subkeys,
                  peer_u, peer_v):
    wi = jnp.concatenate(
        [w_in[:, :, ORIG_GZ:], w_in[:, :, :ORIG_BZ], w_in[:, :, ORIG_CQ:ORIG_GZ],
         w_in[:, :, ORIG_BZ:ORIG_CQ],
         jnp.zeros((DEPTH, D_MODEL, PROJ_W - ORIG_W), w_in.dtype)], axis=2).astype(BF16)
    hd = GLA_HEADS * GLA_DK
    gw = jnp.zeros((DEPTH, 2, 128, hd), F32)
    gw = gw.at[:, 0, 0:GLA_GATE_RANK].set(gla_gate_w[:, 0])
    gw = gw.at[:, 1, GLA_GATE_RANK:2 * GLA_GATE_RANK].set(gla_gate_w[:, 1])
    return dict(
        w_in=wi, gw=gw.astype(BF16), gb=gla_gate_b.astype(F32)[:, :, None, :],
        gq=jnp.tile(gq_qk_g[:, 0], (1, GQ_HEADS))[:, None, :],
        gk=jnp.tile(gq_qk_g[:, 1], (1, GQ_KV_HEADS))[:, None, :],
        wb=w_branch.astype(BF16), wo=w_out.astype(BF16),
        wq_t=jnp.swapaxes(peer_wq, 1, 2).astype(BF16), sk=peer_subkeys.astype(BF16),
        u=peer_u.astype(BF16), v_t=jnp.swapaxes(peer_v, 1, 2).astype(BF16))


def _encoder(x, w, rel_bias, norm1_g, da_lambda, da_subln_g, gla_norm_g, norm2_g, final_g, seg_ones):
    b, n, _ = x.shape
    t = b * n
    da_tq, da_tk = _da_tiles(n)
    band = _bias_band(rel_bias, da_tq, da_tk)
    cos2, sin2 = _rope_tables(n)
    for l in range(DEPTH):
        proj = in_proj(x.reshape(t, D_MODEL), norm1_g[l][None, :], w["w_in"][l]).reshape(b, n, PROJ_W)
        lam_init = 0.8 - 0.6 * math.exp(-0.3 * l)
        lp = da_lambda[l].astype(F32)
        lam = (jnp.exp(jnp.sum(lp[0] * lp[1])) - jnp.exp(jnp.sum(lp[2] * lp[3])) + lam_init).reshape(1)
        ya = diff_attn(proj, band, lam, da_subln_g[l][None, :], lam_init=lam_init, tq=da_tq, tk=da_tk)
        qt, kr, vt = gqa_prep(proj, cos2, sin2, w["gq"][l], w["gk"][l], seg_ones)
        yc = gqa_attn(qt, kr, vt, tq=GQ_TQ, tk=GQ_TK)
        of = gla(proj, w["gw"][l, 0], w["gb"][l, 0], reverse=False)
        ob = gla(proj, w["gw"][l, 1], w["gb"][l, 1], reverse=True)
        x1 = merge(x, ya, of, ob, proj, yc, gla_norm_g[l][None, :], w["wb"][l], w["wo"][l])
        x1 = x1.reshape(t, D_MODEL)
        hn_t, s1, e1, th, e0 = peer_route(x1, norm2_g[l][None, :], w["wq_t"][l], w["sk"][l])
        x = peer_dense(x1, hn_t, w["u"][l], w["v_t"][l], s1, e1, th, e0, final_g[None, :],
                       final_norm=(l == DEPTH - 1)).reshape(b, n, D_MODEL)
    return x


def kernel(x_prompt, x_sample, rel_bias, norm1_g, w_in, da_lambda, da_subln_g, gla_gate_w, gla_gate_b,
           gla_norm_g, gq_qk_g, w_branch, w_out, norm2_g, peer_wq, peer_subkeys, peer_u, peer_v, final_g):
    w = _prep_weights(w_in, gla_gate_w, gla_gate_b, gq_qk_g, w_branch, w_out, peer_wq, peer_subkeys,
                      peer_u, peer_v)
    seg = np.arange(512) // GQ_HEAD_DIM
    seg_ones = jnp.asarray(seg[:, None] == seg[None, :], dtype=BF16)
    args = (w, rel_bias, norm1_g, da_lambda, da_subln_g, gla_norm_g, norm2_g, final_g, seg_ones)
    return (_encoder(x_prompt, *args), _encoder(x_sample, *args))
```

```python
import functools
import math

import numpy as np
import jax
import jax.numpy as jnp
from jax import lax
from jax.experimental import pallas as pl
from jax.experimental.pallas import tpu as pltpu

D_MODEL = 1024
DEPTH = 2
GRID_W = 64
NORM_EPS = 1e-6
N_BRANCH = 3
BRANCH_W = 512

DA_HEADS = 4
DA_HEAD_DIM = 64
DA_V_DIM = 128
REL_BUCKETS = 32
REL_MAX_DIST = 128

GLA_HEADS = 4
GLA_DK = 64
GLA_DV = 128
GLA_GATE_RANK = 16
GLA_TAU = 16.0
GLA_CHUNK = 64

GQ_HEADS = 8
GQ_KV_HEADS = 2
GQ_GROUP = GQ_HEADS // GQ_KV_HEADS
GQ_HEAD_DIM = 64
ROPE_THETA = 10000.0

PEER_HEADS = 8
PEER_KEYS = 128
PEER_EXPERTS = PEER_KEYS * PEER_KEYS
PEER_QDIM = 256
PEER_HALF = 128
PEER_TOPK = 16

COL_GZ = 0
COL_AQ, COL_AK, COL_AV = 3072, 3584, 4096
COL_BQ, COL_BK, COL_BV, COL_BR = 4608, 4864, 5120, 5632
COL_CQ, COL_CK, COL_CV = 6144, 6656, 6784
COL_BZ = 6912
PROJ_W = 7168
ORIG_BZ = 3072
ORIG_CQ = ORIG_BZ + 2 * GLA_GATE_RANK
ORIG_GZ = ORIG_CQ + 768
ORIG_W = ORIG_GZ + N_BRANCH * D_MODEL

GQ_TQ, GQ_TK = 512, 512


def _da_tiles(n):
    return (1024, 1024) if n >= 8192 else (512, 512)

VMEM_LIMIT = 56 * 1024 * 1024

F32 = jnp.float32
BF16 = jnp.bfloat16
NEG_INF = float("-inf")
LOG2E = math.log2(math.e)


def _cparams(sem):
    return pltpu.CompilerParams(dimension_semantics=sem, vmem_limit_bytes=VMEM_LIMIT)


def _dot(a, b):
    return jnp.dot(a, b, preferred_element_type=F32)


def _dot_nt(a, b):
    return lax.dot_general(a, b, (((1,), (1,)), ((), ())), preferred_element_type=F32)


def _dot_tn(a, b):
    return lax.dot_general(a, b, (((0,), (0,)), ((), ())), preferred_element_type=F32)


def _rms(x, g):
    return x * lax.rsqrt(jnp.mean(x * x, axis=-1, keepdims=True) + NORM_EPS) * g


def _in_proj_kernel(x_ref, g_ref, w_ref, o_ref, hn_ref):
    @pl.when(pl.program_id(1) == 0)
    def _():
        hn_ref[...] = _rms(x_ref[...], g_ref[...]).astype(BF16)

    o_ref[...] = _dot(hn_ref[...], w_ref[...]).astype(o_ref.dtype)


def in_proj(x2d, g, w, *, tm=2048, tn=1024):
    t = x2d.shape[0]
    nw = w.shape[1]
    return pl.pallas_call(
        _in_proj_kernel,
        grid=(t // tm, nw // tn),
        in_specs=[
            pl.BlockSpec((tm, D_MODEL), lambda i, j: (i, 0)),
            pl.BlockSpec((1, D_MODEL), lambda i, j: (0, 0)),
            pl.BlockSpec((D_MODEL, tn), lambda i, j: (0, j)),
        ],
        out_specs=pl.BlockSpec((tm, tn), lambda i, j: (i, j)),
        out_shape=jax.ShapeDtypeStruct((t, nw), BF16),
        scratch_shapes=[pltpu.VMEM((tm, D_MODEL), BF16)],
        compiler_params=_cparams(("parallel", "arbitrary")),
        name="in_proj",
    )(x2d, g, w)


SUM_ROWS = 16


def _flash_pipe_step(k, vt, bias, qt_ref, cur, prev, m_ref, acc_ref):
    s_cur, p_cur, al_cur, mt_cur = cur
    s_prev, p_prev, al_prev, mt_prev = prev
    s = _dot(k, qt_ref[...])
    if bias is not None:
        s = s + jnp.concatenate([bias] * (s.shape[1] // bias.shape[1]), axis=1)
    s_cur[...] = s
    mt_cur[...] = jnp.max(s, axis=0, keepdims=True)
    acc_ref[...] = al_cur[...] * acc_ref[...] + _dot(vt, p_cur[...])
    m_prev = m_ref[...]
    m_next = jnp.maximum(m_prev, mt_prev[...])
    p_prev[...] = jnp.exp2(s_prev[...] - m_next).astype(BF16)
    al_prev[...] = jnp.exp2(m_prev - m_next)
    m_ref[...] = m_next


def _pipe_tile(t, lag, nq, nk):
    j = jnp.clip(t - lag, 0, nq * nk - 1)
    return j // nk, j % nk


def _flash_pipe(t, nq, nk, build_qt, k_ref, get_vt, bias_ref, qt_ref, slots, m_ref, acc_ref, finish):
    _, kt_a = _pipe_tile(t, 0, nq, nk)
    _, kt_b = _pipe_tile(t, 1, nq, nk)
    _, kt_c = _pipe_tile(t, 2, nq, nk)

    @pl.when(t == 0)
    def _():
        (_, p0, al0, _), (s1, _, _, mt1) = slots
        for r in (s1, mt1, p0, al0, acc_ref):
            r[...] = jnp.zeros(r.shape, r.dtype)

    @pl.when(kt_a == 0)
    def _():
        build_qt()

    @pl.when(kt_b == 0)
    def _():
        m_ref[...] = jnp.full(m_ref.shape, NEG_INF, F32)

    for c in (0, 1):
        @pl.when(t % 2 == c)
        def _(c=c):
            bias = None if bias_ref is None else bias_ref[0, 0]
            _flash_pipe_step(k_ref[0], get_vt(), bias, qt_ref, slots[c], slots[1 - c], m_ref, acc_ref)

    @pl.when((t >= 2) & (kt_c == nk - 1))
    def _():
        finish()


def _flash_scratch(tk, cols, dv):
    slot = [pltpu.VMEM((tk, cols), F32), pltpu.VMEM((tk, cols), BF16),
            pltpu.VMEM((1, cols), F32), pltpu.VMEM((1, cols), F32)]
    return slot + slot + [pltpu.VMEM((1, cols), F32), pltpu.VMEM((dv + SUM_ROWS, cols), F32)]


def _da_prep_kernel(v_ref, vo_ref):
    vt = v_ref[0].astype(F32).T.astype(BF16)
    ones = jnp.ones((SUM_ROWS, vt.shape[1]), BF16)
    for h in range(DA_HEADS):
        vo_ref[0, h, 0:DA_V_DIM, :] = vt[h * DA_V_DIM:(h + 1) * DA_V_DIM]
        vo_ref[0, h, DA_V_DIM:DA_V_DIM + SUM_ROWS, :] = ones


def da_prep(proj, *, tm=512):
    b, n, _ = proj.shape
    width = DA_HEADS * DA_V_DIM
    return pl.pallas_call(
        _da_prep_kernel,
        grid=(b, n // tm),
        in_specs=[pl.BlockSpec((1, tm, width), lambda bi, i: (bi, i, COL_AV // width))],
        out_specs=pl.BlockSpec((1, DA_HEADS, DA_V_DIM + SUM_ROWS, tm), lambda bi, i: (bi, 0, 0, i)),
        out_shape=jax.ShapeDtypeStruct((b, DA_HEADS, DA_V_DIM + SUM_ROWS, n), BF16),
        compiler_params=_cparams(("parallel", "parallel")),
        name="da_prep",
    )(proj)


def _dattn_kernel(lam_ref, q_ref, k_ref, vt_ref, b_ref, g_ref, o_ref, qt_ref,
                  s0_ref, p0_ref, al0_ref, mt0_ref, s1_ref, p1_ref, al1_ref, mt1_ref,
                  m_ref, acc_ref, *, tq, nq, nk, out_scale):

    def build_qt():
        q = q_ref[0].astype(F32) * (LOG2E * DA_HEAD_DIM ** -0.5)
        lane = lax.broadcasted_iota(jnp.int32, q.shape, 1)
        qt_ref[:, 0:tq] = jnp.where(lane < DA_HEAD_DIM, q, 0.0).T.astype(BF16)
        qt_ref[:, tq:2 * tq] = jnp.where(lane >= DA_HEAD_DIM, q, 0.0).T.astype(BF16)

    def finish():
        o = acc_ref[0:DA_V_DIM, :] / acc_ref[DA_V_DIM:DA_V_DIM + 1, :]
        o = (o[:, 0:tq] - lam_ref[0] * o[:, tq:2 * tq]).T
        o_ref[0] = (_rms(o, g_ref[...]) * out_scale).astype(o_ref.dtype)

    slots = ((s0_ref, p0_ref, al0_ref, mt0_ref), (s1_ref, p1_ref, al1_ref, mt1_ref))
    _flash_pipe(pl.program_id(2), nq, nk, build_qt, k_ref, lambda: vt_ref[0, 0], b_ref, qt_ref, slots,
                m_ref, acc_ref, finish)


def diff_attn(proj, vt, bias_band, lam, sub_g, *, lam_init, tq, tk):
    b, n, _ = proj.shape
    _, rq, rk = _band_units(tq, tk)
    nq, nk = n // tq, n // tk
    kern = functools.partial(_dattn_kernel, tq=tq, nq=nq, nk=nk, out_scale=1.0 - lam_init)
    tile = functools.partial(_pipe_tile, nq=nq, nk=nk)

    def bias_map(bi, h, t):
        qi, ki = tile(t, 0)
        return (jnp.clip(rk * ki - rq * qi, -rk - 1, rq + 1) + rk + 1, h, 0, 0)

    return pl.pallas_call(
        kern,
        grid=(b, DA_HEADS, nq * nk + 2),
        in_specs=[
            pl.BlockSpec(memory_space=pltpu.SMEM),
            pl.BlockSpec((1, tq, 128), lambda bi, h, t: (bi, tile(t, 0)[0], COL_AQ // 128 + h)),
            pl.BlockSpec((1, tk, 128), lambda bi, h, t: (bi, tile(t, 0)[1], COL_AK // 128 + h)),
            pl.BlockSpec((1, 1, DA_V_DIM + SUM_ROWS, tk), lambda bi, h, t: (bi, h, 0, tile(t, 2)[1])),
            pl.BlockSpec((1, 1, tk, tq), bias_map),
            pl.BlockSpec((1, DA_V_DIM), lambda bi, h, t: (0, 0)),
        ],
        out_specs=pl.BlockSpec((1, tq, DA_V_DIM), lambda bi, h, t: (bi, tile(t, 2)[0], h)),
        out_shape=jax.ShapeDtypeStruct((b, n, DA_HEADS * DA_V_DIM), BF16),
        scratch_shapes=[pltpu.VMEM((128, 2 * tq), BF16)] + _flash_scratch(tk, 2 * tq, DA_V_DIM),
        compiler_params=_cparams(("parallel", "parallel", "arbitrary")),
        name="diff_attn",
    )(lam, proj, proj, vt, bias_band, sub_g)


def _swap16(x):
    w = x.shape[1]
    lane = lax.broadcasted_iota(jnp.int32, x.shape, 1)
    return jnp.where(lane % 32 < 16, pltpu.roll(x, w - 16, 1), pltpu.roll(x, 16, 1))


def _seg_rms(x, seg_ones, g):
    sq = x * x
    hi = sq.astype(BF16)
    lo = (sq - hi.astype(F32)).astype(BF16)
    ss = _dot(hi, seg_ones) + _dot(lo, seg_ones)
    return x * lax.rsqrt(ss * (1.0 / GQ_HEAD_DIM) + NORM_EPS) * g


def _gqa_prep_kernel(q_ref, k_ref, v_ref, cos_ref, sin_ref, gq_ref, gk_ref, ones_ref,
                     qo_ref, ko_ref, vo_ref):
    cos = cos_ref[...]
    sin = sin_ref[...]
    cos4 = jnp.concatenate([cos] * 4, axis=1)
    sin4 = jnp.concatenate([sin] * 4, axis=1)
    ones = ones_ref[...]
    q = _seg_rms(q_ref[0].astype(F32), ones, gq_ref[...])
    k = _seg_rms(k_ref[0].astype(F32), ones[0:128, 0:128], gk_ref[...])
    q = (q * cos4 + _swap16(q) * sin4) * (LOG2E * GQ_HEAD_DIM ** -0.5)
    qo_ref[0] = q.T.astype(BF16)
    ko_ref[0] = (k * cos + _swap16(k) * sin).astype(BF16)
    vt = v_ref[0].astype(F32).T.astype(BF16)
    hd = GQ_HEAD_DIM
    ones = jnp.ones((SUM_ROWS, vt.shape[1]), BF16)
    for h in range(GQ_KV_HEADS):
        vo_ref[0, h, 0:hd, :] = vt[h * hd:(h + 1) * hd]
        vo_ref[0, h, hd:hd + SUM_ROWS, :] = ones


def gqa_prep(proj, cos2, sin2, gq, gk, seg_ones, *, tm=512):
    b, n, _ = proj.shape
    return pl.pallas_call(
        _gqa_prep_kernel,
        grid=(b, n // tm),
        in_specs=[
            pl.BlockSpec((1, tm, 512), lambda bi, i: (bi, i, COL_CQ // 512)),
            pl.BlockSpec((1, tm, 128), lambda bi, i: (bi, i, COL_CK // 128)),
            pl.BlockSpec((1, tm, 128), lambda bi, i: (bi, i, COL_CV // 128)),
            pl.BlockSpec((tm, 128), lambda bi, i: (i, 0)),
            pl.BlockSpec((tm, 128), lambda bi, i: (i, 0)),
            pl.BlockSpec((1, 512), lambda bi, i: (0, 0)),
            pl.BlockSpec((1, 128), lambda bi, i: (0, 0)),
            pl.BlockSpec((512, 512), lambda bi, i: (0, 0)),
        ],
        out_specs=[
            pl.BlockSpec((1, GQ_HEADS * GQ_HEAD_DIM, tm), lambda bi, i: (bi, 0, i)),
            pl.BlockSpec((1, tm, GQ_KV_HEADS * GQ_HEAD_DIM), lambda bi, i: (bi, i, 0)),
            pl.BlockSpec((1, GQ_KV_HEADS, GQ_HEAD_DIM + SUM_ROWS, tm), lambda bi, i: (bi, 0, 0, i)),
        ],
        out_shape=[
            jax.ShapeDtypeStruct((b, GQ_HEADS * GQ_HEAD_DIM, n), BF16),
            jax.ShapeDtypeStruct((b, n, GQ_KV_HEADS * GQ_HEAD_DIM), BF16),
            jax.ShapeDtypeStruct((b, GQ_KV_HEADS, GQ_HEAD_DIM + SUM_ROWS, n), BF16),
        ],
        compiler_params=_cparams(("parallel", "parallel")),
        name="gqa_prep",
    )(proj, proj, proj, cos2, sin2, gq, gk, seg_ones)


def _gqa_kernel(q_ref, k_ref, vt_ref, o_ref, qt_ref,
                s0_ref, p0_ref, al0_ref, mt0_ref, s1_ref, p1_ref, al1_ref, mt1_ref,
                m_ref, acc_ref, *, tq, nq, nk):
    hd = GQ_HEAD_DIM

    def build_qt():
        qt_ref[...] = jnp.zeros(qt_ref.shape, BF16)
        base = pl.multiple_of(pl.program_id(1) * hd, hd)
        for r in range(GQ_GROUP):
            qt_ref[pl.ds(base, hd), r * tq:(r + 1) * tq] = q_ref[0, r * hd:(r + 1) * hd, :]

    def finish():
        o = (acc_ref[0:hd, :] / acc_ref[hd:hd + 1, :]).astype(o_ref.dtype)
        for r in range(GQ_GROUP):
            o_ref[0, r * hd:(r + 1) * hd, :] = o[:, r * tq:(r + 1) * tq]

    slots = ((s0_ref, p0_ref, al0_ref, mt0_ref), (s1_ref, p1_ref, al1_ref, mt1_ref))
    _flash_pipe(pl.program_id(2), nq, nk, build_qt, k_ref, lambda: vt_ref[0, 0], None, qt_ref, slots,
                m_ref, acc_ref, finish)


def gqa_attn(qt, k, vt, *, tq, tk):
    b, _, n = qt.shape
    gw = GQ_GROUP * GQ_HEAD_DIM
    kvw = GQ_KV_HEADS * GQ_HEAD_DIM
    nq, nk = n // tq, n // tk
    cols = GQ_GROUP * tq
    kern = functools.partial(_gqa_kernel, tq=tq, nq=nq, nk=nk)
    tile = functools.partial(_pipe_tile, nq=nq, nk=nk)
    return pl.pallas_call(
        kern,
        grid=(b, GQ_KV_HEADS, nq * nk + 2),
        in_specs=[
            pl.BlockSpec((1, gw, tq), lambda bi, g, t: (bi, g, tile(t, 0)[0])),
            pl.BlockSpec((1, tk, kvw), lambda bi, g, t: (bi, tile(t, 0)[1], 0)),
            pl.BlockSpec((1, 1, GQ_HEAD_DIM + SUM_ROWS, tk), lambda bi, g, t: (bi, g, 0, tile(t, 2)[1])),
        ],
        out_specs=pl.BlockSpec((1, gw, tq), lambda bi, g, t: (bi, g, tile(t, 2)[0])),
        out_shape=jax.ShapeDtypeStruct((b, GQ_HEADS * GQ_HEAD_DIM, n), BF16),
        scratch_shapes=[pltpu.VMEM((kvw, cols), BF16)] + _flash_scratch(tk, cols, GQ_HEAD_DIM),
        compiler_params=_cparams(("parallel", "parallel", "arbitrary")),
        name="gqa_attn",
    )(qt, k, vt)


def _gla_kernel(q_ref, k_ref, v_ref, z_ref, gw_ref, gb_ref, o_ref, st_ref, *, reverse, nchunk):
    L = GLA_CHUNK
    hd = GLA_HEADS * GLA_DK

    @pl.when(pl.program_id(1) == 0)
    def _():
        st_ref[...] = jnp.zeros(st_ref.shape, F32)

    row = lax.broadcasted_iota(jnp.int32, (L, L), 0)
    col = lax.broadcasted_iota(jnp.int32, (L, L), 1)
    if reverse:
        tri = (col >= row).astype(BF16)
        keep = col > row
    else:
        tri = (col <= row).astype(BF16)
        keep = col <= row
    lane = lax.broadcasted_iota(jnp.int32, (L, hd), 1)

    order = range(nchunk - 1, -1, -1) if reverse else range(nchunk)
    for c in order:
        rows = pl.ds(c * L, L)
        q = q_ref[0, rows, :].astype(F32) * (GLA_DK ** -0.5)
        k = k_ref[0, rows, :].astype(F32)
        logit = _dot(z_ref[0, rows, :], gw_ref[...]) + gb_ref[...]
        lg = jax.nn.log_sigmoid(logit) * (1.0 / GLA_TAU)
        hi = lg.astype(BF16)
        lo = (lg - hi.astype(F32)).astype(BF16)
        cum = _dot(tri, hi) + _dot(tri, lo)
        last = cum[0:1, :] if reverse else cum[L - 1:L, :]
        q_dec = q * jnp.exp(cum)
        k_dec = (k * jnp.exp(-cum)).astype(BF16)
        k_last = (k * jnp.exp(last - cum)).astype(BF16)
        decay = jnp.exp(last)
        for h in range(GLA_HEADS):
            head = (lane >= h * GLA_DK) & (lane < (h + 1) * GLA_DK)
            qh = jnp.where(head, q_dec, 0.0).astype(BF16)
            vh = v_ref[0, rows, h * GLA_DV:(h + 1) * GLA_DV]
            a = jnp.where(keep, _dot_nt(qh, k_dec), 0.0)
            st = st_ref[h]
            o = _dot(a.astype(BF16), vh) + _dot_nt(qh, st.astype(BF16))
            o_ref[0, rows, h * GLA_DV:(h + 1) * GLA_DV] = o
            st_ref[h] = st * decay + _dot_tn(vh, k_last)


def gla(proj, gw, gb, *, reverse, tm=512):
    b, n, _ = proj.shape
    nb = n // tm
    kern = functools.partial(_gla_kernel, reverse=reverse, nchunk=tm // GLA_CHUNK)
    blk = (lambda i: nb - 1 - i) if reverse else (lambda i: i)
    hd = GLA_HEADS * GLA_DK
    return pl.pallas_call(
        kern,
        grid=(b, nb),
        in_specs=[
            pl.BlockSpec((1, tm, hd), lambda bi, i: (bi, blk(i), COL_BQ // hd)),
            pl.BlockSpec((1, tm, hd), lambda bi, i: (bi, blk(i), COL_BK // hd)),
            pl.BlockSpec((1, tm, 512), lambda bi, i: (bi, blk(i), COL_BV // 512)),
            pl.BlockSpec((1, tm, 128), lambda bi, i: (bi, blk(i), COL_BZ // 128)),
            pl.BlockSpec((128, hd), lambda bi, i: (0, 0)),
            pl.BlockSpec((1, hd), lambda bi, i: (0, 0)),
        ],
        out_specs=pl.BlockSpec((1, tm, GLA_HEADS * GLA_DV), lambda bi, i: (bi, blk(i), 0)),
        out_shape=jax.ShapeDtypeStruct((b, n, GLA_HEADS * GLA_DV), F32),
        scratch_shapes=[pltpu.VMEM((GLA_HEADS, GLA_DV, hd), F32)],
        compiler_params=_cparams(("parallel", "arbitrary")),
        name="gla_bwd" if reverse else "gla_fwd",
    )(proj, proj, proj, proj, gw, gb)


def _merge_kernel(x_ref, ya_ref, of_ref, ob_ref, r_ref, yc_ref, gz_ref, gn_ref, wb_ref, wo_ref,
                  o_ref):
    gn = gn_ref[...]
    o = of_ref[0] + ob_ref[0]
    r = r_ref[0].astype(F32)
    yb = []
    for h in range(GLA_HEADS):
        sl = slice(h * GLA_DV, (h + 1) * GLA_DV)
        yb.append(_rms(o[:, sl], gn) * jax.nn.silu(r[:, sl]))
    yb = jnp.concatenate(yb, axis=1).astype(BF16)

    pa = _dot(ya_ref[0], wb_ref[0])
    pb = _dot(yb, wb_ref[1])
    pc = _dot_tn(yc_ref[0], wb_ref[2])

    gz = gz_ref[0].astype(F32)
    merged = (jax.nn.sigmoid(gz[:, 0:D_MODEL]) * pa
              + jax.nn.sigmoid(gz[:, D_MODEL:2 * D_MODEL]) * pb
              + jax.nn.sigmoid(gz[:, 2 * D_MODEL:3 * D_MODEL]) * pc)
    o_ref[0] = x_ref[0] + _dot(merged.astype(BF16), wo_ref[...])


def merge(x, ya, of, ob, proj, yc, gn, wb, wo, *, tm=512):
    b, n, _ = x.shape
    tok = lambda bi, i: (bi, i, 0)
    return pl.pallas_call(
        _merge_kernel,
        grid=(b, n // tm),
        in_specs=[
            pl.BlockSpec((1, tm, D_MODEL), tok),
            pl.BlockSpec((1, tm, BRANCH_W), tok),
            pl.BlockSpec((1, tm, BRANCH_W), tok),
            pl.BlockSpec((1, tm, BRANCH_W), tok),
            pl.BlockSpec((1, tm, 512), lambda bi, i: (bi, i, COL_BR // 512)),
            pl.BlockSpec((1, BRANCH_W, tm), lambda bi, i: (bi, 0, i)),
            pl.BlockSpec((1, tm, N_BRANCH * D_MODEL), lambda bi, i: (bi, i, COL_GZ // (N_BRANCH * D_MODEL))),
            pl.BlockSpec((1, GLA_DV), lambda bi, i: (0, 0)),
            pl.BlockSpec((N_BRANCH, BRANCH_W, D_MODEL), lambda bi, i: (0, 0, 0)),
            pl.BlockSpec((D_MODEL, D_MODEL), lambda bi, i: (0, 0)),
        ],
        out_specs=pl.BlockSpec((1, tm, D_MODEL), tok),
        out_shape=jax.ShapeDtypeStruct((b, n, D_MODEL), F32),
        compiler_params=_cparams(("parallel", "parallel")),
        name="merge",
    )(x, ya, of, ob, proj, yc, proj, gn, wb, wo)


TOPX = PEER_TOPK + 1
TOPX_PAD = 24


def _top_values(vals, count):
    out = []
    for _ in range(count):
        m = jnp.max(vals, axis=0, keepdims=True)
        out.append(m)
        vals = jnp.where(vals == m, NEG_INF, vals)
    return out


def _route_kernel(x_ref, g_ref, wq_ref, sk_ref, hn_ref, s1_ref, e1_ref, th_ref, e0_ref):
    tb = x_ref.shape[0]
    hn = _rms(x_ref[...], g_ref[...])
    hn_t = hn.T.astype(BF16)
    hn_ref[...] = hn_t
    q_t = _dot(wq_ref[...], hn_t).astype(BF16)
    pad = jnp.full((TOPX_PAD - TOPX, tb), NEG_INF, F32)
    for h in range(PEER_HEADS):
        r0 = (2 * h) * PEER_HALF
        s0 = _dot(sk_ref[h, 0], q_t[r0:r0 + PEER_HALF])
        s1 = _dot(sk_ref[h, 1], q_t[r0 + PEER_HALF:r0 + 2 * PEER_HALF])
        top0 = _top_values(s0, TOPX)
        top1 = _top_values(s1, TOPX)
        slab0 = jnp.concatenate(top0 + [pad], axis=0)
        slab1 = jnp.concatenate(top1 + [pad], axis=0)
        half = slab1[0:8]
        cand = jnp.concatenate([top0[0] + slab1] + [top0[a] + half for a in range(1, 8)]
                               + [slab0[8:TOPX_PAD] + top1[0]], axis=0)
        best = _top_values(cand, TOPX)
        thresh = 0.5 * (best[PEER_TOPK - 1] + best[PEER_TOPK])
        m0, m1 = top0[0], top1[0]
        z = jnp.zeros_like(m0)
        for c in best[:PEER_TOPK]:
            z = z + jnp.exp(c - best[0])
        e1 = jnp.exp(s1 - m1) / z
        th = thresh - s0
        e0 = jnp.exp(s0 - m0)
        for c in range(tb // 128):
            cs = slice(c * 128, (c + 1) * 128)
            s1_ref[h, c] = s1[:, cs]
            e1_ref[h, c] = e1[:, cs]
            th_ref[h, c] = th[:, cs]
            e0_ref[h, c] = e0[:, cs]


def peer_route(x2d, g, wq_t, subkeys, *, tb=256):
    t = x2d.shape[0]
    hk = jax.ShapeDtypeStruct((PEER_HEADS, t // 128, PEER_KEYS, 128), F32)
    hk_spec = pl.BlockSpec((PEER_HEADS, tb // 128, PEER_KEYS, 128), lambda i: (0, i, 0, 0))
    return pl.pallas_call(
        _route_kernel,
        grid=(t // tb,),
        in_specs=[
            pl.BlockSpec((tb, D_MODEL), lambda i: (i, 0)),
            pl.BlockSpec((1, D_MODEL), lambda i: (0, 0)),
            pl.BlockSpec((PEER_HEADS * PEER_QDIM, D_MODEL), lambda i: (0, 0)),
            pl.BlockSpec((PEER_HEADS, 2, PEER_KEYS, PEER_HALF), lambda i: (0, 0, 0, 0)),
        ],
        out_specs=[pl.BlockSpec((D_MODEL, tb), lambda i: (0, i)), hk_spec, hk_spec, hk_spec, hk_spec],
        out_shape=[jax.ShapeDtypeStruct((D_MODEL, t), BF16), hk, hk, hk, hk],
        compiler_params=_cparams(("parallel",)),
        name="peer_route",
    )(x2d, g, wq_t, subkeys)


DENSE_ROWS = 32


def _dense_kernel(x_ref, hn_ref, u_ref, vt_ref, s1_ref, e1_ref, th_ref, e0_ref, fg_ref, o_ref,
                  wg_ref, acc_ref, *, ne, ib, final_norm):
    e = pl.program_id(1)
    tb = hn_ref.shape[1]

    @pl.when(e == 0)
    def _():
        acc_ref[...] = jnp.zeros(acc_ref.shape, F32)

    a = _dot(u_ref[...], hn_ref[...])
    gelu_c = np.float32(np.sqrt(0.5))
    for il in range(ib):
        for tc in range(tb // 128):
            cols = slice(tc * 128, (tc + 1) * 128)
            for jc in range(PEER_KEYS // DENSE_ROWS):
                keys = slice(jc * DENSE_ROWS, (jc + 1) * DENSE_ROWS)
                rows = slice(il * PEER_KEYS + jc * DENSE_ROWS, il * PEER_KEYS + (jc + 1) * DENSE_ROWS)
                w = jnp.zeros((DENSE_ROWS, 128), F32)
                for h in range(PEER_HEADS):
                    sel = s1_ref[h, tc, keys, :] >= th_ref[h, tc, il:il + 1, :]
                    w = w + jnp.where(sel, e1_ref[h, tc, keys, :], 0.0) * e0_ref[h, tc, il:il + 1, :]
                ac = a[rows, cols]
                act = 0.5 * ac * (1.0 + lax.erf(ac * gelu_c))
                wg_ref[rows, cols] = (w * act).astype(BF16)
    acc_ref[...] += _dot(vt_ref[...], wg_ref[...])

    @pl.when(e == ne - 1)
    def _():
        y = x_ref[...] + acc_ref[...].T
        if final_norm:
            y = _rms(y, fg_ref[...])
        o_ref[...] = y


def peer_dense(x2d, hn_t, u, v_t, s1, e1, th, e0, final_g, *, final_norm, tb=512, ib=8):
    t = x2d.shape[0]
    eb = ib * PEER_KEYS
    ne = PEER_EXPERTS // eb
    kern = functools.partial(_dense_kernel, ne=ne, ib=ib, final_norm=final_norm)
    full = pl.BlockSpec((PEER_HEADS, tb // 128, PEER_KEYS, 128), lambda i, e: (0, i, 0, 0))
    part = pl.BlockSpec((PEER_HEADS, tb // 128, ib, 128), lambda i, e: (0, i, e, 0))
    return pl.pallas_call(
        kern,
        grid=(t // tb, ne),
        in_specs=[
            pl.BlockSpec((tb, D_MODEL), lambda i, e: (i, 0)),
            pl.BlockSpec((D_MODEL, tb), lambda i, e: (0, i)),
            pl.BlockSpec((eb, D_MODEL), lambda i, e: (e, 0)),
            pl.BlockSpec((D_MODEL, eb), lambda i, e: (0, e)),
            full, full, part, part,
            pl.BlockSpec((1, D_MODEL), lambda i, e: (0, 0)),
        ],
        out_specs=pl.BlockSpec((tb, D_MODEL), lambda i, e: (i, 0)),
        out_shape=jax.ShapeDtypeStruct((t, D_MODEL), F32),
        scratch_shapes=[pltpu.VMEM((eb, tb), BF16), pltpu.VMEM((D_MODEL, tb), F32)],
        compiler_params=_cparams(("parallel", "arbitrary")),
        name="peer_dense",
    )(x2d, hn_t, u, v_t, s1, e1, th, e0, final_g)


def _rel_bucket(rel):
    nb = REL_BUCKETS // 2
    max_exact = nb // 2
    n = jnp.abs(rel)
    nf = jnp.maximum(n, 1).astype(jnp.float32)
    large = max_exact + (jnp.log(nf / max_exact) / math.log(REL_MAX_DIST / max_exact)
                         * (nb - max_exact)).astype(jnp.int32)
    large = jnp.minimum(large, nb - 1)
    return jnp.where(rel > 0, nb, 0) + jnp.where(n < max_exact, n, large)


def _band_units(tq, tk):
    u = min(tq, tk)
    return u, tq // u, tk // u


def _bias_band(rel_table, tq, tk):
    u, rq, rk = _band_units(tq, tk)
    qpos = jnp.arange(tq)
    tiles = []
    for d in range(-rk - 1, rq + 2):
        rel = (d * u + jnp.arange(tk))[:, None] - qpos[None, :]
        onehot = jax.nn.one_hot(_rel_bucket(rel), REL_BUCKETS, dtype=F32)
        tiles.append(jnp.einsum("kqb,bh->hkq", onehot, rel_table.astype(F32),
                                precision=lax.Precision.HIGHEST))
    return jnp.stack(tiles) * LOG2E


def _rope_tables(n):
    rows = n // GRID_W
    row_id = jnp.repeat(jnp.arange(rows), GRID_W).astype(jnp.float32)
    col_id = (jnp.arange(rows * GRID_W) % GRID_W).astype(jnp.float32)
    sec = GQ_HEAD_DIM // 2
    inv = ROPE_THETA ** (-jnp.arange(0, sec, 2, dtype=jnp.float32) / sec)
    cr, sr = jnp.cos(row_id[:, None] * inv), jnp.sin(row_id[:, None] * inv)
    cc, sc = jnp.cos(col_id[:, None] * inv), jnp.sin(col_id[:, None] * inv)
    cos = jnp.concatenate([cr, cr, cc, cc], axis=1)
    sin = jnp.concatenate([-sr, sr, -sc, sc], axis=1)
    return jnp.concatenate([cos, cos], axis=1), jnp.concatenate([sin, sin], axis=1)


def _prep_weights(w_in, gla_gate_w, gla_gate_b, gq_qk_g, w_branch, w_out, peer_wq, peer_subkeys,
                  peer_u, peer_v):
    wi = jnp.concatenate(
        [w_in[:, :, ORIG_GZ:], w_in[:, :, :ORIG_BZ], w_in[:, :, ORIG_CQ:ORIG_GZ],
         w_in[:, :, ORIG_BZ:ORIG_CQ],
         jnp.zeros((DEPTH, D_MODEL, PROJ_W - ORIG_W), w_in.dtype)], axis=2).astype(BF16)
    hd = GLA_HEADS * GLA_DK
    gw = jnp.zeros((DEPTH, 2, 128, hd), F32)
    gw = gw.at[:, 0, 0:GLA_GATE_RANK].set(gla_gate_w[:, 0])
    gw = gw.at[:, 1, GLA_GATE_RANK:2 * GLA_GATE_RANK].set(gla_gate_w[:, 1])
    return dict(
        w_in=wi, gw=gw.astype(BF16), gb=gla_gate_b.astype(F32)[:, :, None, :],
        gq=jnp.tile(gq_qk_g[:, 0], (1, GQ_HEADS))[:, None, :],
        gk=jnp.tile(gq_qk_g[:, 1], (1, GQ_KV_HEADS))[:, None, :],
        wb=w_branch.astype(BF16), wo=w_out.astype(BF16),
        wq_t=jnp.swapaxes(peer_wq, 1, 2).astype(BF16), sk=peer_subkeys.astype(BF16),
        u=peer_u.astype(BF16), v_t=jnp.swapaxes(peer_v, 1, 2).astype(BF16))


def _encoder(x, w, rel_bias, norm1_g, da_lambda, da_subln_g, gla_norm_g, norm2_g, final_g, seg_ones):
    b, n, _ = x.shape
    t = b * n
    da_tq, da_tk = _da_tiles(n)
    band = _bias_band(rel_bias, da_tq, da_tk)
    cos2, sin2 = _rope_tables(n)
    for l in range(DEPTH):
        proj = in_proj(x.reshape(t, D_MODEL), norm1_g[l][None, :], w["w_in"][l]).reshape(b, n, PROJ_W)
        lam_init = 0.8 - 0.6 * math.exp(-0.3 * l)
        lp = da_lambda[l].astype(F32)
        lam = (jnp.exp(jnp.sum(lp[0] * lp[1])) - jnp.exp(jnp.sum(lp[2] * lp[3])) + lam_init).reshape(1)
        ya = diff_attn(proj, da_prep(proj), band, lam, da_subln_g[l][None, :], lam_init=lam_init,
                       tq=da_tq, tk=da_tk)
        qt, kr, vt = gqa_prep(proj, cos2, sin2, w["gq"][l], w["gk"][l], seg_ones)
        yc = gqa_attn(qt, kr, vt, tq=GQ_TQ, tk=GQ_TK)
        of = gla(proj, w["gw"][l, 0], w["gb"][l, 0], reverse=False)
        ob = gla(proj, w["gw"][l, 1], w["gb"][l, 1], reverse=True)
        x1 = merge(x, ya, of, ob, proj, yc, gla_norm_g[l][None, :], w["wb"][l], w["wo"][l])
        x1 = x1.reshape(t, D_MODEL)
        hn_t, s1, e1, th, e0 = peer_route(x1, norm2_g[l][None, :], w["wq_t"][l], w["sk"][l])
        x = peer_dense(x1, hn_t, w["u"][l], w["v_t"][l], s1, e1, th, e0, final_g[None, :],
                       final_norm=(l == DEPTH - 1)).reshape(b, n, D_MODEL)
    return x


def kernel(x_prompt, x_sample, rel_bias, norm1_g, w_in, da_lambda, da_subln_g, gla_gate_w, gla_gate_b,
           gla_norm_g, gq_qk_g, w_branch, w_out, norm2_g, peer_wq, peer_subkeys, peer_u, peer_v, final_g):
    w = _prep_weights(w_in, gla_gate_w, gla_gate_b, gq_qk_g, w_branch, w_out, peer_wq, peer_subkeys,
                      peer_u, peer_v)
    seg = np.arange(512) // GQ_HEAD_DIM
    seg_ones = jnp.asarray(seg[:, None] == seg[None, :], dtype=BF16)
    args = (w, rel_bias, norm1_g, da_lambda, da_subln_g, gla_norm_g, norm2_g, final_g, seg_ones)
    return (_encoder(x_prompt, *args), _encoder(x_sample, *args))
```

```python
import functools
import math

import numpy as np
import jax
import jax.numpy as jnp
from jax import lax
from jax.experimental import pallas as pl
from jax.experimental.pallas import tpu as pltpu

D_MODEL = 1024
DEPTH = 2
GRID_W = 64
NORM_EPS = 1e-6
N_BRANCH = 3
BRANCH_W = 512

DA_HEADS = 4
DA_HEAD_DIM = 64
DA_V_DIM = 128
REL_BUCKETS = 32
REL_MAX_DIST = 128

GLA_HEADS = 4
GLA_DK = 64
GLA_DV = 128
GLA_GATE_RANK = 16
GLA_TAU = 16.0
GLA_CHUNK = 64

GQ_HEADS = 8
GQ_KV_HEADS = 2
GQ_GROUP = GQ_HEADS // GQ_KV_HEADS
GQ_HEAD_DIM = 64
ROPE_THETA = 10000.0

PEER_HEADS = 8
PEER_KEYS = 128
PEER_EXPERTS = PEER_KEYS * PEER_KEYS
PEER_QDIM = 256
PEER_HALF = 128
PEER_TOPK = 16

COL_GZ = 0
COL_AQ, COL_AK, COL_AV = 3072, 3584, 4096
COL_BQ, COL_BK, COL_BV, COL_BR = 4608, 4864, 5120, 5632
COL_CQ, COL_CK, COL_CV = 6144, 6656, 6784
COL_BZ = 6912
PROJ_W = 7168
ORIG_BZ = 3072
ORIG_CQ = ORIG_BZ + 2 * GLA_GATE_RANK
ORIG_GZ = ORIG_CQ + 768
ORIG_W = ORIG_GZ + N_BRANCH * D_MODEL

GQ_TQ, GQ_TK = 512, 512


def _da_tiles(n):
    return (1024, 1024) if n >= 8192 else (512, 512)

VMEM_LIMIT = 56 * 1024 * 1024

F32 = jnp.float32
BF16 = jnp.bfloat16
NEG_INF = float("-inf")
LOG2E = math.log2(math.e)


def _cparams(sem):
    return pltpu.CompilerParams(dimension_semantics=sem, vmem_limit_bytes=VMEM_LIMIT)


def _dot(a, b):
    return jnp.dot(a, b, preferred_element_type=F32)


def _dot_nt(a, b):
    return lax.dot_general(a, b, (((1,), (1,)), ((), ())), preferred_element_type=F32)


def _dot_tn(a, b):
    return lax.dot_general(a, b, (((0,), (0,)), ((), ())), preferred_element_type=F32)


def _rms(x, g):
    return x * lax.rsqrt(jnp.mean(x * x, axis=-1, keepdims=True) + NORM_EPS) * g


def _in_proj_kernel(x_ref, g_ref, w_ref, o_ref, hn_ref):
    @pl.when(pl.program_id(1) == 0)
    def _():
        hn_ref[...] = _rms(x_ref[...], g_ref[...]).astype(BF16)

    o_ref[...] = _dot(hn_ref[...], w_ref[...]).astype(o_ref.dtype)


def in_proj(x2d, g, w, *, tm=2048, tn=1024):
    t = x2d.shape[0]
    nw = w.shape[1]
    return pl.pallas_call(
        _in_proj_kernel,
        grid=(t // tm, nw // tn),
        in_specs=[
            pl.BlockSpec((tm, D_MODEL), lambda i, j: (i, 0)),
            pl.BlockSpec((1, D_MODEL), lambda i, j: (0, 0)),
            pl.BlockSpec((D_MODEL, tn), lambda i, j: (0, j)),
        ],
        out_specs=pl.BlockSpec((tm, tn), lambda i, j: (i, j)),
        out_shape=jax.ShapeDtypeStruct((t, nw), BF16),
        scratch_shapes=[pltpu.VMEM((tm, D_MODEL), BF16)],
        compiler_params=_cparams(("parallel", "arbitrary")),
        name="in_proj",
    )(x2d, g, w)


SUM_ROWS = 16


def _flash_pipe_step(k, vt, bias, qt_ref, cur, prev, m_ref, acc_ref):
    s_cur, p_cur, al_cur, mt_cur = cur
    s_prev, p_prev, al_prev, mt_prev = prev
    s = _dot(k, qt_ref[...])
    if bias is not None:
        s = s + jnp.concatenate([bias] * (s.shape[1] // bias.shape[1]), axis=1)
    s_cur[...] = s
    mt_cur[...] = jnp.max(s, axis=0, keepdims=True)
    acc_ref[...] = al_cur[...] * acc_ref[...] + _dot(vt, p_cur[...])
    m_prev = m_ref[...]
    m_next = jnp.maximum(m_prev, mt_prev[...])
    p_prev[...] = jnp.exp2(s_prev[...] - m_next).astype(BF16)
    al_prev[...] = jnp.exp2(m_prev - m_next)
    m_ref[...] = m_next


def _pipe_tile(t, lag, nq, nk):
    j = jnp.clip(t - lag, 0, nq * nk - 1)
    return j // nk, j % nk


def _flash_pipe(t, nq, nk, build_qt, k_ref, get_vt, bias_ref, qt_ref, slots, m_ref, acc_ref, finish):
    _, kt_a = _pipe_tile(t, 0, nq, nk)
    _, kt_b = _pipe_tile(t, 1, nq, nk)
    _, kt_c = _pipe_tile(t, 2, nq, nk)

    @pl.when(t == 0)
    def _():
        (_, p0, al0, _), (s1, _, _, mt1) = slots
        for r in (s1, mt1, p0, al0, acc_ref):
            r[...] = jnp.zeros(r.shape, r.dtype)

    @pl.when(kt_a == 0)
    def _():
        build_qt()

    @pl.when(kt_b == 0)
    def _():
        m_ref[...] = jnp.full(m_ref.shape, NEG_INF, F32)

    for c in (0, 1):
        @pl.when(t % 2 == c)
        def _(c=c):
            bias = None if bias_ref is None else bias_ref[0, 0]
            _flash_pipe_step(k_ref[0], get_vt(), bias, qt_ref, slots[c], slots[1 - c], m_ref, acc_ref)

    @pl.when((t >= 2) & (kt_c == nk - 1))
    def _():
        finish()


def _flash_scratch(tk, cols, dv):
    slot = [pltpu.VMEM((tk, cols), F32), pltpu.VMEM((tk, cols), BF16),
            pltpu.VMEM((1, cols), F32), pltpu.VMEM((1, cols), F32)]
    return slot + slot + [pltpu.VMEM((1, cols), F32), pltpu.VMEM((dv + SUM_ROWS, cols), F32)]


def _dattn_kernel(lam_ref, q_ref, k_ref, v_ref, b_ref, g_ref, o_ref, qt_ref,
                  s0_ref, p0_ref, al0_ref, mt0_ref, s1_ref, p1_ref, al1_ref, mt1_ref,
                  m_ref, acc_ref, *, tq, nq, nk, out_scale):

    def build_qt():
        q = q_ref[0].astype(F32) * (LOG2E * DA_HEAD_DIM ** -0.5)
        lane = lax.broadcasted_iota(jnp.int32, q.shape, 1)
        qt_ref[:, 0:tq] = jnp.where(lane < DA_HEAD_DIM, q, 0.0).T.astype(BF16)
        qt_ref[:, tq:2 * tq] = jnp.where(lane >= DA_HEAD_DIM, q, 0.0).T.astype(BF16)

    def get_vt():
        vt = v_ref[0].astype(F32).T.astype(BF16)
        return jnp.concatenate([vt, jnp.ones((SUM_ROWS, vt.shape[1]), BF16)], axis=0)

    def finish():
        o = acc_ref[0:DA_V_DIM, :] / acc_ref[DA_V_DIM:DA_V_DIM + 1, :]
        o = (o[:, 0:tq] - lam_ref[0] * o[:, tq:2 * tq]).T
        o_ref[0] = (_rms(o, g_ref[...]) * out_scale).astype(o_ref.dtype)

    slots = ((s0_ref, p0_ref, al0_ref, mt0_ref), (s1_ref, p1_ref, al1_ref, mt1_ref))
    _flash_pipe(pl.program_id(2), nq, nk, build_qt, k_ref, get_vt, b_ref, qt_ref, slots, m_ref, acc_ref,
                finish)


def diff_attn(proj, bias_band, lam, sub_g, *, lam_init, tq, tk):
    b, n, _ = proj.shape
    _, rq, rk = _band_units(tq, tk)
    nq, nk = n // tq, n // tk
    kern = functools.partial(_dattn_kernel, tq=tq, nq=nq, nk=nk, out_scale=1.0 - lam_init)
    tile = functools.partial(_pipe_tile, nq=nq, nk=nk)

    def bias_map(bi, h, t):
        qi, ki = tile(t, 0)
        return (jnp.clip(rk * ki - rq * qi, -rk - 1, rq + 1) + rk + 1, h, 0, 0)

    return pl.pallas_call(
        kern,
        grid=(b, DA_HEADS, nq * nk + 2),
        in_specs=[
            pl.BlockSpec(memory_space=pltpu.SMEM),
            pl.BlockSpec((1, tq, 128), lambda bi, h, t: (bi, tile(t, 0)[0], COL_AQ // 128 + h)),
            pl.BlockSpec((1, tk, 128), lambda bi, h, t: (bi, tile(t, 0)[1], COL_AK // 128 + h)),
            pl.BlockSpec((1, tk, 128), lambda bi, h, t: (bi, tile(t, 2)[1], COL_AV // 128 + h)),
            pl.BlockSpec((1, 1, tk, tq), bias_map),
            pl.BlockSpec((1, DA_V_DIM), lambda bi, h, t: (0, 0)),
        ],
        out_specs=pl.BlockSpec((1, tq, DA_V_DIM), lambda bi, h, t: (bi, tile(t, 2)[0], h)),
        out_shape=jax.ShapeDtypeStruct((b, n, DA_HEADS * DA_V_DIM), BF16),
        scratch_shapes=[pltpu.VMEM((128, 2 * tq), BF16)] + _flash_scratch(tk, 2 * tq, DA_V_DIM),
        compiler_params=_cparams(("parallel", "parallel", "arbitrary")),
        name="diff_attn",
    )(lam, proj, proj, proj, bias_band, sub_g)


def _swap16(x):
    w = x.shape[1]
    lane = lax.broadcasted_iota(jnp.int32, x.shape, 1)
    return jnp.where(lane % 32 < 16, pltpu.roll(x, w - 16, 1), pltpu.roll(x, 16, 1))


def _seg_rms(x, seg_ones, g):
    sq = x * x
    hi = sq.astype(BF16)
    lo = (sq - hi.astype(F32)).astype(BF16)
    ss = _dot(hi, seg_ones) + _dot(lo, seg_ones)
    return x * lax.rsqrt(ss * (1.0 / GQ_HEAD_DIM) + NORM_EPS) * g


def _gqa_prep_kernel(q_ref, k_ref, v_ref, cos_ref, sin_ref, gq_ref, gk_ref, ones_ref,
                     qo_ref, ko_ref, vo_ref):
    cos = cos_ref[...]
    sin = sin_ref[...]
    cos4 = jnp.concatenate([cos] * 4, axis=1)
    sin4 = jnp.concatenate([sin] * 4, axis=1)
    ones = ones_ref[...]
    q = _seg_rms(q_ref[0].astype(F32), ones, gq_ref[...])
    k = _seg_rms(k_ref[0].astype(F32), ones[0:128, 0:128], gk_ref[...])
    q = (q * cos4 + _swap16(q) * sin4) * (LOG2E * GQ_HEAD_DIM ** -0.5)
    qo_ref[0] = q.T.astype(BF16)
    ko_ref[0] = (k * cos + _swap16(k) * sin).astype(BF16)
    vt = v_ref[0].astype(F32).T.astype(BF16)
    hd = GQ_HEAD_DIM
    ones = jnp.ones((SUM_ROWS, vt.shape[1]), BF16)
    for h in range(GQ_KV_HEADS):
        vo_ref[0, h, 0:hd, :] = vt[h * hd:(h + 1) * hd]
        vo_ref[0, h, hd:hd + SUM_ROWS, :] = ones


def gqa_prep(proj, cos2, sin2, gq, gk, seg_ones, *, tm=512):
    b, n, _ = proj.shape
    return pl.pallas_call(
        _gqa_prep_kernel,
        grid=(b, n // tm),
        in_specs=[
            pl.BlockSpec((1, tm, 512), lambda bi, i: (bi, i, COL_CQ // 512)),
            pl.BlockSpec((1, tm, 128), lambda bi, i: (bi, i, COL_CK // 128)),
            pl.BlockSpec((1, tm, 128), lambda bi, i: (bi, i, COL_CV // 128)),
            pl.BlockSpec((tm, 128), lambda bi, i: (i, 0)),
            pl.BlockSpec((tm, 128), lambda bi, i: (i, 0)),
            pl.BlockSpec((1, 512), lambda bi, i: (0, 0)),
            pl.BlockSpec((1, 128), lambda bi, i: (0, 0)),
            pl.BlockSpec((512, 512), lambda bi, i: (0, 0)),
        ],
        out_specs=[
            pl.BlockSpec((1, GQ_HEADS * GQ_HEAD_DIM, tm), lambda bi, i: (bi, 0, i)),
            pl.BlockSpec((1, tm, GQ_KV_HEADS * GQ_HEAD_DIM), lambda bi, i: (bi, i, 0)),
            pl.BlockSpec((1, GQ_KV_HEADS, GQ_HEAD_DIM + SUM_ROWS, tm), lambda bi, i: (bi, 0, 0, i)),
        ],
        out_shape=[
            jax.ShapeDtypeStruct((b, GQ_HEADS * GQ_HEAD_DIM, n), BF16),
            jax.ShapeDtypeStruct((b, n, GQ_KV_HEADS * GQ_HEAD_DIM), BF16),
            jax.ShapeDtypeStruct((b, GQ_KV_HEADS, GQ_HEAD_DIM + SUM_ROWS, n), BF16),
        ],
        compiler_params=_cparams(("parallel", "parallel")),
        name="gqa_prep",
    )(proj, proj, proj, cos2, sin2, gq, gk, seg_ones)


def _gqa_kernel(q_ref, k_ref, vt_ref, o_ref, qt_ref,
                s0_ref, p0_ref, al0_ref, mt0_ref, s1_ref, p1_ref, al1_ref, mt1_ref,
                m_ref, acc_ref, *, tq, nq, nk):
    hd = GQ_HEAD_DIM

    def build_qt():
        qt_ref[...] = jnp.zeros(qt_ref.shape, BF16)
        base = pl.multiple_of(pl.program_id(1) * hd, hd)
        for r in range(GQ_GROUP):
            qt_ref[pl.ds(base, hd), r * tq:(r + 1) * tq] = q_ref[0, r * hd:(r + 1) * hd, :]

    def finish():
        o = (acc_ref[0:hd, :] / acc_ref[hd:hd + 1, :]).astype(o_ref.dtype)
        for r in range(GQ_GROUP):
            o_ref[0, r * hd:(r + 1) * hd, :] = o[:, r * tq:(r + 1) * tq]

    slots = ((s0_ref, p0_ref, al0_ref, mt0_ref), (s1_ref, p1_ref, al1_ref, mt1_ref))
    _flash_pipe(pl.program_id(2), nq, nk, build_qt, k_ref, lambda: vt_ref[0, 0], None, qt_ref, slots,
                m_ref, acc_ref, finish)


def gqa_attn(qt, k, vt, *, tq, tk):
    b, _, n = qt.shape
    gw = GQ_GROUP * GQ_HEAD_DIM
    kvw = GQ_KV_HEADS * GQ_HEAD_DIM
    nq, nk = n // tq, n // tk
    cols = GQ_GROUP * tq
    kern = functools.partial(_gqa_kernel, tq=tq, nq=nq, nk=nk)
    tile = functools.partial(_pipe_tile, nq=nq, nk=nk)
    return pl.pallas_call(
        kern,
        grid=(b, GQ_KV_HEADS, nq * nk + 2),
        in_specs=[
            pl.BlockSpec((1, gw, tq), lambda bi, g, t: (bi, g, tile(t, 0)[0])),
            pl.BlockSpec((1, tk, kvw), lambda bi, g, t: (bi, tile(t, 0)[1], 0)),
            pl.BlockSpec((1, 1, GQ_HEAD_DIM + SUM_ROWS, tk), lambda bi, g, t: (bi, g, 0, tile(t, 2)[1])),
        ],
        out_specs=pl.BlockSpec((1, gw, tq), lambda bi, g, t: (bi, g, tile(t, 2)[0])),
        out_shape=jax.ShapeDtypeStruct((b, GQ_HEADS * GQ_HEAD_DIM, n), BF16),
        scratch_shapes=[pltpu.VMEM((kvw, cols), BF16)] + _flash_scratch(tk, cols, GQ_HEAD_DIM),
        compiler_params=_cparams(("parallel", "parallel", "arbitrary")),
        name="gqa_attn",
    )(qt, k, vt)


def _gla_kernel(q_ref, k_ref, v_ref, z_ref, gw_ref, gb_ref, o_ref, st_ref, *, reverse, nchunk):
    L = GLA_CHUNK
    hd = GLA_HEADS * GLA_DK

    @pl.when(pl.program_id(1) == 0)
    def _():
        st_ref[...] = jnp.zeros(st_ref.shape, F32)

    row = lax.broadcasted_iota(jnp.int32, (L, L), 0)
    col = lax.broadcasted_iota(jnp.int32, (L, L), 1)
    if reverse:
        tri = (col >= row).astype(BF16)
        keep = col > row
    else:
        tri = (col <= row).astype(BF16)
        keep = col <= row
    lane = lax.broadcasted_iota(jnp.int32, (L, hd), 1)

    order = range(nchunk - 1, -1, -1) if reverse else range(nchunk)
    for c in order:
        rows = pl.ds(c * L, L)
        q = q_ref[0, rows, :].astype(F32) * (GLA_DK ** -0.5)
        k = k_ref[0, rows, :].astype(F32)
        logit = _dot(z_ref[0, rows, :], gw_ref[...]) + gb_ref[...]
        lg = jax.nn.log_sigmoid(logit) * (1.0 / GLA_TAU)
        hi = lg.astype(BF16)
        lo = (lg - hi.astype(F32)).astype(BF16)
        cum = _dot(tri, hi) + _dot(tri, lo)
        last = cum[0:1, :] if reverse else cum[L - 1:L, :]
        q_dec = q * jnp.exp(cum)
        k_dec = (k * jnp.exp(-cum)).astype(BF16)
        k_last = (k * jnp.exp(last - cum)).astype(BF16)
        decay = jnp.exp(last)
        for h in range(GLA_HEADS):
            head = (lane >= h * GLA_DK) & (lane < (h + 1) * GLA_DK)
            qh = jnp.where(head, q_dec, 0.0).astype(BF16)
            vh = v_ref[0, rows, h * GLA_DV:(h + 1) * GLA_DV]
            a = jnp.where(keep, _dot_nt(qh, k_dec), 0.0)
            st = st_ref[h]
            o = _dot(a.astype(BF16), vh) + _dot_nt(qh, st.astype(BF16))
            o_ref[0, rows, h * GLA_DV:(h + 1) * GLA_DV] = o
            st_ref[h] = st * decay + _dot_tn(vh, k_last)


def gla(proj, gw, gb, *, reverse, tm=512):
    b, n, _ = proj.shape
    nb = n // tm
    kern = functools.partial(_gla_kernel, reverse=reverse, nchunk=tm // GLA_CHUNK)
    blk = (lambda i: nb - 1 - i) if reverse else (lambda i: i)
    hd = GLA_HEADS * GLA_DK
    return pl.pallas_call(
        kern,
        grid=(b, nb),
        in_specs=[
            pl.BlockSpec((1, tm, hd), lambda bi, i: (bi, blk(i), COL_BQ // hd)),
            pl.BlockSpec((1, tm, hd), lambda bi, i: (bi, blk(i), COL_BK // hd)),
            pl.BlockSpec((1, tm, 512), lambda bi, i: (bi, blk(i), COL_BV // 512)),
            pl.BlockSpec((1, tm, 128), lambda bi, i: (bi, blk(i), COL_BZ // 128)),
            pl.BlockSpec((128, hd), lambda bi, i: (0, 0)),
            pl.BlockSpec((1, hd), lambda bi, i: (0, 0)),
        ],
        out_specs=pl.BlockSpec((1, tm, GLA_HEADS * GLA_DV), lambda bi, i: (bi, blk(i), 0)),
        out_shape=jax.ShapeDtypeStruct((b, n, GLA_HEADS * GLA_DV), F32),
        scratch_shapes=[pltpu.VMEM((GLA_HEADS, GLA_DV, hd), F32)],
        compiler_params=_cparams(("parallel", "arbitrary")),
        name="gla_bwd" if reverse else "gla_fwd",
    )(proj, proj, proj, proj, gw, gb)


def _merge_kernel(x_ref, ya_ref, of_ref, ob_ref, r_ref, yc_ref, gz_ref, gn_ref, wb_ref, wo_ref,
                  o_ref):
    gn = gn_ref[...]
    o = of_ref[0] + ob_ref[0]
    r = r_ref[0].astype(F32)
    yb = []
    for h in range(GLA_HEADS):
        sl = slice(h * GLA_DV, (h + 1) * GLA_DV)
        yb.append(_rms(o[:, sl], gn) * jax.nn.silu(r[:, sl]))
    yb = jnp.concatenate(yb, axis=1).astype(BF16)

    pa = _dot(ya_ref[0], wb_ref[0])
    pb = _dot(yb, wb_ref[1])
    pc = _dot_tn(yc_ref[0], wb_ref[2])

    gz = gz_ref[0].astype(F32)
    merged = (jax.nn.sigmoid(gz[:, 0:D_MODEL]) * pa
              + jax.nn.sigmoid(gz[:, D_MODEL:2 * D_MODEL]) * pb
              + jax.nn.sigmoid(gz[:, 2 * D_MODEL:3 * D_MODEL]) * pc)
    o_ref[0] = x_ref[0] + _dot(merged.astype(BF16), wo_ref[...])


def merge(x, ya, of, ob, proj, yc, gn, wb, wo, *, tm=512):
    b, n, _ = x.shape
    tok = lambda bi, i: (bi, i, 0)
    return pl.pallas_call(
        _merge_kernel,
        grid=(b, n // tm),
        in_specs=[
            pl.BlockSpec((1, tm, D_MODEL), tok),
            pl.BlockSpec((1, tm, BRANCH_W), tok),
            pl.BlockSpec((1, tm, BRANCH_W), tok),
            pl.BlockSpec((1, tm, BRANCH_W), tok),
            pl.BlockSpec((1, tm, 512), lambda bi, i: (bi, i, COL_BR // 512)),
            pl.BlockSpec((1, BRANCH_W, tm), lambda bi, i: (bi, 0, i)),
            pl.BlockSpec((1, tm, N_BRANCH * D_MODEL), lambda bi, i: (bi, i, COL_GZ // (N_BRANCH * D_MODEL))),
            pl.BlockSpec((1, GLA_DV), lambda bi, i: (0, 0)),
            pl.BlockSpec((N_BRANCH, BRANCH_W, D_MODEL), lambda bi, i: (0, 0, 0)),
            pl.BlockSpec((D_MODEL, D_MODEL), lambda bi, i: (0, 0)),
        ],
        out_specs=pl.BlockSpec((1, tm, D_MODEL), tok),
        out_shape=jax.ShapeDtypeStruct((b, n, D_MODEL), F32),
        compiler_params=_cparams(("parallel", "parallel")),
        name="merge",
    )(x, ya, of, ob, proj, yc, proj, gn, wb, wo)


TOPX = PEER_TOPK + 1
TOPX_PAD = 24


def _top_values(vals, count):
    out = []
    for _ in range(count):
        m = jnp.max(vals, axis=0, keepdims=True)
        out.append(m)
        vals = jnp.where(vals == m, NEG_INF, vals)
    return out


def _route_kernel(x_ref, g_ref, wq_ref, sk_ref, hn_ref, s1_ref, e1_ref, th_ref, e0_ref):
    tb = x_ref.shape[0]
    hn = _rms(x_ref[...], g_ref[...])
    hn_t = hn.T.astype(BF16)
    hn_ref[...] = hn_t
    q_t = _dot(wq_ref[...], hn_t).astype(BF16)
    pad = jnp.full((TOPX_PAD - TOPX, tb), NEG_INF, F32)
    for h in range(PEER_HEADS):
        r0 = (2 * h) * PEER_HALF
        s0 = _dot(sk_ref[h, 0], q_t[r0:r0 + PEER_HALF])
        s1 = _dot(sk_ref[h, 1], q_t[r0 + PEER_HALF:r0 + 2 * PEER_HALF])
        top0 = _top_values(s0, TOPX)
        top1 = _top_values(s1, TOPX)
        slab0 = jnp.concatenate(top0 + [pad], axis=0)
        slab1 = jnp.concatenate(top1 + [pad], axis=0)
        half = slab1[0:8]
        cand = jnp.concatenate([top0[0] + slab1] + [top0[a] + half for a in range(1, 8)]
                               + [slab0[8:TOPX_PAD] + top1[0]], axis=0)
        best = _top_values(cand, TOPX)
        thresh = 0.5 * (best[PEER_TOPK - 1] + best[PEER_TOPK])
        m0, m1 = top0[0], top1[0]
        z = jnp.zeros_like(m0)
        for c in best[:PEER_TOPK]:
            z = z + jnp.exp(c - best[0])
        e1 = jnp.exp(s1 - m1) / z
        th = thresh - s0
        e0 = jnp.exp(s0 - m0)
        for c in range(tb // 128):
            cs = slice(c * 128, (c + 1) * 128)
            s1_ref[h, c] = s1[:, cs]
            e1_ref[h, c] = e1[:, cs]
            th_ref[h, c] = th[:, cs]
            e0_ref[h, c] = e0[:, cs]


def peer_route(x2d, g, wq_t, subkeys, *, tb=256):
    t = x2d.shape[0]
    hk = jax.ShapeDtypeStruct((PEER_HEADS, t // 128, PEER_KEYS, 128), F32)
    hk_spec = pl.BlockSpec((PEER_HEADS, tb // 128, PEER_KEYS, 128), lambda i: (0, i, 0, 0))
    return pl.pallas_call(
        _route_kernel,
        grid=(t // tb,),
        in_specs=[
            pl.BlockSpec((tb, D_MODEL), lambda i: (i, 0)),
            pl.BlockSpec((1, D_MODEL), lambda i: (0, 0)),
            pl.BlockSpec((PEER_HEADS * PEER_QDIM, D_MODEL), lambda i: (0, 0)),
            pl.BlockSpec((PEER_HEADS, 2, PEER_KEYS, PEER_HALF), lambda i: (0, 0, 0, 0)),
        ],
        out_specs=[pl.BlockSpec((D_MODEL, tb), lambda i: (0, i)), hk_spec, hk_spec, hk_spec, hk_spec],
        out_shape=[jax.ShapeDtypeStruct((D_MODEL, t), BF16), hk, hk, hk, hk],
        compiler_params=_cparams(("parallel",)),
        name="peer_route",
    )(x2d, g, wq_t, subkeys)


DENSE_ROWS = 32


def _dense_kernel(x_ref, hn_ref, u_ref, vt_ref, s1_ref, e1_ref, th_ref, e0_ref, fg_ref, o_ref,
                  wg_ref, acc_ref, *, ne, ib, final_norm):
    e = pl.program_id(1)
    tb = hn_ref.shape[1]

    @pl.when(e == 0)
    def _():
        acc_ref[...] = jnp.zeros(acc_ref.shape, F32)

    a = _dot(u_ref[...], hn_ref[...])
    gelu_c = np.float32(np.sqrt(0.5))
    for il in range(ib):
        for tc in range(tb // 128):
            cols = slice(tc * 128, (tc + 1) * 128)
            for jc in range(PEER_KEYS // DENSE_ROWS):
                keys = slice(jc * DENSE_ROWS, (jc + 1) * DENSE_ROWS)
                rows = slice(il * PEER_KEYS + jc * DENSE_ROWS, il * PEER_KEYS + (jc + 1) * DENSE_ROWS)
                w = jnp.zeros((DENSE_ROWS, 128), F32)
                for h in range(PEER_HEADS):
                    sel = s1_ref[h, tc, keys, :] >= th_ref[h, tc, il:il + 1, :]
                    w = w + jnp.where(sel, e1_ref[h, tc, keys, :], 0.0) * e0_ref[h, tc, il:il + 1, :]
                ac = a[rows, cols]
                act = 0.5 * ac * (1.0 + lax.erf(ac * gelu_c))
                wg_ref[rows, cols] = (w * act).astype(BF16)
    acc_ref[...] += _dot(vt_ref[...], wg_ref[...])

    @pl.when(e == ne - 1)
    def _():
        y = x_ref[...] + acc_ref[...].T
        if final_norm:
            y = _rms(y, fg_ref[...])
        o_ref[...] = y


def peer_dense(x2d, hn_t, u, v_t, s1, e1, th, e0, final_g, *, final_norm, tb=512, ib=16):
    t = x2d.shape[0]
    eb = ib * PEER_KEYS
    ne = PEER_EXPERTS // eb
    kern = functools.partial(_dense_kernel, ne=ne, ib=ib, final_norm=final_norm)
    full = pl.BlockSpec((PEER_HEADS, tb // 128, PEER_KEYS, 128), lambda i, e: (0, i, 0, 0))
    part = pl.BlockSpec((PEER_HEADS, tb // 128, ib, 128), lambda i, e: (0, i, e, 0))
    return pl.pallas_call(
        kern,
        grid=(t // tb, ne),
        in_specs=[
            pl.BlockSpec((tb, D_MODEL), lambda i, e: (i, 0)),
            pl.BlockSpec((D_MODEL, tb), lambda i, e: (0, i)),
            pl.BlockSpec((eb, D_MODEL), lambda i, e: (e, 0)),
            pl.BlockSpec((D_MODEL, eb), lambda i, e: (0, e)),
            full, full, part, part,
            pl.BlockSpec((1, D_MODEL), lambda i, e: (0, 0)),
        ],
        out_specs=pl.BlockSpec((tb, D_MODEL), lambda i, e: (i, 0)),
        out_shape=jax.ShapeDtypeStruct((t, D_MODEL), F32),
        scratch_shapes=[pltpu.VMEM((eb, tb), BF16), pltpu.VMEM((D_MODEL, tb), F32)],
        compiler_params=_cparams(("parallel", "arbitrary")),
        name="peer_dense",
    )(x2d, hn_t, u, v_t, s1, e1, th, e0, final_g)


def _rel_bucket(rel):
    nb = REL_BUCKETS // 2
    max_exact = nb // 2
    n = jnp.abs(rel)
    nf = jnp.maximum(n, 1).astype(jnp.float32)
    large = max_exact + (jnp.log(nf / max_exact) / math.log(REL_MAX_DIST / max_exact)
                         * (nb - max_exact)).astype(jnp.int32)
    large = jnp.minimum(large, nb - 1)
    return jnp.where(rel > 0, nb, 0) + jnp.where(n < max_exact, n, large)


def _band_units(tq, tk):
    u = min(tq, tk)
    return u, tq // u, tk // u


def _bias_band(rel_table, tq, tk):
    u, rq, rk = _band_units(tq, tk)
    qpos = jnp.arange(tq)
    tiles = []
    for d in range(-rk - 1, rq + 2):
        rel = (d * u + jnp.arange(tk))[:, None] - qpos[None, :]
        onehot = jax.nn.one_hot(_rel_bucket(rel), REL_BUCKETS, dtype=F32)
        tiles.append(jnp.einsum("kqb,bh->hkq", onehot, rel_table.astype(F32),
                                precision=lax.Precision.HIGHEST))
    return jnp.stack(tiles) * LOG2E


def _rope_tables(n):
    rows = n // GRID_W
    row_id = jnp.repeat(jnp.arange(rows), GRID_W).astype(jnp.float32)
    col_id = (jnp.arange(rows * GRID_W) % GRID_W).astype(jnp.float32)
    sec = GQ_HEAD_DIM // 2
    inv = ROPE_THETA ** (-jnp.arange(0, sec, 2, dtype=jnp.float32) / sec)
    cr, sr = jnp.cos(row_id[:, None] * inv), jnp.sin(row_id[:, None] * inv)
    cc, sc = jnp.cos(col_id[:, None] * inv), jnp.sin(col_id[:, None] * inv)
    cos = jnp.concatenate([cr, cr, cc, cc], axis=1)
    sin = jnp.concatenate([-sr, sr, -sc, sc], axis=1)
    return jnp.concatenate([cos, cos], axis=1), jnp.concatenate([sin, sin], axis=1)


def _prep_weights(w_in, gla_gate_w, gla_gate_b, gq_qk_g, w_branch, w_out, peer_wq, peer_subkeys,
                  peer_u, peer_v):
    wi = jnp.concatenate(
        [w_in[:, :, ORIG_GZ:], w_in[:, :, :ORIG_BZ], w_in[:, :, ORIG_CQ:ORIG_GZ],
         w_in[:, :, ORIG_BZ:ORIG_CQ],
         jnp.zeros((DEPTH, D_MODEL, PROJ_W - ORIG_W), w_in.dtype)], axis=2).astype(BF16)
    hd = GLA_HEADS * GLA_DK
    gw = jnp.zeros((DEPTH, 2, 128, hd), F32)
    gw = gw.at[:, 0, 0:GLA_GATE_RANK].set(gla_gate_w[:, 0])
    gw = gw.at[:, 1, GLA_GATE_RANK:2 * GLA_GATE_RANK].set(gla_gate_w[:, 1])
    return dict(
        w_in=wi, gw=gw.astype(BF16), gb=gla_gate_b.astype(F32)[:, :, None, :],
        gq=jnp.tile(gq_qk_g[:, 0], (1, GQ_HEADS))[:, None, :],
        gk=jnp.tile(gq_qk_g[:, 1], (1, GQ_KV_HEADS))[:, None, :],
        wb=w_branch.astype(BF16), wo=w_out.astype(BF16),
        wq_t=jnp.swapaxes(peer_wq, 1, 2).astype(BF16), sk=peer_subkeys.astype(BF16),
        u=peer_u.astype(BF16), v_t=jnp.swapaxes(peer_v, 1, 2).astype(BF16))


def _encoder(x, w, rel_bias, norm1_g, da_lambda, da_subln_g, gla_norm_g, norm2_g, final_g, seg_ones):
    b, n, _ = x.shape
    t = b * n
    da_tq, da_tk = _da_tiles(n)
    band = _bias_band(rel_bias, da_tq, da_tk)
    cos2, sin2 = _rope_tables(n)
    for l in range(DEPTH):
        proj = in_proj(x.reshape(t, D_MODEL), norm1_g[l][None, :], w["w_in"][l]).reshape(b, n, PROJ_W)
        lam_init = 0.8 - 0.6 * math.exp(-0.3 * l)
        lp = da_lambda[l].astype(F32)
        lam = (jnp.exp(jnp.sum(lp[0] * lp[1])) - jnp.exp(jnp.sum(lp[2] * lp[3])) + lam_init).reshape(1)
        ya = diff_attn(proj, band, lam, da_subln_g[l][None, :], lam_init=lam_init, tq=da_tq, tk=da_tk)
        qt, kr, vt = gqa_prep(proj, cos2, sin2, w["gq"][l], w["gk"][l], seg_ones)
        yc = gqa_attn(qt, kr, vt, tq=GQ_TQ, tk=GQ_TK)
        of = gla(proj, w["gw"][l, 0], w["gb"][l, 0], reverse=False)
        ob = gla(proj, w["gw"][l, 1], w["gb"][l, 1], reverse=True)
        x1 = merge(x, ya, of, ob, proj, yc, gla_norm_g[l][None, :], w["wb"][l], w["wo"][l])
        x1 = x1.reshape(t, D_MODEL)
        hn_t, s1, e1, th, e0 = peer_route(x1, norm2_g[l][None, :], w["wq_t"][l], w["sk"][l])
        x = peer_dense(x1, hn_t, w["u"][l], w["v_t"][l], s1, e1, th, e0, final_g[None, :],
                       final_norm=(l == DEPTH - 1)).reshape(b, n, D_MODEL)
    return x


def kernel(x_prompt, x_sample, rel_bias, norm1_g, w_in, da_lambda, da_subln_g, gla_gate_w, gla_gate_b,
           gla_norm_g, gq_qk_g, w_branch, w_out, norm2_g, peer_wq, peer_subkeys, peer_u, peer_v, final_g):
    w = _prep_weights(w_in, gla_gate_w, gla_gate_b, gq_qk_g, w_branch, w_out, peer_wq, peer_subkeys,
                      peer_u, peer_v)
    seg = np.arange(512) // GQ_HEAD_DIM
    seg_ones = jnp.asarray(seg[:, None] == seg[None, :], dtype=BF16)
    args = (w, rel_bias, norm1_g, da_lambda, da_subln_g, gla_norm_g, norm2_g, final_g, seg_ones)
    return (_encoder(x_prompt, *args), _encoder(x_sample, *args))
```
